```python
import jax, jax.numpy as jnp
from jax import lax
import numpy as np

D_MODEL = 2048
BATCH = 2
SEQ = 4096
DEPTH = 2
DEC_BATCH = 128
DEC_SEQ = 8
PAST_LEN = 8192
PAGE_SIZE = 128

N_HEADS = 16
N_KV_HEADS = 4
HEAD_DIM = 64
ATTN_WIDTH = N_HEADS * HEAD_DIM
KV_WIDTH = N_KV_HEADS * HEAD_DIM
WINDOW = 128
BLOCK = 128
CONV_CH = D_MODEL // 2
CONV_WIDTH = 31
IN_COLS = ATTN_WIDTH + 2 * KV_WIDTH + 2 * CONV_CH
MIX_OUT = ATTN_WIDTH + CONV_CH
POOL_WINDOWS = (2, 4, 8, 16)
N_POOL_GROUPS = 4
POOL_GROUP = D_MODEL // N_POOL_GROUPS
POOL_STATE = max(POOL_WINDOWS) - 1
D_FF = -(-8 * D_MODEL // (3 * 256)) * 256
N_EVEN = (DEPTH + 1) // 2
N_ODD = DEPTH // 2
RMS_EPS = 1e-6
LN_EPS = 1e-5
NEG = -1e30

kernel_name = 'swa_sink_conformer_pool_hybrid_step'


def _rmsnorm(x, g):
    xf = x.astype(jnp.float32)
    y = xf * lax.rsqrt(jnp.mean(xf * xf, axis=-1, keepdims=True) + RMS_EPS)
    return (y * g.astype(jnp.float32)).astype(x.dtype)


def _layernorm(x, g, b):
    xf = x.astype(jnp.float32)
    xc = xf - jnp.mean(xf, axis=-1, keepdims=True)
    var = jnp.mean(xc * xc, axis=-1, keepdims=True)
    return (xc * lax.rsqrt(var + LN_EPS) * g.astype(jnp.float32) + b.astype(jnp.float32)).astype(x.dtype)


def _swiglu(h, w_gate, w_up, w_down):
    return (jax.nn.silu(h @ w_gate) * (h @ w_up)) @ w_down


def _sink_window_attention(q, k, v, qpos, kpos, sinks):
    n, tq = q.shape[:2]
    g = N_HEADS // N_KV_HEADS
    qg = q.reshape(n, tq, N_KV_HEADS, g, HEAD_DIM)
    s = jnp.einsum('ntkgd,nskd->nkgts', qg, k, preferred_element_type=jnp.float32) * (HEAD_DIM ** -0.5)
    diff = qpos[:, :, None] - kpos[:, None, :]
    ok = (diff >= 0) & (diff < WINDOW) & (kpos[:, None, :] >= 0)
    s = jnp.where(ok[:, None, None], s, NEG)
    sink = sinks.astype(jnp.float32).reshape(1, N_KV_HEADS, g, 1, 1)
    m = jnp.maximum(jnp.max(s, axis=-1, keepdims=True), sink)
    p = jnp.exp(s - m)
    denom = jnp.sum(p, axis=-1, keepdims=True) + jnp.exp(sink - m)
    o = jnp.einsum('nkgts,nskd->ntkgd', p / denom, v.astype(jnp.float32))
    return o.reshape(n, tq, ATTN_WIDTH).astype(q.dtype)


def _band_blocks(t):
    b, s = t.shape[:2]
    nb = s // BLOCK
    cur = t.reshape(b, nb, BLOCK, *t.shape[2:])
    prev = jnp.concatenate([jnp.zeros_like(cur[:, :1]), cur[:, :-1]], axis=1)
    return jnp.concatenate([prev, cur], axis=2).reshape(b * nb, 2 * BLOCK, *t.shape[2:])


def _prompt_positions(b, s):
    nb = s // BLOCK
    qpos = jnp.arange(s, dtype=jnp.int32).reshape(nb, BLOCK)
    kpos = jnp.arange(nb, dtype=jnp.int32)[:, None] * BLOCK - BLOCK + jnp.arange(2 * BLOCK, dtype=jnp.int32)[None, :]
    qpos = jnp.broadcast_to(qpos[None], (b, nb, BLOCK)).reshape(b * nb, BLOCK)
    kpos = jnp.broadcast_to(kpos[None], (b, nb, 2 * BLOCK)).reshape(b * nb, 2 * BLOCK)
    return qpos, kpos


def _depthwise_causal_conv(xp, w, b):
    y = lax.conv_general_dilated(xp, w.astype(xp.dtype)[:, None, :], window_strides=(1,), padding='VALID',
                                 dimension_numbers=('NWC', 'WIO', 'NWC'), feature_group_count=xp.shape[-1])
    return y + b.astype(y.dtype)


def _even_mixer(h, w_in, q_gain, k_gain, sinks, conv_w, conv_b, ln_g, ln_b, w_out, k_past, v_past, conv_past, past_len):
    n, t, _ = h.shape
    z = h @ w_in
    q, k, v, u = jnp.split(z, [ATTN_WIDTH, ATTN_WIDTH + KV_WIDTH, ATTN_WIDTH + 2 * KV_WIDTH], axis=-1)
    q = _rmsnorm(q.reshape(n, t, N_HEADS, HEAD_DIM), q_gain)
    k = _rmsnorm(k.reshape(n, t, N_KV_HEADS, HEAD_DIM), k_gain)
    v = v.reshape(n, t, N_KV_HEADS, HEAD_DIM)
    if k_past is None:
        nb = t // BLOCK
        qpos, kpos = _prompt_positions(n, t)
        attn = _sink_window_attention(q.reshape(n * nb, BLOCK, N_HEADS, HEAD_DIM), _band_blocks(k), _band_blocks(v), qpos, kpos, sinks)
        attn = attn.reshape(n, t, ATTN_WIDTH)
        k_all, v_all = k, v
        conv_past = jnp.zeros((n, CONV_WIDTH - 1, CONV_CH), h.dtype)
    else:
        k_all = jnp.concatenate([k_past.astype(k.dtype), k], axis=1)
        v_all = jnp.concatenate([v_past.astype(v.dtype), v], axis=1)
        qpos = jnp.broadcast_to(past_len + jnp.arange(t, dtype=jnp.int32), (n, t))
        kpos = jnp.broadcast_to(past_len - WINDOW + jnp.arange(WINDOW + t, dtype=jnp.int32), (n, WINDOW + t))
        attn = _sink_window_attention(q, k_all, v_all, qpos, kpos, sinks)
    new_k = k_all[:, -WINDOW:]
    new_v = v_all[:, -WINDOW:]
    a, gate = jnp.split(u, 2, axis=-1)
    glu = a * jax.nn.sigmoid(gate)
    xc = jnp.concatenate([conv_past.astype(glu.dtype), glu], axis=1)
    c = jax.nn.silu(_layernorm(_depthwise_causal_conv(xc, conv_w, conv_b), ln_g, ln_b))
    out = jnp.concatenate([attn, c], axis=-1) @ w_out
    return out, new_k, new_v, xc[:, -(CONV_WIDTH - 1):]


def _pool_mixer(h, pool_w, pool_scale, pool_past, start_pos):
    n, t, d = h.shape
    if pool_past is None:
        pool_past = jnp.zeros((n, POOL_STATE, d), h.dtype)
    xp = jnp.concatenate([pool_past.astype(h.dtype), h], axis=1)
    cs = jnp.cumsum(xp.astype(jnp.float32), axis=1)
    cs = jnp.concatenate([jnp.zeros((n, 1, d), jnp.float32), cs], axis=1)
    end = cs[:, POOL_STATE + 1:]
    pos = start_pos + jnp.arange(t, dtype=jnp.int32)
    hf = h.astype(jnp.float32)
    groups = []
    for gi, w in enumerate(POOL_WINDOWS):
        sl = slice(gi * POOL_GROUP, (gi + 1) * POOL_GROUP)
        start = cs[:, POOL_STATE + 1 - w: POOL_STATE + 1 - w + t, sl]
        count = jnp.minimum(pos + 1, w).astype(jnp.float32)[:, None]
        groups.append((end[..., sl] - start) / count - hf[..., sl])
    dpool = jnp.stack(groups, axis=2).astype(h.dtype)
    y = jnp.einsum('ntgc,gce->ntge', dpool, pool_w).reshape(n, t, d)
    return y * pool_scale.astype(y.dtype), xp[:, -POOL_STATE:]


def setup_inputs(seed: int = 0) -> dict:
    key = jax.random.key(seed)
    ks = jax.random.split(key, 22)
    nrm = lambda k, shape, s: jax.random.normal(k, shape, jnp.float32) * s
    return {
        'x_prompt': nrm(ks[0], (BATCH, SEQ, D_MODEL), 1.0),
        'x_sample': nrm(ks[1], (DEC_BATCH, DEC_SEQ, D_MODEL), 1.0),
        'cache_k': nrm(ks[2], (N_EVEN, DEC_BATCH, WINDOW, N_KV_HEADS, HEAD_DIM), 1.0),
        'cache_v': nrm(ks[3], (N_EVEN, DEC_BATCH, WINDOW, N_KV_HEADS, HEAD_DIM), 1.0),
        'state_conv': nrm(ks[4], (N_EVEN, DEC_BATCH, CONV_WIDTH - 1, CONV_CH), 0.5),
        'state_pool': nrm(ks[5], (N_ODD, DEC_BATCH, POOL_STATE, D_MODEL), 1.0),
        'norm_mix': 1.0 + nrm(ks[6], (DEPTH, D_MODEL), 0.1),
        'w_in': nrm(ks[7], (N_EVEN, D_MODEL, IN_COLS), D_MODEL ** -0.5),
        'q_norm': 1.0 + nrm(ks[8], (N_EVEN, HEAD_DIM), 0.1),
        'k_norm': 1.0 + nrm(ks[9], (N_EVEN, HEAD_DIM), 0.1),
        'sinks': nrm(ks[10], (N_EVEN, N_HEADS), 0.5),
        'conv_w': nrm(ks[11], (N_EVEN, CONV_WIDTH, CONV_CH), CONV_WIDTH ** -0.5),
        'conv_b': nrm(ks[12], (N_EVEN, CONV_CH), 0.02),
        'conv_ln_g': 1.0 + nrm(ks[13], (N_EVEN, CONV_CH), 0.1),
        'conv_ln_b': nrm(ks[14], (N_EVEN, CONV_CH), 0.02),
        'w_out': nrm(ks[15], (N_EVEN, MIX_OUT, D_MODEL), MIX_OUT ** -0.5),
        'pool_w': nrm(ks[16], (N_ODD, N_POOL_GROUPS, POOL_GROUP, POOL_GROUP), POOL_GROUP ** -0.5),
        'pool_scale': 1.0 + nrm(ks[17], (N_ODD, D_MODEL), 0.1),
        'norm_ffn': 1.0 + nrm(ks[18], (DEPTH, D_MODEL), 0.1),
        'w_gate': nrm(ks[19], (DEPTH, D_MODEL, D_FF), D_MODEL ** -0.5),
        'w_up': nrm(ks[20], (DEPTH, D_MODEL, D_FF), D_MODEL ** -0.5),
        'w_down': nrm(ks[21], (DEPTH, D_FF, D_MODEL), D_FF ** -0.5),
    }


def reference(x_prompt, x_sample, cache_k, cache_v, state_conv, state_pool, norm_mix, w_in, q_norm, k_norm,
              sinks, conv_w, conv_b, conv_ln_g, conv_ln_b, w_out, pool_w, pool_scale, norm_ffn, w_gate, w_up, w_down):
    xp, xs = x_prompt, x_sample
    kp_l, vp_l, cp_l, pp_l = [], [], [], []
    ks_l, vs_l, cs_l, ps_l = [], [], [], []
    for layer in range(DEPTH):
        i = layer // 2
        hp = _rmsnorm(xp, norm_mix[layer])
        hs = _rmsnorm(xs, norm_mix[layer])
        if layer % 2 == 0:
            params = (w_in[i], q_norm[i], k_norm[i], sinks[i], conv_w[i], conv_b[i], conv_ln_g[i], conv_ln_b[i], w_out[i])
            op, kp, vp, cp = _even_mixer(hp, *params, None, None, None, 0)
            osm, kq, vq, cq = _even_mixer(hs, *params, cache_k[i], cache_v[i], state_conv[i], PAST_LEN)
            kp_l.append(kp); vp_l.append(vp); cp_l.append(cp)
            ks_l.append(kq); vs_l.append(vq); cs_l.append(cq)
        else:
            op, pp = _pool_mixer(hp, pool_w[i], pool_scale[i], None, 0)
            osm, pq = _pool_mixer(hs, pool_w[i], pool_scale[i], state_pool[i], PAST_LEN)
            pp_l.append(pp); ps_l.append(pq)
        xp = xp + op
        xs = xs + osm
        xp = xp + _swiglu(_rmsnorm(xp, norm_ffn[layer]), w_gate[layer], w_up[layer], w_down[layer])
        xs = xs + _swiglu(_rmsnorm(xs, norm_ffn[layer]), w_gate[layer], w_up[layer], w_down[layer])
    new_k_prompt = jnp.stack(kp_l)
    new_v_prompt = jnp.stack(vp_l)
    new_conv_prompt = jnp.stack(cp_l)
    new_pool_prompt = jnp.stack(pp_l)
    new_k_sample = jnp.stack(ks_l)
    new_v_sample = jnp.stack(vs_l)
    new_conv_sample = jnp.stack(cs_l)
    new_pool_sample = jnp.stack(ps_l)
    return (xp, xs, new_k_prompt, new_v_prompt, new_conv_prompt, new_pool_prompt,
            new_k_sample, new_v_sample, new_conv_sample, new_pool_sample)
```

```python
import functools

import jax
import jax.numpy as jnp
from jax import lax
from jax.experimental import pallas as pl
from jax.experimental.pallas import tpu as pltpu

F32 = jnp.float32
BF16 = jnp.bfloat16

HEAD_DIM = 64
WINDOW = 128
PAST_LEN = 8192
POOL_WINDOWS = (2, 4, 8, 16)
RMS_EPS = 1e-6
LN_EPS = 1e-5
NEG = -1e30

LANES = 128
SUBLANES = 8
MXU_DIM = 256
VMEM_LIMIT = 56 * 1024 * 1024


def _params(*semantics):
    return pltpu.CompilerParams(dimension_semantics=semantics, vmem_limit_bytes=VMEM_LIMIT)


def _rms(x, g):
    ms = jnp.mean(x * x, axis=-1, keepdims=True)
    return x * lax.rsqrt(ms + RMS_EPS) * g


def _resident(shape):
    zeros = (0,) * len(shape)
    return pl.BlockSpec(shape, lambda *_: zeros, pipeline_mode=pl.Buffered(1))


def _in_proj_kernel(x_ref, g_ref, w_ref, qg_ref, kg_ref, q_ref, k_ref, v_ref, glu_ref, *, attn_w, kv_w, conv_ch):
    h = _rms(x_ref[...], g_ref[...]).astype(BF16)
    row = lax.broadcasted_iota(jnp.int32, (MXU_DIM, MXU_DIM), 0) // HEAD_DIM
    col = lax.broadcasted_iota(jnp.int32, (MXU_DIM, MXU_DIM), 1) // HEAD_DIM
    pmat = jnp.where(row == col, 1.0 / HEAD_DIM, 0.0).astype(BF16)

    def proj(c0):
        return jnp.dot(h, w_ref[:, c0:c0 + MXU_DIM], preferred_element_type=F32)

    def head_rms(z):
        sq = z * z
        hi = sq.astype(BF16)
        lo = (sq - hi.astype(F32)).astype(BF16)
        ms = jnp.dot(hi, pmat, preferred_element_type=F32) + jnp.dot(lo, pmat, preferred_element_type=F32)
        return z * lax.rsqrt(ms + RMS_EPS)

    scale = HEAD_DIM ** -0.5
    for c in range(attn_w // MXU_DIM):
        c0 = c * MXU_DIM
        q_ref[:, c0:c0 + MXU_DIM] = (head_rms(proj(c0)) * qg_ref[...] * scale).astype(q_ref.dtype)
    for c in range(kv_w // MXU_DIM):
        c0 = c * MXU_DIM
        k_ref[:, c0:c0 + MXU_DIM] = head_rms(proj(attn_w + c0)) * kg_ref[...]
        v_ref[:, c0:c0 + MXU_DIM] = proj(attn_w + kv_w + c0)
    for c in range(conv_ch // MXU_DIM):
        c0 = c * MXU_DIM
        a = proj(attn_w + 2 * kv_w + c0)
        gate = proj(attn_w + 2 * kv_w + conv_ch + c0)
        glu_ref[:, c0:c0 + MXU_DIM] = a * jax.nn.sigmoid(gate)


def _in_proj(x, g, w, qg, kg, *, attn_w, kv_w, conv_ch, tm=512):
    n, d = x.shape
    kern = functools.partial(_in_proj_kernel, attn_w=attn_w, kv_w=kv_w, conv_ch=conv_ch)
    rows = lambda width: pl.BlockSpec((tm, width), lambda i: (i, 0))
    return pl.pallas_call(
        kern,
        grid=(n // tm,),
        in_specs=[rows(d), _resident((1, d)), _resident(w.shape), _resident((1, MXU_DIM)), _resident((1, MXU_DIM))],
        out_specs=[rows(attn_w), rows(kv_w), rows(kv_w), rows(conv_ch)],
        out_shape=[
            jax.ShapeDtypeStruct((n, attn_w), BF16),
            jax.ShapeDtypeStruct((n, kv_w), F32),
            jax.ShapeDtypeStruct((n, kv_w), F32),
            jax.ShapeDtypeStruct((n, conv_ch), F32),
        ],
        compiler_params=_params("parallel"),
        name="in_proj",
    )(x, g, w, qg, kg)


def _pair_blocks(chunk, rolled, odd, pad_rows):
    lane = lax.broadcasted_iota(jnp.int32, chunk.shape, 1)
    low = lane < HEAD_DIM
    if odd:
        top = jnp.where(low, rolled, 0.0)
        bot = jnp.where(low, 0.0, chunk)
    else:
        top = jnp.where(low, chunk, 0.0)
        bot = jnp.where(low, 0.0, rolled)
    parts = [top, bot]
    if pad_rows:
        z = jnp.zeros((pad_rows, LANES), F32)
        parts = [top, z, bot, z]
    return jnp.concatenate(parts, axis=0).astype(BF16)


def _softmax_pv(s, ok, sink_a, sink_b, vb):
    half = s.shape[1] // 2
    s = jnp.where(ok, s, NEG)
    sa, sb = s[:, :half], s[:, half:]
    ma = jnp.maximum(jnp.max(sa, axis=-1, keepdims=True), sink_a)
    mb = jnp.maximum(jnp.max(sb, axis=-1, keepdims=True), sink_b)
    pa = jnp.exp(sa - ma)
    pb = jnp.exp(sb - mb)
    da = jnp.sum(pa, axis=-1, keepdims=True) + jnp.exp(sink_a - ma)
    db = jnp.sum(pb, axis=-1, keepdims=True) + jnp.exp(sink_b - mb)
    p = jnp.concatenate([pa, pb], axis=1).astype(BF16)
    o = jnp.dot(p, vb, preferred_element_type=F32)
    lane = lax.broadcasted_iota(jnp.int32, o.shape, 1)
    return o / jnp.where(lane < HEAD_DIM, da, db)


def _attend(q2, kb, vb, ok, sinks_ref, h0, rows):
    s = lax.dot_general(q2, kb, (((1,), (1,)), ((), ())), preferred_element_type=F32)
    row = lax.broadcasted_iota(jnp.int32, (2 * rows, 1), 0)
    first = row < rows
    sink_a = jnp.where(first, sinks_ref[h0], sinks_ref[h0 + 2])
    sink_b = jnp.where(first, sinks_ref[h0 + 1], sinks_ref[h0 + 3])
    return _softmax_pv(s, ok, sink_a, sink_b, vb)


def _attn_prompt_kernel(sinks_ref, q_ref, kp_ref, kc_ref, vp_ref, vc_ref, o_ref, *, n_kv):
    blk = q_ref.shape[0]
    i = pl.program_id(1)
    shape = (2 * blk, 4 * blk)
    r = lax.broadcasted_iota(jnp.int32, shape, 0) % blk
    c = lax.broadcasted_iota(jnp.int32, shape, 1) % (2 * blk)
    lo = jnp.where(i == 0, blk - 1, -1)
    ok = jnp.logical_and(c > jnp.maximum(r, lo), c <= r + blk)
    k_all = jnp.concatenate([kp_ref[...], kc_ref[...]], axis=0)
    v_all = jnp.concatenate([vp_ref[...], vc_ref[...]], axis=0)
    group = (q_ref.shape[1] // HEAD_DIM) // n_kv
    for cch in range(n_kv // 2):
        kc = k_all[:, cch * LANES:(cch + 1) * LANES]
        vc = v_all[:, cch * LANES:(cch + 1) * LANES]
        kr = pltpu.roll(kc, HEAD_DIM, axis=1)
        vr = pltpu.roll(vc, HEAD_DIM, axis=1)
        for odd in range(2):
            j = 2 * cch + odd
            kb = _pair_blocks(kc, kr, odd, 0)
            vb = _pair_blocks(vc, vr, odd, 0)
            h0 = group * j
            c0 = h0 * HEAD_DIM
            q2 = jnp.concatenate([q_ref[:, c0:c0 + LANES], q_ref[:, c0 + LANES:c0 + 2 * LANES]], axis=0)
            o = _attend(q2, kb, vb, ok, sinks_ref, h0, blk).astype(o_ref.dtype)
            o_ref[:, c0:c0 + LANES] = o[:blk]
            o_ref[:, c0 + LANES:c0 + 2 * LANES] = o[blk:]


def _attn_prompt(sinks, q, k, v, *, batch, seq, n_kv):
    blk = WINDOW
    nb = seq // blk
    aw, kw = q.shape[1], k.shape[1]
    cur = lambda w: pl.BlockSpec((blk, w), lambda b, i: (b * nb + i, 0))
    prev = lambda w: pl.BlockSpec((blk, w), lambda b, i: (b * nb + jnp.maximum(i - 1, 0), 0))
    return pl.pallas_call(
        functools.partial(_attn_prompt_kernel, n_kv=n_kv),
        grid=(batch, nb),
        in_specs=[pl.BlockSpec(memory_space=pltpu.SMEM), cur(aw), prev(kw), cur(kw), prev(kw), cur(kw)],
        out_specs=cur(aw),
        out_shape=jax.ShapeDtypeStruct((batch * seq, aw), BF16),
        compiler_params=_params("parallel", "arbitrary"),
        name="attn_prompt",
    )(sinks, q, k, k, v, v)


def _attn_sample_kernel(sinks_ref, q_ref, kn_ref, vn_ref, ck_ref, cv_ref, o_ref, nk_ref, nv_ref, qf_ref, of_ref, *, n_kv, t_new):
    g_seqs = ck_ref.shape[0]
    qf_ref[...] = q_ref[...].astype(F32)
    keep = WINDOW - t_new
    pad = WINDOW - t_new
    shape = (2 * t_new, 4 * WINDOW)
    t = lax.broadcasted_iota(jnp.int32, shape, 0) % t_new
    c = lax.broadcasted_iota(jnp.int32, shape, 1) % (2 * WINDOW)
    ok = jnp.logical_and(c > t, c <= t + WINDOW)
    group = (q_ref.shape[1] // HEAD_DIM) // n_kv

    def one_seq(g, carry):
        r0 = pl.multiple_of(g * t_new, t_new)
        k_new = kn_ref[pl.ds(r0, t_new), :]
        v_new = vn_ref[pl.ds(r0, t_new), :]
        k_old = ck_ref[g]
        v_old = cv_ref[g]
        nk_ref[g, :keep, :] = k_old[t_new:]
        nk_ref[g, keep:, :] = k_new
        nv_ref[g, :keep, :] = v_old[t_new:]
        nv_ref[g, keep:, :] = v_new
        k_all = jnp.concatenate([k_old, k_new], axis=0)
        v_all = jnp.concatenate([v_old, v_new], axis=0)
        for cch in range(n_kv // 2):
            kc = k_all[:, cch * LANES:(cch + 1) * LANES]
            vc = v_all[:, cch * LANES:(cch + 1) * LANES]
            kr = pltpu.roll(kc, HEAD_DIM, axis=1)
            vr = pltpu.roll(vc, HEAD_DIM, axis=1)
            for odd in range(2):
                j = 2 * cch + odd
                kb = _pair_blocks(kc, kr, odd, pad)
                vb = _pair_blocks(vc, vr, odd, pad)
                h0 = group * j
                c0 = h0 * HEAD_DIM
                q2 = jnp.concatenate(
                    [qf_ref[pl.ds(r0, t_new), c0:c0 + LANES], qf_ref[pl.ds(r0, t_new), c0 + LANES:c0 + 2 * LANES]], axis=0)
                o = _attend(q2.astype(BF16), kb, vb, ok, sinks_ref, h0, t_new)
                of_ref[pl.ds(r0, t_new), c0:c0 + LANES] = o[:t_new]
                of_ref[pl.ds(r0, t_new), c0 + LANES:c0 + 2 * LANES] = o[t_new:]
        return carry

    lax.fori_loop(0, g_seqs, one_seq, 0)
    o_ref[...] = of_ref[...].astype(o_ref.dtype)


def _attn_sample(sinks, q, k, v, cache_k, cache_v, *, row0, t_new, n_kv, g_seqs=16):
    n_seq = cache_k.shape[0]
    aw, kw = q.shape[1], k.shape[1]
    tok0 = row0 // (g_seqs * t_new)
    tok = lambda w: pl.BlockSpec((g_seqs * t_new, w), lambda i: (tok0 + i, 0))
    cache = pl.BlockSpec((g_seqs, WINDOW, kw), lambda i: (i, 0, 0))
    return pl.pallas_call(
        functools.partial(_attn_sample_kernel, n_kv=n_kv, t_new=t_new),
        grid=(n_seq // g_seqs,),
        in_specs=[pl.BlockSpec(memory_space=pltpu.SMEM), tok(aw), tok(kw), tok(kw), cache, cache],
        out_specs=[pl.BlockSpec((g_seqs * t_new, aw), lambda i: (i, 0)), cache, cache],
        out_shape=[
            jax.ShapeDtypeStruct((n_seq * t_new, aw), BF16),
            jax.ShapeDtypeStruct(cache_k.shape, F32),
            jax.ShapeDtypeStruct(cache_v.shape, F32),
        ],
        scratch_shapes=[pltpu.VMEM((g_seqs * t_new, aw), F32), pltpu.VMEM((g_seqs * t_new, aw), F32)],
        compiler_params=_params("parallel"),
        name="attn_sample",
    )(sinks, q, k, v, cache_k, cache_v)


def _ln_silu(y, g, b):
    yc = y - jnp.mean(y, axis=-1, keepdims=True)
    var = jnp.mean(yc * yc, axis=-1, keepdims=True)
    return jax.nn.silu(yc * lax.rsqrt(var + LN_EPS) * g + b)


def _conv_prompt_kernel(cur_ref, halo_ref, w_ref, b_ref, g_ref, beta_ref, o_ref, xc_ref, y_ref, *, width, rows_per, lanes_per):
    tc, ch = cur_ref.shape
    halo = halo_ref.shape[0]
    i = pl.program_id(1)
    xc_ref[:halo, :] = jnp.where(i == 0, 0.0, halo_ref[...])
    xc_ref[halo:, :] = cur_ref[...]
    base = halo - (width - 1)

    def col_chunk(cc, carry):
        l0 = pl.multiple_of(cc * lanes_per, lanes_per)
        cols = pl.ds(l0, lanes_per)
        for rr in range(tc // rows_per):
            acc = jnp.broadcast_to(b_ref[:, cols], (rows_per, lanes_per))
            for phase in range(SUBLANES):
                taps = [j for j in range(width) if (base + j) % SUBLANES == phase]
                if not taps:
                    continue
                m_lo = (base + taps[0]) // SUBLANES
                m_hi = (base + taps[-1]) // SUBLANES
                start = rr * rows_per + m_lo * SUBLANES + phase
                win = xc_ref[pl.ds(start, rows_per + (m_hi - m_lo) * SUBLANES), cols]
                for j in taps:
                    off = ((base + j) // SUBLANES - m_lo) * SUBLANES
                    acc = acc + w_ref[pl.ds(j, 1), cols] * win[off:off + rows_per]
            y_ref[pl.ds(rr * rows_per, rows_per), cols] = acc
        return carry

    lax.fori_loop(0, ch // lanes_per, col_chunk, 0)
    o_ref[...] = _ln_silu(y_ref[...], g_ref[...], beta_ref[...]).astype(o_ref.dtype)


def _conv_prompt(glu, w, b, g, beta, *, batch, seq, tc=256, halo=32):
    ch = glu.shape[1]
    width = w.shape[0]
    nt = seq // tc
    per = tc // halo
    cur = pl.BlockSpec((tc, ch), lambda bi, i: (bi * nt + i, 0))
    prev = pl.BlockSpec((halo, ch), lambda bi, i: (jnp.maximum((bi * nt + i) * per - 1, 0), 0))
    kern = functools.partial(_conv_prompt_kernel, width=width, rows_per=64, lanes_per=LANES)
    return pl.pallas_call(
        kern,
        grid=(batch, nt),
        in_specs=[cur, prev, _resident(w.shape), _resident((1, ch)), _resident((1, ch)), _resident((1, ch))],
        out_specs=cur,
        out_shape=jax.ShapeDtypeStruct((batch * seq, ch), BF16),
        scratch_shapes=[pltpu.VMEM((halo + tc, ch), F32), pltpu.VMEM((tc, ch), F32)],
        compiler_params=_params("parallel", "arbitrary"),
        name="conv_prompt",
    )(glu, glu, w, b, g, beta)


def _append_state(ns_ref, st_ref, g, new, t_new):
    n_past = st_ref.shape[1]
    keep = n_past - t_new
    full = (keep // SUBLANES) * SUBLANES
    rem = keep - full
    if full:
        ns_ref[g, :full, :] = st_ref[g, t_new:t_new + full, :]
    if rem:
        rolled = pltpu.roll(new, rem, axis=0)
        ns_ref[g, full:full + SUBLANES, :] = rolled
        ns_ref[g, full:keep, :] = st_ref[g, t_new + full:n_past, :]
        ns_ref[g, full + SUBLANES:, :] = rolled[:rem]
    else:
        ns_ref[g, full:, :] = new


def _conv_sample_kernel(new_ref, st_ref, toep_ref, b_ref, g_ref, beta_ref, o_ref, ns_ref, of_ref, *, t_new):
    g_seqs, n_past, ch = st_ref.shape

    def one_seq(g, carry):
        r0 = pl.multiple_of(g * t_new, t_new)
        new = new_ref[pl.ds(r0, t_new), :]
        acc = jnp.broadcast_to(b_ref[...], (t_new, ch))
        for r in range(n_past):
            acc = acc + toep_ref[r] * st_ref[g, pl.ds(r, 1), :]
        for r in range(t_new):
            acc = acc + toep_ref[n_past + r] * new_ref[pl.ds(r0 + r, 1), :]
        of_ref[pl.ds(r0, t_new), :] = _ln_silu(acc, g_ref[...], beta_ref[...])
        _append_state(ns_ref, st_ref, g, new, t_new)
        return carry

    lax.fori_loop(0, g_seqs, one_seq, 0)
    o_ref[...] = of_ref[...].astype(o_ref.dtype)


def _conv_sample(glu, state, toep, b, g, beta, *, row0, t_new, g_seqs=16):
    n_seq, n_past, ch = state.shape
    tok0 = row0 // (g_seqs * t_new)
    return pl.pallas_call(
        functools.partial(_conv_sample_kernel, t_new=t_new),
        grid=(n_seq // g_seqs,),
        in_specs=[
            pl.BlockSpec((g_seqs * t_new, ch), lambda i: (tok0 + i, 0)),
            pl.BlockSpec((g_seqs, n_past, ch), lambda i: (i, 0, 0)),
            _resident(toep.shape), _resident((1, ch)), _resident((1, ch)), _resident((1, ch)),
        ],
        out_specs=[
            pl.BlockSpec((g_seqs * t_new, ch), lambda i: (i, 0)),
            pl.BlockSpec((g_seqs, n_past, ch), lambda i: (i, 0, 0)),
        ],
        out_shape=[jax.ShapeDtypeStruct((n_seq * t_new, ch), BF16), jax.ShapeDtypeStruct(state.shape, F32)],
        scratch_shapes=[pltpu.VMEM((g_seqs * t_new, ch), F32)],
        compiler_params=_params("parallel"),
        name="conv_sample",
    )(glu, state, toep, b, g, beta)


def _out_proj_kernel(x_ref, a_ref, c_ref, wa_ref, wc_ref, o_ref):
    o_ref[...] = (x_ref[...]
                  + jnp.dot(a_ref[...], wa_ref[...], preferred_element_type=F32)
                  + jnp.dot(c_ref[...], wc_ref[...], preferred_element_type=F32))


def _out_proj(x, attn, conv, wa, wc, *, tm=512):
    n, d = x.shape
    rows = lambda width: pl.BlockSpec((tm, width), lambda i: (i, 0))
    return pl.pallas_call(
        _out_proj_kernel,
        grid=(n // tm,),
        in_specs=[rows(d), rows(attn.shape[1]), rows(conv.shape[1]), _resident(wa.shape), _resident(wc.shape)],
        out_specs=rows(d),
        out_shape=jax.ShapeDtypeStruct((n, d), F32),
        compiler_params=_params("parallel"),
        name="out_proj",
    )(x, attn, conv, wa, wc)


def _ffn_kernel(x_ref, g_ref, wg_ref, wu_ref, wd_ref, o_ref, h_ref):
    @pl.when(pl.program_id(1) == 0)
    def _():
        x = x_ref[...]
        h_ref[...] = _rms(x, g_ref[...]).astype(BF16)
        o_ref[...] = x

    h = h_ref[...]
    a = jnp.dot(h, wg_ref[...], preferred_element_type=F32)
    u = jnp.dot(h, wu_ref[...], preferred_element_type=F32)
    act = (jax.nn.silu(a) * u).astype(BF16)
    o_ref[...] += jnp.dot(act, wd_ref[...], preferred_element_type=F32)


def _ffn(x, g, wg, wu, wd, *, tm=512, tf=512):
    n, d = x.shape
    dff = wg.shape[1]
    return pl.pallas_call(
        _ffn_kernel,
        grid=(n // tm, dff // tf),
        in_specs=[
            pl.BlockSpec((tm, d), lambda i, f: (i, 0)),
            _resident((1, d)),
            pl.BlockSpec((d, tf), lambda i, f: (0, f)),
            pl.BlockSpec((d, tf), lambda i, f: (0, f)),
            pl.BlockSpec((tf, d), lambda i, f: (f, 0)),
        ],
        out_specs=pl.BlockSpec((tm, d), lambda i, f: (i, 0)),
        out_shape=jax.ShapeDtypeStruct((n, d), F32),
        scratch_shapes=[pltpu.VMEM((tm, d), BF16)],
        compiler_params=_params("parallel", "arbitrary"),
        name="ffn",
    )(x, g, wg, wu, wd)


def _pool_prompt_kernel(cur_ref, halo_ref, g_ref, pw_ref, ps_ref, o_ref, tail_ref, xp_ref, *, rows_per):
    tp, d = cur_ref.shape
    halo = halo_ref.shape[0]
    i = pl.program_id(1)
    gain = g_ref[...]
    xp_ref[:halo, :] = jnp.where(i == 0, 0.0, _rms(halo_ref[...], gain))
    xp_ref[halo:, :] = _rms(cur_ref[...], gain)
    tail_ref[0] = xp_ref[tp:, :]
    pg = d // len(POOL_WINDOWS)
    for rr in range(tp // rows_per):
        r0 = halo + rr * rows_per
        pos = i * tp + rr * rows_per + lax.broadcasted_iota(jnp.int32, (rows_per, 1), 0)
        for gi, w in enumerate(POOL_WINDOWS):
            cols = slice(gi * pg, (gi + 1) * pg)
            h = xp_ref[r0:r0 + rows_per, cols]
            tot = h
            for back in range(1, w):
                tot = tot + xp_ref[r0 - back:r0 - back + rows_per, cols]
            count = jnp.minimum(pos + 1, w).astype(F32)
            dpool = (tot / count - h).astype(BF16)
            y = jnp.dot(dpool, pw_ref[gi], preferred_element_type=F32)
            rows = slice(rr * rows_per, (rr + 1) * rows_per)
            o_ref[rows, cols] = cur_ref[rows, cols] + y * ps_ref[:, cols]


def _pool_prompt(x, g, pw, ps, *, batch, seq, tp=512, halo=16):
    d = x.shape[1]
    nt = seq // tp
    per = tp // halo
    cur = pl.BlockSpec((tp, d), lambda bi, i: (bi * nt + i, 0))
    prev = pl.BlockSpec((halo, d), lambda bi, i: (jnp.maximum((bi * nt + i) * per - 1, 0), 0))
    return pl.pallas_call(
        functools.partial(_pool_prompt_kernel, rows_per=128),
        grid=(batch, nt),
        in_specs=[cur, prev, _resident((1, d)), _resident(pw.shape), _resident((1, d))],
        out_specs=[cur, pl.BlockSpec((1, halo, d), lambda bi, i: (bi, 0, 0))],
        out_shape=[jax.ShapeDtypeStruct((batch * seq, d), F32), jax.ShapeDtypeStruct((batch, halo, d), F32)],
        scratch_shapes=[pltpu.VMEM((halo + tp, d), F32)],
        compiler_params=_params("parallel", "arbitrary"),
        name="pool_prompt",
    )(x, x, g, pw, ps)


def _pool_sample_kernel(x_ref, st_ref, g_ref, pw_ref, ps_ref, o_ref, ns_ref, h_ref, dp_ref, *, t_new, start_pos):
    g_seqs, n_past, d = st_ref.shape
    pg = d // len(POOL_WINDOWS)
    h_ref[...] = _rms(x_ref[...], g_ref[...])
    t = lax.broadcasted_iota(jnp.int32, (t_new, 1), 0)

    def one_seq(g, carry):
        r0 = pl.multiple_of(g * t_new, t_new)
        h = h_ref[pl.ds(r0, t_new), :]
        for gi, w in enumerate(POOL_WINDOWS):
            cols = slice(gi * pg, (gi + 1) * pg)
            inv = 1.0 / jnp.minimum(start_pos + t + 1, w).astype(F32)
            tot = jnp.zeros((t_new, pg), F32)
            for r in range(max(0, n_past + 1 - w), n_past + t_new):
                if r < n_past:
                    row = st_ref[g, pl.ds(r, 1), cols]
                else:
                    row = h_ref[pl.ds(r0 + (r - n_past), 1), cols]
                hit = jnp.logical_and(t + n_past - w < r, r <= t + n_past)
                tot = tot + jnp.where(hit, inv, 0.0) * row
            dp_ref[pl.ds(r0, t_new), cols] = tot - h[:, cols]
        _append_state(ns_ref, st_ref, g, h, t_new)
        return carry

    lax.fori_loop(0, g_seqs, one_seq, 0)
    for gi in range(len(POOL_WINDOWS)):
        cols = slice(gi * pg, (gi + 1) * pg)
        y = jnp.dot(dp_ref[:, cols].astype(BF16), pw_ref[gi], preferred_element_type=F32)
        o_ref[:, cols] = x_ref[:, cols] + y * ps_ref[:, cols]


def _pool_sample(x, state, g, pw, ps, *, row0, t_new, g_seqs=16):
    n_seq, n_past, d = state.shape
    assert t_new == SUBLANES and SUBLANES <= n_past < 2 * SUBLANES
    tok0 = row0 // (g_seqs * t_new)
    kern = functools.partial(_pool_sample_kernel, t_new=t_new, start_pos=PAST_LEN)
    return pl.pallas_call(
        kern,
        grid=(n_seq // g_seqs,),
        in_specs=[
            pl.BlockSpec((g_seqs * t_new, d), lambda i: (tok0 + i, 0)),
            pl.BlockSpec((g_seqs, n_past, d), lambda i: (i, 0, 0)),
            _resident((1, d)), _resident(pw.shape), _resident((1, d)),
        ],
        out_specs=[
            pl.BlockSpec((g_seqs * t_new, d), lambda i: (i, 0)),
            pl.BlockSpec((g_seqs, n_past, d), lambda i: (i, 0, 0)),
        ],
        out_shape=[jax.ShapeDtypeStruct((n_seq * t_new, d), F32), jax.ShapeDtypeStruct(state.shape, F32)],
        scratch_shapes=[pltpu.VMEM((g_seqs * t_new, d), F32), pltpu.VMEM((g_seqs * t_new, d), F32)],
        compiler_params=_params("parallel"),
        name="pool_sample",
    )(x, state, g, pw, ps)


def _toeplitz(conv_w, t_new):
    width = conv_w.shape[0]
    idx = jnp.arange(width - 1 + t_new)[:, None] - jnp.arange(t_new)[None, :]
    valid = jnp.logical_and(idx >= 0, idx < width)
    return jnp.where(valid[:, :, None], conv_w[jnp.clip(idx, 0, width - 1)], 0.0)


def kernel(x_prompt, x_sample, cache_k, cache_v, state_conv, state_pool, norm_mix, w_in, q_norm, k_norm, sinks,
           conv_w, conv_b, conv_ln_g, conv_ln_b, w_out, pool_w, pool_scale, norm_ffn, w_gate, w_up, w_down):
    batch, seq, d = x_prompt.shape
    n_seq, t_new, _ = x_sample.shape
    n_kv = cache_k.shape[3]
    kv_w = n_kv * HEAD_DIM
    conv_ch = state_conv.shape[-1]
    attn_w = w_in.shape[-1] - 2 * kv_w - 2 * conv_ch
    n_prompt = batch * seq
    row = lambda a: a.reshape(1, -1)

    x = jnp.concatenate([x_prompt.reshape(n_prompt, d), x_sample.reshape(n_seq * t_new, d)], axis=0)
    depth = norm_mix.shape[0]
    outs = {}
    for layer in range(depth):
        i = layer // 2
        if layer % 2 == 0:
            reps = MXU_DIM // HEAD_DIM
            q, k, v, glu = _in_proj(
                x, row(norm_mix[layer]), w_in[i].astype(BF16), row(jnp.tile(q_norm[i], reps)), row(jnp.tile(k_norm[i], reps)),
                attn_w=attn_w, kv_w=kv_w, conv_ch=conv_ch)
            attn_p = _attn_prompt(sinks[i], q, k, v, batch=batch, seq=seq, n_kv=n_kv)
            attn_s, nk_s, nv_s = _attn_sample(
                sinks[i], q, k, v, cache_k[i].reshape(n_seq, WINDOW, kv_w), cache_v[i].reshape(n_seq, WINDOW, kv_w),
                row0=n_prompt, t_new=t_new, n_kv=n_kv)
            ln = (row(conv_b[i]), row(conv_ln_g[i]), row(conv_ln_b[i]))
            conv_p = _conv_prompt(glu, conv_w[i], *ln, batch=batch, seq=seq)
            conv_s, nc_s = _conv_sample(glu, state_conv[i], _toeplitz(conv_w[i], t_new), *ln, row0=n_prompt, t_new=t_new)
            wo = w_out[i].astype(BF16)
            x = _out_proj(x, jnp.concatenate([attn_p, attn_s], axis=0), jnp.concatenate([conv_p, conv_s], axis=0),
                          wo[:attn_w], wo[attn_w:])
            last = lambda a: a[:n_prompt].reshape(batch, seq, -1)
            outs.setdefault("kp", []).append(last(k)[:, -WINDOW:].reshape(batch, WINDOW, n_kv, HEAD_DIM))
            outs.setdefault("vp", []).append(last(v)[:, -WINDOW:].reshape(batch, WINDOW, n_kv, HEAD_DIM))
            outs.setdefault("cp", []).append(last(glu)[:, -(conv_w.shape[1] - 1):])
            outs.setdefault("ks", []).append(nk_s.reshape(n_seq, WINDOW, n_kv, HEAD_DIM))
            outs.setdefault("vs", []).append(nv_s.reshape(n_seq, WINDOW, n_kv, HEAD_DIM))
            outs.setdefault("cs", []).append(nc_s)
        else:
            pw = pool_w[i].astype(BF16)
            args = (row(norm_mix[layer]), pw, row(pool_scale[i]))
            n_past = state_pool.shape[2]
            xp, tail = _pool_prompt(x, *args, batch=batch, seq=seq)
            xs, np_s = _pool_sample(x, state_pool[i], *args, row0=n_prompt, t_new=t_new)
            x = jnp.concatenate([xp, xs], axis=0)
            outs.setdefault("pp", []).append(tail[:, -n_past:])
            outs.setdefault("ps", []).append(np_s)
        x = _ffn(x, row(norm_ffn[layer]), w_gate[layer].astype(BF16), w_up[layer].astype(BF16), w_down[layer].astype(BF16))
    y_prompt = x[:n_prompt].reshape(batch, seq, d)
    y_sample = x[n_prompt:].reshape(n_seq, t_new, d)
    st = lambda name: jnp.stack(outs[name])
    return (y_prompt, y_sample, st("kp"), st("vp"), st("cp"), st("pp"), st("ks"), st("vs"), st("cs"), st("ps"))
```

```python
import functools

import jax
import jax.numpy as jnp
from jax import lax
from jax.experimental import pallas as pl
from jax.experimental.pallas import tpu as pltpu

F32 = jnp.float32
BF16 = jnp.bfloat16

HEAD_DIM = 64
WINDOW = 128
PAST_LEN = 8192
POOL_WINDOWS = (2, 4, 8, 16)
RMS_EPS = 1e-6
LN_EPS = 1e-5
NEG = -1e30

LANES = 128
SUBLANES = 8
MXU_DIM = 256
VMEM_LIMIT = 56 * 1024 * 1024


def _params(*semantics):
    return pltpu.CompilerParams(dimension_semantics=semantics, vmem_limit_bytes=VMEM_LIMIT)


def _rms(x, g):
    ms = jnp.mean(x * x, axis=-1, keepdims=True)
    return x * lax.rsqrt(ms + RMS_EPS) * g


def _resident(shape):
    zeros = (0,) * len(shape)
    return pl.BlockSpec(shape, lambda *_: zeros, pipeline_mode=pl.Buffered(1))


def _in_proj_kernel(x_ref, g_ref, w_ref, qg_ref, kg_ref, q_ref, k_ref, v_ref, glu_ref, *, attn_w, kv_w, conv_ch):
    h = _rms(x_ref[...], g_ref[...]).astype(BF16)
    row = lax.broadcasted_iota(jnp.int32, (MXU_DIM, MXU_DIM), 0) // HEAD_DIM
    col = lax.broadcasted_iota(jnp.int32, (MXU_DIM, MXU_DIM), 1) // HEAD_DIM
    pmat = jnp.where(row == col, 1.0 / HEAD_DIM, 0.0).astype(BF16)

    def proj(c0):
        return jnp.dot(h, w_ref[:, c0:c0 + MXU_DIM], preferred_element_type=F32)

    def head_rms(z):
        sq = z * z
        hi = sq.astype(BF16)
        lo = (sq - hi.astype(F32)).astype(BF16)
        ms = jnp.dot(hi, pmat, preferred_element_type=F32) + jnp.dot(lo, pmat, preferred_element_type=F32)
        return z * lax.rsqrt(ms + RMS_EPS)

    scale = HEAD_DIM ** -0.5
    for c in range(attn_w // MXU_DIM):
        c0 = c * MXU_DIM
        q_ref[:, c0:c0 + MXU_DIM] = (head_rms(proj(c0)) * qg_ref[...] * scale).astype(q_ref.dtype)
    for c in range(kv_w // MXU_DIM):
        c0 = c * MXU_DIM
        k_ref[:, c0:c0 + MXU_DIM] = head_rms(proj(attn_w + c0)) * kg_ref[...]
        v_ref[:, c0:c0 + MXU_DIM] = proj(attn_w + kv_w + c0)
    for c in range(conv_ch // MXU_DIM):
        c0 = c * MXU_DIM
        a = proj(attn_w + 2 * kv_w + c0)
        gate = proj(attn_w + 2 * kv_w + conv_ch + c0)
        glu_ref[:, c0:c0 + MXU_DIM] = a * jax.nn.sigmoid(gate)


def _in_proj(x, g, w, qg, kg, *, attn_w, kv_w, conv_ch, tm=512):
    n, d = x.shape
    kern = functools.partial(_in_proj_kernel, attn_w=attn_w, kv_w=kv_w, conv_ch=conv_ch)
    rows = lambda width: pl.BlockSpec((tm, width), lambda i: (i, 0))
    return pl.pallas_call(
        kern,
        grid=(n // tm,),
        in_specs=[rows(d), _resident((1, d)), _resident(w.shape), _resident((1, MXU_DIM)), _resident((1, MXU_DIM))],
        out_specs=[rows(attn_w), rows(kv_w), rows(kv_w), rows(conv_ch)],
        out_shape=[
            jax.ShapeDtypeStruct((n, attn_w), BF16),
            jax.ShapeDtypeStruct((n, kv_w), F32),
            jax.ShapeDtypeStruct((n, kv_w), F32),
            jax.ShapeDtypeStruct((n, conv_ch), F32),
        ],
        compiler_params=_params("parallel"),
        name="in_proj",
    )(x, g, w, qg, kg)


def _pair_blocks(chunk, rolled, odd, pad_rows):
    lane = lax.broadcasted_iota(jnp.int32, chunk.shape, 1)
    low = lane < HEAD_DIM
    if odd:
        top = jnp.where(low, rolled, 0.0)
        bot = jnp.where(low, 0.0, chunk)
    else:
        top = jnp.where(low, chunk, 0.0)
        bot = jnp.where(low, 0.0, rolled)
    parts = [top, bot]
    if pad_rows:
        z = jnp.zeros((pad_rows, LANES), F32)
        parts = [top, z, bot, z]
    return jnp.concatenate(parts, axis=0).astype(BF16)


def _softmax_pv(s, ok, sink_a, sink_b, vb):
    half = s.shape[1] // 2
    s = jnp.where(ok, s, NEG)
    sa, sb = s[:, :half], s[:, half:]
    ma = jnp.maximum(jnp.max(sa, axis=-1, keepdims=True), sink_a)
    mb = jnp.maximum(jnp.max(sb, axis=-1, keepdims=True), sink_b)
    pa = jnp.exp(sa - ma)
    pb = jnp.exp(sb - mb)
    da = jnp.sum(pa, axis=-1, keepdims=True) + jnp.exp(sink_a - ma)
    db = jnp.sum(pb, axis=-1, keepdims=True) + jnp.exp(sink_b - mb)
    p = jnp.concatenate([pa, pb], axis=1).astype(BF16)
    o = jnp.dot(p, vb, preferred_element_type=F32)
    lane = lax.broadcasted_iota(jnp.int32, o.shape, 1)
    return o / jnp.where(lane < HEAD_DIM, da, db)


def _attend(q2, kb, vb, ok, sinks_ref, h0, rows):
    s = lax.dot_general(q2, kb, (((1,), (1,)), ((), ())), preferred_element_type=F32)
    row = lax.broadcasted_iota(jnp.int32, (2 * rows, 1), 0)
    first = row < rows
    sink_a = jnp.where(first, sinks_ref[h0], sinks_ref[h0 + 2])
    sink_b = jnp.where(first, sinks_ref[h0 + 1], sinks_ref[h0 + 3])
    return _softmax_pv(s, ok, sink_a, sink_b, vb)


def _attn_prompt_kernel(sinks_ref, q_ref, kp_ref, kc_ref, vp_ref, vc_ref, o_ref, *, n_kv):
    blk = q_ref.shape[0]
    i = pl.program_id(1)
    shape = (2 * blk, 4 * blk)
    r = lax.broadcasted_iota(jnp.int32, shape, 0) % blk
    c = lax.broadcasted_iota(jnp.int32, shape, 1) % (2 * blk)
    lo = jnp.where(i == 0, blk - 1, -1)
    ok = jnp.logical_and(c > jnp.maximum(r, lo), c <= r + blk)
    k_all = jnp.concatenate([kp_ref[...], kc_ref[...]], axis=0)
    v_all = jnp.concatenate([vp_ref[...], vc_ref[...]], axis=0)
    group = (q_ref.shape[1] // HEAD_DIM) // n_kv
    for cch in range(n_kv // 2):
        kc = k_all[:, cch * LANES:(cch + 1) * LANES]
        vc = v_all[:, cch * LANES:(cch + 1) * LANES]
        kr = pltpu.roll(kc, HEAD_DIM, axis=1)
        vr = pltpu.roll(vc, HEAD_DIM, axis=1)
        for odd in range(2):
            j = 2 * cch + odd
            kb = _pair_blocks(kc, kr, odd, 0)
            vb = _pair_blocks(vc, vr, odd, 0)
            h0 = group * j
            c0 = h0 * HEAD_DIM
            q2 = jnp.concatenate([q_ref[:, c0:c0 + LANES], q_ref[:, c0 + LANES:c0 + 2 * LANES]], axis=0)
            o = _attend(q2, kb, vb, ok, sinks_ref, h0, blk).astype(o_ref.dtype)
            o_ref[:, c0:c0 + LANES] = o[:blk]
            o_ref[:, c0 + LANES:c0 + 2 * LANES] = o[blk:]


def _attn_prompt(sinks, q, k, v, *, batch, seq, n_kv):
    blk = WINDOW
    nb = seq // blk
    aw, kw = q.shape[1], k.shape[1]
    cur = lambda w: pl.BlockSpec((blk, w), lambda b, i: (b * nb + i, 0))
    prev = lambda w: pl.BlockSpec((blk, w), lambda b, i: (b * nb + jnp.maximum(i - 1, 0), 0))
    return pl.pallas_call(
        functools.partial(_attn_prompt_kernel, n_kv=n_kv),
        grid=(batch, nb),
        in_specs=[pl.BlockSpec(memory_space=pltpu.SMEM), cur(aw), prev(kw), cur(kw), prev(kw), cur(kw)],
        out_specs=cur(aw),
        out_shape=jax.ShapeDtypeStruct((batch * seq, aw), BF16),
        compiler_params=_params("parallel", "arbitrary"),
        name="attn_prompt",
    )(sinks, q, k, k, v, v)


def _attn_sample_kernel(sinks_ref, q_ref, kn_ref, vn_ref, ck_ref, cv_ref, o_ref, nk_ref, nv_ref, qf_ref, of_ref, *, n_kv, t_new):
    g_seqs = ck_ref.shape[0]
    qf_ref[...] = q_ref[...].astype(F32)
    keep = WINDOW - t_new
    pad = WINDOW - t_new
    shape = (2 * t_new, 4 * WINDOW)
    t = lax.broadcasted_iota(jnp.int32, shape, 0) % t_new
    c = lax.broadcasted_iota(jnp.int32, shape, 1) % (2 * WINDOW)
    ok = jnp.logical_and(c > t, c <= t + WINDOW)
    group = (q_ref.shape[1] // HEAD_DIM) // n_kv

    def one_seq(g, carry):
        r0 = pl.multiple_of(g * t_new, t_new)
        k_new = kn_ref[pl.ds(r0, t_new), :]
        v_new = vn_ref[pl.ds(r0, t_new), :]
        k_old = ck_ref[g]
        v_old = cv_ref[g]
        nk_ref[g, :keep, :] = k_old[t_new:]
        nk_ref[g, keep:, :] = k_new
        nv_ref[g, :keep, :] = v_old[t_new:]
        nv_ref[g, keep:, :] = v_new
        k_all = jnp.concatenate([k_old, k_new], axis=0)
        v_all = jnp.concatenate([v_old, v_new], axis=0)
        for cch in range(n_kv // 2):
            kc = k_all[:, cch * LANES:(cch + 1) * LANES]
            vc = v_all[:, cch * LANES:(cch + 1) * LANES]
            kr = pltpu.roll(kc, HEAD_DIM, axis=1)
            vr = pltpu.roll(vc, HEAD_DIM, axis=1)
            for odd in range(2):
                j = 2 * cch + odd
                kb = _pair_blocks(kc, kr, odd, pad)
                vb = _pair_blocks(vc, vr, odd, pad)
                h0 = group * j
                c0 = h0 * HEAD_DIM
                q2 = jnp.concatenate(
                    [qf_ref[pl.ds(r0, t_new), c0:c0 + LANES], qf_ref[pl.ds(r0, t_new), c0 + LANES:c0 + 2 * LANES]], axis=0)
                o = _attend(q2.astype(BF16), kb, vb, ok, sinks_ref, h0, t_new)
                of_ref[pl.ds(r0, t_new), c0:c0 + LANES] = o[:t_new]
                of_ref[pl.ds(r0, t_new), c0 + LANES:c0 + 2 * LANES] = o[t_new:]
        return carry

    lax.fori_loop(0, g_seqs, one_seq, 0)
    o_ref[...] = of_ref[...].astype(o_ref.dtype)


def _attn_sample(sinks, q, k, v, cache_k, cache_v, *, t_new, n_kv, g_seqs=16):
    n_seq = cache_k.shape[0]
    aw, kw = q.shape[1], k.shape[1]
    tok = lambda w: pl.BlockSpec((g_seqs * t_new, w), lambda i: (i, 0))
    cache = pl.BlockSpec((g_seqs, WINDOW, kw), lambda i: (i, 0, 0))
    return pl.pallas_call(
        functools.partial(_attn_sample_kernel, n_kv=n_kv, t_new=t_new),
        grid=(n_seq // g_seqs,),
        in_specs=[pl.BlockSpec(memory_space=pltpu.SMEM), tok(aw), tok(kw), tok(kw), cache, cache],
        out_specs=[tok(aw), cache, cache],
        out_shape=[
            jax.ShapeDtypeStruct((n_seq * t_new, aw), BF16),
            jax.ShapeDtypeStruct(cache_k.shape, F32),
            jax.ShapeDtypeStruct(cache_v.shape, F32),
        ],
        scratch_shapes=[pltpu.VMEM((g_seqs * t_new, aw), F32), pltpu.VMEM((g_seqs * t_new, aw), F32)],
        compiler_params=_params("parallel"),
        name="attn_sample",
    )(sinks, q, k, v, cache_k, cache_v)


def _ln_silu(y, g, b):
    yc = y - jnp.mean(y, axis=-1, keepdims=True)
    var = jnp.mean(yc * yc, axis=-1, keepdims=True)
    return jax.nn.silu(yc * lax.rsqrt(var + LN_EPS) * g + b)


def _conv_prompt_kernel(cur_ref, halo_ref, w_ref, b_ref, g_ref, beta_ref, o_ref, xc_ref, y_ref, *, width, rows_per, lanes_per):
    tc, ch = cur_ref.shape
    halo = halo_ref.shape[0]
    i = pl.program_id(1)
    xc_ref[:halo, :] = jnp.where(i == 0, 0.0, halo_ref[...])
    xc_ref[halo:, :] = cur_ref[...]
    base = halo - (width - 1)

    def col_chunk(cc, carry):
        l0 = pl.multiple_of(cc * lanes_per, lanes_per)
        cols = pl.ds(l0, lanes_per)
        for rr in range(tc // rows_per):
            acc = jnp.broadcast_to(b_ref[:, cols], (rows_per, lanes_per))
            for phase in range(SUBLANES):
                taps = [j for j in range(width) if (base + j) % SUBLANES == phase]
                if not taps:
                    continue
                m_lo = (base + taps[0]) // SUBLANES
                m_hi = (base + taps[-1]) // SUBLANES
                start = rr * rows_per + m_lo * SUBLANES + phase
                win = xc_ref[pl.ds(start, rows_per + (m_hi - m_lo) * SUBLANES), cols]
                for j in taps:
                    off = ((base + j) // SUBLANES - m_lo) * SUBLANES
                    acc = acc + w_ref[pl.ds(j, 1), cols] * win[off:off + rows_per]
            y_ref[pl.ds(rr * rows_per, rows_per), cols] = acc
        return carry

    lax.fori_loop(0, ch // lanes_per, col_chunk, 0)
    o_ref[...] = _ln_silu(y_ref[...], g_ref[...], beta_ref[...]).astype(o_ref.dtype)


def _conv_prompt(glu, w, b, g, beta, *, batch, seq, tc=256, halo=32):
    ch = glu.shape[1]
    width = w.shape[0]
    nt = seq // tc
    per = tc // halo
    cur = pl.BlockSpec((tc, ch), lambda bi, i: (bi * nt + i, 0))
    prev = pl.BlockSpec((halo, ch), lambda bi, i: (jnp.maximum((bi * nt + i) * per - 1, 0), 0))
    kern = functools.partial(_conv_prompt_kernel, width=width, rows_per=64, lanes_per=LANES)
    return pl.pallas_call(
        kern,
        grid=(batch, nt),
        in_specs=[cur, prev, _resident(w.shape), _resident((1, ch)), _resident((1, ch)), _resident((1, ch))],
        out_specs=cur,
        out_shape=jax.ShapeDtypeStruct((batch * seq, ch), BF16),
        scratch_shapes=[pltpu.VMEM((halo + tc, ch), F32), pltpu.VMEM((tc, ch), F32)],
        compiler_params=_params("parallel", "arbitrary"),
        name="conv_prompt",
    )(glu, glu, w, b, g, beta)


def _append_state(ns_ref, st_ref, g, new, t_new):
    n_past = st_ref.shape[1]
    keep = n_past - t_new
    full = (keep // SUBLANES) * SUBLANES
    rem = keep - full
    if full:
        ns_ref[g, :full, :] = st_ref[g, t_new:t_new + full, :]
    if rem:
        rolled = pltpu.roll(new, rem, axis=0)
        ns_ref[g, full:full + SUBLANES, :] = rolled
        ns_ref[g, full:keep, :] = st_ref[g, t_new + full:n_past, :]
        ns_ref[g, full + SUBLANES:, :] = rolled[:rem]
    else:
        ns_ref[g, full:, :] = new


def _conv_sample_kernel(new_ref, st_ref, toep_ref, b_ref, g_ref, beta_ref, o_ref, ns_ref, of_ref, *, t_new):
    g_seqs, n_past, ch = st_ref.shape

    def one_seq(g, carry):
        r0 = pl.multiple_of(g * t_new, t_new)
        new = new_ref[pl.ds(r0, t_new), :]
        acc = jnp.broadcast_to(b_ref[...], (t_new, ch))
        for r in range(n_past):
            acc = acc + toep_ref[r] * st_ref[g, pl.ds(r, 1), :]
        for r in range(t_new):
            acc = acc + toep_ref[n_past + r] * new_ref[pl.ds(r0 + r, 1), :]
        of_ref[pl.ds(r0, t_new), :] = _ln_silu(acc, g_ref[...], beta_ref[...])
        _append_state(ns_ref, st_ref, g, new, t_new)
        return carry

    lax.fori_loop(0, g_seqs, one_seq, 0)
    o_ref[...] = of_ref[...].astype(o_ref.dtype)


def _conv_sample(glu, state, toep, b, g, beta, *, t_new, g_seqs=16):
    n_seq, n_past, ch = state.shape
    return pl.pallas_call(
        functools.partial(_conv_sample_kernel, t_new=t_new),
        grid=(n_seq // g_seqs,),
        in_specs=[
            pl.BlockSpec((g_seqs * t_new, ch), lambda i: (i, 0)),
            pl.BlockSpec((g_seqs, n_past, ch), lambda i: (i, 0, 0)),
            _resident(toep.shape), _resident((1, ch)), _resident((1, ch)), _resident((1, ch)),
        ],
        out_specs=[
            pl.BlockSpec((g_seqs * t_new, ch), lambda i: (i, 0)),
            pl.BlockSpec((g_seqs, n_past, ch), lambda i: (i, 0, 0)),
        ],
        out_shape=[jax.ShapeDtypeStruct((n_seq * t_new, ch), BF16), jax.ShapeDtypeStruct(state.shape, F32)],
        scratch_shapes=[pltpu.VMEM((g_seqs * t_new, ch), F32)],
        compiler_params=_params("parallel"),
        name="conv_sample",
    )(glu, state, toep, b, g, beta)


def _out_proj_kernel(x_ref, a_ref, c_ref, wa_ref, wc_ref, o_ref):
    o_ref[...] = (x_ref[...]
                  + jnp.dot(a_ref[...], wa_ref[...], preferred_element_type=F32)
                  + jnp.dot(c_ref[...], wc_ref[...], preferred_element_type=F32))


def _out_proj(x, attn, conv, wa, wc, *, tm=512):
    n, d = x.shape
    rows = lambda width: pl.BlockSpec((tm, width), lambda i: (i, 0))
    return pl.pallas_call(
        _out_proj_kernel,
        grid=(n // tm,),
        in_specs=[rows(d), rows(attn.shape[1]), rows(conv.shape[1]), _resident(wa.shape), _resident(wc.shape)],
        out_specs=rows(d),
        out_shape=jax.ShapeDtypeStruct((n, d), F32),
        compiler_params=_params("parallel"),
        name="out_proj",
    )(x, attn, conv, wa, wc)


def _ffn_kernel(x_ref, g_ref, wg_ref, wu_ref, wd_ref, o_ref, h_ref):
    @pl.when(pl.program_id(1) == 0)
    def _():
        x = x_ref[...]
        h_ref[...] = _rms(x, g_ref[...]).astype(BF16)
        o_ref[...] = x

    h = h_ref[...]
    a = jnp.dot(h, wg_ref[...].astype(BF16), preferred_element_type=F32)
    u = jnp.dot(h, wu_ref[...].astype(BF16), preferred_element_type=F32)
    act = (jax.nn.silu(a) * u).astype(BF16)
    o_ref[...] += jnp.dot(act, wd_ref[...].astype(BF16), preferred_element_type=F32)


def _ffn(x, g, wg, wu, wd, *, layer, tm=1024, tf=256):
    n, d = x.shape
    dff = wg.shape[2]
    return pl.pallas_call(
        _ffn_kernel,
        grid=(n // tm, dff // tf),
        in_specs=[
            pl.BlockSpec((tm, d), lambda i, f: (i, 0)),
            _resident((1, d)),
            pl.BlockSpec((None, d, tf), lambda i, f: (layer, 0, f)),
            pl.BlockSpec((None, d, tf), lambda i, f: (layer, 0, f)),
            pl.BlockSpec((None, tf, d), lambda i, f: (layer, f, 0)),
        ],
        out_specs=pl.BlockSpec((tm, d), lambda i, f: (i, 0)),
        out_shape=jax.ShapeDtypeStruct((n, d), F32),
        scratch_shapes=[pltpu.VMEM((tm, d), BF16)],
        compiler_params=_params("parallel", "arbitrary"),
        name="ffn",
    )(x, g, wg, wu, wd)


def _pool_prompt_kernel(cur_ref, halo_ref, g_ref, pw_ref, ps_ref, o_ref, tail_ref, xp_ref, *, rows_per):
    tp, d = cur_ref.shape
    halo = halo_ref.shape[0]
    i = pl.program_id(1)
    gain = g_ref[...]
    xp_ref[:halo, :] = jnp.where(i == 0, 0.0, _rms(halo_ref[...], gain))
    xp_ref[halo:, :] = _rms(cur_ref[...], gain)
    tail_ref[0] = xp_ref[tp:, :]
    pg = d // len(POOL_WINDOWS)
    for rr in range(tp // rows_per):
        r0 = halo + rr * rows_per
        pos = i * tp + rr * rows_per + lax.broadcasted_iota(jnp.int32, (rows_per, 1), 0)
        for gi, w in enumerate(POOL_WINDOWS):
            cols = slice(gi * pg, (gi + 1) * pg)
            h = xp_ref[r0:r0 + rows_per, cols]
            tot = h
            for back in range(1, w):
                tot = tot + xp_ref[r0 - back:r0 - back + rows_per, cols]
            count = jnp.minimum(pos + 1, w).astype(F32)
            dpool = (tot / count - h).astype(BF16)
            y = jnp.dot(dpool, pw_ref[gi], preferred_element_type=F32)
            rows = slice(rr * rows_per, (rr + 1) * rows_per)
            o_ref[rows, cols] = cur_ref[rows, cols] + y * ps_ref[:, cols]


def _pool_prompt(x, g, pw, ps, *, batch, seq, tp=512, halo=16):
    d = x.shape[1]
    nt = seq // tp
    per = tp // halo
    cur = pl.BlockSpec((tp, d), lambda bi, i: (bi * nt + i, 0))
    prev = pl.BlockSpec((halo, d), lambda bi, i: (jnp.maximum((bi * nt + i) * per - 1, 0), 0))
    return pl.pallas_call(
        functools.partial(_pool_prompt_kernel, rows_per=128),
        grid=(batch, nt),
        in_specs=[cur, prev, _resident((1, d)), _resident(pw.shape), _resident((1, d))],
        out_specs=[cur, pl.BlockSpec((1, halo, d), lambda bi, i: (bi, 0, 0))],
        out_shape=[jax.ShapeDtypeStruct((batch * seq, d), F32), jax.ShapeDtypeStruct((batch, halo, d), F32)],
        scratch_shapes=[pltpu.VMEM((halo + tp, d), F32)],
        compiler_params=_params("parallel", "arbitrary"),
        name="pool_prompt",
    )(x, x, g, pw, ps)


def _pool_sample_kernel(x_ref, st_ref, g_ref, pw_ref, ps_ref, o_ref, ns_ref, h_ref, dp_ref, *, t_new, start_pos):
    g_seqs, n_past, d = st_ref.shape
    pg = d // len(POOL_WINDOWS)
    h_ref[...] = _rms(x_ref[...], g_ref[...])
    t = lax.broadcasted_iota(jnp.int32, (t_new, 1), 0)

    def one_seq(g, carry):
        r0 = pl.multiple_of(g * t_new, t_new)
        h = h_ref[pl.ds(r0, t_new), :]
        for gi, w in enumerate(POOL_WINDOWS):
            cols = slice(gi * pg, (gi + 1) * pg)
            inv = 1.0 / jnp.minimum(start_pos + t + 1, w).astype(F32)
            tot = jnp.zeros((t_new, pg), F32)
            for r in range(max(0, n_past + 1 - w), n_past + t_new):
                if r < n_past:
                    row = st_ref[g, pl.ds(r, 1), cols]
                else:
                    row = h_ref[pl.ds(r0 + (r - n_past), 1), cols]
                hit = jnp.logical_and(t + n_past - w < r, r <= t + n_past)
                tot = tot + jnp.where(hit, inv, 0.0) * row
            dp_ref[pl.ds(r0, t_new), cols] = tot - h[:, cols]
        _append_state(ns_ref, st_ref, g, h, t_new)
        return carry

    lax.fori_loop(0, g_seqs, one_seq, 0)
    for gi in range(len(POOL_WINDOWS)):
        cols = slice(gi * pg, (gi + 1) * pg)
        y = jnp.dot(dp_ref[:, cols].astype(BF16), pw_ref[gi], preferred_element_type=F32)
        o_ref[:, cols] = x_ref[:, cols] + y * ps_ref[:, cols]


def _pool_sample(x, state, g, pw, ps, *, t_new, g_seqs=16):
    n_seq, n_past, d = state.shape
    assert t_new == SUBLANES and SUBLANES <= n_past < 2 * SUBLANES
    kern = functools.partial(_pool_sample_kernel, t_new=t_new, start_pos=PAST_LEN)
    return pl.pallas_call(
        kern,
        grid=(n_seq // g_seqs,),
        in_specs=[
            pl.BlockSpec((g_seqs * t_new, d), lambda i: (i, 0)),
            pl.BlockSpec((g_seqs, n_past, d), lambda i: (i, 0, 0)),
            _resident((1, d)), _resident(pw.shape), _resident((1, d)),
        ],
        out_specs=[
            pl.BlockSpec((g_seqs * t_new, d), lambda i: (i, 0)),
            pl.BlockSpec((g_seqs, n_past, d), lambda i: (i, 0, 0)),
        ],
        out_shape=[jax.ShapeDtypeStruct((n_seq * t_new, d), F32), jax.ShapeDtypeStruct(state.shape, F32)],
        scratch_shapes=[pltpu.VMEM((g_seqs * t_new, d), F32), pltpu.VMEM((g_seqs * t_new, d), F32)],
        compiler_params=_params("parallel"),
        name="pool_sample",
    )(x, state, g, pw, ps)


def _toeplitz(conv_w, t_new):
    width = conv_w.shape[0]
    idx = jnp.arange(width - 1 + t_new)[:, None] - jnp.arange(t_new)[None, :]
    valid = jnp.logical_and(idx >= 0, idx < width)
    return jnp.where(valid[:, :, None], conv_w[jnp.clip(idx, 0, width - 1)], 0.0)


def kernel(x_prompt, x_sample, cache_k, cache_v, state_conv, state_pool, norm_mix, w_in, q_norm, k_norm, sinks,
           conv_w, conv_b, conv_ln_g, conv_ln_b, w_out, pool_w, pool_scale, norm_ffn, w_gate, w_up, w_down):
    batch, seq, d = x_prompt.shape
    n_seq, t_new, _ = x_sample.shape
    n_kv = cache_k.shape[3]
    kv_w = n_kv * HEAD_DIM
    conv_ch = state_conv.shape[-1]
    attn_w = w_in.shape[-1] - 2 * kv_w - 2 * conv_ch
    n_prompt = batch * seq
    row = lambda a: a.reshape(1, -1)

    xp = x_prompt.reshape(n_prompt, d)
    xs = x_sample.reshape(n_seq * t_new, d)
    depth = norm_mix.shape[0]
    outs = {}
    for layer in range(depth):
        i = layer // 2
        if layer % 2 == 0:
            reps = MXU_DIM // HEAD_DIM
            proj = functools.partial(
                _in_proj, g=row(norm_mix[layer]), w=w_in[i].astype(BF16), qg=row(jnp.tile(q_norm[i], reps)),
                kg=row(jnp.tile(k_norm[i], reps)), attn_w=attn_w, kv_w=kv_w, conv_ch=conv_ch)
            q_p, k_p, v_p, glu_p = proj(xp)
            q_s, k_s, v_s, glu_s = proj(xs)
            attn_p = _attn_prompt(sinks[i], q_p, k_p, v_p, batch=batch, seq=seq, n_kv=n_kv)
            attn_s, nk_s, nv_s = _attn_sample(
                sinks[i], q_s, k_s, v_s, cache_k[i].reshape(n_seq, WINDOW, kv_w), cache_v[i].reshape(n_seq, WINDOW, kv_w),
                t_new=t_new, n_kv=n_kv)
            ln = (row(conv_b[i]), row(conv_ln_g[i]), row(conv_ln_b[i]))
            conv_p = _conv_prompt(glu_p, conv_w[i], *ln, batch=batch, seq=seq)
            conv_s, nc_s = _conv_sample(glu_s, state_conv[i], _toeplitz(conv_w[i], t_new), *ln, t_new=t_new)
            wo = w_out[i].astype(BF16)
            xp = _out_proj(xp, attn_p, conv_p, wo[:attn_w], wo[attn_w:])
            xs = _out_proj(xs, attn_s, conv_s, wo[:attn_w], wo[attn_w:])
            last = lambda a, n: a.reshape(batch, seq, -1)[:, -n:]
            outs.setdefault("kp", []).append(last(k_p, WINDOW).reshape(batch, WINDOW, n_kv, HEAD_DIM))
            outs.setdefault("vp", []).append(last(v_p, WINDOW).reshape(batch, WINDOW, n_kv, HEAD_DIM))
            outs.setdefault("cp", []).append(last(glu_p, conv_w.shape[1] - 1))
            outs.setdefault("ks", []).append(nk_s.reshape(n_seq, WINDOW, n_kv, HEAD_DIM))
            outs.setdefault("vs", []).append(nv_s.reshape(n_seq, WINDOW, n_kv, HEAD_DIM))
            outs.setdefault("cs", []).append(nc_s)
        else:
            args = (row(norm_mix[layer]), pool_w[i].astype(BF16), row(pool_scale[i]))
            n_past = state_pool.shape[2]
            xp, tail = _pool_prompt(xp, *args, batch=batch, seq=seq)
            xs, np_s = _pool_sample(xs, state_pool[i], *args, t_new=t_new)
            outs.setdefault("pp", []).append(tail[:, -n_past:])
            outs.setdefault("ps", []).append(np_s)
        ffn = functools.partial(_ffn, g=row(norm_ffn[layer]), wg=w_gate, wu=w_up, wd=w_down, layer=layer)
        xp = ffn(xp)
        xs = ffn(xs)
    st = lambda name: jnp.stack(outs[name])
    return (xp.reshape(batch, seq, d), xs.reshape(n_seq, t_new, d),
            st("kp"), st("vp"), st("cp"), st("pp"), st("ks"), st("vs"), st("cs"), st("ps"))
```

```python
import functools

import jax
import jax.numpy as jnp
from jax import lax
from jax.experimental import pallas as pl
from jax.experimental.pallas import tpu as pltpu

F32 = jnp.float32
BF16 = jnp.bfloat16

HEAD_DIM = 64
WINDOW = 128
PAST_LEN = 8192
POOL_WINDOWS = (2, 4, 8, 16)
RMS_EPS = 1e-6
LN_EPS = 1e-5
NEG = -1e30

LANES = 128
SUBLANES = 8
MXU_DIM = 256
VMEM_LIMIT = 56 * 1024 * 1024


def _params(*semantics):
    return pltpu.CompilerParams(dimension_semantics=semantics, vmem_limit_bytes=VMEM_LIMIT)


def _rms(x, g):
    ms = jnp.mean(x * x, axis=-1, keepdims=True)
    return x * lax.rsqrt(ms + RMS_EPS) * g


def _resident(shape):
    zeros = (0,) * len(shape)
    return pl.BlockSpec(shape, lambda *_: zeros, pipeline_mode=pl.Buffered(1))


def _resident_slab(shape, slab):
    index = (slab,) + (0,) * (len(shape) - 1)
    return pl.BlockSpec((None,) + tuple(shape[1:]), lambda *_: index, pipeline_mode=pl.Buffered(1))


def _in_proj_kernel(x_ref, g_ref, w_ref, qg_ref, kg_ref, q_ref, k_ref, v_ref, glu_ref, *, attn_w, kv_w, conv_ch):
    h = _rms(x_ref[...], g_ref[...]).astype(BF16)
    row = lax.broadcasted_iota(jnp.int32, (MXU_DIM, MXU_DIM), 0) // HEAD_DIM
    col = lax.broadcasted_iota(jnp.int32, (MXU_DIM, MXU_DIM), 1) // HEAD_DIM
    pmat = jnp.where(row == col, 1.0 / HEAD_DIM, 0.0).astype(BF16)

    def proj(c0):
        return jnp.dot(h, w_ref[:, c0:c0 + MXU_DIM].astype(BF16), preferred_element_type=F32)

    def head_rms(z):
        sq = z * z
        hi = sq.astype(BF16)
        lo = (sq - hi.astype(F32)).astype(BF16)
        ms = jnp.dot(hi, pmat, preferred_element_type=F32) + jnp.dot(lo, pmat, preferred_element_type=F32)
        return z * lax.rsqrt(ms + RMS_EPS)

    scale = HEAD_DIM ** -0.5
    for c in range(attn_w // MXU_DIM):
        c0 = c * MXU_DIM
        q_ref[:, c0:c0 + MXU_DIM] = (head_rms(proj(c0)) * qg_ref[...] * scale).astype(q_ref.dtype)
    for c in range(kv_w // MXU_DIM):
        c0 = c * MXU_DIM
        k_ref[:, c0:c0 + MXU_DIM] = head_rms(proj(attn_w + c0)) * kg_ref[...]
        v_ref[:, c0:c0 + MXU_DIM] = proj(attn_w + kv_w + c0)
    for c in range(conv_ch // MXU_DIM):
        c0 = c * MXU_DIM
        a = proj(attn_w + 2 * kv_w + c0)
        gate = proj(attn_w + 2 * kv_w + conv_ch + c0)
        glu_ref[:, c0:c0 + MXU_DIM] = a * jax.nn.sigmoid(gate)


def _in_proj(x, g, w, qg, kg, *, slab, attn_w, kv_w, conv_ch, tm=512):
    n, d = x.shape
    kern = functools.partial(_in_proj_kernel, attn_w=attn_w, kv_w=kv_w, conv_ch=conv_ch)
    rows = lambda width: pl.BlockSpec((tm, width), lambda i: (i, 0))
    return pl.pallas_call(
        kern,
        grid=(n // tm,),
        in_specs=[rows(d), _resident((1, d)), _resident_slab(w.shape, slab), _resident((1, MXU_DIM)), _resident((1, MXU_DIM))],
        out_specs=[rows(attn_w), rows(kv_w), rows(kv_w), rows(conv_ch)],
        out_shape=[
            jax.ShapeDtypeStruct((n, attn_w), BF16),
            jax.ShapeDtypeStruct((n, kv_w), F32),
            jax.ShapeDtypeStruct((n, kv_w), F32),
            jax.ShapeDtypeStruct((n, conv_ch), F32),
        ],
        compiler_params=_params("parallel"),
        name="in_proj",
    )(x, g, w, qg, kg)


def _pair_blocks(chunk, rolled, odd, pad_rows):
    lane = lax.broadcasted_iota(jnp.int32, chunk.shape, 1)
    low = lane < HEAD_DIM
    if odd:
        top = jnp.where(low, rolled, 0.0)
        bot = jnp.where(low, 0.0, chunk)
    else:
        top = jnp.where(low, chunk, 0.0)
        bot = jnp.where(low, 0.0, rolled)
    parts = [top, bot]
    if pad_rows:
        z = jnp.zeros((pad_rows, LANES), F32)
        parts = [top, z, bot, z]
    return jnp.concatenate(parts, axis=0).astype(BF16)


def _softmax_pv(s, ok, sink_a, sink_b, vb):
    half = s.shape[1] // 2
    s = jnp.where(ok, s, NEG)
    sa, sb = s[:, :half], s[:, half:]
    ma = jnp.maximum(jnp.max(sa, axis=-1, keepdims=True), sink_a)
    mb = jnp.maximum(jnp.max(sb, axis=-1, keepdims=True), sink_b)
    pa = jnp.exp(sa - ma)
    pb = jnp.exp(sb - mb)
    da = jnp.sum(pa, axis=-1, keepdims=True) + jnp.exp(sink_a - ma)
    db = jnp.sum(pb, axis=-1, keepdims=True) + jnp.exp(sink_b - mb)
    p = jnp.concatenate([pa, pb], axis=1).astype(BF16)
    o = jnp.dot(p, vb, preferred_element_type=F32)
    lane = lax.broadcasted_iota(jnp.int32, o.shape, 1)
    return o / jnp.where(lane < HEAD_DIM, da, db)


def _attend(q2, kb, vb, ok, sinks_ref, h0, rows):
    s = lax.dot_general(q2, kb, (((1,), (1,)), ((), ())), preferred_element_type=F32)
    row = lax.broadcasted_iota(jnp.int32, (2 * rows, 1), 0)
    first = row < rows
    sink_a = jnp.where(first, sinks_ref[h0], sinks_ref[h0 + 2])
    sink_b = jnp.where(first, sinks_ref[h0 + 1], sinks_ref[h0 + 3])
    return _softmax_pv(s, ok, sink_a, sink_b, vb)


def _attn_prompt_kernel(sinks_ref, q_ref, kp_ref, kc_ref, vp_ref, vc_ref, o_ref, *, n_kv):
    blk = q_ref.shape[0]
    i = pl.program_id(1)
    shape = (2 * blk, 4 * blk)
    r = lax.broadcasted_iota(jnp.int32, shape, 0) % blk
    c = lax.broadcasted_iota(jnp.int32, shape, 1) % (2 * blk)
    lo = jnp.where(i == 0, blk - 1, -1)
    ok = jnp.logical_and(c > jnp.maximum(r, lo), c <= r + blk)
    k_all = jnp.concatenate([kp_ref[...], kc_ref[...]], axis=0)
    v_all = jnp.concatenate([vp_ref[...], vc_ref[...]], axis=0)
    group = (q_ref.shape[1] // HEAD_DIM) // n_kv
    for cch in range(n_kv // 2):
        kc = k_all[:, cch * LANES:(cch + 1) * LANES]
        vc = v_all[:, cch * LANES:(cch + 1) * LANES]
        kr = pltpu.roll(kc, HEAD_DIM, axis=1)
        vr = pltpu.roll(vc, HEAD_DIM, axis=1)
        for odd in range(2):
            j = 2 * cch + odd
            kb = _pair_blocks(kc, kr, odd, 0)
            vb = _pair_blocks(vc, vr, odd, 0)
            h0 = group * j
            c0 = h0 * HEAD_DIM
            q2 = jnp.concatenate([q_ref[:, c0:c0 + LANES], q_ref[:, c0 + LANES:c0 + 2 * LANES]], axis=0)
            o = _attend(q2, kb, vb, ok, sinks_ref, h0, blk).astype(o_ref.dtype)
            o_ref[:, c0:c0 + LANES] = o[:blk]
            o_ref[:, c0 + LANES:c0 + 2 * LANES] = o[blk:]


def _attn_prompt(sinks, q, k, v, *, batch, seq, n_kv):
    blk = WINDOW
    nb = seq // blk
    aw, kw = q.shape[1], k.shape[1]
    cur = lambda w: pl.BlockSpec((blk, w), lambda b, i: (b * nb + i, 0))
    prev = lambda w: pl.BlockSpec((blk, w), lambda b, i: (b * nb + jnp.maximum(i - 1, 0), 0))
    return pl.pallas_call(
        functools.partial(_attn_prompt_kernel, n_kv=n_kv),
        grid=(batch, nb),
        in_specs=[pl.BlockSpec(memory_space=pltpu.SMEM), cur(aw), prev(kw), cur(kw), prev(kw), cur(kw)],
        out_specs=cur(aw),
        out_shape=jax.ShapeDtypeStruct((batch * seq, aw), BF16),
        compiler_params=_params("parallel", "arbitrary"),
        name="attn_prompt",
    )(sinks, q, k, k, v, v)


def _attn_sample_kernel(sinks_ref, q_ref, kn_ref, vn_ref, ck_ref, cv_ref, o_ref, nk_ref, nv_ref, qf_ref, of_ref, *, n_kv, t_new):
    g_seqs, _, kw = ck_ref.shape
    n_heads = q_ref.shape[1] // HEAD_DIM
    group = n_heads // n_kv
    keep = WINDOW - t_new
    n_rows = n_heads * t_new
    n_keys = 2 * WINDOW
    qf_ref[...] = q_ref[...].astype(F32)
    t = lax.broadcasted_iota(jnp.int32, (n_rows, n_keys), 0) % t_new
    c = lax.broadcasted_iota(jnp.int32, (n_rows, n_keys), 1)
    ok = jnp.logical_and(c > t, c <= t + WINDOW)
    sink = jnp.concatenate([jnp.full((t_new, 1), sinks_ref[h], F32) for h in range(n_heads)], axis=0)
    low = lax.broadcasted_iota(jnp.int32, (t_new, LANES), 1) < HEAD_DIM
    zero_tile = jnp.zeros((t_new, LANES), F32)
    zero_keys = jnp.zeros((n_keys - WINDOW - t_new, kw), F32)

    def one_seq(g, carry):
        r0 = pl.multiple_of(g * t_new, t_new)
        k_new = kn_ref[pl.ds(r0, t_new), :]
        v_new = vn_ref[pl.ds(r0, t_new), :]
        k_old = ck_ref[g]
        v_old = cv_ref[g]
        nk_ref[g, :keep, :] = k_old[t_new:]
        nk_ref[g, keep:, :] = k_new
        nv_ref[g, :keep, :] = v_old[t_new:]
        nv_ref[g, keep:, :] = v_new
        keys = jnp.concatenate([k_old, k_new, zero_keys], axis=0).astype(BF16)
        vals = jnp.concatenate([v_old, v_new, zero_keys], axis=0).astype(BF16)
        q_rows = []
        for h in range(n_heads):
            j = h // group
            src = qf_ref[pl.ds(r0, t_new), (h // 2) * LANES:(h // 2 + 1) * LANES]
            if h % 2 != j % 2:
                src = pltpu.roll(src, HEAD_DIM, axis=1)
            half = jnp.where(low, src, 0.0) if j % 2 == 0 else jnp.where(low, 0.0, src)
            tiles = [zero_tile] * (kw // LANES)
            tiles[j // 2] = half
            q_rows.append(jnp.concatenate(tiles, axis=1))
        qb = jnp.concatenate(q_rows, axis=0).astype(BF16)
        s = lax.dot_general(qb, keys, (((1,), (1,)), ((), ())), preferred_element_type=F32)
        s = jnp.where(ok, s, NEG)
        m = jnp.maximum(jnp.max(s, axis=-1, keepdims=True), sink)
        p = jnp.exp(s - m)
        denom = jnp.sum(p, axis=-1, keepdims=True) + jnp.exp(sink - m)
        o = jnp.dot(p.astype(BF16), vals, preferred_element_type=F32) / denom
        for pair in range(n_heads // 2):
            j = (2 * pair) // group
            lanes = slice((j // 2) * LANES, (j // 2 + 1) * LANES)
            a = o[2 * pair * t_new:(2 * pair + 1) * t_new, lanes]
            b = o[(2 * pair + 1) * t_new:(2 * pair + 2) * t_new, lanes]
            if j % 2 == 0:
                b = pltpu.roll(b, HEAD_DIM, axis=1)
            else:
                a = pltpu.roll(a, HEAD_DIM, axis=1)
            of_ref[pl.ds(r0, t_new), pair * LANES:(pair + 1) * LANES] = jnp.where(low, a, b)
        return carry

    lax.fori_loop(0, g_seqs, one_seq, 0, unroll=2)
    o_ref[...] = of_ref[...].astype(o_ref.dtype)


def _attn_sample(sinks, q, k, v, cache_k, cache_v, *, t_new, n_kv, g_seqs=16):
    n_seq = cache_k.shape[0]
    aw, kw = q.shape[1], k.shape[1]
    tok = lambda w: pl.BlockSpec((g_seqs * t_new, w), lambda i: (i, 0))
    cache = pl.BlockSpec((g_seqs, WINDOW, kw), lambda i: (i, 0, 0))
    return pl.pallas_call(
        functools.partial(_attn_sample_kernel, n_kv=n_kv, t_new=t_new),
        grid=(n_seq // g_seqs,),
        in_specs=[pl.BlockSpec(memory_space=pltpu.SMEM), tok(aw), tok(kw), tok(kw), cache, cache],
        out_specs=[tok(aw), cache, cache],
        out_shape=[
            jax.ShapeDtypeStruct((n_seq * t_new, aw), BF16),
            jax.ShapeDtypeStruct(cache_k.shape, F32),
            jax.ShapeDtypeStruct(cache_v.shape, F32),
        ],
        scratch_shapes=[pltpu.VMEM((g_seqs * t_new, aw), F32), pltpu.VMEM((g_seqs * t_new, aw), F32)],
        compiler_params=_params("parallel"),
        name="attn_sample",
    )(sinks, q, k, v, cache_k, cache_v)


def _ln_silu(y, g, b):
    yc = y - jnp.mean(y, axis=-1, keepdims=True)
    var = jnp.mean(yc * yc, axis=-1, keepdims=True)
    return jax.nn.silu(yc * lax.rsqrt(var + LN_EPS) * g + b)


def _conv_prompt_kernel(cur_ref, halo_ref, w_ref, b_ref, g_ref, beta_ref, o_ref, xc_ref, y_ref, *, width, rows_per, lanes_per):
    tc, ch = cur_ref.shape
    halo = halo_ref.shape[0]
    i = pl.program_id(1)
    xc_ref[:halo, :] = jnp.where(i == 0, 0.0, halo_ref[...])
    xc_ref[halo:, :] = cur_ref[...]
    base = halo - (width - 1)

    def col_chunk(cc, carry):
        l0 = pl.multiple_of(cc * lanes_per, lanes_per)
        cols = pl.ds(l0, lanes_per)
        for rr in range(tc // rows_per):
            acc = jnp.broadcast_to(b_ref[:, cols], (rows_per, lanes_per))
            for phase in range(SUBLANES):
                n_win = rows_per + (SUBLANES if phase else 0)
                part = None
                for j in range(width):
                    if (base + j) % SUBLANES != phase:
                        continue
                    start = rr * rows_per + ((base + j) // SUBLANES) * SUBLANES
                    term = w_ref[pl.ds(j, 1), cols] * xc_ref[pl.ds(start, n_win), cols]
                    part = term if part is None else part + term
                if part is None:
                    continue
                if phase:
                    part = pltpu.roll(part, n_win - phase, axis=0)
                acc = acc + part[:rows_per]
            y_ref[pl.ds(rr * rows_per, rows_per), cols] = acc
        return carry

    lax.fori_loop(0, ch // lanes_per, col_chunk, 0)
    o_ref[...] = _ln_silu(y_ref[...], g_ref[...], beta_ref[...]).astype(o_ref.dtype)


def _conv_prompt(glu, w, b, g, beta, *, batch, seq, tc=256, halo=32):
    ch = glu.shape[1]
    width = w.shape[0]
    nt = seq // tc
    per = tc // halo
    cur = pl.BlockSpec((tc, ch), lambda bi, i: (bi * nt + i, 0))
    prev = pl.BlockSpec((halo, ch), lambda bi, i: (jnp.maximum((bi * nt + i) * per - 1, 0), 0))
    kern = functools.partial(_conv_prompt_kernel, width=width, rows_per=64, lanes_per=LANES)
    return pl.pallas_call(
        kern,
        grid=(batch, nt),
        in_specs=[cur, prev, _resident(w.shape), _resident((1, ch)), _resident((1, ch)), _resident((1, ch))],
        out_specs=cur,
        out_shape=jax.ShapeDtypeStruct((batch * seq, ch), BF16),
        scratch_shapes=[pltpu.VMEM((halo + tc, ch), F32), pltpu.VMEM((tc, ch), F32)],
        compiler_params=_params("parallel", "arbitrary"),
        name="conv_prompt",
    )(glu, glu, w, b, g, beta)


def _append_state(ns_ref, st_ref, g, new, t_new):
    n_past = st_ref.shape[1]
    keep = n_past - t_new
    full = (keep // SUBLANES) * SUBLANES
    rem = keep - full
    if full:
        ns_ref[g, :full, :] = st_ref[g, t_new:t_new + full, :]
    if rem:
        rolled = pltpu.roll(new, rem, axis=0)
        ns_ref[g, full:full + SUBLANES, :] = rolled
        ns_ref[g, full:keep, :] = st_ref[g, t_new + full:n_past, :]
        ns_ref[g, full + SUBLANES:, :] = rolled[:rem]
    else:
        ns_ref[g, full:, :] = new


def _conv_sample_kernel(new_ref, st_ref, toep_ref, b_ref, g_ref, beta_ref, o_ref, ns_ref, of_ref, *, t_new):
    g_seqs, n_past, ch = st_ref.shape

    def one_seq(g, carry):
        r0 = pl.multiple_of(g * t_new, t_new)
        new = new_ref[pl.ds(r0, t_new), :]
        acc = jnp.broadcast_to(b_ref[...], (t_new, ch))
        for r in range(n_past):
            acc = acc + toep_ref[r] * st_ref[g, pl.ds(r, 1), :]
        for r in range(t_new):
            acc = acc + toep_ref[n_past + r] * new_ref[pl.ds(r0 + r, 1), :]
        of_ref[pl.ds(r0, t_new), :] = _ln_silu(acc, g_ref[...], beta_ref[...])
        _append_state(ns_ref, st_ref, g, new, t_new)
        return carry

    lax.fori_loop(0, g_seqs, one_seq, 0)
    o_ref[...] = of_ref[...].astype(o_ref.dtype)


def _conv_sample(glu, state, toep, b, g, beta, *, t_new, g_seqs=16):
    n_seq, n_past, ch = state.shape
    return pl.pallas_call(
        functools.partial(_conv_sample_kernel, t_new=t_new),
        grid=(n_seq // g_seqs,),
        in_specs=[
            pl.BlockSpec((g_seqs * t_new, ch), lambda i: (i, 0)),
            pl.BlockSpec((g_seqs, n_past, ch), lambda i: (i, 0, 0)),
            _resident(toep.shape), _resident((1, ch)), _resident((1, ch)), _resident((1, ch)),
        ],
        out_specs=[
            pl.BlockSpec((g_seqs * t_new, ch), lambda i: (i, 0)),
            pl.BlockSpec((g_seqs, n_past, ch), lambda i: (i, 0, 0)),
        ],
        out_shape=[jax.ShapeDtypeStruct((n_seq * t_new, ch), BF16), jax.ShapeDtypeStruct(state.shape, F32)],
        scratch_shapes=[pltpu.VMEM((g_seqs * t_new, ch), F32)],
        compiler_params=_params("parallel"),
        name="conv_sample",
    )(glu, state, toep, b, g, beta)


def _out_proj_kernel(x_ref, a_ref, c_ref, w_ref, o_ref):
    aw = a_ref.shape[1]
    o_ref[...] = (x_ref[...]
                  + jnp.dot(a_ref[...], w_ref[:aw, :].astype(BF16), preferred_element_type=F32)
                  + jnp.dot(c_ref[...], w_ref[aw:, :].astype(BF16), preferred_element_type=F32))


def _out_proj(x, attn, conv, w, *, slab, tm=512):
    n, d = x.shape
    rows = lambda width: pl.BlockSpec((tm, width), lambda i: (i, 0))
    return pl.pallas_call(
        _out_proj_kernel,
        grid=(n // tm,),
        in_specs=[rows(d), rows(attn.shape[1]), rows(conv.shape[1]), _resident_slab(w.shape, slab)],
        out_specs=rows(d),
        out_shape=jax.ShapeDtypeStruct((n, d), F32),
        compiler_params=_params("parallel"),
        name="out_proj",
    )(x, attn, conv, w)


def _ffn_kernel(x_ref, g_ref, wg_ref, wu_ref, wd_ref, o_ref, h_ref):
    @pl.when(pl.program_id(1) == 0)
    def _():
        x = x_ref[...]
        h_ref[...] = _rms(x, g_ref[...]).astype(BF16)
        o_ref[...] = x

    h = h_ref[...]
    a = jnp.dot(h, wg_ref[...].astype(BF16), preferred_element_type=F32)
    u = jnp.dot(h, wu_ref[...].astype(BF16), preferred_element_type=F32)
    act = (jax.nn.silu(a) * u).astype(BF16)
    o_ref[...] += jnp.dot(act, wd_ref[...].astype(BF16), preferred_element_type=F32)


def _ffn(x, g, wg, wu, wd, *, layer, tm=1024, tf=256):
    n, d = x.shape
    dff = wg.shape[2]
    return pl.pallas_call(
        _ffn_kernel,
        grid=(n // tm, dff // tf),
        in_specs=[
            pl.BlockSpec((tm, d), lambda i, f: (i, 0)),
            _resident((1, d)),
            pl.BlockSpec((None, d, tf), lambda i, f: (layer, 0, f)),
            pl.BlockSpec((None, d, tf), lambda i, f: (layer, 0, f)),
            pl.BlockSpec((None, tf, d), lambda i, f: (layer, f, 0)),
        ],
        out_specs=pl.BlockSpec((tm, d), lambda i, f: (i, 0)),
        out_shape=jax.ShapeDtypeStruct((n, d), F32),
        scratch_shapes=[pltpu.VMEM((tm, d), BF16)],
        compiler_params=_params("parallel", "arbitrary"),
        name="ffn",
    )(x, g, wg, wu, wd)


def _pool_prompt_kernel(cur_ref, halo_ref, g_ref, pw_ref, ps_ref, o_ref, tail_ref, xp_ref, *, rows_per):
    tp, d = cur_ref.shape
    halo = halo_ref.shape[0]
    i = pl.program_id(1)
    gain = g_ref[...]
    xp_ref[:halo, :] = jnp.where(i == 0, 0.0, _rms(halo_ref[...], gain))
    xp_ref[halo:, :] = _rms(cur_ref[...], gain)
    tail_ref[0] = xp_ref[tp:, :]
    pg = d // len(POOL_WINDOWS)
    for rr in range(tp // rows_per):
        r0 = halo + rr * rows_per
        pos = i * tp + rr * rows_per + lax.broadcasted_iota(jnp.int32, (rows_per, 1), 0)
        for gi, w in enumerate(POOL_WINDOWS):
            cols = slice(gi * pg, (gi + 1) * pg)
            ext = xp_ref[r0 - halo:r0 + rows_per, cols]
            h = ext[halo:]
            span = 1
            while span < w:
                ext = ext + pltpu.roll(ext, span, axis=0)
                span *= 2
            inv_count = 1.0 / jnp.minimum(pos + 1, w).astype(F32)
            dpool = (ext[halo:] * inv_count - h).astype(BF16)
            y = jnp.dot(dpool, pw_ref[gi].astype(BF16), preferred_element_type=F32)
            rows = slice(rr * rows_per, (rr + 1) * rows_per)
            o_ref[rows, cols] = cur_ref[rows, cols] + y * ps_ref[:, cols]


def _pool_prompt(x, g, pw, ps, *, slab, batch, seq, tp=512, halo=16):
    assert all(w & (w - 1) == 0 and w <= halo for w in POOL_WINDOWS)
    d = x.shape[1]
    nt = seq // tp
    per = tp // halo
    cur = pl.BlockSpec((tp, d), lambda bi, i: (bi * nt + i, 0))
    prev = pl.BlockSpec((halo, d), lambda bi, i: (jnp.maximum((bi * nt + i) * per - 1, 0), 0))
    return pl.pallas_call(
        functools.partial(_pool_prompt_kernel, rows_per=128),
        grid=(batch, nt),
        in_specs=[cur, prev, _resident((1, d)), _resident_slab(pw.shape, slab), _resident((1, d))],
        out_specs=[cur, pl.BlockSpec((1, halo, d), lambda bi, i: (bi, 0, 0))],
        out_shape=[jax.ShapeDtypeStruct((batch * seq, d), F32), jax.ShapeDtypeStruct((batch, halo, d), F32)],
        scratch_shapes=[pltpu.VMEM((halo + tp, d), F32)],
        compiler_params=_params("parallel", "arbitrary"),
        name="pool_prompt",
    )(x, x, g, pw, ps)


def _pool_sample_kernel(x_ref, st_ref, g_ref, pw_ref, ps_ref, o_ref, ns_ref, h_ref, dp_ref, *, t_new, start_pos):
    g_seqs, n_past, d = st_ref.shape
    pg = d // len(POOL_WINDOWS)
    h_ref[...] = _rms(x_ref[...], g_ref[...])
    t = lax.broadcasted_iota(jnp.int32, (t_new, 1), 0)

    def one_seq(g, carry):
        r0 = pl.multiple_of(g * t_new, t_new)
        h = h_ref[pl.ds(r0, t_new), :]
        for gi, w in enumerate(POOL_WINDOWS):
            cols = slice(gi * pg, (gi + 1) * pg)
            inv = 1.0 / jnp.minimum(start_pos + t + 1, w).astype(F32)
            tot = jnp.zeros((t_new, pg), F32)
            for r in range(max(0, n_past + 1 - w), n_past + t_new):
                if r < n_past:
                    row = st_ref[g, pl.ds(r, 1), cols]
                else:
                    row = h_ref[pl.ds(r0 + (r - n_past), 1), cols]
                hit = jnp.logical_and(t + n_past - w < r, r <= t + n_past)
                tot = tot + jnp.where(hit, inv, 0.0) * row
            dp_ref[pl.ds(r0, t_new), cols] = tot - h[:, cols]
        _append_state(ns_ref, st_ref, g, h, t_new)
        return carry

    lax.fori_loop(0, g_seqs, one_seq, 0)
    for gi in range(len(POOL_WINDOWS)):
        cols = slice(gi * pg, (gi + 1) * pg)
        y = jnp.dot(dp_ref[:, cols].astype(BF16), pw_ref[gi].astype(BF16), preferred_element_type=F32)
        o_ref[:, cols] = x_ref[:, cols] + y * ps_ref[:, cols]


def _pool_sample(x, state, g, pw, ps, *, slab, t_new, g_seqs=16):
    n_seq, n_past, d = state.shape
    assert t_new == SUBLANES and SUBLANES <= n_past < 2 * SUBLANES
    kern = functools.partial(_pool_sample_kernel, t_new=t_new, start_pos=PAST_LEN)
    return pl.pallas_call(
        kern,
        grid=(n_seq // g_seqs,),
        in_specs=[
            pl.BlockSpec((g_seqs * t_new, d), lambda i: (i, 0)),
            pl.BlockSpec((g_seqs, n_past, d), lambda i: (i, 0, 0)),
            _resident((1, d)), _resident_slab(pw.shape, slab), _resident((1, d)),
        ],
        out_specs=[
            pl.BlockSpec((g_seqs * t_new, d), lambda i: (i, 0)),
            pl.BlockSpec((g_seqs, n_past, d), lambda i: (i, 0, 0)),
        ],
        out_shape=[jax.ShapeDtypeStruct((n_seq * t_new, d), F32), jax.ShapeDtypeStruct(state.shape, F32)],
        scratch_shapes=[pltpu.VMEM((g_seqs * t_new, d), F32), pltpu.VMEM((g_seqs * t_new, d), F32)],
        compiler_params=_params("parallel"),
        name="pool_sample",
    )(x, state, g, pw, ps)


def _toeplitz(conv_w, t_new):
    width = conv_w.shape[0]
    idx = jnp.arange(width - 1 + t_new)[:, None] - jnp.arange(t_new)[None, :]
    valid = jnp.logical_and(idx >= 0, idx < width)
    return jnp.where(valid[:, :, None], conv_w[jnp.clip(idx, 0, width - 1)], 0.0)


def kernel(x_prompt, x_sample, cache_k, cache_v, state_conv, state_pool, norm_mix, w_in, q_norm, k_norm, sinks,
           conv_w, conv_b, conv_ln_g, conv_ln_b, w_out, pool_w, pool_scale, norm_ffn, w_gate, w_up, w_down):
    batch, seq, d = x_prompt.shape
    n_seq, t_new, _ = x_sample.shape
    n_kv = cache_k.shape[3]
    kv_w = n_kv * HEAD_DIM
    conv_ch = state_conv.shape[-1]
    attn_w = w_in.shape[-1] - 2 * kv_w - 2 * conv_ch
    n_prompt = batch * seq
    row = lambda a: a.reshape(1, -1)

    xp = x_prompt.reshape(n_prompt, d)
    xs = x_sample.reshape(n_seq * t_new, d)
    depth = norm_mix.shape[0]
    outs = {}
    for layer in range(depth):
        i = layer // 2
        if layer % 2 == 0:
            reps = MXU_DIM // HEAD_DIM
            proj = functools.partial(
                _in_proj, g=row(norm_mix[layer]), w=w_in, slab=i, qg=row(jnp.tile(q_norm[i], reps)),
                kg=row(jnp.tile(k_norm[i], reps)), attn_w=attn_w, kv_w=kv_w, conv_ch=conv_ch)
            q_p, k_p, v_p, glu_p = proj(xp)
            q_s, k_s, v_s, glu_s = proj(xs)
            attn_p = _attn_prompt(sinks[i], q_p, k_p, v_p, batch=batch, seq=seq, n_kv=n_kv)
            attn_s, nk_s, nv_s = _attn_sample(
                sinks[i], q_s, k_s, v_s, cache_k[i].reshape(n_seq, WINDOW, kv_w), cache_v[i].reshape(n_seq, WINDOW, kv_w),
                t_new=t_new, n_kv=n_kv)
            ln = (row(conv_b[i]), row(conv_ln_g[i]), row(conv_ln_b[i]))
            conv_p = _conv_prompt(glu_p, conv_w[i], *ln, batch=batch, seq=seq)
            conv_s, nc_s = _conv_sample(glu_s, state_conv[i], _toeplitz(conv_w[i], t_new), *ln, t_new=t_new)
            xp = _out_proj(xp, attn_p, conv_p, w_out, slab=i)
            xs = _out_proj(xs, attn_s, conv_s, w_out, slab=i)
            last = lambda a, n: a.reshape(batch, seq, -1)[:, -n:]
            outs.setdefault("kp", []).append(last(k_p, WINDOW).reshape(batch, WINDOW, n_kv, HEAD_DIM))
            outs.setdefault("vp", []).append(last(v_p, WINDOW).reshape(batch, WINDOW, n_kv, HEAD_DIM))
            outs.setdefault("cp", []).append(last(glu_p, conv_w.shape[1] - 1))
            outs.setdefault("ks", []).append(nk_s.reshape(n_seq, WINDOW, n_kv, HEAD_DIM))
            outs.setdefault("vs", []).append(nv_s.reshape(n_seq, WINDOW, n_kv, HEAD_DIM))
            outs.setdefault("cs", []).append(nc_s)
        else:
            args = (row(norm_mix[layer]), pool_w, row(pool_scale[i]))
            n_past = state_pool.shape[2]
            xp, tail = _pool_prompt(xp, *args, slab=i, batch=batch, seq=seq)
            xs, np_s = _pool_sample(xs, state_pool[i], *args, slab=i, t_new=t_new)
            outs.setdefault("pp", []).append(tail[:, -n_past:])
            outs.setdefault("ps", []).append(np_s)
        ffn = functools.partial(_ffn, g=row(norm_ffn[layer]), wg=w_gate, wu=w_up, wd=w_down, layer=layer)
        xp = ffn(xp)
        xs = ffn(xs)
    st = lambda name: jnp.stack(outs[name])
    return (xp.reshape(batch, seq, d), xs.reshape(n_seq, t_new, d),
            st("kp"), st("vp"), st("cp"), st("pp"), st("ks"), st("vs"), st("cs"), st("ps"))
```

```python
import functools

import jax
import jax.numpy as jnp
from jax import lax
from jax.experimental import pallas as pl
from jax.experimental.pallas import tpu as pltpu

F32 = jnp.float32
BF16 = jnp.bfloat16

HEAD_DIM = 64
WINDOW = 128
PAST_LEN = 8192
POOL_WINDOWS = (2, 4, 8, 16)
RMS_EPS = 1e-6
LN_EPS = 1e-5
NEG = -1e30

LANES = 128
SUBLANES = 8
MXU_DIM = 256
VMEM_LIMIT = 56 * 1024 * 1024


def _params(*semantics):
    return pltpu.CompilerParams(dimension_semantics=semantics, vmem_limit_bytes=VMEM_LIMIT)


def _rms(x, g):
    ms = jnp.mean(x * x, axis=-1, keepdims=True)
    return x * lax.rsqrt(ms + RMS_EPS) * g


def _resident(shape):
    zeros = (0,) * len(shape)
    return pl.BlockSpec(shape, lambda *_: zeros, pipeline_mode=pl.Buffered(1))


def _resident_slab(shape, slab):
    index = (slab,) + (0,) * (len(shape) - 1)
    return pl.BlockSpec((None,) + tuple(shape[1:]), lambda *_: index, pipeline_mode=pl.Buffered(1))


def _in_proj_kernel(x_ref, g_ref, w_ref, qg_ref, kg_ref, q_ref, k_ref, v_ref, glu_ref, *, attn_w, kv_w, conv_ch):
    h = _rms(x_ref[...], g_ref[...]).astype(BF16)
    row = lax.broadcasted_iota(jnp.int32, (MXU_DIM, MXU_DIM), 0) // HEAD_DIM
    col = lax.broadcasted_iota(jnp.int32, (MXU_DIM, MXU_DIM), 1) // HEAD_DIM
    pmat = jnp.where(row == col, 1.0 / HEAD_DIM, 0.0).astype(BF16)

    def proj(c0):
        return jnp.dot(h, w_ref[:, c0:c0 + MXU_DIM].astype(BF16), preferred_element_type=F32)

    def head_rms(z):
        sq = z * z
        hi = sq.astype(BF16)
        lo = (sq - hi.astype(F32)).astype(BF16)
        ms = jnp.dot(hi, pmat, preferred_element_type=F32) + jnp.dot(lo, pmat, preferred_element_type=F32)
        return z * lax.rsqrt(ms + RMS_EPS)

    scale = HEAD_DIM ** -0.5

    def put_q(c0, z):
        q_ref[:, c0:c0 + MXU_DIM] = (head_rms(z) * qg_ref[...] * scale).astype(q_ref.dtype)

    def put_k(c0, z):
        k_ref[:, c0:c0 + MXU_DIM] = head_rms(z) * kg_ref[...]

    def put_v(c0, z):
        v_ref[:, c0:c0 + MXU_DIM] = z

    def put_glu(c0, a, gate):
        glu_ref[:, c0:c0 + MXU_DIM] = a * jax.nn.sigmoid(gate)

    units = [((c0,), functools.partial(put_q, c0)) for c0 in range(0, attn_w, MXU_DIM)]
    units += [((attn_w + c0,), functools.partial(put_k, c0)) for c0 in range(0, kv_w, MXU_DIM)]
    units += [((attn_w + kv_w + c0,), functools.partial(put_v, c0)) for c0 in range(0, kv_w, MXU_DIM)]
    units += [((attn_w + 2 * kv_w + c0, attn_w + 2 * kv_w + conv_ch + c0), functools.partial(put_glu, c0))
              for c0 in range(0, conv_ch, MXU_DIM)]
    pending = None
    for cols, epilogue in units:
        zs = [proj(c0) for c0 in cols]
        if pending is not None:
            pending[0](*pending[1])
        pending = (epilogue, zs)
    pending[0](*pending[1])


def _in_proj(x, g, w, qg, kg, *, slab, attn_w, kv_w, conv_ch, tm=512):
    n, d = x.shape
    kern = functools.partial(_in_proj_kernel, attn_w=attn_w, kv_w=kv_w, conv_ch=conv_ch)
    rows = lambda width: pl.BlockSpec((tm, width), lambda i: (i, 0))
    return pl.pallas_call(
        kern,
        grid=(n // tm,),
        in_specs=[rows(d), _resident((1, d)), _resident_slab(w.shape, slab), _resident((1, MXU_DIM)), _resident((1, MXU_DIM))],
        out_specs=[rows(attn_w), rows(kv_w), rows(kv_w), rows(conv_ch)],
        out_shape=[
            jax.ShapeDtypeStruct((n, attn_w), BF16),
            jax.ShapeDtypeStruct((n, kv_w), F32),
            jax.ShapeDtypeStruct((n, kv_w), F32),
            jax.ShapeDtypeStruct((n, conv_ch), F32),
        ],
        compiler_params=_params("parallel"),
        name="in_proj",
    )(x, g, w, qg, kg)


def _pair_blocks(chunk, rolled, odd, pad_rows):
    lane = lax.broadcasted_iota(jnp.int32, chunk.shape, 1)
    low = lane < HEAD_DIM
    if odd:
        top = jnp.where(low, rolled, 0.0)
        bot = jnp.where(low, 0.0, chunk)
    else:
        top = jnp.where(low, chunk, 0.0)
        bot = jnp.where(low, 0.0, rolled)
    parts = [top, bot]
    if pad_rows:
        z = jnp.zeros((pad_rows, LANES), F32)
        parts = [top, z, bot, z]
    return jnp.concatenate(parts, axis=0).astype(BF16)


def _softmax_pv(s, ok, sink_a, sink_b, vb):
    half = s.shape[1] // 2
    s = jnp.where(ok, s, NEG)
    sa, sb = s[:, :half], s[:, half:]
    ma = jnp.maximum(jnp.max(sa, axis=-1, keepdims=True), sink_a)
    mb = jnp.maximum(jnp.max(sb, axis=-1, keepdims=True), sink_b)
    pa = jnp.exp(sa - ma)
    pb = jnp.exp(sb - mb)
    da = jnp.sum(pa, axis=-1, keepdims=True) + jnp.exp(sink_a - ma)
    db = jnp.sum(pb, axis=-1, keepdims=True) + jnp.exp(sink_b - mb)
    p = jnp.concatenate([pa, pb], axis=1).astype(BF16)
    o = jnp.dot(p, vb, preferred_element_type=F32)
    lane = lax.broadcasted_iota(jnp.int32, o.shape, 1)
    return o / jnp.where(lane < HEAD_DIM, da, db)


def _attend(q2, kb, vb, ok, sinks_ref, h0, rows):
    s = lax.dot_general(q2, kb, (((1,), (1,)), ((), ())), preferred_element_type=F32)
    row = lax.broadcasted_iota(jnp.int32, (2 * rows, 1), 0)
    first = row < rows
    sink_a = jnp.where(first, sinks_ref[h0], sinks_ref[h0 + 2])
    sink_b = jnp.where(first, sinks_ref[h0 + 1], sinks_ref[h0 + 3])
    return _softmax_pv(s, ok, sink_a, sink_b, vb)


def _attn_prompt_kernel(sinks_ref, q_ref, kp_ref, kc_ref, vp_ref, vc_ref, o_ref, *, n_kv):
    blk = q_ref.shape[0]
    i = pl.program_id(1)
    shape = (2 * blk, 4 * blk)
    r = lax.broadcasted_iota(jnp.int32, shape, 0) % blk
    c = lax.broadcasted_iota(jnp.int32, shape, 1) % (2 * blk)
    lo = jnp.where(i == 0, blk - 1, -1)
    ok = jnp.logical_and(c > jnp.maximum(r, lo), c <= r + blk)
    k_all = jnp.concatenate([kp_ref[...], kc_ref[...]], axis=0)
    v_all = jnp.concatenate([vp_ref[...], vc_ref[...]], axis=0)
    group = (q_ref.shape[1] // HEAD_DIM) // n_kv
    for cch in range(n_kv // 2):
        kc = k_all[:, cch * LANES:(cch + 1) * LANES]
        vc = v_all[:, cch * LANES:(cch + 1) * LANES]
        kr = pltpu.roll(kc, HEAD_DIM, axis=1)
        vr = pltpu.roll(vc, HEAD_DIM, axis=1)
        for odd in range(2):
            j = 2 * cch + odd
            kb = _pair_blocks(kc, kr, odd, 0)
            vb = _pair_blocks(vc, vr, odd, 0)
            h0 = group * j
            c0 = h0 * HEAD_DIM
            q2 = jnp.concatenate([q_ref[:, c0:c0 + LANES], q_ref[:, c0 + LANES:c0 + 2 * LANES]], axis=0)
            o = _attend(q2, kb, vb, ok, sinks_ref, h0, blk).astype(o_ref.dtype)
            o_ref[:, c0:c0 + LANES] = o[:blk]
            o_ref[:, c0 + LANES:c0 + 2 * LANES] = o[blk:]


def _attn_prompt(sinks, q, k, v, *, batch, seq, n_kv):
    blk = WINDOW
    nb = seq // blk
    aw, kw = q.shape[1], k.shape[1]
    cur = lambda w: pl.BlockSpec((blk, w), lambda b, i: (b * nb + i, 0))
    prev = lambda w: pl.BlockSpec((blk, w), lambda b, i: (b * nb + jnp.maximum(i - 1, 0), 0))
    return pl.pallas_call(
        functools.partial(_attn_prompt_kernel, n_kv=n_kv),
        grid=(batch, nb),
        in_specs=[pl.BlockSpec(memory_space=pltpu.SMEM), cur(aw), prev(kw), cur(kw), prev(kw), cur(kw)],
        out_specs=cur(aw),
        out_shape=jax.ShapeDtypeStruct((batch * seq, aw), BF16),
        compiler_params=_params("parallel", "arbitrary"),
        name="attn_prompt",
    )(sinks, q, k, k, v, v)


def _attn_sample_kernel(sinks_ref, q_ref, kn_ref, vn_ref, ck_ref, cv_ref, o_ref, nk_ref, nv_ref, qf_ref, of_ref, *, n_kv, t_new):
    g_seqs, _, kw = ck_ref.shape
    n_heads = q_ref.shape[1] // HEAD_DIM
    group = n_heads // n_kv
    keep = WINDOW - t_new
    n_rows = n_heads * t_new
    n_keys = 2 * WINDOW
    qf_ref[...] = q_ref[...].astype(F32)
    t = lax.broadcasted_iota(jnp.int32, (n_rows, n_keys), 0) % t_new
    c = lax.broadcasted_iota(jnp.int32, (n_rows, n_keys), 1)
    ok = jnp.logical_and(c > t, c <= t + WINDOW)
    sink = jnp.concatenate([jnp.full((t_new, 1), sinks_ref[h], F32) for h in range(n_heads)], axis=0)
    low = lax.broadcasted_iota(jnp.int32, (t_new, LANES), 1) < HEAD_DIM
    zero_tile = jnp.zeros((t_new, LANES), F32)
    zero_keys = jnp.zeros((n_keys - WINDOW - t_new, kw), F32)

    def one_seq(g, carry):
        r0 = pl.multiple_of(g * t_new, t_new)
        k_new = kn_ref[pl.ds(r0, t_new), :]
        v_new = vn_ref[pl.ds(r0, t_new), :]
        k_old = ck_ref[g]
        v_old = cv_ref[g]
        nk_ref[g, :keep, :] = k_old[t_new:]
        nk_ref[g, keep:, :] = k_new
        nv_ref[g, :keep, :] = v_old[t_new:]
        nv_ref[g, keep:, :] = v_new
        keys = jnp.concatenate([k_old, k_new, zero_keys], axis=0).astype(BF16)
        vals = jnp.concatenate([v_old, v_new, zero_keys], axis=0).astype(BF16)
        q_rows = []
        for h in range(n_heads):
            j = h // group
            src = qf_ref[pl.ds(r0, t_new), (h // 2) * LANES:(h // 2 + 1) * LANES]
            if h % 2 != j % 2:
                src = pltpu.roll(src, HEAD_DIM, axis=1)
            half = jnp.where(low, src, 0.0) if j % 2 == 0 else jnp.where(low, 0.0, src)
            tiles = [zero_tile] * (kw // LANES)
            tiles[j // 2] = half
            q_rows.append(jnp.concatenate(tiles, axis=1))
        qb = jnp.concatenate(q_rows, axis=0).astype(BF16)
        s = lax.dot_general(qb, keys, (((1,), (1,)), ((), ())), preferred_element_type=F32)
        s = jnp.where(ok, s, NEG)
        m = jnp.maximum(jnp.max(s, axis=-1, keepdims=True), sink)
        p = jnp.exp(s - m)
        denom = jnp.sum(p, axis=-1, keepdims=True) + jnp.exp(sink - m)
        o = jnp.dot(p.astype(BF16), vals, preferred_element_type=F32) / denom
        for pair in range(n_heads // 2):
            j = (2 * pair) // group
            lanes = slice((j // 2) * LANES, (j // 2 + 1) * LANES)
            a = o[2 * pair * t_new:(2 * pair + 1) * t_new, lanes]
            b = o[(2 * pair + 1) * t_new:(2 * pair + 2) * t_new, lanes]
            if j % 2 == 0:
                b = pltpu.roll(b, HEAD_DIM, axis=1)
            else:
                a = pltpu.roll(a, HEAD_DIM, axis=1)
            of_ref[pl.ds(r0, t_new), pair * LANES:(pair + 1) * LANES] = jnp.where(low, a, b)
        return carry

    lax.fori_loop(0, g_seqs, one_seq, 0, unroll=4)
    o_ref[...] = of_ref[...].astype(o_ref.dtype)


def _attn_sample(sinks, q, k, v, cache_k, cache_v, *, t_new, n_kv, g_seqs=16):
    n_seq = cache_k.shape[0]
    aw, kw = q.shape[1], k.shape[1]
    tok = lambda w: pl.BlockSpec((g_seqs * t_new, w), lambda i: (i, 0))
    cache = pl.BlockSpec((g_seqs, WINDOW, kw), lambda i: (i, 0, 0))
    return pl.pallas_call(
        functools.partial(_attn_sample_kernel, n_kv=n_kv, t_new=t_new),
        grid=(n_seq // g_seqs,),
        in_specs=[pl.BlockSpec(memory_space=pltpu.SMEM), tok(aw), tok(kw), tok(kw), cache, cache],
        out_specs=[tok(aw), cache, cache],
        out_shape=[
            jax.ShapeDtypeStruct((n_seq * t_new, aw), BF16),
            jax.ShapeDtypeStruct(cache_k.shape, F32),
            jax.ShapeDtypeStruct(cache_v.shape, F32),
        ],
        scratch_shapes=[pltpu.VMEM((g_seqs * t_new, aw), F32), pltpu.VMEM((g_seqs * t_new, aw), F32)],
        compiler_params=_params("parallel"),
        name="attn_sample",
    )(sinks, q, k, v, cache_k, cache_v)


def _ln_silu(y, g, b):
    yc = y - jnp.mean(y, axis=-1, keepdims=True)
    var = jnp.mean(yc * yc, axis=-1, keepdims=True)
    return jax.nn.silu(yc * lax.rsqrt(var + LN_EPS) * g + b)


def _conv_prompt_kernel(cur_ref, halo_ref, w_ref, b_ref, g_ref, beta_ref, o_ref, xc_ref, y_ref, *, width, rows_per, lanes_per):
    tc, ch = cur_ref.shape
    halo = halo_ref.shape[0]
    i = pl.program_id(1)
    xc_ref[:halo, :] = jnp.where(i == 0, 0.0, halo_ref[...])
    xc_ref[halo:, :] = cur_ref[...]
    base = halo - (width - 1)

    def col_chunk(cc, carry):
        l0 = pl.multiple_of(cc * lanes_per, lanes_per)
        cols = pl.ds(l0, lanes_per)
        for rr in range(tc // rows_per):
            acc = jnp.broadcast_to(b_ref[:, cols], (rows_per, lanes_per))
            for phase in range(SUBLANES):
                n_win = rows_per + (SUBLANES if phase else 0)
                part = None
                for j in range(width):
                    if (base + j) % SUBLANES != phase:
                        continue
                    start = rr * rows_per + ((base + j) // SUBLANES) * SUBLANES
                    term = w_ref[pl.ds(j, 1), cols] * xc_ref[pl.ds(start, n_win), cols]
                    part = term if part is None else part + term
                if part is None:
                    continue
                if phase:
                    part = pltpu.roll(part, n_win - phase, axis=0)
                acc = acc + part[:rows_per]
            y_ref[pl.ds(rr * rows_per, rows_per), cols] = acc
        return carry

    lax.fori_loop(0, ch // lanes_per, col_chunk, 0)
    o_ref[...] = _ln_silu(y_ref[...], g_ref[...], beta_ref[...]).astype(o_ref.dtype)


def _conv_prompt(glu, w, b, g, beta, *, batch, seq, tc=256, halo=32):
    ch = glu.shape[1]
    width = w.shape[0]
    nt = seq // tc
    per = tc // halo
    cur = pl.BlockSpec((tc, ch), lambda bi, i: (bi * nt + i, 0))
    prev = pl.BlockSpec((halo, ch), lambda bi, i: (jnp.maximum((bi * nt + i) * per - 1, 0), 0))
    kern = functools.partial(_conv_prompt_kernel, width=width, rows_per=64, lanes_per=LANES)
    return pl.pallas_call(
        kern,
        grid=(batch, nt),
        in_specs=[cur, prev, _resident(w.shape), _resident((1, ch)), _resident((1, ch)), _resident((1, ch))],
        out_specs=cur,
        out_shape=jax.ShapeDtypeStruct((batch * seq, ch), BF16),
        scratch_shapes=[pltpu.VMEM((halo + tc, ch), F32), pltpu.VMEM((tc, ch), F32)],
        compiler_params=_params("parallel", "arbitrary"),
        name="conv_prompt",
    )(glu, glu, w, b, g, beta)


def _append_state(ns_ref, st_ref, g, new, t_new):
    n_past = st_ref.shape[1]
    keep = n_past - t_new
    full = (keep // SUBLANES) * SUBLANES
    rem = keep - full
    if full:
        ns_ref[g, :full, :] = st_ref[g, t_new:t_new + full, :]
    if rem:
        rolled = pltpu.roll(new, rem, axis=0)
        ns_ref[g, full:full + SUBLANES, :] = rolled
        ns_ref[g, full:keep, :] = st_ref[g, t_new + full:n_past, :]
        ns_ref[g, full + SUBLANES:, :] = rolled[:rem]
    else:
        ns_ref[g, full:, :] = new


def _conv_sample_kernel(new_ref, st_ref, toep_ref, b_ref, g_ref, beta_ref, o_ref, ns_ref, of_ref, *, t_new):
    g_seqs, n_past, ch = st_ref.shape

    def one_seq(g, carry):
        r0 = pl.multiple_of(g * t_new, t_new)
        new = new_ref[pl.ds(r0, t_new), :]
        acc = jnp.broadcast_to(b_ref[...], (t_new, ch))
        for r in range(n_past):
            acc = acc + toep_ref[r] * st_ref[g, pl.ds(r, 1), :]
        for r in range(t_new):
            acc = acc + toep_ref[n_past + r] * new_ref[pl.ds(r0 + r, 1), :]
        of_ref[pl.ds(r0, t_new), :] = _ln_silu(acc, g_ref[...], beta_ref[...])
        _append_state(ns_ref, st_ref, g, new, t_new)
        return carry

    lax.fori_loop(0, g_seqs, one_seq, 0, unroll=2)
    o_ref[...] = of_ref[...].astype(o_ref.dtype)


def _conv_sample(glu, state, toep, b, g, beta, *, t_new, g_seqs=16):
    n_seq, n_past, ch = state.shape
    return pl.pallas_call(
        functools.partial(_conv_sample_kernel, t_new=t_new),
        grid=(n_seq // g_seqs,),
        in_specs=[
            pl.BlockSpec((g_seqs * t_new, ch), lambda i: (i, 0)),
            pl.BlockSpec((g_seqs, n_past, ch), lambda i: (i, 0, 0)),
            _resident(toep.shape), _resident((1, ch)), _resident((1, ch)), _resident((1, ch)),
        ],
        out_specs=[
            pl.BlockSpec((g_seqs * t_new, ch), lambda i: (i, 0)),
            pl.BlockSpec((g_seqs, n_past, ch), lambda i: (i, 0, 0)),
        ],
        out_shape=[jax.ShapeDtypeStruct((n_seq * t_new, ch), BF16), jax.ShapeDtypeStruct(state.shape, F32)],
        scratch_shapes=[pltpu.VMEM((g_seqs * t_new, ch), F32)],
        compiler_params=_params("parallel"),
        name="conv_sample",
    )(glu, state, toep, b, g, beta)


def _out_proj_kernel(x_ref, a_ref, c_ref, w_ref, o_ref):
    aw = a_ref.shape[1]
    o_ref[...] = (x_ref[...]
                  + jnp.dot(a_ref[...], w_ref[:aw, :].astype(BF16), preferred_element_type=F32)
                  + jnp.dot(c_ref[...], w_ref[aw:, :].astype(BF16), preferred_element_type=F32))


def _out_proj(x, attn, conv, w, *, slab, tm=512):
    n, d = x.shape
    rows = lambda width: pl.BlockSpec((tm, width), lambda i: (i, 0))
    return pl.pallas_call(
        _out_proj_kernel,
        grid=(n // tm,),
        in_specs=[rows(d), rows(attn.shape[1]), rows(conv.shape[1]), _resident_slab(w.shape, slab)],
        out_specs=rows(d),
        out_shape=jax.ShapeDtypeStruct((n, d), F32),
        compiler_params=_params("parallel"),
        name="out_proj",
    )(x, attn, conv, w)


def _ffn_kernel(x_ref, g_ref, wg_ref, wu_ref, wd_ref, o_ref, h_ref):
    @pl.when(pl.program_id(1) == 0)
    def _():
        x = x_ref[...]
        h_ref[...] = _rms(x, g_ref[...]).astype(BF16)
        o_ref[...] = x

    h = h_ref[...]
    a = jnp.dot(h, wg_ref[...].astype(BF16), preferred_element_type=F32)
    u = jnp.dot(h, wu_ref[...].astype(BF16), preferred_element_type=F32)
    act = (jax.nn.silu(a) * u).astype(BF16)
    o_ref[...] += jnp.dot(act, wd_ref[...].astype(BF16), preferred_element_type=F32)


def _ffn(x, g, wg, wu, wd, *, layer, tm=1024, tf=256):
    n, d = x.shape
    dff = wg.shape[2]
    return pl.pallas_call(
        _ffn_kernel,
        grid=(n // tm, dff // tf),
        in_specs=[
            pl.BlockSpec((tm, d), lambda i, f: (i, 0)),
            _resident((1, d)),
            pl.BlockSpec((None, d, tf), lambda i, f: (layer, 0, f)),
            pl.BlockSpec((None, d, tf), lambda i, f: (layer, 0, f)),
            pl.BlockSpec((None, tf, d), lambda i, f: (layer, f, 0)),
        ],
        out_specs=pl.BlockSpec((tm, d), lambda i, f: (i, 0)),
        out_shape=jax.ShapeDtypeStruct((n, d), F32),
        scratch_shapes=[pltpu.VMEM((tm, d), BF16)],
        compiler_params=_params("parallel", "arbitrary"),
        name="ffn",
    )(x, g, wg, wu, wd)


def _pool_prompt_kernel(cur_ref, halo_ref, g_ref, pw_ref, ps_ref, o_ref, tail_ref, xp_ref, *, rows_per):
    tp, d = cur_ref.shape
    halo = halo_ref.shape[0]
    i = pl.program_id(1)
    gain = g_ref[...]
    xp_ref[:halo, :] = jnp.where(i == 0, 0.0, _rms(halo_ref[...], gain))
    xp_ref[halo:, :] = _rms(cur_ref[...], gain)
    tail_ref[0] = xp_ref[tp:, :]
    pg = d // len(POOL_WINDOWS)
    for rr in range(tp // rows_per):
        r0 = halo + rr * rows_per
        pos = i * tp + rr * rows_per + lax.broadcasted_iota(jnp.int32, (rows_per, 1), 0)
        for gi, w in enumerate(POOL_WINDOWS):
            cols = slice(gi * pg, (gi + 1) * pg)
            ext = xp_ref[r0 - halo:r0 + rows_per, cols]
            h = ext[halo:]
            span = 1
            while span < w:
                ext = ext + pltpu.roll(ext, span, axis=0)
                span *= 2
            inv_count = 1.0 / jnp.minimum(pos + 1, w).astype(F32)
            dpool = (ext[halo:] * inv_count - h).astype(BF16)
            y = jnp.dot(dpool, pw_ref[gi].astype(BF16), preferred_element_type=F32)
            rows = slice(rr * rows_per, (rr + 1) * rows_per)
            o_ref[rows, cols] = cur_ref[rows, cols] + y * ps_ref[:, cols]


def _pool_prompt(x, g, pw, ps, *, slab, batch, seq, tp=512, halo=16):
    assert all(w & (w - 1) == 0 and w <= halo for w in POOL_WINDOWS)
    d = x.shape[1]
    nt = seq // tp
    per = tp // halo
    cur = pl.BlockSpec((tp, d), lambda bi, i: (bi * nt + i, 0))
    prev = pl.BlockSpec((halo, d), lambda bi, i: (jnp.maximum((bi * nt + i) * per - 1, 0), 0))
    return pl.pallas_call(
        functools.partial(_pool_prompt_kernel, rows_per=128),
        grid=(batch, nt),
        in_specs=[cur, prev, _resident((1, d)), _resident_slab(pw.shape, slab), _resident((1, d))],
        out_specs=[cur, pl.BlockSpec((1, halo, d), lambda bi, i: (bi, 0, 0))],
        out_shape=[jax.ShapeDtypeStruct((batch * seq, d), F32), jax.ShapeDtypeStruct((batch, halo, d), F32)],
        scratch_shapes=[pltpu.VMEM((halo + tp, d), F32)],
        compiler_params=_params("parallel", "arbitrary"),
        name="pool_prompt",
    )(x, x, g, pw, ps)


def _pool_sample_kernel(x_ref, st_ref, g_ref, pw_ref, ps_ref, o_ref, ns_ref, h_ref, dp_ref, *, t_new, start_pos):
    g_seqs, n_past, d = st_ref.shape
    pg = d // len(POOL_WINDOWS)
    h_ref[...] = _rms(x_ref[...], g_ref[...])
    t = lax.broadcasted_iota(jnp.int32, (t_new, 1), 0)

    def one_seq(g, carry):
        r0 = pl.multiple_of(g * t_new, t_new)
        h = h_ref[pl.ds(r0, t_new), :]
        for gi, w in enumerate(POOL_WINDOWS):
            cols = slice(gi * pg, (gi + 1) * pg)
            inv = 1.0 / jnp.minimum(start_pos + t + 1, w).astype(F32)
            tot = jnp.zeros((t_new, pg), F32)
            for r in range(max(0, n_past + 1 - w), n_past + t_new):
                if r < n_past:
                    row = st_ref[g, pl.ds(r, 1), cols]
                else:
                    row = h_ref[pl.ds(r0 + (r - n_past), 1), cols]
                hit = jnp.logical_and(t + n_past - w < r, r <= t + n_past)
                tot = tot + jnp.where(hit, inv, 0.0) * row
            dp_ref[pl.ds(r0, t_new), cols] = tot - h[:, cols]
        _append_state(ns_ref, st_ref, g, h, t_new)
        return carry

    lax.fori_loop(0, g_seqs, one_seq, 0, unroll=2)
    for gi in range(len(POOL_WINDOWS)):
        cols = slice(gi * pg, (gi + 1) * pg)
        y = jnp.dot(dp_ref[:, cols].astype(BF16), pw_ref[gi].astype(BF16), preferred_element_type=F32)
        o_ref[:, cols] = x_ref[:, cols] + y * ps_ref[:, cols]


def _pool_sample(x, state, g, pw, ps, *, slab, t_new, g_seqs=16):
    n_seq, n_past, d = state.shape
    assert t_new == SUBLANES and SUBLANES <= n_past < 2 * SUBLANES
    kern = functools.partial(_pool_sample_kernel, t_new=t_new, start_pos=PAST_LEN)
    return pl.pallas_call(
        kern,
        grid=(n_seq // g_seqs,),
        in_specs=[
            pl.BlockSpec((g_seqs * t_new, d), lambda i: (i, 0)),
            pl.BlockSpec((g_seqs, n_past, d), lambda i: (i, 0, 0)),
            _resident((1, d)), _resident_slab(pw.shape, slab), _resident((1, d)),
        ],
        out_specs=[
            pl.BlockSpec((g_seqs * t_new, d), lambda i: (i, 0)),
            pl.BlockSpec((g_seqs, n_past, d), lambda i: (i, 0, 0)),
        ],
        out_shape=[jax.ShapeDtypeStruct((n_seq * t_new, d), F32), jax.ShapeDtypeStruct(state.shape, F32)],
        scratch_shapes=[pltpu.VMEM((g_seqs * t_new, d), F32), pltpu.VMEM((g_seqs * t_new, d), F32)],
        compiler_params=_params("parallel"),
        name="pool_sample",
    )(x, state, g, pw, ps)


def _toeplitz(conv_w, t_new):
    width = conv_w.shape[0]
    idx = jnp.arange(width - 1 + t_new)[:, None] - jnp.arange(t_new)[None, :]
    valid = jnp.logical_and(idx >= 0, idx < width)
    return jnp.where(valid[:, :, None], conv_w[jnp.clip(idx, 0, width - 1)], 0.0)


def kernel(x_prompt, x_sample, cache_k, cache_v, state_conv, state_pool, norm_mix, w_in, q_norm, k_norm, sinks,
           conv_w, conv_b, conv_ln_g, conv_ln_b, w_out, pool_w, pool_scale, norm_ffn, w_gate, w_up, w_down):
    batch, seq, d = x_prompt.shape
    n_seq, t_new, _ = x_sample.shape
    n_kv = cache_k.shape[3]
    kv_w = n_kv * HEAD_DIM
    conv_ch = state_conv.shape[-1]
    attn_w = w_in.shape[-1] - 2 * kv_w - 2 * conv_ch
    n_prompt = batch * seq
    row = lambda a: a.reshape(1, -1)

    xp = x_prompt.reshape(n_prompt, d)
    xs = x_sample.reshape(n_seq * t_new, d)
    depth = norm_mix.shape[0]
    outs = {}
    for layer in range(depth):
        i = layer // 2
        if layer % 2 == 0:
            reps = MXU_DIM // HEAD_DIM
            proj = functools.partial(
                _in_proj, g=row(norm_mix[layer]), w=w_in, slab=i, qg=row(jnp.tile(q_norm[i], reps)),
                kg=row(jnp.tile(k_norm[i], reps)), attn_w=attn_w, kv_w=kv_w, conv_ch=conv_ch)
            q_p, k_p, v_p, glu_p = proj(xp)
            q_s, k_s, v_s, glu_s = proj(xs)
            attn_p = _attn_prompt(sinks[i], q_p, k_p, v_p, batch=batch, seq=seq, n_kv=n_kv)
            attn_s, nk_s, nv_s = _attn_sample(
                sinks[i], q_s, k_s, v_s, cache_k[i].reshape(n_seq, WINDOW, kv_w), cache_v[i].reshape(n_seq, WINDOW, kv_w),
                t_new=t_new, n_kv=n_kv)
            ln = (row(conv_b[i]), row(conv_ln_g[i]), row(conv_ln_b[i]))
            conv_p = _conv_prompt(glu_p, conv_w[i], *ln, batch=batch, seq=seq)
            conv_s, nc_s = _conv_sample(glu_s, state_conv[i], _toeplitz(conv_w[i], t_new), *ln, t_new=t_new)
            xp = _out_proj(xp, attn_p, conv_p, w_out, slab=i)
            xs = _out_proj(xs, attn_s, conv_s, w_out, slab=i)
            last = lambda a, n: a.reshape(batch, seq, -1)[:, -n:]
            outs.setdefault("kp", []).append(last(k_p, WINDOW).reshape(batch, WINDOW, n_kv, HEAD_DIM))
            outs.setdefault("vp", []).append(last(v_p, WINDOW).reshape(batch, WINDOW, n_kv, HEAD_DIM))
            outs.setdefault("cp", []).append(last(glu_p, conv_w.shape[1] - 1))
            outs.setdefault("ks", []).append(nk_s.reshape(n_seq, WINDOW, n_kv, HEAD_DIM))
            outs.setdefault("vs", []).append(nv_s.reshape(n_seq, WINDOW, n_kv, HEAD_DIM))
            outs.setdefault("cs", []).append(nc_s)
        else:
            args = (row(norm_mix[layer]), pool_w, row(pool_scale[i]))
            n_past = state_pool.shape[2]
            xp, tail = _pool_prompt(xp, *args, slab=i, batch=batch, seq=seq)
            xs, np_s = _pool_sample(xs, state_pool[i], *args, slab=i, t_new=t_new)
            outs.setdefault("pp", []).append(tail[:, -n_past:])
            outs.setdefault("ps", []).append(np_s)
        ffn = functools.partial(_ffn, g=row(norm_ffn[layer]), wg=w_gate, wu=w_up, wd=w_down, layer=layer)
        xp = ffn(xp)
        xs = ffn(xs)
    st = lambda name: jnp.stack(outs[name])
    return (xp.reshape(batch, seq, d), xs.reshape(n_seq, t_new, d),
            st("kp"), st("vp"), st("cp"), st("pp"), st("ks"), st("vs"), st("cs"), st("ps"))
```

```python
import functools

import jax
import jax.numpy as jnp
from jax import lax
from jax.experimental import pallas as pl
from jax.experimental.pallas import tpu as pltpu

F32 = jnp.float32
BF16 = jnp.bfloat16

HEAD_DIM = 64
WINDOW = 128
PAST_LEN = 8192
POOL_WINDOWS = (2, 4, 8, 16)
RMS_EPS = 1e-6
LN_EPS = 1e-5
NEG = -1e30

LANES = 128
SUBLANES = 8
MXU_DIM = 256
VMEM_LIMIT = 56 * 1024 * 1024


def _params(*semantics):
    return pltpu.CompilerParams(dimension_semantics=semantics, vmem_limit_bytes=VMEM_LIMIT)


def _rms(x, g):
    ms = jnp.mean(x * x, axis=-1, keepdims=True)
    return x * lax.rsqrt(ms + RMS_EPS) * g


def _resident(shape):
    zeros = (0,) * len(shape)
    return pl.BlockSpec(shape, lambda *_: zeros, pipeline_mode=pl.Buffered(1))


def _resident_slab(shape, slab):
    index = (slab,) + (0,) * (len(shape) - 1)
    return pl.BlockSpec((None,) + tuple(shape[1:]), lambda *_: index, pipeline_mode=pl.Buffered(1))


def _in_proj_kernel(x_ref, g_ref, w_ref, qg_ref, kg_ref, q_ref, k_ref, v_ref, glu_ref, *, attn_w, kv_w, conv_ch):
    h = _rms(x_ref[...], g_ref[...]).astype(BF16)
    row = lax.broadcasted_iota(jnp.int32, (MXU_DIM, MXU_DIM), 0) // HEAD_DIM
    col = lax.broadcasted_iota(jnp.int32, (MXU_DIM, MXU_DIM), 1) // HEAD_DIM
    pmat = jnp.where(row == col, 1.0 / HEAD_DIM, 0.0).astype(BF16)

    def proj(c0):
        return jnp.dot(h, w_ref[:, c0:c0 + MXU_DIM].astype(BF16), preferred_element_type=F32)

    def head_rms(z):
        sq = z * z
        hi = sq.astype(BF16)
        lo = (sq - hi.astype(F32)).astype(BF16)
        ms = jnp.dot(hi, pmat, preferred_element_type=F32) + jnp.dot(lo, pmat, preferred_element_type=F32)
        return z * lax.rsqrt(ms + RMS_EPS)

    scale = HEAD_DIM ** -0.5

    def put_q(c0, z):
        q_ref[:, c0:c0 + MXU_DIM] = (head_rms(z) * qg_ref[...] * scale).astype(q_ref.dtype)

    def put_k(c0, z):
        k_ref[:, c0:c0 + MXU_DIM] = head_rms(z) * kg_ref[...]

    def put_v(c0, z):
        v_ref[:, c0:c0 + MXU_DIM] = z

    def put_glu(c0, a, gate):
        glu_ref[:, c0:c0 + MXU_DIM] = a * jax.nn.sigmoid(gate)

    units = [((c0,), functools.partial(put_q, c0)) for c0 in range(0, attn_w, MXU_DIM)]
    units += [((attn_w + c0,), functools.partial(put_k, c0)) for c0 in range(0, kv_w, MXU_DIM)]
    units += [((attn_w + kv_w + c0,), functools.partial(put_v, c0)) for c0 in range(0, kv_w, MXU_DIM)]
    units += [((attn_w + 2 * kv_w + c0, attn_w + 2 * kv_w + conv_ch + c0), functools.partial(put_glu, c0))
              for c0 in range(0, conv_ch, MXU_DIM)]
    pending = None
    for cols, epilogue in units:
        zs = [proj(c0) for c0 in cols]
        if pending is not None:
            pending[0](*pending[1])
        pending = (epilogue, zs)
    pending[0](*pending[1])


def _in_proj(x, g, w, qg, kg, *, slab, attn_w, kv_w, conv_ch, tm=512):
    n, d = x.shape
    kern = functools.partial(_in_proj_kernel, attn_w=attn_w, kv_w=kv_w, conv_ch=conv_ch)
    rows = lambda width: pl.BlockSpec((tm, width), lambda i: (i, 0))
    return pl.pallas_call(
        kern,
        grid=(n // tm,),
        in_specs=[rows(d), _resident((1, d)), _resident_slab(w.shape, slab), _resident((1, MXU_DIM)), _resident((1, MXU_DIM))],
        out_specs=[rows(attn_w), rows(kv_w), rows(kv_w), rows(conv_ch)],
        out_shape=[
            jax.ShapeDtypeStruct((n, attn_w), BF16),
            jax.ShapeDtypeStruct((n, kv_w), F32),
            jax.ShapeDtypeStruct((n, kv_w), F32),
            jax.ShapeDtypeStruct((n, conv_ch), F32),
        ],
        compiler_params=_params("parallel"),
        name="in_proj",
    )(x, g, w, qg, kg)


def _pair_blocks(chunk, rolled, odd):
    lane = lax.broadcasted_iota(jnp.int32, chunk.shape, 1)
    low = lane < HEAD_DIM
    if odd:
        top = jnp.where(low, rolled, 0.0)
        bot = jnp.where(low, 0.0, chunk)
    else:
        top = jnp.where(low, chunk, 0.0)
        bot = jnp.where(low, 0.0, rolled)
    return jnp.concatenate([top, bot], axis=0).astype(BF16)


def _softmax_pv(s, ok, sink_a, sink_b, vb_ones):
    half = s.shape[1] // 2
    s = jnp.where(ok, s, NEG)
    sa, sb = s[:, :half], s[:, half:]
    ma = jnp.maximum(jnp.max(sa, axis=-1, keepdims=True), sink_a)
    mb = jnp.maximum(jnp.max(sb, axis=-1, keepdims=True), sink_b)
    p = jnp.concatenate([jnp.exp(sa - ma), jnp.exp(sb - mb)], axis=1).astype(BF16)
    r = jnp.dot(p, vb_ones, preferred_element_type=F32)
    lane = lax.broadcasted_iota(jnp.int32, (s.shape[0], LANES), 1)
    sink_term = jnp.where(lane < HEAD_DIM, jnp.exp(sink_a - ma), jnp.exp(sink_b - mb))
    return r[:, :LANES] / (r[:, LANES:] + sink_term)


def _attend(q2, kb, vb, ok, sinks_ref, h0, rows):
    s = lax.dot_general(q2, kb, (((1,), (1,)), ((), ())), preferred_element_type=F32)
    row = lax.broadcasted_iota(jnp.int32, (2 * rows, 1), 0)
    first = row < rows
    sink_a = jnp.where(first, sinks_ref[h0], sinks_ref[h0 + 2])
    sink_b = jnp.where(first, sinks_ref[h0 + 1], sinks_ref[h0 + 3])
    return _softmax_pv(s, ok, sink_a, sink_b, vb)


def _attn_prompt_kernel(sinks_ref, q_ref, kp_ref, kc_ref, vp_ref, vc_ref, o_ref, *, n_kv):
    blk = q_ref.shape[0]
    i = pl.program_id(1)
    shape = (2 * blk, 4 * blk)
    r = lax.broadcasted_iota(jnp.int32, shape, 0) % blk
    c = lax.broadcasted_iota(jnp.int32, shape, 1) % (2 * blk)
    lo = jnp.where(i == 0, blk - 1, -1)
    ok = jnp.logical_and(c > jnp.maximum(r, lo), c <= r + blk)
    k_all = jnp.concatenate([kp_ref[...], kc_ref[...]], axis=0)
    v_all = jnp.concatenate([vp_ref[...], vc_ref[...]], axis=0)
    group = (q_ref.shape[1] // HEAD_DIM) // n_kv
    key_row = lax.broadcasted_iota(jnp.int32, (4 * blk, LANES), 0)
    key_lane = lax.broadcasted_iota(jnp.int32, (4 * blk, LANES), 1)
    ones_blocks = jnp.where((key_row < 2 * blk) == (key_lane < HEAD_DIM), 1.0, 0.0).astype(BF16)
    for cch in range(n_kv // 2):
        kc = k_all[:, cch * LANES:(cch + 1) * LANES]
        vc = v_all[:, cch * LANES:(cch + 1) * LANES]
        kr = pltpu.roll(kc, HEAD_DIM, axis=1)
        vr = pltpu.roll(vc, HEAD_DIM, axis=1)
        for odd in range(2):
            j = 2 * cch + odd
            kb = _pair_blocks(kc, kr, odd)
            vb = jnp.concatenate([_pair_blocks(vc, vr, odd), ones_blocks], axis=1)
            h0 = group * j
            c0 = h0 * HEAD_DIM
            q2 = jnp.concatenate([q_ref[:, c0:c0 + LANES], q_ref[:, c0 + LANES:c0 + 2 * LANES]], axis=0)
            o = _attend(q2, kb, vb, ok, sinks_ref, h0, blk).astype(o_ref.dtype)
            o_ref[:, c0:c0 + LANES] = o[:blk]
            o_ref[:, c0 + LANES:c0 + 2 * LANES] = o[blk:]


def _attn_prompt(sinks, q, k, v, *, batch, seq, n_kv):
    blk = WINDOW
    nb = seq // blk
    aw, kw = q.shape[1], k.shape[1]
    cur = lambda w: pl.BlockSpec((blk, w), lambda b, i: (b * nb + i, 0))
    prev = lambda w: pl.BlockSpec((blk, w), lambda b, i: (b * nb + jnp.maximum(i - 1, 0), 0))
    return pl.pallas_call(
        functools.partial(_attn_prompt_kernel, n_kv=n_kv),
        grid=(batch, nb),
        in_specs=[pl.BlockSpec(memory_space=pltpu.SMEM), cur(aw), prev(kw), cur(kw), prev(kw), cur(kw)],
        out_specs=cur(aw),
        out_shape=jax.ShapeDtypeStruct((batch * seq, aw), BF16),
        compiler_params=_params("parallel", "arbitrary"),
        name="attn_prompt",
    )(sinks, q, k, k, v, v)


def _attn_sample_kernel(sinks_ref, q_ref, kn_ref, vn_ref, ck_ref, cv_ref, o_ref, nk_ref, nv_ref, qf_ref, of_ref, *, n_kv, t_new):
    g_seqs, _, kw = ck_ref.shape
    n_heads = q_ref.shape[1] // HEAD_DIM
    group = n_heads // n_kv
    keep = WINDOW - t_new
    n_rows = n_heads * t_new
    n_keys = 2 * WINDOW
    qf_ref[...] = q_ref[...].astype(F32)
    t = lax.broadcasted_iota(jnp.int32, (n_rows, n_keys), 0) % t_new
    c = lax.broadcasted_iota(jnp.int32, (n_rows, n_keys), 1)
    ok = jnp.logical_and(c > t, c <= t + WINDOW)
    sink = jnp.concatenate([jnp.full((t_new, 1), sinks_ref[h], F32) for h in range(n_heads)], axis=0)
    low = lax.broadcasted_iota(jnp.int32, (t_new, LANES), 1) < HEAD_DIM
    zero_tile = jnp.zeros((t_new, LANES), F32)
    zero_keys = jnp.zeros((n_keys - WINDOW - t_new, kw), F32)

    def one_seq(g, carry):
        r0 = pl.multiple_of(g * t_new, t_new)
        k_new = kn_ref[pl.ds(r0, t_new), :]
        v_new = vn_ref[pl.ds(r0, t_new), :]
        k_old = ck_ref[g]
        v_old = cv_ref[g]
        nk_ref[g, :keep, :] = k_old[t_new:]
        nk_ref[g, keep:, :] = k_new
        nv_ref[g, :keep, :] = v_old[t_new:]
        nv_ref[g, keep:, :] = v_new
        keys = jnp.concatenate([k_old, k_new, zero_keys], axis=0).astype(BF16)
        vals = jnp.concatenate([v_old, v_new, zero_keys], axis=0).astype(BF16)
        vals = jnp.concatenate([vals, jnp.ones((n_keys, LANES), BF16)], axis=1)
        q_rows = []
        for h in range(n_heads):
            j = h // group
            src = qf_ref[pl.ds(r0, t_new), (h // 2) * LANES:(h // 2 + 1) * LANES]
            if h % 2 != j % 2:
                src = pltpu.roll(src, HEAD_DIM, axis=1)
            half = jnp.where(low, src, 0.0) if j % 2 == 0 else jnp.where(low, 0.0, src)
            tiles = [zero_tile] * (kw // LANES)
            tiles[j // 2] = half
            q_rows.append(jnp.concatenate(tiles, axis=1))
        qb = jnp.concatenate(q_rows, axis=0).astype(BF16)
        s = lax.dot_general(qb, keys, (((1,), (1,)), ((), ())), preferred_element_type=F32)
        s = jnp.where(ok, s, NEG)
        m = jnp.maximum(jnp.max(s, axis=-1, keepdims=True), sink)
        r = jnp.dot(jnp.exp(s - m).astype(BF16), vals, preferred_element_type=F32)
        denom = r[:, kw:] + jnp.exp(sink - m)
        o = r[:, :kw] / jnp.concatenate([denom] * (kw // LANES), axis=1)
        for pair in range(n_heads // 2):
            j = (2 * pair) // group
            lanes = slice((j // 2) * LANES, (j // 2 + 1) * LANES)
            a = o[2 * pair * t_new:(2 * pair + 1) * t_new, lanes]
            b = o[(2 * pair + 1) * t_new:(2 * pair + 2) * t_new, lanes]
            if j % 2 == 0:
                b = pltpu.roll(b, HEAD_DIM, axis=1)
            else:
                a = pltpu.roll(a, HEAD_DIM, axis=1)
            of_ref[pl.ds(r0, t_new), pair * LANES:(pair + 1) * LANES] = jnp.where(low, a, b)
        return carry

    lax.fori_loop(0, g_seqs, one_seq, 0, unroll=4)
    o_ref[...] = of_ref[...].astype(o_ref.dtype)


def _attn_sample(sinks, q, k, v, cache_k, cache_v, *, t_new, n_kv, g_seqs=16):
    n_seq = cache_k.shape[0]
    aw, kw = q.shape[1], k.shape[1]
    tok = lambda w: pl.BlockSpec((g_seqs * t_new, w), lambda i: (i, 0))
    cache = pl.BlockSpec((g_seqs, WINDOW, kw), lambda i: (i, 0, 0))
    return pl.pallas_call(
        functools.partial(_attn_sample_kernel, n_kv=n_kv, t_new=t_new),
        grid=(n_seq // g_seqs,),
        in_specs=[pl.BlockSpec(memory_space=pltpu.SMEM), tok(aw), tok(kw), tok(kw), cache, cache],
        out_specs=[tok(aw), cache, cache],
        out_shape=[
            jax.ShapeDtypeStruct((n_seq * t_new, aw), BF16),
            jax.ShapeDtypeStruct(cache_k.shape, F32),
            jax.ShapeDtypeStruct(cache_v.shape, F32),
        ],
        scratch_shapes=[pltpu.VMEM((g_seqs * t_new, aw), F32), pltpu.VMEM((g_seqs * t_new, aw), F32)],
        compiler_params=_params("parallel"),
        name="attn_sample",
    )(sinks, q, k, v, cache_k, cache_v)


def _ln_silu(y, g, b):
    yc = y - jnp.mean(y, axis=-1, keepdims=True)
    var = jnp.mean(yc * yc, axis=-1, keepdims=True)
    return jax.nn.silu(yc * lax.rsqrt(var + LN_EPS) * g + b)


def _conv_prompt_kernel(cur_ref, halo_ref, w_ref, b_ref, g_ref, beta_ref, o_ref, xc_ref, y_ref, *, width, rows_per, lanes_per):
    tc, ch = cur_ref.shape
    halo = halo_ref.shape[0]
    i = pl.program_id(1)
    xc_ref[:halo, :] = jnp.where(i == 0, 0.0, halo_ref[...])
    xc_ref[halo:, :] = cur_ref[...]
    base = halo - (width - 1)

    def col_chunk(cc, carry):
        l0 = pl.multiple_of(cc * lanes_per, lanes_per)
        cols = pl.ds(l0, lanes_per)
        for rr in range(tc // rows_per):
            acc = jnp.broadcast_to(b_ref[:, cols], (rows_per, lanes_per))
            for phase in range(SUBLANES):
                n_win = rows_per + (SUBLANES if phase else 0)
                part = None
                for j in range(width):
                    if (base + j) % SUBLANES != phase:
                        continue
                    start = rr * rows_per + ((base + j) // SUBLANES) * SUBLANES
                    term = w_ref[pl.ds(j, 1), cols] * xc_ref[pl.ds(start, n_win), cols]
                    part = term if part is None else part + term
                if part is None:
                    continue
                if phase:
                    part = pltpu.roll(part, n_win - phase, axis=0)
                acc = acc + part[:rows_per]
            y_ref[pl.ds(rr * rows_per, rows_per), cols] = acc
        return carry

    lax.fori_loop(0, ch // lanes_per, col_chunk, 0)
    o_ref[...] = _ln_silu(y_ref[...], g_ref[...], beta_ref[...]).astype(o_ref.dtype)


def _conv_prompt(glu, w, b, g, beta, *, batch, seq, tc=256, halo=32):
    ch = glu.shape[1]
    width = w.shape[0]
    nt = seq // tc
    per = tc // halo
    cur = pl.BlockSpec((tc, ch), lambda bi, i: (bi * nt + i, 0))
    prev = pl.BlockSpec((halo, ch), lambda bi, i: (jnp.maximum((bi * nt + i) * per - 1, 0), 0))
    kern = functools.partial(_conv_prompt_kernel, width=width, rows_per=64, lanes_per=LANES)
    return pl.pallas_call(
        kern,
        grid=(batch, nt),
        in_specs=[cur, prev, _resident(w.shape), _resident((1, ch)), _resident((1, ch)), _resident((1, ch))],
        out_specs=cur,
        out_shape=jax.ShapeDtypeStruct((batch * seq, ch), BF16),
        scratch_shapes=[pltpu.VMEM((halo + tc, ch), F32), pltpu.VMEM((tc, ch), F32)],
        compiler_params=_params("parallel", "arbitrary"),
        name="conv_prompt",
    )(glu, glu, w, b, g, beta)


def _append_state(ns_ref, st_ref, g, new, t_new):
    n_past = st_ref.shape[1]
    keep = n_past - t_new
    full = (keep // SUBLANES) * SUBLANES
    rem = keep - full
    if full:
        ns_ref[g, :full, :] = st_ref[g, t_new:t_new + full, :]
    if rem:
        rolled = pltpu.roll(new, rem, axis=0)
        ns_ref[g, full:full + SUBLANES, :] = rolled
        ns_ref[g, full:keep, :] = st_ref[g, t_new + full:n_past, :]
        ns_ref[g, full + SUBLANES:, :] = rolled[:rem]
    else:
        ns_ref[g, full:, :] = new


def _conv_sample_kernel(new_ref, st_ref, toep_ref, b_ref, g_ref, beta_ref, o_ref, ns_ref, of_ref, *, t_new):
    g_seqs, n_past, ch = st_ref.shape

    def one_seq(g, carry):
        r0 = pl.multiple_of(g * t_new, t_new)
        new = new_ref[pl.ds(r0, t_new), :]
        acc = jnp.broadcast_to(b_ref[...], (t_new, ch))
        for r in range(n_past):
            acc = acc + toep_ref[r] * st_ref[g, pl.ds(r, 1), :]
        for r in range(t_new):
            acc = acc + toep_ref[n_past + r] * new_ref[pl.ds(r0 + r, 1), :]
        of_ref[pl.ds(r0, t_new), :] = _ln_silu(acc, g_ref[...], beta_ref[...])
        _append_state(ns_ref, st_ref, g, new, t_new)
        return carry

    lax.fori_loop(0, g_seqs, one_seq, 0, unroll=2)
    o_ref[...] = of_ref[...].astype(o_ref.dtype)


def _conv_sample(glu, state, toep, b, g, beta, *, t_new, g_seqs=16):
    n_seq, n_past, ch = state.shape
    return pl.pallas_call(
        functools.partial(_conv_sample_kernel, t_new=t_new),
        grid=(n_seq // g_seqs,),
        in_specs=[
            pl.BlockSpec((g_seqs * t_new, ch), lambda i: (i, 0)),
            pl.BlockSpec((g_seqs, n_past, ch), lambda i: (i, 0, 0)),
            _resident(toep.shape), _resident((1, ch)), _resident((1, ch)), _resident((1, ch)),
        ],
        out_specs=[
            pl.BlockSpec((g_seqs * t_new, ch), lambda i: (i, 0)),
            pl.BlockSpec((g_seqs, n_past, ch), lambda i: (i, 0, 0)),
        ],
        out_shape=[jax.ShapeDtypeStruct((n_seq * t_new, ch), BF16), jax.ShapeDtypeStruct(state.shape, F32)],
        scratch_shapes=[pltpu.VMEM((g_seqs * t_new, ch), F32)],
        compiler_params=_params("parallel"),
        name="conv_sample",
    )(glu, state, toep, b, g, beta)


def _out_proj_kernel(x_ref, a_ref, c_ref, w_ref, o_ref):
    aw = a_ref.shape[1]
    o_ref[...] = (x_ref[...]
                  + jnp.dot(a_ref[...], w_ref[:aw, :].astype(BF16), preferred_element_type=F32)
                  + jnp.dot(c_ref[...], w_ref[aw:, :].astype(BF16), preferred_element_type=F32))


def _out_proj(x, attn, conv, w, *, slab, tm=512):
    n, d = x.shape
    rows = lambda width: pl.BlockSpec((tm, width), lambda i: (i, 0))
    return pl.pallas_call(
        _out_proj_kernel,
        grid=(n // tm,),
        in_specs=[rows(d), rows(attn.shape[1]), rows(conv.shape[1]), _resident_slab(w.shape, slab)],
        out_specs=rows(d),
        out_shape=jax.ShapeDtypeStruct((n, d), F32),
        compiler_params=_params("parallel"),
        name="out_proj",
    )(x, attn, conv, w)


def _ffn_kernel(x_ref, g_ref, wg_ref, wu_ref, wd_ref, o_ref, h_ref):
    @pl.when(pl.program_id(1) == 0)
    def _():
        x = x_ref[...]
        h_ref[...] = _rms(x, g_ref[...]).astype(BF16)
        o_ref[...] = x

    h = h_ref[...]
    a = jnp.dot(h, wg_ref[...].astype(BF16), preferred_element_type=F32)
    u = jnp.dot(h, wu_ref[...].astype(BF16), preferred_element_type=F32)
    act = (jax.nn.silu(a) * u).astype(BF16)
    o_ref[...] += jnp.dot(act, wd_ref[...].astype(BF16), preferred_element_type=F32)


def _ffn(x, g, wg, wu, wd, *, layer, tm=1024, tf=256):
    n, d = x.shape
    dff = wg.shape[2]
    nt, nf = n // tm, dff // tf
    x_index = lambda i, f: (jnp.minimum(i + (f > nf // 2).astype(jnp.int32), nt - 1), 0)
    return pl.pallas_call(
        _ffn_kernel,
        grid=(nt, nf),
        in_specs=[
            pl.BlockSpec((tm, d), x_index),
            _resident((1, d)),
            pl.BlockSpec((None, d, tf), lambda i, f: (layer, 0, f)),
            pl.BlockSpec((None, d, tf), lambda i, f: (layer, 0, f)),
            pl.BlockSpec((None, tf, d), lambda i, f: (layer, f, 0)),
        ],
        out_specs=pl.BlockSpec((tm, d), lambda i, f: (i, 0)),
        out_shape=jax.ShapeDtypeStruct((n, d), F32),
        scratch_shapes=[pltpu.VMEM((tm, d), BF16)],
        compiler_params=_params("parallel", "arbitrary"),
        name="ffn",
    )(x, g, wg, wu, wd)


def _pool_prompt_kernel(cur_ref, halo_ref, g_ref, pw_ref, ps_ref, o_ref, tail_ref, xp_ref):
    tp, d = cur_ref.shape
    halo = halo_ref.shape[0]
    i = pl.program_id(1)
    gain = g_ref[...]
    xp_ref[:halo, :] = jnp.where(i == 0, 0.0, _rms(halo_ref[...], gain))
    xp_ref[halo:, :] = _rms(cur_ref[...], gain)
    tail_ref[0] = xp_ref[tp:, :]
    pg = d // len(POOL_WINDOWS)
    pos = i * tp + lax.broadcasted_iota(jnp.int32, (tp, 1), 0)
    for gi, w in enumerate(POOL_WINDOWS):
        cols = slice(gi * pg, (gi + 1) * pg)
        ext = xp_ref[:, cols]
        h = ext[halo:]
        span = 1
        while span < w:
            ext = ext + pltpu.roll(ext, span, axis=0)
            span *= 2
        inv_count = 1.0 / jnp.minimum(pos + 1, w).astype(F32)
        dpool = (ext[halo:] * inv_count - h).astype(BF16)
        y = jnp.dot(dpool, pw_ref[gi].astype(BF16), preferred_element_type=F32)
        o_ref[:, cols] = cur_ref[:, cols] + y * ps_ref[:, cols]


def _pool_prompt(x, g, pw, ps, *, slab, batch, seq, tp=512, halo=16):
    assert all(w & (w - 1) == 0 and w <= halo for w in POOL_WINDOWS)
    d = x.shape[1]
    nt = seq // tp
    per = tp // halo
    cur = pl.BlockSpec((tp, d), lambda bi, i: (bi * nt + i, 0))
    prev = pl.BlockSpec((halo, d), lambda bi, i: (jnp.maximum((bi * nt + i) * per - 1, 0), 0))
    return pl.pallas_call(
        _pool_prompt_kernel,
        grid=(batch, nt),
        in_specs=[cur, prev, _resident((1, d)), _resident_slab(pw.shape, slab), _resident((1, d))],
        out_specs=[cur, pl.BlockSpec((1, halo, d), lambda bi, i: (bi, 0, 0))],
        out_shape=[jax.ShapeDtypeStruct((batch * seq, d), F32), jax.ShapeDtypeStruct((batch, halo, d), F32)],
        scratch_shapes=[pltpu.VMEM((halo + tp, d), F32)],
        compiler_params=_params("parallel", "arbitrary"),
        name="pool_prompt",
    )(x, x, g, pw, ps)


def _pool_sample_kernel(x_ref, st_ref, g_ref, pw_ref, ps_ref, o_ref, ns_ref, h_ref, dp_ref, *, t_new, start_pos):
    g_seqs, n_past, d = st_ref.shape
    pg = d // len(POOL_WINDOWS)
    h_ref[...] = _rms(x_ref[...], g_ref[...])
    t = lax.broadcasted_iota(jnp.int32, (t_new, 1), 0)

    def one_seq(g, carry):
        r0 = pl.multiple_of(g * t_new, t_new)
        h = h_ref[pl.ds(r0, t_new), :]
        for gi, w in enumerate(POOL_WINDOWS):
            cols = slice(gi * pg, (gi + 1) * pg)
            inv = 1.0 / jnp.minimum(start_pos + t + 1, w).astype(F32)
            tot = jnp.zeros((t_new, pg), F32)
            for r in range(max(0, n_past + 1 - w), n_past + t_new):
                if r < n_past:
                    row = st_ref[g, pl.ds(r, 1), cols]
                else:
                    row = h_ref[pl.ds(r0 + (r - n_past), 1), cols]
                hit = jnp.logical_and(t + n_past - w < r, r <= t + n_past)
                tot = tot + jnp.where(hit, inv, 0.0) * row
            dp_ref[pl.ds(r0, t_new), cols] = tot - h[:, cols]
        _append_state(ns_ref, st_ref, g, h, t_new)
        return carry

    lax.fori_loop(0, g_seqs, one_seq, 0, unroll=2)
    for gi in range(len(POOL_WINDOWS)):
        cols = slice(gi * pg, (gi + 1) * pg)
        y = jnp.dot(dp_ref[:, cols].astype(BF16), pw_ref[gi].astype(BF16), preferred_element_type=F32)
        o_ref[:, cols] = x_ref[:, cols] + y * ps_ref[:, cols]


def _pool_sample(x, state, g, pw, ps, *, slab, t_new, g_seqs=16):
    n_seq, n_past, d = state.shape
    assert t_new == SUBLANES and SUBLANES <= n_past < 2 * SUBLANES
    kern = functools.partial(_pool_sample_kernel, t_new=t_new, start_pos=PAST_LEN)
    return pl.pallas_call(
        kern,
        grid=(n_seq // g_seqs,),
        in_specs=[
            pl.BlockSpec((g_seqs * t_new, d), lambda i: (i, 0)),
            pl.BlockSpec((g_seqs, n_past, d), lambda i: (i, 0, 0)),
            _resident((1, d)), _resident_slab(pw.shape, slab), _resident((1, d)),
        ],
        out_specs=[
            pl.BlockSpec((g_seqs * t_new, d), lambda i: (i, 0)),
            pl.BlockSpec((g_seqs, n_past, d), lambda i: (i, 0, 0)),
        ],
        out_shape=[jax.ShapeDtypeStruct((n_seq * t_new, d), F32), jax.ShapeDtypeStruct(state.shape, F32)],
        scratch_shapes=[pltpu.VMEM((g_seqs * t_new, d), F32), pltpu.VMEM((g_seqs * t_new, d), F32)],
        compiler_params=_params("parallel"),
        name="pool_sample",
    )(x, state, g, pw, ps)


def _toeplitz(conv_w, t_new):
    width = conv_w.shape[0]
    idx = jnp.arange(width - 1 + t_new)[:, None] - jnp.arange(t_new)[None, :]
    valid = jnp.logical_and(idx >= 0, idx < width)
    return jnp.where(valid[:, :, None], conv_w[jnp.clip(idx, 0, width - 1)], 0.0)


def kernel(x_prompt, x_sample, cache_k, cache_v, state_conv, state_pool, norm_mix, w_in, q_norm, k_norm, sinks,
           conv_w, conv_b, conv_ln_g, conv_ln_b, w_out, pool_w, pool_scale, norm_ffn, w_gate, w_up, w_down):
    batch, seq, d = x_prompt.shape
    n_seq, t_new, _ = x_sample.shape
    n_kv = cache_k.shape[3]
    kv_w = n_kv * HEAD_DIM
    conv_ch = state_conv.shape[-1]
    attn_w = w_in.shape[-1] - 2 * kv_w - 2 * conv_ch
    n_prompt = batch * seq
    row = lambda a: a.reshape(1, -1)

    xp = x_prompt.reshape(n_prompt, d)
    xs = x_sample.reshape(n_seq * t_new, d)
    depth = norm_mix.shape[0]
    outs = {}
    for layer in range(depth):
        i = layer // 2
        if layer % 2 == 0:
            reps = MXU_DIM // HEAD_DIM
            proj = functools.partial(
                _in_proj, g=row(norm_mix[layer]), w=w_in, slab=i, qg=row(jnp.tile(q_norm[i], reps)),
                kg=row(jnp.tile(k_norm[i], reps)), attn_w=attn_w, kv_w=kv_w, conv_ch=conv_ch)
            q_p, k_p, v_p, glu_p = proj(xp)
            q_s, k_s, v_s, glu_s = proj(xs)
            attn_p = _attn_prompt(sinks[i], q_p, k_p, v_p, batch=batch, seq=seq, n_kv=n_kv)
            attn_s, nk_s, nv_s = _attn_sample(
                sinks[i], q_s, k_s, v_s, cache_k[i].reshape(n_seq, WINDOW, kv_w), cache_v[i].reshape(n_seq, WINDOW, kv_w),
                t_new=t_new, n_kv=n_kv)
            ln = (row(conv_b[i]), row(conv_ln_g[i]), row(conv_ln_b[i]))
            conv_p = _conv_prompt(glu_p, conv_w[i], *ln, batch=batch, seq=seq)
            conv_s, nc_s = _conv_sample(glu_s, state_conv[i], _toeplitz(conv_w[i], t_new), *ln, t_new=t_new)
            xp = _out_proj(xp, attn_p, conv_p, w_out, slab=i)
            xs = _out_proj(xs, attn_s, conv_s, w_out, slab=i)
            last = lambda a, n: a.reshape(batch, seq, -1)[:, -n:]
            outs.setdefault("kp", []).append(last(k_p, WINDOW).reshape(batch, WINDOW, n_kv, HEAD_DIM))
            outs.setdefault("vp", []).append(last(v_p, WINDOW).reshape(batch, WINDOW, n_kv, HEAD_DIM))
            outs.setdefault("cp", []).append(last(glu_p, conv_w.shape[1] - 1))
            outs.setdefault("ks", []).append(nk_s.reshape(n_seq, WINDOW, n_kv, HEAD_DIM))
            outs.setdefault("vs", []).append(nv_s.reshape(n_seq, WINDOW, n_kv, HEAD_DIM))
            outs.setdefault("cs", []).append(nc_s)
        else:
            args = (row(norm_mix[layer]), pool_w, row(pool_scale[i]))
            n_past = state_pool.shape[2]
            xp, tail = _pool_prompt(xp, *args, slab=i, batch=batch, seq=seq)
            xs, np_s = _pool_sample(xs, state_pool[i], *args, slab=i, t_new=t_new)
            outs.setdefault("pp", []).append(tail[:, -n_past:])
            outs.setdefault("ps", []).append(np_s)
        ffn = functools.partial(_ffn, g=row(norm_ffn[layer]), wg=w_gate, wu=w_up, wd=w_down, layer=layer)
        xp = ffn(xp)
        xs = ffn(xs)
    st = lambda name: jnp.stack(outs[name])
    return (xp.reshape(batch, seq, d), xs.reshape(n_seq, t_new, d),
            st("kp"), st("vp"), st("cp"), st("pp"), st("ks"), st("vs"), st("cs"), st("ps"))
```

```python
import functools

import jax
import jax.numpy as jnp
from jax import lax
from jax.experimental import pallas as pl
from jax.experimental.pallas import tpu as pltpu

F32 = jnp.float32
BF16 = jnp.bfloat16

HEAD_DIM = 64
WINDOW = 128
PAST_LEN = 8192
POOL_WINDOWS = (2, 4, 8, 16)
RMS_EPS = 1e-6
LN_EPS = 1e-5
NEG = -1e30

LANES = 128
SUBLANES = 8
MXU_DIM = 256
VMEM_LIMIT = 56 * 1024 * 1024


def _params(*semantics):
    return pltpu.CompilerParams(dimension_semantics=semantics, vmem_limit_bytes=VMEM_LIMIT)


def _rms(x, g):
    ms = jnp.mean(x * x, axis=-1, keepdims=True)
    return x * lax.rsqrt(ms + RMS_EPS) * g


def _resident(shape):
    zeros = (0,) * len(shape)
    return pl.BlockSpec(shape, lambda *_: zeros, pipeline_mode=pl.Buffered(1))


def _resident_slab(shape, slab):
    index = (slab,) + (0,) * (len(shape) - 1)
    return pl.BlockSpec((None,) + tuple(shape[1:]), lambda *_: index, pipeline_mode=pl.Buffered(1))


def _in_proj_kernel(x_ref, g_ref, w_ref, qg_ref, kg_ref, q_ref, k_ref, v_ref, glu_ref, *, attn_w, kv_w, conv_ch):
    h = _rms(x_ref[...], g_ref[...]).astype(BF16)
    row = lax.broadcasted_iota(jnp.int32, (MXU_DIM, MXU_DIM), 0) // HEAD_DIM
    col = lax.broadcasted_iota(jnp.int32, (MXU_DIM, MXU_DIM), 1) // HEAD_DIM
    pmat = jnp.where(row == col, 1.0 / HEAD_DIM, 0.0).astype(BF16)

    def proj(c0):
        return jnp.dot(h, w_ref[:, c0:c0 + MXU_DIM].astype(BF16), preferred_element_type=F32)

    def head_rms(z):
        sq = z * z
        hi = sq.astype(BF16)
        lo = (sq - hi.astype(F32)).astype(BF16)
        ms = jnp.dot(hi, pmat, preferred_element_type=F32) + jnp.dot(lo, pmat, preferred_element_type=F32)
        return z * lax.rsqrt(ms + RMS_EPS)

    scale = HEAD_DIM ** -0.5

    def put_q(c0, z):
        q_ref[:, c0:c0 + MXU_DIM] = (head_rms(z) * qg_ref[...] * scale).astype(q_ref.dtype)

    def put_k(c0, z):
        k_ref[:, c0:c0 + MXU_DIM] = head_rms(z) * kg_ref[...]

    def put_v(c0, z):
        v_ref[:, c0:c0 + MXU_DIM] = z

    def put_glu(c0, a, gate):
        glu_ref[:, c0:c0 + MXU_DIM] = a * jax.nn.sigmoid(gate)

    units = [((c0,), functools.partial(put_q, c0)) for c0 in range(0, attn_w, MXU_DIM)]
    units += [((attn_w + c0,), functools.partial(put_k, c0)) for c0 in range(0, kv_w, MXU_DIM)]
    units += [((attn_w + kv_w + c0,), functools.partial(put_v, c0)) for c0 in range(0, kv_w, MXU_DIM)]
    units += [((attn_w + 2 * kv_w + c0, attn_w + 2 * kv_w + conv_ch + c0), functools.partial(put_glu, c0))
              for c0 in range(0, conv_ch, MXU_DIM)]
    pending = None
    for cols, epilogue in units:
        zs = [proj(c0) for c0 in cols]
        if pending is not None:
            pending[0](*pending[1])
        pending = (epilogue, zs)
    pending[0](*pending[1])


def _in_proj(x, g, w, qg, kg, *, slab, attn_w, kv_w, conv_ch, tm=512):
    n, d = x.shape
    kern = functools.partial(_in_proj_kernel, attn_w=attn_w, kv_w=kv_w, conv_ch=conv_ch)
    rows = lambda width: pl.BlockSpec((tm, width), lambda i: (i, 0))
    return pl.pallas_call(
        kern,
        grid=(n // tm,),
        in_specs=[rows(d), _resident((1, d)), _resident_slab(w.shape, slab), _resident((1, MXU_DIM)), _resident((1, MXU_DIM))],
        out_specs=[rows(attn_w), rows(kv_w), rows(kv_w), rows(conv_ch)],
        out_shape=[
            jax.ShapeDtypeStruct((n, attn_w), BF16),
            jax.ShapeDtypeStruct((n, kv_w), F32),
            jax.ShapeDtypeStruct((n, kv_w), F32),
            jax.ShapeDtypeStruct((n, conv_ch), F32),
        ],
        compiler_params=_params("parallel"),
        name="in_proj",
    )(x, g, w, qg, kg)


def _pair_blocks(chunk, rolled, odd):
    lane = lax.broadcasted_iota(jnp.int32, chunk.shape, 1)
    low = lane < HEAD_DIM
    if odd:
        top = jnp.where(low, rolled, 0.0)
        bot = jnp.where(low, 0.0, chunk)
    else:
        top = jnp.where(low, chunk, 0.0)
        bot = jnp.where(low, 0.0, rolled)
    return jnp.concatenate([top, bot], axis=0).astype(BF16)


def _softmax_pv(s, ok, sink_a, sink_b, vb_ones):
    half = s.shape[1] // 2
    s = jnp.where(ok, s, NEG)
    sa, sb = s[:, :half], s[:, half:]
    ma = jnp.maximum(jnp.max(sa, axis=-1, keepdims=True), sink_a)
    mb = jnp.maximum(jnp.max(sb, axis=-1, keepdims=True), sink_b)
    p = jnp.concatenate([jnp.exp(sa - ma), jnp.exp(sb - mb)], axis=1).astype(BF16)
    r = jnp.dot(p, vb_ones, preferred_element_type=F32)
    lane = lax.broadcasted_iota(jnp.int32, (s.shape[0], LANES), 1)
    sink_term = jnp.where(lane < HEAD_DIM, jnp.exp(sink_a - ma), jnp.exp(sink_b - mb))
    return r[:, :LANES] / (r[:, LANES:] + sink_term)


def _attend(q2, kb, vb, ok, sinks_ref, h0, rows):
    s = lax.dot_general(q2, kb, (((1,), (1,)), ((), ())), preferred_element_type=F32)
    row = lax.broadcasted_iota(jnp.int32, (2 * rows, 1), 0)
    first = row < rows
    sink_a = jnp.where(first, sinks_ref[h0], sinks_ref[h0 + 2])
    sink_b = jnp.where(first, sinks_ref[h0 + 1], sinks_ref[h0 + 3])
    return _softmax_pv(s, ok, sink_a, sink_b, vb)


def _attn_prompt_kernel(sinks_ref, q_ref, kp_ref, kc_ref, vp_ref, vc_ref, o_ref, *, n_kv):
    blk = q_ref.shape[0]
    i = pl.program_id(1)
    shape = (2 * blk, 4 * blk)
    r = lax.broadcasted_iota(jnp.int32, shape, 0) % blk
    c = lax.broadcasted_iota(jnp.int32, shape, 1) % (2 * blk)
    lo = jnp.where(i == 0, blk - 1, -1)
    ok = jnp.logical_and(c > jnp.maximum(r, lo), c <= r + blk)
    k_all = jnp.concatenate([kp_ref[...], kc_ref[...]], axis=0)
    v_all = jnp.concatenate([vp_ref[...], vc_ref[...]], axis=0)
    group = (q_ref.shape[1] // HEAD_DIM) // n_kv
    key_row = lax.broadcasted_iota(jnp.int32, (4 * blk, LANES), 0)
    key_lane = lax.broadcasted_iota(jnp.int32, (4 * blk, LANES), 1)
    ones_blocks = jnp.where((key_row < 2 * blk) == (key_lane < HEAD_DIM), 1.0, 0.0).astype(BF16)
    for cch in range(n_kv // 2):
        kc = k_all[:, cch * LANES:(cch + 1) * LANES]
        vc = v_all[:, cch * LANES:(cch + 1) * LANES]
        kr = pltpu.roll(kc, HEAD_DIM, axis=1)
        vr = pltpu.roll(vc, HEAD_DIM, axis=1)
        for odd in range(2):
            j = 2 * cch + odd
            kb = _pair_blocks(kc, kr, odd)
            vb = jnp.concatenate([_pair_blocks(vc, vr, odd), ones_blocks], axis=1)
            h0 = group * j
            c0 = h0 * HEAD_DIM
            q2 = jnp.concatenate([q_ref[:, c0:c0 + LANES], q_ref[:, c0 + LANES:c0 + 2 * LANES]], axis=0)
            o = _attend(q2, kb, vb, ok, sinks_ref, h0, blk).astype(o_ref.dtype)
            o_ref[:, c0:c0 + LANES] = o[:blk]
            o_ref[:, c0 + LANES:c0 + 2 * LANES] = o[blk:]


def _attn_prompt(sinks, q, k, v, *, batch, seq, n_kv):
    blk = WINDOW
    nb = seq // blk
    aw, kw = q.shape[1], k.shape[1]
    cur = lambda w: pl.BlockSpec((blk, w), lambda b, i: (b * nb + i, 0))
    prev = lambda w: pl.BlockSpec((blk, w), lambda b, i: (b * nb + jnp.maximum(i - 1, 0), 0))
    return pl.pallas_call(
        functools.partial(_attn_prompt_kernel, n_kv=n_kv),
        grid=(batch, nb),
        in_specs=[pl.BlockSpec(memory_space=pltpu.SMEM), cur(aw), prev(kw), cur(kw), prev(kw), cur(kw)],
        out_specs=cur(aw),
        out_shape=jax.ShapeDtypeStruct((batch * seq, aw), BF16),
        compiler_params=_params("parallel", "arbitrary"),
        name="attn_prompt",
    )(sinks, q, k, k, v, v)


def _attn_sample_kernel(sinks_ref, q_ref, kn_ref, vn_ref, ck_ref, cv_ref, o_ref, nk_ref, nv_ref, qf_ref, of_ref, *, n_kv, t_new):
    g_seqs, _, kw = ck_ref.shape
    n_heads = q_ref.shape[1] // HEAD_DIM
    group = n_heads // n_kv
    keep = WINDOW - t_new
    n_rows = n_heads * t_new
    n_keys = 2 * WINDOW
    qf_ref[...] = q_ref[...].astype(F32)
    t = lax.broadcasted_iota(jnp.int32, (n_rows, n_keys), 0) % t_new
    c = lax.broadcasted_iota(jnp.int32, (n_rows, n_keys), 1)
    ok = jnp.logical_and(c > t, c <= t + WINDOW)
    sink = jnp.concatenate([jnp.full((t_new, 1), sinks_ref[h], F32) for h in range(n_heads)], axis=0)
    low = lax.broadcasted_iota(jnp.int32, (t_new, LANES), 1) < HEAD_DIM
    zero_tile = jnp.zeros((t_new, LANES), F32)
    zero_keys = jnp.zeros((n_keys - WINDOW - t_new, kw), F32)

    def one_seq(g, carry):
        r0 = pl.multiple_of(g * t_new, t_new)
        k_new = kn_ref[pl.ds(r0, t_new), :]
        v_new = vn_ref[pl.ds(r0, t_new), :]
        k_old = ck_ref[g]
        v_old = cv_ref[g]
        nk_ref[g, :keep, :] = k_old[t_new:]
        nk_ref[g, keep:, :] = k_new
        nv_ref[g, :keep, :] = v_old[t_new:]
        nv_ref[g, keep:, :] = v_new
        keys = jnp.concatenate([k_old, k_new, zero_keys], axis=0).astype(BF16)
        vals = jnp.concatenate([v_old, v_new, zero_keys], axis=0).astype(BF16)
        vals = jnp.concatenate([vals, jnp.ones((n_keys, LANES), BF16)], axis=1)
        q_rows = []
        for h in range(n_heads):
            j = h // group
            src = qf_ref[pl.ds(r0, t_new), (h // 2) * LANES:(h // 2 + 1) * LANES]
            if h % 2 != j % 2:
                src = pltpu.roll(src, HEAD_DIM, axis=1)
            half = jnp.where(low, src, 0.0) if j % 2 == 0 else jnp.where(low, 0.0, src)
            tiles = [zero_tile] * (kw // LANES)
            tiles[j // 2] = half
            q_rows.append(jnp.concatenate(tiles, axis=1))
        qb = jnp.concatenate(q_rows, axis=0).astype(BF16)
        s = lax.dot_general(qb, keys, (((1,), (1,)), ((), ())), preferred_element_type=F32)
        s = jnp.where(ok, s, NEG)
        m = jnp.maximum(jnp.max(s, axis=-1, keepdims=True), sink)
        r = jnp.dot(jnp.exp(s - m).astype(BF16), vals, preferred_element_type=F32)
        denom = r[:, kw:] + jnp.exp(sink - m)
        o = r[:, :kw] / jnp.concatenate([denom] * (kw // LANES), axis=1)
        for pair in range(n_heads // 2):
            j = (2 * pair) // group
            lanes = slice((j // 2) * LANES, (j // 2 + 1) * LANES)
            a = o[2 * pair * t_new:(2 * pair + 1) * t_new, lanes]
            b = o[(2 * pair + 1) * t_new:(2 * pair + 2) * t_new, lanes]
            if j % 2 == 0:
                b = pltpu.roll(b, HEAD_DIM, axis=1)
            else:
                a = pltpu.roll(a, HEAD_DIM, axis=1)
            of_ref[pl.ds(r0, t_new), pair * LANES:(pair + 1) * LANES] = jnp.where(low, a, b)
        return carry

    lax.fori_loop(0, g_seqs, one_seq, 0, unroll=4)
    o_ref[...] = of_ref[...].astype(o_ref.dtype)


def _attn_sample(sinks, q, k, v, cache_k, cache_v, *, t_new, n_kv, g_seqs=16):
    n_seq = cache_k.shape[0]
    aw, kw = q.shape[1], k.shape[1]
    tok = lambda w: pl.BlockSpec((g_seqs * t_new, w), lambda i: (i, 0))
    cache = pl.BlockSpec((g_seqs, WINDOW, kw), lambda i: (i, 0, 0))
    return pl.pallas_call(
        functools.partial(_attn_sample_kernel, n_kv=n_kv, t_new=t_new),
        grid=(n_seq // g_seqs,),
        in_specs=[pl.BlockSpec(memory_space=pltpu.SMEM), tok(aw), tok(kw), tok(kw), cache, cache],
        out_specs=[tok(aw), cache, cache],
        out_shape=[
            jax.ShapeDtypeStruct((n_seq * t_new, aw), BF16),
            jax.ShapeDtypeStruct(cache_k.shape, F32),
            jax.ShapeDtypeStruct(cache_v.shape, F32),
        ],
        scratch_shapes=[pltpu.VMEM((g_seqs * t_new, aw), F32), pltpu.VMEM((g_seqs * t_new, aw), F32)],
        compiler_params=_params("parallel"),
        name="attn_sample",
    )(sinks, q, k, v, cache_k, cache_v)


def _ln_silu(y, g, b):
    yc = y - jnp.mean(y, axis=-1, keepdims=True)
    var = jnp.mean(yc * yc, axis=-1, keepdims=True)
    return jax.nn.silu(yc * lax.rsqrt(var + LN_EPS) * g + b)


def _conv_prompt_kernel(cur_ref, halo_ref, w_ref, b_ref, g_ref, beta_ref, o_ref, xc_ref, y_ref, *, width, rows_per, lanes_per):
    tc, ch = cur_ref.shape
    halo = halo_ref.shape[0]
    i = pl.program_id(1)
    xc_ref[:halo, :] = jnp.where(i == 0, 0.0, halo_ref[...])
    xc_ref[halo:, :] = cur_ref[...]
    base = halo - (width - 1)

    def col_chunk(cc, carry):
        l0 = pl.multiple_of(cc * lanes_per, lanes_per)
        cols = pl.ds(l0, lanes_per)
        for rr in range(tc // rows_per):
            acc = jnp.broadcast_to(b_ref[:, cols], (rows_per, lanes_per))
            for phase in range(SUBLANES):
                n_win = rows_per + (SUBLANES if phase else 0)
                part = None
                for j in range(width):
                    if (base + j) % SUBLANES != phase:
                        continue
                    start = rr * rows_per + ((base + j) // SUBLANES) * SUBLANES
                    term = w_ref[pl.ds(j, 1), cols] * xc_ref[pl.ds(start, n_win), cols]
                    part = term if part is None else part + term
                if part is None:
                    continue
                if phase:
                    part = pltpu.roll(part, n_win - phase, axis=0)
                acc = acc + part[:rows_per]
            y_ref[pl.ds(rr * rows_per, rows_per), cols] = acc
        return carry

    lax.fori_loop(0, ch // lanes_per, col_chunk, 0)
    o_ref[...] = _ln_silu(y_ref[...], g_ref[...], beta_ref[...]).astype(o_ref.dtype)


def _conv_prompt(glu, w, b, g, beta, *, batch, seq, tc=256, halo=32):
    ch = glu.shape[1]
    width = w.shape[0]
    nt = seq // tc
    per = tc // halo
    cur = pl.BlockSpec((tc, ch), lambda bi, i: (bi * nt + i, 0))
    prev = pl.BlockSpec((halo, ch), lambda bi, i: (jnp.maximum((bi * nt + i) * per - 1, 0), 0))
    kern = functools.partial(_conv_prompt_kernel, width=width, rows_per=64, lanes_per=LANES)
    return pl.pallas_call(
        kern,
        grid=(batch, nt),
        in_specs=[cur, prev, _resident(w.shape), _resident((1, ch)), _resident((1, ch)), _resident((1, ch))],
        out_specs=cur,
        out_shape=jax.ShapeDtypeStruct((batch * seq, ch), BF16),
        scratch_shapes=[pltpu.VMEM((halo + tc, ch), F32), pltpu.VMEM((tc, ch), F32)],
        compiler_params=_params("parallel", "arbitrary"),
        name="conv_prompt",
    )(glu, glu, w, b, g, beta)


def _append_state(ns_ref, st_ref, g, new, t_new):
    n_past = st_ref.shape[1]
    keep = n_past - t_new
    full = (keep // SUBLANES) * SUBLANES
    rem = keep - full
    if full:
        ns_ref[g, :full, :] = st_ref[g, t_new:t_new + full, :]
    if rem:
        rolled = pltpu.roll(new, rem, axis=0)
        ns_ref[g, full:full + SUBLANES, :] = rolled
        ns_ref[g, full:keep, :] = st_ref[g, t_new + full:n_past, :]
        ns_ref[g, full + SUBLANES:, :] = rolled[:rem]
    else:
        ns_ref[g, full:, :] = new


def _conv_sample_kernel(new_ref, st_ref, toep_ref, b_ref, g_ref, beta_ref, o_ref, ns_ref, of_ref, *, t_new):
    g_seqs, n_past, ch = st_ref.shape

    def one_seq(g, carry):
        r0 = pl.multiple_of(g * t_new, t_new)
        new = new_ref[pl.ds(r0, t_new), :]
        acc = jnp.broadcast_to(b_ref[...], (t_new, ch))
        for r in range(n_past):
            acc = acc + toep_ref[r] * st_ref[g, pl.ds(r, 1), :]
        for r in range(t_new):
            acc = acc + toep_ref[n_past + r] * new_ref[pl.ds(r0 + r, 1), :]
        of_ref[pl.ds(r0, t_new), :] = _ln_silu(acc, g_ref[...], beta_ref[...])
        _append_state(ns_ref, st_ref, g, new, t_new)
        return carry

    lax.fori_loop(0, g_seqs, one_seq, 0, unroll=2)
    o_ref[...] = of_ref[...].astype(o_ref.dtype)


def _conv_sample(glu, state, toep, b, g, beta, *, t_new, g_seqs=16):
    n_seq, n_past, ch = state.shape
    return pl.pallas_call(
        functools.partial(_conv_sample_kernel, t_new=t_new),
        grid=(n_seq // g_seqs,),
        in_specs=[
            pl.BlockSpec((g_seqs * t_new, ch), lambda i: (i, 0)),
            pl.BlockSpec((g_seqs, n_past, ch), lambda i: (i, 0, 0)),
            _resident(toep.shape), _resident((1, ch)), _resident((1, ch)), _resident((1, ch)),
        ],
        out_specs=[
            pl.BlockSpec((g_seqs * t_new, ch), lambda i: (i, 0)),
            pl.BlockSpec((g_seqs, n_past, ch), lambda i: (i, 0, 0)),
        ],
        out_shape=[jax.ShapeDtypeStruct((n_seq * t_new, ch), BF16), jax.ShapeDtypeStruct(state.shape, F32)],
        scratch_shapes=[pltpu.VMEM((g_seqs * t_new, ch), F32)],
        compiler_params=_params("parallel"),
        name="conv_sample",
    )(glu, state, toep, b, g, beta)


def _out_proj_kernel(x_ref, a_ref, c_ref, w_ref, o_ref):
    aw = a_ref.shape[1]
    o_ref[...] = (x_ref[...]
                  + jnp.dot(a_ref[...], w_ref[:aw, :].astype(BF16), preferred_element_type=F32)
                  + jnp.dot(c_ref[...], w_ref[aw:, :].astype(BF16), preferred_element_type=F32))


def _out_proj(x, attn, conv, w, *, slab, tm=512):
    n, d = x.shape
    rows = lambda width: pl.BlockSpec((tm, width), lambda i: (i, 0))
    return pl.pallas_call(
        _out_proj_kernel,
        grid=(n // tm,),
        in_specs=[rows(d), rows(attn.shape[1]), rows(conv.shape[1]), _resident_slab(w.shape, slab)],
        out_specs=rows(d),
        out_shape=jax.ShapeDtypeStruct((n, d), F32),
        compiler_params=_params("parallel"),
        name="out_proj",
    )(x, attn, conv, w)


def _ffn_kernel(x_ref, g_ref, wg_ref, wu_ref, wd_ref, o_ref, *rest):
    h_ref = rest[-1]

    @pl.when(pl.program_id(1) == 0)
    def _():
        x = x_ref[...]
        h_ref[...] = _rms(x, g_ref[...]).astype(BF16)
        o_ref[...] = x

    weights = [w_ref[...].astype(BF16) for w_ref in (wg_ref, wu_ref, wd_ref)]
    for w_out, w in zip(rest[:-1], weights):
        w_out[...] = w
    wg, wu, wd = weights
    h = h_ref[...]
    a = jnp.dot(h, wg, preferred_element_type=F32)
    u = jnp.dot(h, wu, preferred_element_type=F32)
    act = (jax.nn.silu(a) * u).astype(BF16)
    o_ref[...] += jnp.dot(act, wd, preferred_element_type=F32)


def _ffn(x, g, wg, wu, wd, *, layer=None, emit_bf16=False, tm=1024, tf=256):
    n, d = x.shape
    dff = wg.shape[-1]
    nt, nf = n // tm, dff // tf
    if layer is None:
        w_in_specs = [pl.BlockSpec((d, tf), lambda i, f: (0, f)), pl.BlockSpec((d, tf), lambda i, f: (0, f)),
                      pl.BlockSpec((tf, d), lambda i, f: (f, 0))]
    else:
        w_in_specs = [pl.BlockSpec((None, d, tf), lambda i, f: (layer, 0, f)),
                      pl.BlockSpec((None, d, tf), lambda i, f: (layer, 0, f)),
                      pl.BlockSpec((None, tf, d), lambda i, f: (layer, f, 0))]
    rows = pl.BlockSpec((tm, d), lambda i, f: (i, 0))
    out_specs, out_shape = [rows], [jax.ShapeDtypeStruct((n, d), F32)]
    if emit_bf16:
        assert nt == 1, "one row tile, so every bf16 weight tile is written exactly once"
        out_specs += [pl.BlockSpec((d, tf), lambda i, f: (0, f)), pl.BlockSpec((d, tf), lambda i, f: (0, f)),
                      pl.BlockSpec((tf, d), lambda i, f: (f, 0))]
        out_shape += [jax.ShapeDtypeStruct((d, dff), BF16), jax.ShapeDtypeStruct((d, dff), BF16),
                      jax.ShapeDtypeStruct((dff, d), BF16)]
    x_spec = _resident((tm, d)) if nt == 1 else rows
    outs = pl.pallas_call(
        _ffn_kernel,
        grid=(nt, nf),
        in_specs=[x_spec, _resident((1, d))] + w_in_specs,
        out_specs=out_specs,
        out_shape=out_shape,
        scratch_shapes=[pltpu.VMEM((tm, d), BF16)],
        compiler_params=_params("parallel", "arbitrary"),
        name="ffn",
    )(x, g, wg, wu, wd)
    return outs if emit_bf16 else outs[0]


def _pool_prompt_kernel(cur_ref, halo_ref, g_ref, pw_ref, ps_ref, o_ref, tail_ref, xp_ref):
    tp, d = cur_ref.shape
    halo = halo_ref.shape[0]
    i = pl.program_id(1)
    gain = g_ref[...]
    xp_ref[:halo, :] = jnp.where(i == 0, 0.0, _rms(halo_ref[...], gain))
    xp_ref[halo:, :] = _rms(cur_ref[...], gain)
    tail_ref[0] = xp_ref[tp:, :]
    pg = d // len(POOL_WINDOWS)
    pos = i * tp + lax.broadcasted_iota(jnp.int32, (tp, 1), 0)
    for gi, w in enumerate(POOL_WINDOWS):
        cols = slice(gi * pg, (gi + 1) * pg)
        ext = xp_ref[:, cols]
        h = ext[halo:]
        span = 1
        while span < w:
            ext = ext + pltpu.roll(ext, span, axis=0)
            span *= 2
        inv_count = 1.0 / jnp.minimum(pos + 1, w).astype(F32)
        dpool = (ext[halo:] * inv_count - h).astype(BF16)
        y = jnp.dot(dpool, pw_ref[gi].astype(BF16), preferred_element_type=F32)
        o_ref[:, cols] = cur_ref[:, cols] + y * ps_ref[:, cols]


def _pool_prompt(x, g, pw, ps, *, slab, batch, seq, tp=512, halo=16):
    assert all(w & (w - 1) == 0 and w <= halo for w in POOL_WINDOWS)
    d = x.shape[1]
    nt = seq // tp
    per = tp // halo
    cur = pl.BlockSpec((tp, d), lambda bi, i: (bi * nt + i, 0))
    prev = pl.BlockSpec((halo, d), lambda bi, i: (jnp.maximum((bi * nt + i) * per - 1, 0), 0))
    return pl.pallas_call(
        _pool_prompt_kernel,
        grid=(batch, nt),
        in_specs=[cur, prev, _resident((1, d)), _resident_slab(pw.shape, slab), _resident((1, d))],
        out_specs=[cur, pl.BlockSpec((1, halo, d), lambda bi, i: (bi, 0, 0))],
        out_shape=[jax.ShapeDtypeStruct((batch * seq, d), F32), jax.ShapeDtypeStruct((batch, halo, d), F32)],
        scratch_shapes=[pltpu.VMEM((halo + tp, d), F32)],
        compiler_params=_params("parallel", "arbitrary"),
        name="pool_prompt",
    )(x, x, g, pw, ps)


def _pool_sample_kernel(x_ref, st_ref, g_ref, pw_ref, ps_ref, o_ref, ns_ref, h_ref, dp_ref, *, t_new, start_pos):
    g_seqs, n_past, d = st_ref.shape
    pg = d // len(POOL_WINDOWS)
    h_ref[...] = _rms(x_ref[...], g_ref[...])
    t = lax.broadcasted_iota(jnp.int32, (t_new, 1), 0)

    def one_seq(g, carry):
        r0 = pl.multiple_of(g * t_new, t_new)
        h = h_ref[pl.ds(r0, t_new), :]
        for gi, w in enumerate(POOL_WINDOWS):
            cols = slice(gi * pg, (gi + 1) * pg)
            inv = 1.0 / jnp.minimum(start_pos + t + 1, w).astype(F32)
            tot = jnp.zeros((t_new, pg), F32)
            for r in range(max(0, n_past + 1 - w), n_past + t_new):
                if r < n_past:
                    row = st_ref[g, pl.ds(r, 1), cols]
                else:
                    row = h_ref[pl.ds(r0 + (r - n_past), 1), cols]
                hit = jnp.logical_and(t + n_past - w < r, r <= t + n_past)
                tot = tot + jnp.where(hit, inv, 0.0) * row
            dp_ref[pl.ds(r0, t_new), cols] = tot - h[:, cols]
        _append_state(ns_ref, st_ref, g, h, t_new)
        return carry

    lax.fori_loop(0, g_seqs, one_seq, 0, unroll=2)
    for gi in range(len(POOL_WINDOWS)):
        cols = slice(gi * pg, (gi + 1) * pg)
        y = jnp.dot(dp_ref[:, cols].astype(BF16), pw_ref[gi].astype(BF16), preferred_element_type=F32)
        o_ref[:, cols] = x_ref[:, cols] + y * ps_ref[:, cols]


def _pool_sample(x, state, g, pw, ps, *, slab, t_new, g_seqs=16):
    n_seq, n_past, d = state.shape
    assert t_new == SUBLANES and SUBLANES <= n_past < 2 * SUBLANES
    kern = functools.partial(_pool_sample_kernel, t_new=t_new, start_pos=PAST_LEN)
    return pl.pallas_call(
        kern,
        grid=(n_seq // g_seqs,),
        in_specs=[
            pl.BlockSpec((g_seqs * t_new, d), lambda i: (i, 0)),
            pl.BlockSpec((g_seqs, n_past, d), lambda i: (i, 0, 0)),
            _resident((1, d)), _resident_slab(pw.shape, slab), _resident((1, d)),
        ],
        out_specs=[
            pl.BlockSpec((g_seqs * t_new, d), lambda i: (i, 0)),
            pl.BlockSpec((g_seqs, n_past, d), lambda i: (i, 0, 0)),
        ],
        out_shape=[jax.ShapeDtypeStruct((n_seq * t_new, d), F32), jax.ShapeDtypeStruct(state.shape, F32)],
        scratch_shapes=[pltpu.VMEM((g_seqs * t_new, d), F32), pltpu.VMEM((g_seqs * t_new, d), F32)],
        compiler_params=_params("parallel"),
        name="pool_sample",
    )(x, state, g, pw, ps)


def _toeplitz(conv_w, t_new):
    width = conv_w.shape[0]
    idx = jnp.arange(width - 1 + t_new)[:, None] - jnp.arange(t_new)[None, :]
    valid = jnp.logical_and(idx >= 0, idx < width)
    return jnp.where(valid[:, :, None], conv_w[jnp.clip(idx, 0, width - 1)], 0.0)


def kernel(x_prompt, x_sample, cache_k, cache_v, state_conv, state_pool, norm_mix, w_in, q_norm, k_norm, sinks,
           conv_w, conv_b, conv_ln_g, conv_ln_b, w_out, pool_w, pool_scale, norm_ffn, w_gate, w_up, w_down):
    batch, seq, d = x_prompt.shape
    n_seq, t_new, _ = x_sample.shape
    n_kv = cache_k.shape[3]
    kv_w = n_kv * HEAD_DIM
    conv_ch = state_conv.shape[-1]
    attn_w = w_in.shape[-1] - 2 * kv_w - 2 * conv_ch
    n_prompt = batch * seq
    row = lambda a: a.reshape(1, -1)

    xp = x_prompt.reshape(n_prompt, d)
    xs = x_sample.reshape(n_seq * t_new, d)
    depth = norm_mix.shape[0]
    outs = {}
    for layer in range(depth):
        i = layer // 2
        if layer % 2 == 0:
            reps = MXU_DIM // HEAD_DIM
            proj = functools.partial(
                _in_proj, g=row(norm_mix[layer]), w=w_in, slab=i, qg=row(jnp.tile(q_norm[i], reps)),
                kg=row(jnp.tile(k_norm[i], reps)), attn_w=attn_w, kv_w=kv_w, conv_ch=conv_ch)
            q_p, k_p, v_p, glu_p = proj(xp)
            q_s, k_s, v_s, glu_s = proj(xs)
            attn_p = _attn_prompt(sinks[i], q_p, k_p, v_p, batch=batch, seq=seq, n_kv=n_kv)
            attn_s, nk_s, nv_s = _attn_sample(
                sinks[i], q_s, k_s, v_s, cache_k[i].reshape(n_seq, WINDOW, kv_w), cache_v[i].reshape(n_seq, WINDOW, kv_w),
                t_new=t_new, n_kv=n_kv)
            ln = (row(conv_b[i]), row(conv_ln_g[i]), row(conv_ln_b[i]))
            conv_p = _conv_prompt(glu_p, conv_w[i], *ln, batch=batch, seq=seq)
            conv_s, nc_s = _conv_sample(glu_s, state_conv[i], _toeplitz(conv_w[i], t_new), *ln, t_new=t_new)
            xp = _out_proj(xp, attn_p, conv_p, w_out, slab=i)
            xs = _out_proj(xs, attn_s, conv_s, w_out, slab=i)
            last = lambda a, n: a.reshape(batch, seq, -1)[:, -n:]
            outs.setdefault("kp", []).append(last(k_p, WINDOW).reshape(batch, WINDOW, n_kv, HEAD_DIM))
            outs.setdefault("vp", []).append(last(v_p, WINDOW).reshape(batch, WINDOW, n_kv, HEAD_DIM))
            outs.setdefault("cp", []).append(last(glu_p, conv_w.shape[1] - 1))
            outs.setdefault("ks", []).append(nk_s.reshape(n_seq, WINDOW, n_kv, HEAD_DIM))
            outs.setdefault("vs", []).append(nv_s.reshape(n_seq, WINDOW, n_kv, HEAD_DIM))
            outs.setdefault("cs", []).append(nc_s)
        else:
            args = (row(norm_mix[layer]), pool_w, row(pool_scale[i]))
            n_past = state_pool.shape[2]
            xp, tail = _pool_prompt(xp, *args, slab=i, batch=batch, seq=seq)
            xs, np_s = _pool_sample(xs, state_pool[i], *args, slab=i, t_new=t_new)
            outs.setdefault("pp", []).append(tail[:, -n_past:])
            outs.setdefault("ps", []).append(np_s)
        g_ffn = row(norm_ffn[layer])
        xs, wg16, wu16, wd16 = _ffn(xs, g_ffn, w_gate, w_up, w_down, layer=layer, emit_bf16=True)
        xp = _ffn(xp, g_ffn, wg16, wu16, wd16, tf=512)
    st = lambda name: jnp.stack(outs[name])
    return (xp.reshape(batch, seq, d), xs.reshape(n_seq, t_new, d),
            st("kp"), st("vp"), st("cp"), st("pp"), st("ks"), st("vs"), st("cs"), st("ps"))
```

```python
import functools

import jax
import jax.numpy as jnp
from jax import lax
from jax.experimental import pallas as pl
from jax.experimental.pallas import tpu as pltpu

F32 = jnp.float32
BF16 = jnp.bfloat16

HEAD_DIM = 64
WINDOW = 128
PAST_LEN = 8192
POOL_WINDOWS = (2, 4, 8, 16)
RMS_EPS = 1e-6
LN_EPS = 1e-5
NEG = -1e30

LANES = 128
SUBLANES = 8
MXU_DIM = 256
VMEM_LIMIT = 56 * 1024 * 1024


def _params(*semantics):
    return pltpu.CompilerParams(dimension_semantics=semantics, vmem_limit_bytes=VMEM_LIMIT)


def _rms(x, g):
    ms = jnp.mean(x * x, axis=-1, keepdims=True)
    return x * lax.rsqrt(ms + RMS_EPS) * g


def _resident(shape):
    zeros = (0,) * len(shape)
    return pl.BlockSpec(shape, lambda *_: zeros, pipeline_mode=pl.Buffered(1))


def _resident_slab(shape, slab):
    index = (slab,) + (0,) * (len(shape) - 1)
    return pl.BlockSpec((None,) + tuple(shape[1:]), lambda *_: index, pipeline_mode=pl.Buffered(1))


def _in_proj_kernel(x_ref, g_ref, w_ref, qg_ref, kg_ref, q_ref, k_ref, v_ref, glu_ref, *, attn_w, kv_w, conv_ch):
    h = _rms(x_ref[...], g_ref[...]).astype(BF16)
    row = lax.broadcasted_iota(jnp.int32, (MXU_DIM, MXU_DIM), 0) // HEAD_DIM
    col = lax.broadcasted_iota(jnp.int32, (MXU_DIM, MXU_DIM), 1) // HEAD_DIM
    pmat = jnp.where(row == col, 1.0 / HEAD_DIM, 0.0).astype(BF16)

    def proj(c0):
        return jnp.dot(h, w_ref[:, c0:c0 + MXU_DIM].astype(BF16), preferred_element_type=F32)

    def head_rms(z):
        sq = z * z
        hi = sq.astype(BF16)
        lo = (sq - hi.astype(F32)).astype(BF16)
        ms = jnp.dot(hi, pmat, preferred_element_type=F32) + jnp.dot(lo, pmat, preferred_element_type=F32)
        return z * lax.rsqrt(ms + RMS_EPS)

    scale = HEAD_DIM ** -0.5

    def put_q(c0, z):
        q_ref[:, c0:c0 + MXU_DIM] = (head_rms(z) * qg_ref[...] * scale).astype(q_ref.dtype)

    def put_k(c0, z):
        k_ref[:, c0:c0 + MXU_DIM] = head_rms(z) * kg_ref[...]

    def put_v(c0, z):
        v_ref[:, c0:c0 + MXU_DIM] = z

    def put_glu(c0, a, gate):
        glu_ref[:, c0:c0 + MXU_DIM] = a * jax.nn.sigmoid(gate)

    units = [((c0,), functools.partial(put_q, c0)) for c0 in range(0, attn_w, MXU_DIM)]
    units += [((attn_w + c0,), functools.partial(put_k, c0)) for c0 in range(0, kv_w, MXU_DIM)]
    units += [((attn_w + kv_w + c0,), functools.partial(put_v, c0)) for c0 in range(0, kv_w, MXU_DIM)]
    units += [((attn_w + 2 * kv_w + c0, attn_w + 2 * kv_w + conv_ch + c0), functools.partial(put_glu, c0))
              for c0 in range(0, conv_ch, MXU_DIM)]
    pending = None
    for cols, epilogue in units:
        zs = [proj(c0) for c0 in cols]
        if pending is not None:
            pending[0](*pending[1])
        pending = (epilogue, zs)
    pending[0](*pending[1])


def _in_proj(x, g, w, qg, kg, *, slab, attn_w, kv_w, conv_ch, tm=512):
    n, d = x.shape
    kern = functools.partial(_in_proj_kernel, attn_w=attn_w, kv_w=kv_w, conv_ch=conv_ch)
    rows = lambda width: pl.BlockSpec((tm, width), lambda i: (i, 0))
    return pl.pallas_call(
        kern,
        grid=(n // tm,),
        in_specs=[rows(d), _resident((1, d)), _resident_slab(w.shape, slab), _resident((1, MXU_DIM)), _resident((1, MXU_DIM))],
        out_specs=[rows(attn_w), rows(kv_w), rows(kv_w), rows(conv_ch)],
        out_shape=[
            jax.ShapeDtypeStruct((n, attn_w), BF16),
            jax.ShapeDtypeStruct((n, kv_w), F32),
            jax.ShapeDtypeStruct((n, kv_w), F32),
            jax.ShapeDtypeStruct((n, conv_ch), F32),
        ],
        compiler_params=_params("parallel"),
        name="in_proj",
    )(x, g, w, qg, kg)


def _pair_blocks(chunk, rolled, odd):
    lane = lax.broadcasted_iota(jnp.int32, chunk.shape, 1)
    low = lane < HEAD_DIM
    if odd:
        top = jnp.where(low, rolled, 0.0)
        bot = jnp.where(low, 0.0, chunk)
    else:
        top = jnp.where(low, chunk, 0.0)
        bot = jnp.where(low, 0.0, rolled)
    return jnp.concatenate([top, bot], axis=0).astype(BF16)


def _softmax_pv(s, ok, sink_a, sink_b, vb_ones):
    half = s.shape[1] // 2
    s = jnp.where(ok, s, NEG)
    sa, sb = s[:, :half], s[:, half:]
    ma = jnp.maximum(jnp.max(sa, axis=-1, keepdims=True), sink_a)
    mb = jnp.maximum(jnp.max(sb, axis=-1, keepdims=True), sink_b)
    p = jnp.concatenate([jnp.exp(sa - ma), jnp.exp(sb - mb)], axis=1).astype(BF16)
    r = jnp.dot(p, vb_ones, preferred_element_type=F32)
    lane = lax.broadcasted_iota(jnp.int32, (s.shape[0], LANES), 1)
    sink_term = jnp.where(lane < HEAD_DIM, jnp.exp(sink_a - ma), jnp.exp(sink_b - mb))
    return r[:, :LANES] / (r[:, LANES:] + sink_term)


def _attend(q2, kb, vb, ok, sinks_ref, h0, rows):
    s = lax.dot_general(q2, kb, (((1,), (1,)), ((), ())), preferred_element_type=F32)
    row = lax.broadcasted_iota(jnp.int32, (2 * rows, 1), 0)
    first = row < rows
    sink_a = jnp.where(first, sinks_ref[h0], sinks_ref[h0 + 2])
    sink_b = jnp.where(first, sinks_ref[h0 + 1], sinks_ref[h0 + 3])
    return _softmax_pv(s, ok, sink_a, sink_b, vb)


def _attn_prompt_kernel(sinks_ref, q_ref, kp_ref, kc_ref, vp_ref, vc_ref, o_ref, *, n_kv):
    blk = kp_ref.shape[0]
    i = pl.program_id(1)
    shape = (2 * blk, 4 * blk)
    r = lax.broadcasted_iota(jnp.int32, shape, 0) % blk
    c = lax.broadcasted_iota(jnp.int32, shape, 1) % (2 * blk)
    band = jnp.logical_and(c > r, c <= r + blk)
    band_first = jnp.logical_and(band, jnp.logical_or(c >= blk, i > 0))
    group = (q_ref.shape[1] // HEAD_DIM) // n_kv
    key_row = lax.broadcasted_iota(jnp.int32, (4 * blk, LANES), 0)
    key_lane = lax.broadcasted_iota(jnp.int32, (4 * blk, LANES), 1)
    ones_blocks = jnp.where((key_row < 2 * blk) == (key_lane < HEAD_DIM), 1.0, 0.0).astype(BF16)
    for sub in range(q_ref.shape[0] // blk):
        rows = slice(sub * blk, (sub + 1) * blk)
        if sub == 0:
            k_all = jnp.concatenate([kp_ref[...], kc_ref[rows, :]], axis=0)
            v_all = jnp.concatenate([vp_ref[...], vc_ref[rows, :]], axis=0)
        else:
            k_all = kc_ref[(sub - 1) * blk:(sub + 1) * blk, :]
            v_all = vc_ref[(sub - 1) * blk:(sub + 1) * blk, :]
        ok = band if sub else band_first
        for cch in range(n_kv // 2):
            kc = k_all[:, cch * LANES:(cch + 1) * LANES]
            vc = v_all[:, cch * LANES:(cch + 1) * LANES]
            kr = pltpu.roll(kc, HEAD_DIM, axis=1)
            vr = pltpu.roll(vc, HEAD_DIM, axis=1)
            for odd in range(2):
                j = 2 * cch + odd
                kb = _pair_blocks(kc, kr, odd)
                vb = jnp.concatenate([_pair_blocks(vc, vr, odd), ones_blocks], axis=1)
                h0 = group * j
                c0 = h0 * HEAD_DIM
                q2 = jnp.concatenate([q_ref[rows, c0:c0 + LANES], q_ref[rows, c0 + LANES:c0 + 2 * LANES]], axis=0)
                o = _attend(q2, kb, vb, ok, sinks_ref, h0, blk).astype(o_ref.dtype)
                o_ref[rows, c0:c0 + LANES] = o[:blk]
                o_ref[rows, c0 + LANES:c0 + 2 * LANES] = o[blk:]


def _attn_prompt(sinks, q, k, v, *, batch, seq, n_kv, blocks_per_step=4):
    blk = WINDOW
    tq = blocks_per_step * blk
    nb = seq // tq
    aw, kw = q.shape[1], k.shape[1]
    cur = lambda w: pl.BlockSpec((tq, w), lambda b, i: (b * nb + i, 0))
    prev = lambda w: pl.BlockSpec(
        (blk, w), lambda b, i: ((b * nb + i) * blocks_per_step - jnp.minimum(i, 1), 0))
    return pl.pallas_call(
        functools.partial(_attn_prompt_kernel, n_kv=n_kv),
        grid=(batch, nb),
        in_specs=[pl.BlockSpec(memory_space=pltpu.SMEM), cur(aw), prev(kw), cur(kw), prev(kw), cur(kw)],
        out_specs=cur(aw),
        out_shape=jax.ShapeDtypeStruct((batch * seq, aw), BF16),
        compiler_params=_params("parallel", "arbitrary"),
        name="attn_prompt",
    )(sinks, q, k, k, v, v)


def _attn_sample_kernel(sinks_ref, q_ref, kn_ref, vn_ref, ck_ref, cv_ref, o_ref, nk_ref, nv_ref, qf_ref, of_ref, *, n_kv, t_new):
    g_seqs, _, kw = ck_ref.shape
    n_heads = q_ref.shape[1] // HEAD_DIM
    group = n_heads // n_kv
    keep = WINDOW - t_new
    n_rows = n_heads * t_new
    n_keys = 2 * WINDOW
    qf_ref[...] = q_ref[...].astype(F32)
    t = lax.broadcasted_iota(jnp.int32, (n_rows, n_keys), 0) % t_new
    c = lax.broadcasted_iota(jnp.int32, (n_rows, n_keys), 1)
    ok = jnp.logical_and(c > t, c <= t + WINDOW)
    sink = jnp.concatenate([jnp.full((t_new, 1), sinks_ref[h], F32) for h in range(n_heads)], axis=0)
    low = lax.broadcasted_iota(jnp.int32, (t_new, LANES), 1) < HEAD_DIM
    zero_tile = jnp.zeros((t_new, LANES), F32)
    zero_keys = jnp.zeros((n_keys - WINDOW - t_new, kw), F32)

    def one_seq(g, carry):
        r0 = pl.multiple_of(g * t_new, t_new)
        k_new = kn_ref[pl.ds(r0, t_new), :]
        v_new = vn_ref[pl.ds(r0, t_new), :]
        k_old = ck_ref[g]
        v_old = cv_ref[g]
        nk_ref[g, :keep, :] = k_old[t_new:]
        nk_ref[g, keep:, :] = k_new
        nv_ref[g, :keep, :] = v_old[t_new:]
        nv_ref[g, keep:, :] = v_new
        keys = jnp.concatenate([k_old, k_new, zero_keys], axis=0).astype(BF16)
        vals = jnp.concatenate([v_old, v_new, zero_keys], axis=0).astype(BF16)
        vals = jnp.concatenate([vals, jnp.ones((n_keys, LANES), BF16)], axis=1)
        q_rows = []
        for h in range(n_heads):
            j = h // group
            src = qf_ref[pl.ds(r0, t_new), (h // 2) * LANES:(h // 2 + 1) * LANES]
            if h % 2 != j % 2:
                src = pltpu.roll(src, HEAD_DIM, axis=1)
            half = jnp.where(low, src, 0.0) if j % 2 == 0 else jnp.where(low, 0.0, src)
            tiles = [zero_tile] * (kw // LANES)
            tiles[j // 2] = half
            q_rows.append(jnp.concatenate(tiles, axis=1))
        qb = jnp.concatenate(q_rows, axis=0).astype(BF16)
        s = lax.dot_general(qb, keys, (((1,), (1,)), ((), ())), preferred_element_type=F32)
        s = jnp.where(ok, s, NEG)
        m = jnp.maximum(jnp.max(s, axis=-1, keepdims=True), sink)
        r = jnp.dot(jnp.exp(s - m).astype(BF16), vals, preferred_element_type=F32)
        denom = r[:, kw:] + jnp.exp(sink - m)
        o = r[:, :kw] / jnp.concatenate([denom] * (kw // LANES), axis=1)
        for pair in range(n_heads // 2):
            j = (2 * pair) // group
            lanes = slice((j // 2) * LANES, (j // 2 + 1) * LANES)
            a = o[2 * pair * t_new:(2 * pair + 1) * t_new, lanes]
            b = o[(2 * pair + 1) * t_new:(2 * pair + 2) * t_new, lanes]
            if j % 2 == 0:
                b = pltpu.roll(b, HEAD_DIM, axis=1)
            else:
                a = pltpu.roll(a, HEAD_DIM, axis=1)
            of_ref[pl.ds(r0, t_new), pair * LANES:(pair + 1) * LANES] = jnp.where(low, a, b)
        return carry

    lax.fori_loop(0, g_seqs, one_seq, 0, unroll=4)
    o_ref[...] = of_ref[...].astype(o_ref.dtype)


def _attn_sample(sinks, q, k, v, cache_k, cache_v, *, t_new, n_kv, g_seqs=16):
    n_seq = cache_k.shape[0]
    aw, kw = q.shape[1], k.shape[1]
    tok = lambda w: pl.BlockSpec((g_seqs * t_new, w), lambda i: (i, 0))
    cache = pl.BlockSpec((g_seqs, WINDOW, kw), lambda i: (i, 0, 0))
    return pl.pallas_call(
        functools.partial(_attn_sample_kernel, n_kv=n_kv, t_new=t_new),
        grid=(n_seq // g_seqs,),
        in_specs=[pl.BlockSpec(memory_space=pltpu.SMEM), tok(aw), tok(kw), tok(kw), cache, cache],
        out_specs=[tok(aw), cache, cache],
        out_shape=[
            jax.ShapeDtypeStruct((n_seq * t_new, aw), BF16),
            jax.ShapeDtypeStruct(cache_k.shape, F32),
            jax.ShapeDtypeStruct(cache_v.shape, F32),
        ],
        scratch_shapes=[pltpu.VMEM((g_seqs * t_new, aw), F32), pltpu.VMEM((g_seqs * t_new, aw), F32)],
        compiler_params=_params("parallel"),
        name="attn_sample",
    )(sinks, q, k, v, cache_k, cache_v)


def _ln_silu(y, g, b):
    yc = y - jnp.mean(y, axis=-1, keepdims=True)
    var = jnp.mean(yc * yc, axis=-1, keepdims=True)
    return jax.nn.silu(yc * lax.rsqrt(var + LN_EPS) * g + b)


def _conv_prompt_kernel(cur_ref, halo_ref, w_ref, b_ref, g_ref, beta_ref, o_ref, xc_ref, y_ref, *, width, rows_per, lanes_per):
    tc, ch = cur_ref.shape
    halo = halo_ref.shape[0]
    i = pl.program_id(1)
    xc_ref[:halo, :] = jnp.where(i == 0, 0.0, halo_ref[...])
    xc_ref[halo:, :] = cur_ref[...]
    base = halo - (width - 1)

    def col_chunk(cc, carry):
        l0 = pl.multiple_of(cc * lanes_per, lanes_per)
        cols = pl.ds(l0, lanes_per)
        for rr in range(tc // rows_per):
            acc = jnp.broadcast_to(b_ref[:, cols], (rows_per, lanes_per))
            for phase in range(SUBLANES):
                n_win = rows_per + (SUBLANES if phase else 0)
                part = None
                for j in range(width):
                    if (base + j) % SUBLANES != phase:
                        continue
                    start = rr * rows_per + ((base + j) // SUBLANES) * SUBLANES
                    term = w_ref[pl.ds(j, 1), cols] * xc_ref[pl.ds(start, n_win), cols]
                    part = term if part is None else part + term
                if part is None:
                    continue
                if phase:
                    part = pltpu.roll(part, n_win - phase, axis=0)
                acc = acc + part[:rows_per]
            y_ref[pl.ds(rr * rows_per, rows_per), cols] = acc
        return carry

    lax.fori_loop(0, ch // lanes_per, col_chunk, 0)
    o_ref[...] = _ln_silu(y_ref[...], g_ref[...], beta_ref[...]).astype(o_ref.dtype)


def _conv_prompt(glu, w, b, g, beta, *, batch, seq, tc=512, halo=32):
    ch = glu.shape[1]
    width = w.shape[0]
    nt = seq // tc
    per = tc // halo
    cur = pl.BlockSpec((tc, ch), lambda bi, i: (bi * nt + i, 0))
    prev = pl.BlockSpec((halo, ch), lambda bi, i: (jnp.maximum((bi * nt + i) * per - 1, 0), 0))
    kern = functools.partial(_conv_prompt_kernel, width=width, rows_per=64, lanes_per=LANES)
    return pl.pallas_call(
        kern,
        grid=(batch, nt),
        in_specs=[cur, prev, _resident(w.shape), _resident((1, ch)), _resident((1, ch)), _resident((1, ch))],
        out_specs=cur,
        out_shape=jax.ShapeDtypeStruct((batch * seq, ch), BF16),
        scratch_shapes=[pltpu.VMEM((halo + tc, ch), F32), pltpu.VMEM((tc, ch), F32)],
        compiler_params=_params("parallel", "arbitrary"),
        name="conv_prompt",
    )(glu, glu, w, b, g, beta)


def _append_state(ns_ref, st_ref, g, new, t_new):
    n_past = st_ref.shape[1]
    keep = n_past - t_new
    full = (keep // SUBLANES) * SUBLANES
    rem = keep - full
    if full:
        ns_ref[g, :full, :] = st_ref[g, t_new:t_new + full, :]
    if rem:
        rolled = pltpu.roll(new, rem, axis=0)
        ns_ref[g, full:full + SUBLANES, :] = rolled
        ns_ref[g, full:keep, :] = st_ref[g, t_new + full:n_past, :]
        ns_ref[g, full + SUBLANES:, :] = rolled[:rem]
    else:
        ns_ref[g, full:, :] = new


def _conv_sample_kernel(new_ref, st_ref, toep_ref, b_ref, g_ref, beta_ref, o_ref, ns_ref, of_ref, *, t_new):
    g_seqs, n_past, ch = st_ref.shape

    def one_seq(g, carry):
        r0 = pl.multiple_of(g * t_new, t_new)
        new = new_ref[pl.ds(r0, t_new), :]
        acc = jnp.broadcast_to(b_ref[...], (t_new, ch))
        for r in range(n_past):
            acc = acc + toep_ref[r] * st_ref[g, pl.ds(r, 1), :]
        for r in range(t_new):
            acc = acc + toep_ref[n_past + r] * new_ref[pl.ds(r0 + r, 1), :]
        of_ref[pl.ds(r0, t_new), :] = _ln_silu(acc, g_ref[...], beta_ref[...])
        _append_state(ns_ref, st_ref, g, new, t_new)
        return carry

    lax.fori_loop(0, g_seqs, one_seq, 0, unroll=4)
    o_ref[...] = of_ref[...].astype(o_ref.dtype)


def _conv_sample(glu, state, toep, b, g, beta, *, t_new, g_seqs=16):
    n_seq, n_past, ch = state.shape
    return pl.pallas_call(
        functools.partial(_conv_sample_kernel, t_new=t_new),
        grid=(n_seq // g_seqs,),
        in_specs=[
            pl.BlockSpec((g_seqs * t_new, ch), lambda i: (i, 0)),
            pl.BlockSpec((g_seqs, n_past, ch), lambda i: (i, 0, 0)),
            _resident(toep.shape), _resident((1, ch)), _resident((1, ch)), _resident((1, ch)),
        ],
        out_specs=[
            pl.BlockSpec((g_seqs * t_new, ch), lambda i: (i, 0)),
            pl.BlockSpec((g_seqs, n_past, ch), lambda i: (i, 0, 0)),
        ],
        out_shape=[jax.ShapeDtypeStruct((n_seq * t_new, ch), BF16), jax.ShapeDtypeStruct(state.shape, F32)],
        scratch_shapes=[pltpu.VMEM((g_seqs * t_new, ch), F32)],
        compiler_params=_params("parallel"),
        name="conv_sample",
    )(glu, state, toep, b, g, beta)


def _out_proj_kernel(x_ref, a_ref, c_ref, w_ref, o_ref):
    aw = a_ref.shape[1]
    o_ref[...] = (x_ref[...]
                  + jnp.dot(a_ref[...], w_ref[:aw, :].astype(BF16), preferred_element_type=F32)
                  + jnp.dot(c_ref[...], w_ref[aw:, :].astype(BF16), preferred_element_type=F32))


def _out_proj(x, attn, conv, w, *, slab, tm=512):
    n, d = x.shape
    rows = lambda width: pl.BlockSpec((tm, width), lambda i: (i, 0))
    return pl.pallas_call(
        _out_proj_kernel,
        grid=(n // tm,),
        in_specs=[rows(d), rows(attn.shape[1]), rows(conv.shape[1]), _resident_slab(w.shape, slab)],
        out_specs=rows(d),
        out_shape=jax.ShapeDtypeStruct((n, d), F32),
        compiler_params=_params("parallel"),
        name="out_proj",
    )(x, attn, conv, w)


def _ffn_kernel(x_ref, g_ref, wg_ref, wu_ref, wd_ref, o_ref, *rest):
    h_ref = rest[-1]

    @pl.when(pl.program_id(1) == 0)
    def _():
        x = x_ref[...]
        h_ref[...] = _rms(x, g_ref[...]).astype(BF16)
        o_ref[...] = x

    weights = [w_ref[...].astype(BF16) for w_ref in (wg_ref, wu_ref, wd_ref)]
    for w_out, w in zip(rest[:-1], weights):
        w_out[...] = w
    wg, wu, wd = weights
    h = h_ref[...]
    a = jnp.dot(h, wg, preferred_element_type=F32)
    u = jnp.dot(h, wu, preferred_element_type=F32)
    act = (jax.nn.silu(a) * u).astype(BF16)
    o_ref[...] += jnp.dot(act, wd, preferred_element_type=F32)


def _ffn(x, g, wg, wu, wd, *, layer=None, emit_bf16=False, tm=1024, tf=256):
    n, d = x.shape
    dff = wg.shape[-1]
    nt, nf = n // tm, dff // tf
    if layer is None:
        w_in_specs = [pl.BlockSpec((d, tf), lambda i, f: (0, f)), pl.BlockSpec((d, tf), lambda i, f: (0, f)),
                      pl.BlockSpec((tf, d), lambda i, f: (f, 0))]
    else:
        w_in_specs = [pl.BlockSpec((None, d, tf), lambda i, f: (layer, 0, f)),
                      pl.BlockSpec((None, d, tf), lambda i, f: (layer, 0, f)),
                      pl.BlockSpec((None, tf, d), lambda i, f: (layer, f, 0))]
    rows = pl.BlockSpec((tm, d), lambda i, f: (i, 0))
    out_specs, out_shape = [rows], [jax.ShapeDtypeStruct((n, d), F32)]
    if emit_bf16:
        assert nt == 1, "one row tile, so every bf16 weight tile is written exactly once"
        out_specs += [pl.BlockSpec((d, tf), lambda i, f: (0, f)), pl.BlockSpec((d, tf), lambda i, f: (0, f)),
                      pl.BlockSpec((tf, d), lambda i, f: (f, 0))]
        out_shape += [jax.ShapeDtypeStruct((d, dff), BF16), jax.ShapeDtypeStruct((d, dff), BF16),
                      jax.ShapeDtypeStruct((dff, d), BF16)]
    x_spec = _resident((tm, d)) if nt == 1 else rows
    outs = pl.pallas_call(
        _ffn_kernel,
        grid=(nt, nf),
        in_specs=[x_spec, _resident((1, d))] + w_in_specs,
        out_specs=out_specs,
        out_shape=out_shape,
        scratch_shapes=[pltpu.VMEM((tm, d), BF16)],
        compiler_params=_params("parallel", "arbitrary"),
        name="ffn",
    )(x, g, wg, wu, wd)
    return outs if emit_bf16 else outs[0]


def _pool_prompt_kernel(cur_ref, halo_ref, g_ref, pw_ref, ps_ref, o_ref, tail_ref, xp_ref):
    tp, d = cur_ref.shape
    halo = halo_ref.shape[0]
    i = pl.program_id(1)
    gain = g_ref[...]
    xp_ref[:halo, :] = jnp.where(i == 0, 0.0, _rms(halo_ref[...], gain))
    xp_ref[halo:, :] = _rms(cur_ref[...], gain)
    tail_ref[0] = xp_ref[tp:, :]
    pg = d // len(POOL_WINDOWS)
    pos = i * tp + lax.broadcasted_iota(jnp.int32, (tp, 1), 0)
    for gi, w in enumerate(POOL_WINDOWS):
        cols = slice(gi * pg, (gi + 1) * pg)
        ext = xp_ref[:, cols]
        h = ext[halo:]
        span = 1
        while span < w:
            ext = ext + pltpu.roll(ext, span, axis=0)
            span *= 2
        inv_count = 1.0 / jnp.minimum(pos + 1, w).astype(F32)
        dpool = (ext[halo:] * inv_count - h).astype(BF16)
        y = jnp.dot(dpool, pw_ref[gi].astype(BF16), preferred_element_type=F32)
        o_ref[:, cols] = cur_ref[:, cols] + y * ps_ref[:, cols]


def _pool_prompt(x, g, pw, ps, *, slab, batch, seq, tp=512, halo=16):
    assert all(w & (w - 1) == 0 and w <= halo for w in POOL_WINDOWS)
    d = x.shape[1]
    nt = seq // tp
    per = tp // halo
    cur = pl.BlockSpec((tp, d), lambda bi, i: (bi * nt + i, 0))
    prev = pl.BlockSpec((halo, d), lambda bi, i: (jnp.maximum((bi * nt + i) * per - 1, 0), 0))
    return pl.pallas_call(
        _pool_prompt_kernel,
        grid=(batch, nt),
        in_specs=[cur, prev, _resident((1, d)), _resident_slab(pw.shape, slab), _resident((1, d))],
        out_specs=[cur, pl.BlockSpec((1, halo, d), lambda bi, i: (bi, 0, 0))],
        out_shape=[jax.ShapeDtypeStruct((batch * seq, d), F32), jax.ShapeDtypeStruct((batch, halo, d), F32)],
        scratch_shapes=[pltpu.VMEM((halo + tp, d), F32)],
        compiler_params=_params("parallel", "arbitrary"),
        name="pool_prompt",
    )(x, x, g, pw, ps)


def _pool_sample_kernel(x_ref, st_ref, g_ref, pw_ref, ps_ref, o_ref, ns_ref, h_ref, dp_ref, *, t_new, start_pos):
    g_seqs, n_past, d = st_ref.shape
    pg = d // len(POOL_WINDOWS)
    h_ref[...] = _rms(x_ref[...], g_ref[...])
    t = lax.broadcasted_iota(jnp.int32, (t_new, 1), 0)

    def one_seq(g, carry):
        r0 = pl.multiple_of(g * t_new, t_new)
        h = h_ref[pl.ds(r0, t_new), :]
        for gi, w in enumerate(POOL_WINDOWS):
            cols = slice(gi * pg, (gi + 1) * pg)
            inv = 1.0 / jnp.minimum(start_pos + t + 1, w).astype(F32)
            tot = jnp.zeros((t_new, pg), F32)
            for r in range(max(0, n_past + 1 - w), n_past + t_new):
                if r < n_past:
                    row = st_ref[g, pl.ds(r, 1), cols]
                else:
                    row = h_ref[pl.ds(r0 + (r - n_past), 1), cols]
                hit = jnp.logical_and(t + n_past - w < r, r <= t + n_past)
                tot = tot + jnp.where(hit, inv, 0.0) * row
            dp_ref[pl.ds(r0, t_new), cols] = tot - h[:, cols]
        _append_state(ns_ref, st_ref, g, h, t_new)
        return carry

    lax.fori_loop(0, g_seqs, one_seq, 0, unroll=4)
    for gi in range(len(POOL_WINDOWS)):
        cols = slice(gi * pg, (gi + 1) * pg)
        y = jnp.dot(dp_ref[:, cols].astype(BF16), pw_ref[gi].astype(BF16), preferred_element_type=F32)
        o_ref[:, cols] = x_ref[:, cols] + y * ps_ref[:, cols]


def _pool_sample(x, state, g, pw, ps, *, slab, t_new, g_seqs=16):
    n_seq, n_past, d = state.shape
    assert t_new == SUBLANES and SUBLANES <= n_past < 2 * SUBLANES
    kern = functools.partial(_pool_sample_kernel, t_new=t_new, start_pos=PAST_LEN)
    return pl.pallas_call(
        kern,
        grid=(n_seq // g_seqs,),
        in_specs=[
            pl.BlockSpec((g_seqs * t_new, d), lambda i: (i, 0)),
            pl.BlockSpec((g_seqs, n_past, d), lambda i: (i, 0, 0)),
            _resident((1, d)), _resident_slab(pw.shape, slab), _resident((1, d)),
        ],
        out_specs=[
            pl.BlockSpec((g_seqs * t_new, d), lambda i: (i, 0)),
            pl.BlockSpec((g_seqs, n_past, d), lambda i: (i, 0, 0)),
        ],
        out_shape=[jax.ShapeDtypeStruct((n_seq * t_new, d), F32), jax.ShapeDtypeStruct(state.shape, F32)],
        scratch_shapes=[pltpu.VMEM((g_seqs * t_new, d), F32), pltpu.VMEM((g_seqs * t_new, d), F32)],
        compiler_params=_params("parallel"),
        name="pool_sample",
    )(x, state, g, pw, ps)


def _toeplitz(conv_w, t_new):
    width = conv_w.shape[0]
    idx = jnp.arange(width - 1 + t_new)[:, None] - jnp.arange(t_new)[None, :]
    valid = jnp.logical_and(idx >= 0, idx < width)
    return jnp.where(valid[:, :, None], conv_w[jnp.clip(idx, 0, width - 1)], 0.0)


def kernel(x_prompt, x_sample, cache_k, cache_v, state_conv, state_pool, norm_mix, w_in, q_norm, k_norm, sinks,
           conv_w, conv_b, conv_ln_g, conv_ln_b, w_out, pool_w, pool_scale, norm_ffn, w_gate, w_up, w_down):
    batch, seq, d = x_prompt.shape
    n_seq, t_new, _ = x_sample.shape
    n_kv = cache_k.shape[3]
    kv_w = n_kv * HEAD_DIM
    conv_ch = state_conv.shape[-1]
    attn_w = w_in.shape[-1] - 2 * kv_w - 2 * conv_ch
    n_prompt = batch * seq
    row = lambda a: a.reshape(1, -1)

    xp = x_prompt.reshape(n_prompt, d)
    xs = x_sample.reshape(n_seq * t_new, d)
    depth = norm_mix.shape[0]
    outs = {}
    for layer in range(depth):
        i = layer // 2
        if layer % 2 == 0:
            reps = MXU_DIM // HEAD_DIM
            proj = functools.partial(
                _in_proj, g=row(norm_mix[layer]), w=w_in, slab=i, qg=row(jnp.tile(q_norm[i], reps)),
                kg=row(jnp.tile(k_norm[i], reps)), attn_w=attn_w, kv_w=kv_w, conv_ch=conv_ch)
            q_p, k_p, v_p, glu_p = proj(xp)
            q_s, k_s, v_s, glu_s = proj(xs)
            attn_p = _attn_prompt(sinks[i], q_p, k_p, v_p, batch=batch, seq=seq, n_kv=n_kv)
            attn_s, nk_s, nv_s = _attn_sample(
                sinks[i], q_s, k_s, v_s, cache_k[i].reshape(n_seq, WINDOW, kv_w), cache_v[i].reshape(n_seq, WINDOW, kv_w),
                t_new=t_new, n_kv=n_kv)
            ln = (row(conv_b[i]), row(conv_ln_g[i]), row(conv_ln_b[i]))
            conv_p = _conv_prompt(glu_p, conv_w[i], *ln, batch=batch, seq=seq)
            conv_s, nc_s = _conv_sample(glu_s, state_conv[i], _toeplitz(conv_w[i], t_new), *ln, t_new=t_new)
            xp = _out_proj(xp, attn_p, conv_p, w_out, slab=i)
            xs = _out_proj(xs, attn_s, conv_s, w_out, slab=i)
            last = lambda a, n: a.reshape(batch, seq, -1)[:, -n:]
            outs.setdefault("kp", []).append(last(k_p, WINDOW).reshape(batch, WINDOW, n_kv, HEAD_DIM))
            outs.setdefault("vp", []).append(last(v_p, WINDOW).reshape(batch, WINDOW, n_kv, HEAD_DIM))
            outs.setdefault("cp", []).append(last(glu_p, conv_w.shape[1] - 1))
            outs.setdefault("ks", []).append(nk_s.reshape(n_seq, WINDOW, n_kv, HEAD_DIM))
            outs.setdefault("vs", []).append(nv_s.reshape(n_seq, WINDOW, n_kv, HEAD_DIM))
            outs.setdefault("cs", []).append(nc_s)
        else:
            args = (row(norm_mix[layer]), pool_w, row(pool_scale[i]))
            n_past = state_pool.shape[2]
            xp, tail = _pool_prompt(xp, *args, slab=i, batch=batch, seq=seq)
            xs, np_s = _pool_sample(xs, state_pool[i], *args, slab=i, t_new=t_new)
            outs.setdefault("pp", []).append(tail[:, -n_past:])
            outs.setdefault("ps", []).append(np_s)
        g_ffn = row(norm_ffn[layer])
        xs, wg16, wu16, wd16 = _ffn(xs, g_ffn, w_gate, w_up, w_down, layer=layer, emit_bf16=True)
        xp = _ffn(xp, g_ffn, wg16, wu16, wd16, tf=512)
    st = lambda name: jnp.stack(outs[name])
    return (xp.reshape(batch, seq, d), xs.reshape(n_seq, t_new, d),
            st("kp"), st("vp"), st("cp"), st("pp"), st("ks"), st("vs"), st("cs"), st("ps"))
```

```python
import functools

import jax
import jax.numpy as jnp
from jax import lax
from jax.experimental import pallas as pl
from jax.experimental.pallas import tpu as pltpu

F32 = jnp.float32
BF16 = jnp.bfloat16

HEAD_DIM = 64
WINDOW = 128
PAST_LEN = 8192
POOL_WINDOWS = (2, 4, 8, 16)
RMS_EPS = 1e-6
LN_EPS = 1e-5
NEG = -1e30

LANES = 128
SUBLANES = 8
MXU_DIM = 256
VMEM_LIMIT = 56 * 1024 * 1024


def _params(*semantics):
    return pltpu.CompilerParams(dimension_semantics=semantics, vmem_limit_bytes=VMEM_LIMIT)


def _rms(x, g):
    ms = jnp.mean(x * x, axis=-1, keepdims=True)
    return x * lax.rsqrt(ms + RMS_EPS) * g


def _resident(shape):
    zeros = (0,) * len(shape)
    return pl.BlockSpec(shape, lambda *_: zeros, pipeline_mode=pl.Buffered(1))


def _resident_slab(shape, slab):
    index = (slab,) + (0,) * (len(shape) - 1)
    return pl.BlockSpec((None,) + tuple(shape[1:]), lambda *_: index, pipeline_mode=pl.Buffered(1))


def _in_proj_kernel(x_ref, g_ref, w_ref, qg_ref, kg_ref, q_ref, k_ref, v_ref, glu_ref, *, attn_w, kv_w, conv_ch):
    h = _rms(x_ref[...], g_ref[...]).astype(BF16)
    row = lax.broadcasted_iota(jnp.int32, (MXU_DIM, MXU_DIM), 0) // HEAD_DIM
    col = lax.broadcasted_iota(jnp.int32, (MXU_DIM, MXU_DIM), 1) // HEAD_DIM
    pmat = jnp.where(row == col, 1.0 / HEAD_DIM, 0.0).astype(BF16)

    def proj(c0):
        return jnp.dot(h, w_ref[:, c0:c0 + MXU_DIM].astype(BF16), preferred_element_type=F32)

    def head_rms(z):
        sq = z * z
        hi = sq.astype(BF16)
        lo = (sq - hi.astype(F32)).astype(BF16)
        ms = jnp.dot(hi, pmat, preferred_element_type=F32) + jnp.dot(lo, pmat, preferred_element_type=F32)
        return z * lax.rsqrt(ms + RMS_EPS)

    scale = HEAD_DIM ** -0.5

    def put_q(c0, z):
        q_ref[:, c0:c0 + MXU_DIM] = (head_rms(z) * qg_ref[...] * scale).astype(q_ref.dtype)

    def put_k(c0, z):
        k_ref[:, c0:c0 + MXU_DIM] = head_rms(z) * kg_ref[...]

    def put_v(c0, z):
        v_ref[:, c0:c0 + MXU_DIM] = z

    def put_glu(c0, a, gate):
        glu_ref[:, c0:c0 + MXU_DIM] = a * jax.nn.sigmoid(gate)

    units = [((c0,), functools.partial(put_q, c0)) for c0 in range(0, attn_w, MXU_DIM)]
    units += [((attn_w + c0,), functools.partial(put_k, c0)) for c0 in range(0, kv_w, MXU_DIM)]
    units += [((attn_w + kv_w + c0,), functools.partial(put_v, c0)) for c0 in range(0, kv_w, MXU_DIM)]
    units += [((attn_w + 2 * kv_w + c0, attn_w + 2 * kv_w + conv_ch + c0), functools.partial(put_glu, c0))
              for c0 in range(0, conv_ch, MXU_DIM)]
    pending = None
    for cols, epilogue in units:
        zs = [proj(c0) for c0 in cols]
        if pending is not None:
            pending[0](*pending[1])
        pending = (epilogue, zs)
    pending[0](*pending[1])


def _in_proj(x, g, w, qg, kg, *, slab, attn_w, kv_w, conv_ch, tm=512):
    n, d = x.shape
    kern = functools.partial(_in_proj_kernel, attn_w=attn_w, kv_w=kv_w, conv_ch=conv_ch)
    rows = lambda width: pl.BlockSpec((tm, width), lambda i: (i, 0))
    return pl.pallas_call(
        kern,
        grid=(n // tm,),
        in_specs=[rows(d), _resident((1, d)), _resident_slab(w.shape, slab), _resident((1, MXU_DIM)), _resident((1, MXU_DIM))],
        out_specs=[rows(attn_w), rows(kv_w), rows(kv_w), rows(conv_ch)],
        out_shape=[
            jax.ShapeDtypeStruct((n, attn_w), BF16),
            jax.ShapeDtypeStruct((n, kv_w), F32),
            jax.ShapeDtypeStruct((n, kv_w), F32),
            jax.ShapeDtypeStruct((n, conv_ch), F32),
        ],
        compiler_params=_params("parallel"),
        name="in_proj",
    )(x, g, w, qg, kg)


def _pair_blocks(chunk, rolled, odd):
    lane = lax.broadcasted_iota(jnp.int32, chunk.shape, 1)
    low = lane < HEAD_DIM
    if odd:
        top = jnp.where(low, rolled, 0.0)
        bot = jnp.where(low, 0.0, chunk)
    else:
        top = jnp.where(low, chunk, 0.0)
        bot = jnp.where(low, 0.0, rolled)
    return jnp.concatenate([top, bot], axis=0).astype(BF16)


def _softmax_pv(s, ok, sink_a, sink_b, vb_ones):
    half = s.shape[1] // 2
    s = jnp.where(ok, s, NEG)
    sa, sb = s[:, :half], s[:, half:]
    ma = jnp.maximum(jnp.max(sa, axis=-1, keepdims=True), sink_a)
    mb = jnp.maximum(jnp.max(sb, axis=-1, keepdims=True), sink_b)
    p = jnp.concatenate([jnp.exp(sa - ma), jnp.exp(sb - mb)], axis=1).astype(BF16)
    r = jnp.dot(p, vb_ones, preferred_element_type=F32)
    lane = lax.broadcasted_iota(jnp.int32, (s.shape[0], LANES), 1)
    sink_term = jnp.where(lane < HEAD_DIM, jnp.exp(sink_a - ma), jnp.exp(sink_b - mb))
    return r[:, :LANES] / (r[:, LANES:] + sink_term)


def _attend(q2, kb, vb, ok, sinks_ref, h0, rows):
    s = lax.dot_general(q2, kb, (((1,), (1,)), ((), ())), preferred_element_type=F32)
    row = lax.broadcasted_iota(jnp.int32, (2 * rows, 1), 0)
    first = row < rows
    sink_a = jnp.where(first, sinks_ref[h0], sinks_ref[h0 + 2])
    sink_b = jnp.where(first, sinks_ref[h0 + 1], sinks_ref[h0 + 3])
    return _softmax_pv(s, ok, sink_a, sink_b, vb)


def _attn_prompt_kernel(sinks_ref, q_ref, kp_ref, kc_ref, vp_ref, vc_ref, o_ref, *, n_kv):
    blk = kp_ref.shape[0]
    i = pl.program_id(1)
    shape = (2 * blk, 4 * blk)
    r = lax.broadcasted_iota(jnp.int32, shape, 0) % blk
    c = lax.broadcasted_iota(jnp.int32, shape, 1) % (2 * blk)
    band = jnp.logical_and(c > r, c <= r + blk)
    band_first = jnp.logical_and(band, jnp.logical_or(c >= blk, i > 0))
    group = (q_ref.shape[1] // HEAD_DIM) // n_kv
    key_row = lax.broadcasted_iota(jnp.int32, (4 * blk, LANES), 0)
    key_lane = lax.broadcasted_iota(jnp.int32, (4 * blk, LANES), 1)
    ones_blocks = jnp.where((key_row < 2 * blk) == (key_lane < HEAD_DIM), 1.0, 0.0).astype(BF16)
    for sub in range(q_ref.shape[0] // blk):
        rows = slice(sub * blk, (sub + 1) * blk)
        if sub == 0:
            k_all = jnp.concatenate([kp_ref[...], kc_ref[rows, :]], axis=0)
            v_all = jnp.concatenate([vp_ref[...], vc_ref[rows, :]], axis=0)
        else:
            k_all = kc_ref[(sub - 1) * blk:(sub + 1) * blk, :]
            v_all = vc_ref[(sub - 1) * blk:(sub + 1) * blk, :]
        ok = band if sub else band_first
        for cch in range(n_kv // 2):
            kc = k_all[:, cch * LANES:(cch + 1) * LANES]
            vc = v_all[:, cch * LANES:(cch + 1) * LANES]
            kr = pltpu.roll(kc, HEAD_DIM, axis=1)
            vr = pltpu.roll(vc, HEAD_DIM, axis=1)
            for odd in range(2):
                j = 2 * cch + odd
                kb = _pair_blocks(kc, kr, odd)
                vb = jnp.concatenate([_pair_blocks(vc, vr, odd), ones_blocks], axis=1)
                h0 = group * j
                c0 = h0 * HEAD_DIM
                q2 = jnp.concatenate([q_ref[rows, c0:c0 + LANES], q_ref[rows, c0 + LANES:c0 + 2 * LANES]], axis=0)
                o = _attend(q2, kb, vb, ok, sinks_ref, h0, blk).astype(o_ref.dtype)
                o_ref[rows, c0:c0 + LANES] = o[:blk]
                o_ref[rows, c0 + LANES:c0 + 2 * LANES] = o[blk:]


def _attn_prompt(sinks, q, k, v, *, batch, seq, n_kv, blocks_per_step=4):
    blk = WINDOW
    tq = blocks_per_step * blk
    nb = seq // tq
    aw, kw = q.shape[1], k.shape[1]
    cur = lambda w: pl.BlockSpec((tq, w), lambda b, i: (b * nb + i, 0))
    prev = lambda w: pl.BlockSpec(
        (blk, w), lambda b, i: ((b * nb + i) * blocks_per_step - jnp.minimum(i, 1), 0))
    return pl.pallas_call(
        functools.partial(_attn_prompt_kernel, n_kv=n_kv),
        grid=(batch, nb),
        in_specs=[pl.BlockSpec(memory_space=pltpu.SMEM), cur(aw), prev(kw), cur(kw), prev(kw), cur(kw)],
        out_specs=cur(aw),
        out_shape=jax.ShapeDtypeStruct((batch * seq, aw), BF16),
        compiler_params=_params("parallel", "arbitrary"),
        name="attn_prompt",
    )(sinks, q, k, k, v, v)


def _attn_sample_kernel(sinks_ref, q_ref, kn_ref, vn_ref, ck_ref, cv_ref, o_ref, nk_ref, nv_ref, qf_ref, of_ref, *, n_kv, t_new):
    g_seqs, _, kw = ck_ref.shape
    n_heads = q_ref.shape[1] // HEAD_DIM
    group = n_heads // n_kv
    keep = WINDOW - t_new
    n_rows = n_heads * t_new
    n_keys = 2 * WINDOW
    qf_ref[...] = q_ref[...].astype(F32)
    t = lax.broadcasted_iota(jnp.int32, (n_rows, n_keys), 0) % t_new
    c = lax.broadcasted_iota(jnp.int32, (n_rows, n_keys), 1)
    ok = jnp.logical_and(c > t, c <= t + WINDOW)
    sink = jnp.concatenate([jnp.full((t_new, 1), sinks_ref[h], F32) for h in range(n_heads)], axis=0)
    low = lax.broadcasted_iota(jnp.int32, (t_new, LANES), 1) < HEAD_DIM
    zero_tile = jnp.zeros((t_new, LANES), F32)
    zero_keys = jnp.zeros((n_keys - WINDOW - t_new, kw), F32)

    def one_seq(g, carry):
        r0 = pl.multiple_of(g * t_new, t_new)
        k_new = kn_ref[pl.ds(r0, t_new), :]
        v_new = vn_ref[pl.ds(r0, t_new), :]
        k_old = ck_ref[g]
        v_old = cv_ref[g]
        nk_ref[g, :keep, :] = k_old[t_new:]
        nk_ref[g, keep:, :] = k_new
        nv_ref[g, :keep, :] = v_old[t_new:]
        nv_ref[g, keep:, :] = v_new
        keys = jnp.concatenate([k_old, k_new, zero_keys], axis=0).astype(BF16)
        vals = jnp.concatenate([v_old, v_new, zero_keys], axis=0).astype(BF16)
        vals = jnp.concatenate([vals, jnp.ones((n_keys, LANES), BF16)], axis=1)
        q_rows = []
        for h in range(n_heads):
            j = h // group
            src = qf_ref[pl.ds(r0, t_new), (h // 2) * LANES:(h // 2 + 1) * LANES]
            if h % 2 != j % 2:
                src = pltpu.roll(src, HEAD_DIM, axis=1)
            half = jnp.where(low, src, 0.0) if j % 2 == 0 else jnp.where(low, 0.0, src)
            tiles = [zero_tile] * (kw // LANES)
            tiles[j // 2] = half
            q_rows.append(jnp.concatenate(tiles, axis=1))
        qb = jnp.concatenate(q_rows, axis=0).astype(BF16)
        s = lax.dot_general(qb, keys, (((1,), (1,)), ((), ())), preferred_element_type=F32)
        s = jnp.where(ok, s, NEG)
        m = jnp.maximum(jnp.max(s, axis=-1, keepdims=True), sink)
        r = jnp.dot(jnp.exp(s - m).astype(BF16), vals, preferred_element_type=F32)
        denom = r[:, kw:] + jnp.exp(sink - m)
        o = r[:, :kw] / jnp.concatenate([denom] * (kw // LANES), axis=1)
        for pair in range(n_heads // 2):
            j = (2 * pair) // group
            lanes = slice((j // 2) * LANES, (j // 2 + 1) * LANES)
            a = o[2 * pair * t_new:(2 * pair + 1) * t_new, lanes]
            b = o[(2 * pair + 1) * t_new:(2 * pair + 2) * t_new, lanes]
            if j % 2 == 0:
                b = pltpu.roll(b, HEAD_DIM, axis=1)
            else:
                a = pltpu.roll(a, HEAD_DIM, axis=1)
            of_ref[pl.ds(r0, t_new), pair * LANES:(pair + 1) * LANES] = jnp.where(low, a, b)
        return carry

    lax.fori_loop(0, g_seqs, one_seq, 0, unroll=4)
    o_ref[...] = of_ref[...].astype(o_ref.dtype)


def _attn_sample(sinks, q, k, v, cache_k, cache_v, *, t_new, n_kv, g_seqs=16):
    n_seq = cache_k.shape[0]
    aw, kw = q.shape[1], k.shape[1]
    tok = lambda w: pl.BlockSpec((g_seqs * t_new, w), lambda i: (i, 0))
    cache = pl.BlockSpec((g_seqs, WINDOW, kw), lambda i: (i, 0, 0))
    return pl.pallas_call(
        functools.partial(_attn_sample_kernel, n_kv=n_kv, t_new=t_new),
        grid=(n_seq // g_seqs,),
        in_specs=[pl.BlockSpec(memory_space=pltpu.SMEM), tok(aw), tok(kw), tok(kw), cache, cache],
        out_specs=[tok(aw), cache, cache],
        out_shape=[
            jax.ShapeDtypeStruct((n_seq * t_new, aw), BF16),
            jax.ShapeDtypeStruct(cache_k.shape, F32),
            jax.ShapeDtypeStruct(cache_v.shape, F32),
        ],
        scratch_shapes=[pltpu.VMEM((g_seqs * t_new, aw), F32), pltpu.VMEM((g_seqs * t_new, aw), F32)],
        compiler_params=_params("parallel"),
        name="attn_sample",
    )(sinks, q, k, v, cache_k, cache_v)


def _ln_silu(y, g, b):
    yc = y - jnp.mean(y, axis=-1, keepdims=True)
    var = jnp.mean(yc * yc, axis=-1, keepdims=True)
    return jax.nn.silu(yc * lax.rsqrt(var + LN_EPS) * g + b)


def _conv_prompt_kernel(cur_ref, halo_ref, w_ref, b_ref, g_ref, beta_ref, o_ref, xc_ref, y_ref, *, width, rows_per, lanes_per):
    tc, ch = cur_ref.shape
    halo = halo_ref.shape[0]
    i = pl.program_id(1)
    xc_ref[:halo, :] = jnp.where(i == 0, 0.0, halo_ref[...])
    xc_ref[halo:, :] = cur_ref[...]
    base = halo - (width - 1)

    def col_chunk(cc, carry):
        l0 = pl.multiple_of(cc * lanes_per, lanes_per)
        cols = pl.ds(l0, lanes_per)
        for rr in range(tc // rows_per):
            acc = jnp.broadcast_to(b_ref[:, cols], (rows_per, lanes_per))
            for phase in range(SUBLANES):
                n_win = rows_per + (SUBLANES if phase else 0)
                part = None
                for j in range(width):
                    if (base + j) % SUBLANES != phase:
                        continue
                    start = rr * rows_per + ((base + j) // SUBLANES) * SUBLANES
                    term = w_ref[pl.ds(j, 1), cols] * xc_ref[pl.ds(start, n_win), cols]
                    part = term if part is None else part + term
                if part is None:
                    continue
                if phase:
                    part = pltpu.roll(part, n_win - phase, axis=0)
                acc = acc + part[:rows_per]
            y_ref[pl.ds(rr * rows_per, rows_per), cols] = acc
        return carry

    lax.fori_loop(0, ch // lanes_per, col_chunk, 0)
    o_ref[...] = _ln_silu(y_ref[...], g_ref[...], beta_ref[...]).astype(o_ref.dtype)


def _conv_prompt(glu, w, b, g, beta, *, batch, seq, tc=512, halo=32):
    ch = glu.shape[1]
    width = w.shape[0]
    nt = seq // tc
    per = tc // halo
    cur = pl.BlockSpec((tc, ch), lambda bi, i: (bi * nt + i, 0))
    prev = pl.BlockSpec((halo, ch), lambda bi, i: (jnp.maximum((bi * nt + i) * per - 1, 0), 0))
    kern = functools.partial(_conv_prompt_kernel, width=width, rows_per=64, lanes_per=LANES)
    return pl.pallas_call(
        kern,
        grid=(batch, nt),
        in_specs=[cur, prev, _resident(w.shape), _resident((1, ch)), _resident((1, ch)), _resident((1, ch))],
        out_specs=cur,
        out_shape=jax.ShapeDtypeStruct((batch * seq, ch), BF16),
        scratch_shapes=[pltpu.VMEM((halo + tc, ch), F32), pltpu.VMEM((tc, ch), F32)],
        compiler_params=_params("parallel", "arbitrary"),
        name="conv_prompt",
    )(glu, glu, w, b, g, beta)


def _conv_sample_kernel(new_ref, st_ref, w_ref, b_ref, g_ref, beta_ref, o_ref, ns_ref, y_ref, *, t_new):
    n_past, n_seq, lanes = st_ref.shape
    width = w_ref.shape[0]
    c = pl.program_id(0)
    new = [new_ref[pl.ds(t, n_seq, stride=t_new), :] for t in range(t_new)]
    row = lambda r: st_ref[r] if r < n_past else new[r - n_past]
    for t in range(t_new):
        acc = jnp.broadcast_to(b_ref[...], (n_seq, lanes))
        for j in range(width):
            acc = acc + w_ref[pl.ds(j, 1), :] * row(t + j)
        y_ref[c, pl.ds(t, n_seq, stride=t_new), :] = acc
    for r in range(n_past):
        ns_ref[r] = row(r + t_new)

    @pl.when(c == pl.num_programs(0) - 1)
    def _():
        n_chunks = y_ref.shape[0]
        ch = n_chunks * lanes
        mean = jnp.sum(sum(y_ref[k] for k in range(n_chunks)), axis=-1, keepdims=True) / ch
        var = jnp.sum(sum((y_ref[k] - mean) ** 2 for k in range(n_chunks)), axis=-1, keepdims=True) / ch
        inv = lax.rsqrt(var + LN_EPS)
        for k in range(n_chunks):
            cols = slice(k * lanes, (k + 1) * lanes)
            z = (y_ref[k] - mean) * inv * g_ref[:, cols] + beta_ref[:, cols]
            o_ref[:, cols] = jax.nn.silu(z).astype(o_ref.dtype)


def _conv_sample(glu, state_t, w, b, g, beta, *, t_new):
    n_past, n_seq, ch = state_t.shape
    n_tok = n_seq * t_new
    assert w.shape[0] == n_past + 1
    chunk = lambda rows: pl.BlockSpec((rows, LANES), lambda c: (0, c))
    slabs = pl.BlockSpec((n_past, n_seq, LANES), lambda c: (0, 0, c))
    return pl.pallas_call(
        functools.partial(_conv_sample_kernel, t_new=t_new),
        grid=(ch // LANES,),
        in_specs=[chunk(n_tok), slabs, chunk(w.shape[0]), chunk(1), _resident((1, ch)), _resident((1, ch))],
        out_specs=[pl.BlockSpec((n_tok, ch), lambda c: (0, 0)), slabs],
        out_shape=[jax.ShapeDtypeStruct((n_tok, ch), BF16), jax.ShapeDtypeStruct(state_t.shape, F32)],
        scratch_shapes=[pltpu.VMEM((ch // LANES, n_tok, LANES), F32)],
        compiler_params=_params("arbitrary"),
        name="conv_sample",
    )(glu, state_t, w, b, g, beta)


def _out_proj_kernel(x_ref, a_ref, c_ref, w_ref, o_ref):
    aw = a_ref.shape[1]
    o_ref[...] = (x_ref[...]
                  + jnp.dot(a_ref[...], w_ref[:aw, :].astype(BF16), preferred_element_type=F32)
                  + jnp.dot(c_ref[...], w_ref[aw:, :].astype(BF16), preferred_element_type=F32))


def _out_proj(x, attn, conv, w, *, slab, tm=512):
    n, d = x.shape
    rows = lambda width: pl.BlockSpec((tm, width), lambda i: (i, 0))
    return pl.pallas_call(
        _out_proj_kernel,
        grid=(n // tm,),
        in_specs=[rows(d), rows(attn.shape[1]), rows(conv.shape[1]), _resident_slab(w.shape, slab)],
        out_specs=rows(d),
        out_shape=jax.ShapeDtypeStruct((n, d), F32),
        compiler_params=_params("parallel"),
        name="out_proj",
    )(x, attn, conv, w)


def _ffn_kernel(x_ref, g_ref, wg_ref, wu_ref, wd_ref, o_ref, *rest):
    h_ref = rest[-1]

    @pl.when(pl.program_id(1) == 0)
    def _():
        x = x_ref[...]
        h_ref[...] = _rms(x, g_ref[...]).astype(BF16)
        o_ref[...] = x

    weights = [w_ref[...].astype(BF16) for w_ref in (wg_ref, wu_ref, wd_ref)]
    for w_out, w in zip(rest[:-1], weights):
        w_out[...] = w
    wg, wu, wd = weights
    h = h_ref[...]
    a = jnp.dot(h, wg, preferred_element_type=F32)
    u = jnp.dot(h, wu, preferred_element_type=F32)
    act = (jax.nn.silu(a) * u).astype(BF16)
    o_ref[...] += jnp.dot(act, wd, preferred_element_type=F32)


def _ffn(x, g, wg, wu, wd, *, layer=None, emit_bf16=False, tm=1024, tf=256):
    n, d = x.shape
    dff = wg.shape[-1]
    nt, nf = n // tm, dff // tf
    if layer is None:
        w_in_specs = [pl.BlockSpec((d, tf), lambda i, f: (0, f)), pl.BlockSpec((d, tf), lambda i, f: (0, f)),
                      pl.BlockSpec((tf, d), lambda i, f: (f, 0))]
    else:
        w_in_specs = [pl.BlockSpec((None, d, tf), lambda i, f: (layer, 0, f)),
                      pl.BlockSpec((None, d, tf), lambda i, f: (layer, 0, f)),
                      pl.BlockSpec((None, tf, d), lambda i, f: (layer, f, 0))]
    rows = pl.BlockSpec((tm, d), lambda i, f: (i, 0))
    out_specs, out_shape = [rows], [jax.ShapeDtypeStruct((n, d), F32)]
    if emit_bf16:
        assert nt == 1, "one row tile, so every bf16 weight tile is written exactly once"
        out_specs += [pl.BlockSpec((d, tf), lambda i, f: (0, f)), pl.BlockSpec((d, tf), lambda i, f: (0, f)),
                      pl.BlockSpec((tf, d), lambda i, f: (f, 0))]
        out_shape += [jax.ShapeDtypeStruct((d, dff), BF16), jax.ShapeDtypeStruct((d, dff), BF16),
                      jax.ShapeDtypeStruct((dff, d), BF16)]
    x_spec = _resident((tm, d)) if nt == 1 else rows
    outs = pl.pallas_call(
        _ffn_kernel,
        grid=(nt, nf),
        in_specs=[x_spec, _resident((1, d))] + w_in_specs,
        out_specs=out_specs,
        out_shape=out_shape,
        scratch_shapes=[pltpu.VMEM((tm, d), BF16)],
        compiler_params=_params("parallel", "arbitrary"),
        name="ffn",
    )(x, g, wg, wu, wd)
    return outs if emit_bf16 else outs[0]


def _pool_prompt_kernel(cur_ref, halo_ref, g_ref, pw_ref, ps_ref, o_ref, tail_ref, xp_ref):
    tp, d = cur_ref.shape
    halo = halo_ref.shape[0]
    i = pl.program_id(1)
    gain = g_ref[...]
    xp_ref[:halo, :] = jnp.where(i == 0, 0.0, _rms(halo_ref[...], gain))
    xp_ref[halo:, :] = _rms(cur_ref[...], gain)
    tail_ref[0] = xp_ref[tp:, :]
    pg = d // len(POOL_WINDOWS)
    pos = i * tp + lax.broadcasted_iota(jnp.int32, (tp, 1), 0)
    for gi, w in enumerate(POOL_WINDOWS):
        cols = slice(gi * pg, (gi + 1) * pg)
        ext = xp_ref[:, cols]
        h = ext[halo:]
        span = 1
        while span < w:
            ext = ext + pltpu.roll(ext, span, axis=0)
            span *= 2
        inv_count = 1.0 / jnp.minimum(pos + 1, w).astype(F32)
        dpool = (ext[halo:] * inv_count - h).astype(BF16)
        y = jnp.dot(dpool, pw_ref[gi].astype(BF16), preferred_element_type=F32)
        o_ref[:, cols] = cur_ref[:, cols] + y * ps_ref[:, cols]


def _pool_prompt(x, g, pw, ps, *, slab, batch, seq, tp=512, halo=16):
    assert all(w & (w - 1) == 0 and w <= halo for w in POOL_WINDOWS)
    d = x.shape[1]
    nt = seq // tp
    per = tp // halo
    cur = pl.BlockSpec((tp, d), lambda bi, i: (bi * nt + i, 0))
    prev = pl.BlockSpec((halo, d), lambda bi, i: (jnp.maximum((bi * nt + i) * per - 1, 0), 0))
    return pl.pallas_call(
        _pool_prompt_kernel,
        grid=(batch, nt),
        in_specs=[cur, prev, _resident((1, d)), _resident_slab(pw.shape, slab), _resident((1, d))],
        out_specs=[cur, pl.BlockSpec((1, halo, d), lambda bi, i: (bi, 0, 0))],
        out_shape=[jax.ShapeDtypeStruct((batch * seq, d), F32), jax.ShapeDtypeStruct((batch, halo, d), F32)],
        scratch_shapes=[pltpu.VMEM((halo + tp, d), F32)],
        compiler_params=_params("parallel", "arbitrary"),
        name="pool_prompt",
    )(x, x, g, pw, ps)


def _pool_sample_kernel(xfull_ref, x_ref, st_ref, g_ref, pw_ref, ps_ref, o_ref, ns_ref, inv_ref, y_ref, *, t_new, start_pos):
    n_past, n_seq, lanes = st_ref.shape
    phase, c = pl.program_id(0), pl.program_id(1)
    n_tok, d = xfull_ref.shape
    pg = d // len(POOL_WINDOWS)
    per_group = pg // lanes
    token_rows = lambda t: pl.ds(t, n_seq, stride=t_new)

    @pl.when(jnp.logical_and(phase == 0, c == 0))
    def _():
        x = xfull_ref[...]
        inv = lax.rsqrt(jnp.mean(x * x, axis=-1, keepdims=True) + RMS_EPS)
        inv_ref[...] = jnp.broadcast_to(inv, (n_tok, lanes))
        y_ref[...] = jnp.zeros_like(y_ref)

    @pl.when(phase == 0)
    def _():
        new = [x_ref[token_rows(t), :] * inv_ref[token_rows(t), :] * g_ref[...] for t in range(t_new)]
        row = lambda r: st_ref[n_past + r] if r < 0 else new[r]
        for r in range(n_past):
            ns_ref[r] = row(r + t_new - n_past)
        sums = {1: {r: row(r) for r in range(-n_past, t_new)}}
        span = 1
        while span < max(POOL_WINDOWS):
            prev = sums[span]
            sums[2 * span] = {r: prev[r] + prev[r - span] for r in prev if r - span in prev}
            span *= 2
        gi = c // per_group
        diffs = []
        for t in range(t_new):
            pooled = None
            for k, w in enumerate(POOL_WINDOWS):
                cand = sums[w][t] * (1.0 / min(start_pos + t + 1, w))
                pooled = cand if pooled is None else jnp.where(gi == k, cand, pooled)
            diffs.append(pooled - new[t])
        dp = jnp.concatenate(diffs, axis=0).astype(BF16)
        k0 = pl.multiple_of((c % per_group) * lanes, lanes)
        w_rows = pw_ref[gi, pl.ds(k0, lanes), :].astype(BF16)
        cols = pl.ds(pl.multiple_of(gi * pg, pg), pg)
        y_ref[:, cols] += jnp.dot(dp, w_rows, preferred_element_type=F32)

    @pl.when(phase == 1)
    def _():
        y = y_ref[:, pl.ds(pl.multiple_of(c * lanes, lanes), lanes)] * ps_ref[...]
        for t in range(t_new):
            o_ref[token_rows(t), :] = x_ref[token_rows(t), :] + y[t * n_seq:(t + 1) * n_seq]


def _pool_sample(x, state_t, g, pw, ps, *, slab, t_new):
    n_past, n_seq, d = state_t.shape
    n_tok = n_seq * t_new
    n_chunks = d // LANES
    assert all(w & (w - 1) == 0 and w <= n_past + 1 for w in POOL_WINDOWS)
    kern = functools.partial(_pool_sample_kernel, t_new=t_new, start_pos=PAST_LEN)
    chunk = lambda rows: pl.BlockSpec((rows, LANES), lambda p, c: (0, c))
    slabs = pl.BlockSpec((n_past, n_seq, LANES), lambda p, c: (0, 0, c * (1 - p) + (n_chunks - 1) * p))
    return pl.pallas_call(
        kern,
        grid=(2, n_chunks),
        in_specs=[_resident((n_tok, d)), chunk(n_tok), slabs, chunk(1), _resident_slab(pw.shape, slab), chunk(1)],
        out_specs=[pl.BlockSpec((n_tok, LANES), lambda p, c: (0, c * p)), slabs],
        out_shape=[jax.ShapeDtypeStruct((n_tok, d), F32), jax.ShapeDtypeStruct(state_t.shape, F32)],
        scratch_shapes=[pltpu.VMEM((n_tok, LANES), F32), pltpu.VMEM((n_tok, d), F32)],
        compiler_params=_params("arbitrary", "arbitrary"),
        name="pool_sample",
    )(x, x, state_t, g, pw, ps)


def kernel(x_prompt, x_sample, cache_k, cache_v, state_conv, state_pool, norm_mix, w_in, q_norm, k_norm, sinks,
           conv_w, conv_b, conv_ln_g, conv_ln_b, w_out, pool_w, pool_scale, norm_ffn, w_gate, w_up, w_down):
    batch, seq, d = x_prompt.shape
    n_seq, t_new, _ = x_sample.shape
    n_kv = cache_k.shape[3]
    kv_w = n_kv * HEAD_DIM
    conv_ch = state_conv.shape[-1]
    attn_w = w_in.shape[-1] - 2 * kv_w - 2 * conv_ch
    n_prompt = batch * seq
    row = lambda a: a.reshape(1, -1)

    xp = x_prompt.reshape(n_prompt, d)
    xs = x_sample.reshape(n_seq * t_new, d)
    depth = norm_mix.shape[0]
    outs = {}
    for layer in range(depth):
        i = layer // 2
        if layer % 2 == 0:
            reps = MXU_DIM // HEAD_DIM
            proj = functools.partial(
                _in_proj, g=row(norm_mix[layer]), w=w_in, slab=i, qg=row(jnp.tile(q_norm[i], reps)),
                kg=row(jnp.tile(k_norm[i], reps)), attn_w=attn_w, kv_w=kv_w, conv_ch=conv_ch)
            q_p, k_p, v_p, glu_p = proj(xp)
            q_s, k_s, v_s, glu_s = proj(xs)
            attn_p = _attn_prompt(sinks[i], q_p, k_p, v_p, batch=batch, seq=seq, n_kv=n_kv)
            attn_s, nk_s, nv_s = _attn_sample(
                sinks[i], q_s, k_s, v_s, cache_k[i].reshape(n_seq, WINDOW, kv_w), cache_v[i].reshape(n_seq, WINDOW, kv_w),
                t_new=t_new, n_kv=n_kv)
            ln = (row(conv_b[i]), row(conv_ln_g[i]), row(conv_ln_b[i]))
            conv_p = _conv_prompt(glu_p, conv_w[i], *ln, batch=batch, seq=seq)
            conv_s, nc_t = _conv_sample(glu_s, jnp.transpose(state_conv[i], (1, 0, 2)), conv_w[i], *ln, t_new=t_new)
            nc_s = jnp.transpose(nc_t, (1, 0, 2))
            xp = _out_proj(xp, attn_p, conv_p, w_out, slab=i)
            xs = _out_proj(xs, attn_s, conv_s, w_out, slab=i)
            last = lambda a, n: a.reshape(batch, seq, -1)[:, -n:]
            outs.setdefault("kp", []).append(last(k_p, WINDOW).reshape(batch, WINDOW, n_kv, HEAD_DIM))
            outs.setdefault("vp", []).append(last(v_p, WINDOW).reshape(batch, WINDOW, n_kv, HEAD_DIM))
            outs.setdefault("cp", []).append(last(glu_p, conv_w.shape[1] - 1))
            outs.setdefault("ks", []).append(nk_s.reshape(n_seq, WINDOW, n_kv, HEAD_DIM))
            outs.setdefault("vs", []).append(nv_s.reshape(n_seq, WINDOW, n_kv, HEAD_DIM))
            outs.setdefault("cs", []).append(nc_s)
        else:
            args = (row(norm_mix[layer]), pool_w, row(pool_scale[i]))
            n_past = state_pool.shape[2]
            xp, tail = _pool_prompt(xp, *args, slab=i, batch=batch, seq=seq)
            xs, np_t = _pool_sample(xs, jnp.transpose(state_pool[i], (1, 0, 2)), *args, slab=i, t_new=t_new)
            np_s = jnp.transpose(np_t, (1, 0, 2))
            outs.setdefault("pp", []).append(tail[:, -n_past:])
            outs.setdefault("ps", []).append(np_s)
        g_ffn = row(norm_ffn[layer])
        xs, wg16, wu16, wd16 = _ffn(xs, g_ffn, w_gate, w_up, w_down, layer=layer, emit_bf16=True)
        xp = _ffn(xp, g_ffn, wg16, wu16, wd16, tf=512)
    st = lambda name: jnp.stack(outs[name])
    return (xp.reshape(batch, seq, d), xs.reshape(n_seq, t_new, d),
            st("kp"), st("vp"), st("cp"), st("pp"), st("ks"), st("vs"), st("cs"), st("ps"))
```

```python
import functools

import jax
import jax.numpy as jnp
from jax import lax
from jax.experimental import pallas as pl
from jax.experimental.pallas import tpu as pltpu

F32 = jnp.float32
BF16 = jnp.bfloat16

HEAD_DIM = 64
WINDOW = 128
PAST_LEN = 8192
POOL_WINDOWS = (2, 4, 8, 16)
RMS_EPS = 1e-6
LN_EPS = 1e-5
NEG = -1e30

LANES = 128
SUBLANES = 8
MXU_DIM = 256
VMEM_LIMIT = 56 * 1024 * 1024


def _params(*semantics):
    return pltpu.CompilerParams(dimension_semantics=semantics, vmem_limit_bytes=VMEM_LIMIT)


def _rms(x, g):
    ms = jnp.mean(x * x, axis=-1, keepdims=True)
    return x * lax.rsqrt(ms + RMS_EPS) * g


def _resident(shape):
    zeros = (0,) * len(shape)
    return pl.BlockSpec(shape, lambda *_: zeros, pipeline_mode=pl.Buffered(1))


def _resident_slab(shape, slab):
    index = (slab,) + (0,) * (len(shape) - 1)
    return pl.BlockSpec((None,) + tuple(shape[1:]), lambda *_: index, pipeline_mode=pl.Buffered(1))


def _in_proj_kernel(x_ref, g_ref, w_ref, qg_ref, kg_ref, q_ref, k_ref, v_ref, glu_ref, *, attn_w, kv_w, conv_ch):
    h = _rms(x_ref[...], g_ref[...]).astype(BF16)
    row = lax.broadcasted_iota(jnp.int32, (MXU_DIM, MXU_DIM), 0) // HEAD_DIM
    col = lax.broadcasted_iota(jnp.int32, (MXU_DIM, MXU_DIM), 1) // HEAD_DIM
    pmat = jnp.where(row == col, 1.0 / HEAD_DIM, 0.0).astype(BF16)

    def proj(c0):
        return jnp.dot(h, w_ref[:, c0:c0 + MXU_DIM].astype(BF16), preferred_element_type=F32)

    def head_rms(z):
        sq = z * z
        hi = sq.astype(BF16)
        lo = (sq - hi.astype(F32)).astype(BF16)
        ms = jnp.dot(hi, pmat, preferred_element_type=F32) + jnp.dot(lo, pmat, preferred_element_type=F32)
        return z * lax.rsqrt(ms + RMS_EPS)

    scale = HEAD_DIM ** -0.5

    def put_q(c0, z):
        q_ref[:, c0:c0 + MXU_DIM] = (head_rms(z) * qg_ref[...] * scale).astype(q_ref.dtype)

    def put_k(c0, z):
        k_ref[:, c0:c0 + MXU_DIM] = head_rms(z) * kg_ref[...]

    def put_v(c0, z):
        v_ref[:, c0:c0 + MXU_DIM] = z

    def put_glu(c0, a, gate):
        glu_ref[:, c0:c0 + MXU_DIM] = a * jax.nn.sigmoid(gate)

    units = [((c0,), functools.partial(put_q, c0)) for c0 in range(0, attn_w, MXU_DIM)]
    units += [((attn_w + c0,), functools.partial(put_k, c0)) for c0 in range(0, kv_w, MXU_DIM)]
    units += [((attn_w + kv_w + c0,), functools.partial(put_v, c0)) for c0 in range(0, kv_w, MXU_DIM)]
    units += [((attn_w + 2 * kv_w + c0, attn_w + 2 * kv_w + conv_ch + c0), functools.partial(put_glu, c0))
              for c0 in range(0, conv_ch, MXU_DIM)]
    pending = None
    for cols, epilogue in units:
        zs = [proj(c0) for c0 in cols]
        if pending is not None:
            pending[0](*pending[1])
        pending = (epilogue, zs)
    pending[0](*pending[1])


def _in_proj(x, g, w, qg, kg, *, slab, attn_w, kv_w, conv_ch, tm=512):
    n, d = x.shape
    kern = functools.partial(_in_proj_kernel, attn_w=attn_w, kv_w=kv_w, conv_ch=conv_ch)
    rows = lambda width: pl.BlockSpec((tm, width), lambda i: (i, 0))
    return pl.pallas_call(
        kern,
        grid=(n // tm,),
        in_specs=[rows(d), _resident((1, d)), _resident_slab(w.shape, slab), _resident((1, MXU_DIM)), _resident((1, MXU_DIM))],
        out_specs=[rows(attn_w), rows(kv_w), rows(kv_w), rows(conv_ch)],
        out_shape=[
            jax.ShapeDtypeStruct((n, attn_w), BF16),
            jax.ShapeDtypeStruct((n, kv_w), F32),
            jax.ShapeDtypeStruct((n, kv_w), F32),
            jax.ShapeDtypeStruct((n, conv_ch), F32),
        ],
        compiler_params=_params("parallel"),
        name="in_proj",
    )(x, g, w, qg, kg)


def _pair_blocks(chunk, rolled, odd):
    lane = lax.broadcasted_iota(jnp.int32, chunk.shape, 1)
    low = lane < HEAD_DIM
    if odd:
        top = jnp.where(low, rolled, 0.0)
        bot = jnp.where(low, 0.0, chunk)
    else:
        top = jnp.where(low, chunk, 0.0)
        bot = jnp.where(low, 0.0, rolled)
    return jnp.concatenate([top, bot], axis=0).astype(BF16)


def _softmax_pv(s, ok, sink_a, sink_b, vb_ones):
    half = s.shape[1] // 2
    s = jnp.where(ok, s, NEG)
    sa, sb = s[:, :half], s[:, half:]
    ma = jnp.maximum(jnp.max(sa, axis=-1, keepdims=True), sink_a)
    mb = jnp.maximum(jnp.max(sb, axis=-1, keepdims=True), sink_b)
    p = jnp.concatenate([jnp.exp(sa - ma), jnp.exp(sb - mb)], axis=1).astype(BF16)
    r = jnp.dot(p, vb_ones, preferred_element_type=F32)
    lane = lax.broadcasted_iota(jnp.int32, (s.shape[0], LANES), 1)
    sink_term = jnp.where(lane < HEAD_DIM, jnp.exp(sink_a - ma), jnp.exp(sink_b - mb))
    return r[:, :LANES] / (r[:, LANES:] + sink_term)


def _attend(q2, kb, vb, ok, sinks_ref, h0, rows):
    s = lax.dot_general(q2, kb, (((1,), (1,)), ((), ())), preferred_element_type=F32)
    row = lax.broadcasted_iota(jnp.int32, (2 * rows, 1), 0)
    first = row < rows
    sink_a = jnp.where(first, sinks_ref[h0], sinks_ref[h0 + 2])
    sink_b = jnp.where(first, sinks_ref[h0 + 1], sinks_ref[h0 + 3])
    return _softmax_pv(s, ok, sink_a, sink_b, vb)


def _attn_prompt_kernel(sinks_ref, q_ref, kp_ref, kc_ref, vp_ref, vc_ref, o_ref, *, n_kv):
    blk = kp_ref.shape[0]
    i = pl.program_id(1)
    shape = (2 * blk, 4 * blk)
    r = lax.broadcasted_iota(jnp.int32, shape, 0) % blk
    c = lax.broadcasted_iota(jnp.int32, shape, 1) % (2 * blk)
    band = jnp.logical_and(c > r, c <= r + blk)
    band_first = jnp.logical_and(band, jnp.logical_or(c >= blk, i > 0))
    group = (q_ref.shape[1] // HEAD_DIM) // n_kv
    key_row = lax.broadcasted_iota(jnp.int32, (4 * blk, LANES), 0)
    key_lane = lax.broadcasted_iota(jnp.int32, (4 * blk, LANES), 1)
    ones_blocks = jnp.where((key_row < 2 * blk) == (key_lane < HEAD_DIM), 1.0, 0.0).astype(BF16)
    for sub in range(q_ref.shape[0] // blk):
        rows = slice(sub * blk, (sub + 1) * blk)
        if sub == 0:
            k_all = jnp.concatenate([kp_ref[...], kc_ref[rows, :]], axis=0)
            v_all = jnp.concatenate([vp_ref[...], vc_ref[rows, :]], axis=0)
        else:
            k_all = kc_ref[(sub - 1) * blk:(sub + 1) * blk, :]
            v_all = vc_ref[(sub - 1) * blk:(sub + 1) * blk, :]
        ok = band if sub else band_first
        for cch in range(n_kv // 2):
            kc = k_all[:, cch * LANES:(cch + 1) * LANES]
            vc = v_all[:, cch * LANES:(cch + 1) * LANES]
            kr = pltpu.roll(kc, HEAD_DIM, axis=1)
            vr = pltpu.roll(vc, HEAD_DIM, axis=1)
            for odd in range(2):
                j = 2 * cch + odd
                kb = _pair_blocks(kc, kr, odd)
                vb = jnp.concatenate([_pair_blocks(vc, vr, odd), ones_blocks], axis=1)
                h0 = group * j
                c0 = h0 * HEAD_DIM
                q2 = jnp.concatenate([q_ref[rows, c0:c0 + LANES], q_ref[rows, c0 + LANES:c0 + 2 * LANES]], axis=0)
                o = _attend(q2, kb, vb, ok, sinks_ref, h0, blk).astype(o_ref.dtype)
                o_ref[rows, c0:c0 + LANES] = o[:blk]
                o_ref[rows, c0 + LANES:c0 + 2 * LANES] = o[blk:]


def _attn_prompt(sinks, q, k, v, *, batch, seq, n_kv, blocks_per_step=4):
    blk = WINDOW
    tq = blocks_per_step * blk
    nb = seq // tq
    aw, kw = q.shape[1], k.shape[1]
    cur = lambda w: pl.BlockSpec((tq, w), lambda b, i: (b * nb + i, 0))
    prev = lambda w: pl.BlockSpec(
        (blk, w), lambda b, i: ((b * nb + i) * blocks_per_step - jnp.minimum(i, 1), 0))
    return pl.pallas_call(
        functools.partial(_attn_prompt_kernel, n_kv=n_kv),
        grid=(batch, nb),
        in_specs=[pl.BlockSpec(memory_space=pltpu.SMEM), cur(aw), prev(kw), cur(kw), prev(kw), cur(kw)],
        out_specs=cur(aw),
        out_shape=jax.ShapeDtypeStruct((batch * seq, aw), BF16),
        compiler_params=_params("parallel", "arbitrary"),
        name="attn_prompt",
    )(sinks, q, k, k, v, v)


def _attn_sample_kernel(sinks_ref, q_ref, kn_ref, vn_ref, ck_ref, cv_ref, o_ref, nk_ref, nv_ref, qf_ref, of_ref, *, n_kv, t_new):
    g_seqs, kw, win = ck_ref.shape
    n_heads = q_ref.shape[1] // HEAD_DIM
    group = n_heads // n_kv
    n_rows = n_heads * t_new
    qf_ref[...] = q_ref[...].astype(F32)
    t = lax.broadcasted_iota(jnp.int32, (n_rows, 2 * win), 0) % t_new
    c = lax.broadcasted_iota(jnp.int32, (n_rows, 2 * win), 1)
    ok = jnp.logical_or(jnp.logical_and(c < win, c > t),
                        jnp.logical_and(c >= 2 * win - t_new, c - (2 * win - t_new) <= t))
    sink = jnp.concatenate([jnp.full((t_new, 1), sinks_ref[h], F32) for h in range(n_heads)], axis=0)
    low = lax.broadcasted_iota(jnp.int32, (t_new, LANES), 1) < HEAD_DIM
    zero_tile = jnp.zeros((t_new, LANES), F32)
    keep_old = lax.broadcasted_iota(jnp.int32, (kw, win), 1) < win - t_new
    ones_rows = jnp.ones((LANES, 2 * win), BF16)
    kn_all = kn_ref[0]
    vn_all = vn_ref[0]

    def one_seq(g, carry):
        r0 = pl.multiple_of(g * t_new, t_new)
        k_old = ck_ref[g]
        v_old = cv_ref[g]
        shift = win - t_new * (g + 1)
        k_new = pltpu.roll(kn_all, shift, axis=1)
        v_new = pltpu.roll(vn_all, shift, axis=1)
        nk_ref[g] = jnp.where(keep_old, pltpu.roll(k_old, win - t_new, axis=1), k_new)
        nv_ref[g] = jnp.where(keep_old, pltpu.roll(v_old, win - t_new, axis=1), v_new)
        keys = jnp.concatenate([k_old, k_new], axis=1).astype(BF16)
        vals = jnp.concatenate([jnp.concatenate([v_old, v_new], axis=1).astype(BF16), ones_rows], axis=0)
        q_rows = []
        for h in range(n_heads):
            j = h // group
            src = qf_ref[pl.ds(r0, t_new), (h // 2) * LANES:(h // 2 + 1) * LANES]
            if h % 2 != j % 2:
                src = pltpu.roll(src, HEAD_DIM, axis=1)
            half = jnp.where(low, src, 0.0) if j % 2 == 0 else jnp.where(low, 0.0, src)
            tiles = [zero_tile] * (kw // LANES)
            tiles[j // 2] = half
            q_rows.append(jnp.concatenate(tiles, axis=1))
        qb = jnp.concatenate(q_rows, axis=0).astype(BF16)
        s = jnp.dot(qb, keys, preferred_element_type=F32)
        s = jnp.where(ok, s, NEG)
        m = jnp.maximum(jnp.max(s, axis=-1, keepdims=True), sink)
        p = jnp.exp(s - m).astype(BF16)
        r = lax.dot_general(p, vals, (((1,), (1,)), ((), ())), preferred_element_type=F32)
        denom = r[:, kw:] + jnp.exp(sink - m)
        o = r[:, :kw] / jnp.concatenate([denom] * (kw // LANES), axis=1)
        for pair in range(n_heads // 2):
            j = (2 * pair) // group
            lanes = slice((j // 2) * LANES, (j // 2 + 1) * LANES)
            a = o[2 * pair * t_new:(2 * pair + 1) * t_new, lanes]
            b = o[(2 * pair + 1) * t_new:(2 * pair + 2) * t_new, lanes]
            if j % 2 == 0:
                b = pltpu.roll(b, HEAD_DIM, axis=1)
            else:
                a = pltpu.roll(a, HEAD_DIM, axis=1)
            of_ref[pl.ds(r0, t_new), pair * LANES:(pair + 1) * LANES] = jnp.where(low, a, b)
        return carry

    lax.fori_loop(0, g_seqs, one_seq, 0, unroll=4)
    o_ref[...] = of_ref[...].astype(o_ref.dtype)


def _attn_sample(sinks, q, k_t, v_t, cache_k_t, cache_v_t, *, t_new, n_kv):
    n_seq, kw, win = cache_k_t.shape
    g_seqs = win // t_new
    aw = q.shape[1]
    assert k_t.shape == (n_seq // g_seqs, kw, win)
    tok = pl.BlockSpec((g_seqs * t_new, aw), lambda i: (i, 0))
    new = pl.BlockSpec((1, kw, win), lambda i: (i, 0, 0))
    cache = pl.BlockSpec((g_seqs, kw, win), lambda i: (i, 0, 0))
    return pl.pallas_call(
        functools.partial(_attn_sample_kernel, n_kv=n_kv, t_new=t_new),
        grid=(n_seq // g_seqs,),
        in_specs=[pl.BlockSpec(memory_space=pltpu.SMEM), tok, new, new, cache, cache],
        out_specs=[tok, cache, cache],
        out_shape=[
            jax.ShapeDtypeStruct((n_seq * t_new, aw), BF16),
            jax.ShapeDtypeStruct(cache_k_t.shape, F32),
            jax.ShapeDtypeStruct(cache_v_t.shape, F32),
        ],
        scratch_shapes=[pltpu.VMEM((g_seqs * t_new, aw), F32), pltpu.VMEM((g_seqs * t_new, aw), F32)],
        compiler_params=_params("parallel"),
        name="attn_sample",
    )(sinks, q, k_t, v_t, cache_k_t, cache_v_t)


def _ln_silu(y, g, b):
    yc = y - jnp.mean(y, axis=-1, keepdims=True)
    var = jnp.mean(yc * yc, axis=-1, keepdims=True)
    return jax.nn.silu(yc * lax.rsqrt(var + LN_EPS) * g + b)


def _conv_prompt_kernel(cur_ref, halo_ref, w_ref, b_ref, g_ref, beta_ref, o_ref, xc_ref, y_ref, *, width, rows_per, lanes_per):
    tc, ch = cur_ref.shape
    halo = halo_ref.shape[0]
    i = pl.program_id(1)
    xc_ref[:halo, :] = jnp.where(i == 0, 0.0, halo_ref[...])
    xc_ref[halo:, :] = cur_ref[...]
    base = halo - (width - 1)

    def col_chunk(cc, carry):
        l0 = pl.multiple_of(cc * lanes_per, lanes_per)
        cols = pl.ds(l0, lanes_per)
        for rr in range(tc // rows_per):
            acc = jnp.broadcast_to(b_ref[:, cols], (rows_per, lanes_per))
            for phase in range(SUBLANES):
                n_win = rows_per + (SUBLANES if phase else 0)
                part = None
                for j in range(width):
                    if (base + j) % SUBLANES != phase:
                        continue
                    start = rr * rows_per + ((base + j) // SUBLANES) * SUBLANES
                    term = w_ref[pl.ds(j, 1), cols] * xc_ref[pl.ds(start, n_win), cols]
                    part = term if part is None else part + term
                if part is None:
                    continue
                if phase:
                    part = pltpu.roll(part, n_win - phase, axis=0)
                acc = acc + part[:rows_per]
            y_ref[pl.ds(rr * rows_per, rows_per), cols] = acc
        return carry

    lax.fori_loop(0, ch // lanes_per, col_chunk, 0)
    o_ref[...] = _ln_silu(y_ref[...], g_ref[...], beta_ref[...]).astype(o_ref.dtype)


def _conv_prompt(glu, w, b, g, beta, *, batch, seq, tc=512, halo=32):
    ch = glu.shape[1]
    width = w.shape[0]
    nt = seq // tc
    per = tc // halo
    cur = pl.BlockSpec((tc, ch), lambda bi, i: (bi * nt + i, 0))
    prev = pl.BlockSpec((halo, ch), lambda bi, i: (jnp.maximum((bi * nt + i) * per - 1, 0), 0))
    kern = functools.partial(_conv_prompt_kernel, width=width, rows_per=64, lanes_per=LANES)
    return pl.pallas_call(
        kern,
        grid=(batch, nt),
        in_specs=[cur, prev, _resident(w.shape), _resident((1, ch)), _resident((1, ch)), _resident((1, ch))],
        out_specs=cur,
        out_shape=jax.ShapeDtypeStruct((batch * seq, ch), BF16),
        scratch_shapes=[pltpu.VMEM((halo + tc, ch), F32), pltpu.VMEM((tc, ch), F32)],
        compiler_params=_params("parallel", "arbitrary"),
        name="conv_prompt",
    )(glu, glu, w, b, g, beta)


def _conv_sample_kernel(new_ref, st_ref, w_ref, b_ref, g_ref, beta_ref, o_ref, ns_ref, y_ref, *, t_new):
    n_past, n_seq, lanes = st_ref.shape
    width = w_ref.shape[0]
    c = pl.program_id(0)
    new = [new_ref[pl.ds(t, n_seq, stride=t_new), :] for t in range(t_new)]
    row = lambda r: st_ref[r] if r < n_past else new[r - n_past]
    for t in range(t_new):
        acc = jnp.broadcast_to(b_ref[...], (n_seq, lanes))
        for j in range(width):
            acc = acc + w_ref[pl.ds(j, 1), :] * row(t + j)
        y_ref[c, pl.ds(t, n_seq, stride=t_new), :] = acc
    for r in range(n_past):
        ns_ref[r] = row(r + t_new)

    @pl.when(c == pl.num_programs(0) - 1)
    def _():
        n_chunks = y_ref.shape[0]
        ch = n_chunks * lanes
        mean = jnp.sum(sum(y_ref[k] for k in range(n_chunks)), axis=-1, keepdims=True) / ch
        var = jnp.sum(sum((y_ref[k] - mean) ** 2 for k in range(n_chunks)), axis=-1, keepdims=True) / ch
        inv = lax.rsqrt(var + LN_EPS)
        for k in range(n_chunks):
            cols = slice(k * lanes, (k + 1) * lanes)
            z = (y_ref[k] - mean) * inv * g_ref[:, cols] + beta_ref[:, cols]
            o_ref[:, cols] = jax.nn.silu(z).astype(o_ref.dtype)


def _conv_sample(glu, state_t, w, b, g, beta, *, t_new):
    n_past, n_seq, ch = state_t.shape
    n_tok = n_seq * t_new
    assert w.shape[0] == n_past + 1
    chunk = lambda rows: pl.BlockSpec((rows, LANES), lambda c: (0, c))
    slabs = pl.BlockSpec((n_past, n_seq, LANES), lambda c: (0, 0, c))
    return pl.pallas_call(
        functools.partial(_conv_sample_kernel, t_new=t_new),
        grid=(ch // LANES,),
        in_specs=[chunk(n_tok), slabs, chunk(w.shape[0]), chunk(1), _resident((1, ch)), _resident((1, ch))],
        out_specs=[pl.BlockSpec((n_tok, ch), lambda c: (0, 0)), slabs],
        out_shape=[jax.ShapeDtypeStruct((n_tok, ch), BF16), jax.ShapeDtypeStruct(state_t.shape, F32)],
        scratch_shapes=[pltpu.VMEM((ch // LANES, n_tok, LANES), F32)],
        compiler_params=_params("arbitrary"),
        name="conv_sample",
    )(glu, state_t, w, b, g, beta)


def _out_proj_kernel(x_ref, a_ref, c_ref, w_ref, o_ref):
    aw = a_ref.shape[1]
    o_ref[...] = (x_ref[...]
                  + jnp.dot(a_ref[...], w_ref[:aw, :].astype(BF16), preferred_element_type=F32)
                  + jnp.dot(c_ref[...], w_ref[aw:, :].astype(BF16), preferred_element_type=F32))


def _out_proj(x, attn, conv, w, *, slab, tm=512):
    n, d = x.shape
    rows = lambda width: pl.BlockSpec((tm, width), lambda i: (i, 0))
    return pl.pallas_call(
        _out_proj_kernel,
        grid=(n // tm,),
        in_specs=[rows(d), rows(attn.shape[1]), rows(conv.shape[1]), _resident_slab(w.shape, slab)],
        out_specs=rows(d),
        out_shape=jax.ShapeDtypeStruct((n, d), F32),
        compiler_params=_params("parallel"),
        name="out_proj",
    )(x, attn, conv, w)


def _ffn_kernel(x_ref, g_ref, wg_ref, wu_ref, wd_ref, o_ref, *rest):
    h_ref = rest[-1]

    @pl.when(pl.program_id(1) == 0)
    def _():
        x = x_ref[...]
        h_ref[...] = _rms(x, g_ref[...]).astype(BF16)
        o_ref[...] = x

    weights = [w_ref[...].astype(BF16) for w_ref in (wg_ref, wu_ref, wd_ref)]
    for w_out, w in zip(rest[:-1], weights):
        w_out[...] = w
    wg, wu, wd = weights
    h = h_ref[...]
    a = jnp.dot(h, wg, preferred_element_type=F32)
    u = jnp.dot(h, wu, preferred_element_type=F32)
    act = (jax.nn.silu(a) * u).astype(BF16)
    o_ref[...] += jnp.dot(act, wd, preferred_element_type=F32)


def _ffn(x, g, wg, wu, wd, *, layer=None, emit_bf16=False, tm=1024, tf=256):
    n, d = x.shape
    dff = wg.shape[-1]
    nt, nf = n // tm, dff // tf
    if layer is None:
        w_in_specs = [pl.BlockSpec((d, tf), lambda i, f: (0, f)), pl.BlockSpec((d, tf), lambda i, f: (0, f)),
                      pl.BlockSpec((tf, d), lambda i, f: (f, 0))]
    else:
        w_in_specs = [pl.BlockSpec((None, d, tf), lambda i, f: (layer, 0, f)),
                      pl.BlockSpec((None, d, tf), lambda i, f: (layer, 0, f)),
                      pl.BlockSpec((None, tf, d), lambda i, f: (layer, f, 0))]
    rows = pl.BlockSpec((tm, d), lambda i, f: (i, 0))
    out_specs, out_shape = [rows], [jax.ShapeDtypeStruct((n, d), F32)]
    if emit_bf16:
        assert nt == 1, "one row tile, so every bf16 weight tile is written exactly once"
        out_specs += [pl.BlockSpec((d, tf), lambda i, f: (0, f)), pl.BlockSpec((d, tf), lambda i, f: (0, f)),
                      pl.BlockSpec((tf, d), lambda i, f: (f, 0))]
        out_shape += [jax.ShapeDtypeStruct((d, dff), BF16), jax.ShapeDtypeStruct((d, dff), BF16),
                      jax.ShapeDtypeStruct((dff, d), BF16)]
    x_spec = _resident((tm, d)) if nt == 1 else rows
    outs = pl.pallas_call(
        _ffn_kernel,
        grid=(nt, nf),
        in_specs=[x_spec, _resident((1, d))] + w_in_specs,
        out_specs=out_specs,
        out_shape=out_shape,
        scratch_shapes=[pltpu.VMEM((tm, d), BF16)],
        compiler_params=_params("parallel", "arbitrary"),
        name="ffn",
    )(x, g, wg, wu, wd)
    return outs if emit_bf16 else outs[0]


def _pool_prompt_kernel(cur_ref, halo_ref, g_ref, pw_ref, ps_ref, o_ref, tail_ref, xp_ref):
    tp, d = cur_ref.shape
    halo = halo_ref.shape[0]
    i = pl.program_id(1)
    gain = g_ref[...]
    xp_ref[:halo, :] = jnp.where(i == 0, 0.0, _rms(halo_ref[...], gain))
    xp_ref[halo:, :] = _rms(cur_ref[...], gain)
    tail_ref[0] = xp_ref[tp:, :]
    pg = d // len(POOL_WINDOWS)
    pos = i * tp + lax.broadcasted_iota(jnp.int32, (tp, 1), 0)
    for gi, w in enumerate(POOL_WINDOWS):
        cols = slice(gi * pg, (gi + 1) * pg)
        ext = xp_ref[:, cols]
        h = ext[halo:]
        span = 1
        while span < w:
            ext = ext + pltpu.roll(ext, span, axis=0)
            span *= 2
        inv_count = 1.0 / jnp.minimum(pos + 1, w).astype(F32)
        dpool = (ext[halo:] * inv_count - h).astype(BF16)
        y = jnp.dot(dpool, pw_ref[gi].astype(BF16), preferred_element_type=F32)
        o_ref[:, cols] = cur_ref[:, cols] + y * ps_ref[:, cols]


def _pool_prompt(x, g, pw, ps, *, slab, batch, seq, tp=512, halo=16):
    assert all(w & (w - 1) == 0 and w <= halo for w in POOL_WINDOWS)
    d = x.shape[1]
    nt = seq // tp
    per = tp // halo
    cur = pl.BlockSpec((tp, d), lambda bi, i: (bi * nt + i, 0))
    prev = pl.BlockSpec((halo, d), lambda bi, i: (jnp.maximum((bi * nt + i) * per - 1, 0), 0))
    return pl.pallas_call(
        _pool_prompt_kernel,
        grid=(batch, nt),
        in_specs=[cur, prev, _resident((1, d)), _resident_slab(pw.shape, slab), _resident((1, d))],
        out_specs=[cur, pl.BlockSpec((1, halo, d), lambda bi, i: (bi, 0, 0))],
        out_shape=[jax.ShapeDtypeStruct((batch * seq, d), F32), jax.ShapeDtypeStruct((batch, halo, d), F32)],
        scratch_shapes=[pltpu.VMEM((halo + tp, d), F32)],
        compiler_params=_params("parallel", "arbitrary"),
        name="pool_prompt",
    )(x, x, g, pw, ps)


def _pool_sample_kernel(xfull_ref, x_ref, st_ref, g_ref, pw_ref, ps_ref, o_ref, ns_ref, inv_ref, y_ref, *, t_new, start_pos):
    n_past, n_seq, lanes = st_ref.shape
    phase, c = pl.program_id(0), pl.program_id(1)
    n_tok, d = xfull_ref.shape
    pg = d // len(POOL_WINDOWS)
    per_group = pg // lanes
    token_rows = lambda t: pl.ds(t, n_seq, stride=t_new)

    @pl.when(jnp.logical_and(phase == 0, c == 0))
    def _():
        x = xfull_ref[...]
        inv = lax.rsqrt(jnp.mean(x * x, axis=-1, keepdims=True) + RMS_EPS)
        inv_ref[...] = jnp.broadcast_to(inv, (n_tok, lanes))
        y_ref[...] = jnp.zeros_like(y_ref)

    @pl.when(phase == 0)
    def _():
        new = [x_ref[token_rows(t), :] * inv_ref[token_rows(t), :] * g_ref[...] for t in range(t_new)]
        row = lambda r: st_ref[n_past + r] if r < 0 else new[r]
        for r in range(n_past):
            ns_ref[r] = row(r + t_new - n_past)
        sums = {1: {r: row(r) for r in range(-n_past, t_new)}}
        span = 1
        while span < max(POOL_WINDOWS):
            prev = sums[span]
            sums[2 * span] = {r: prev[r] + prev[r - span] for r in prev if r - span in prev}
            span *= 2
        gi = c // per_group
        diffs = []
        for t in range(t_new):
            pooled = None
            for k, w in enumerate(POOL_WINDOWS):
                cand = sums[w][t] * (1.0 / min(start_pos + t + 1, w))
                pooled = cand if pooled is None else jnp.where(gi == k, cand, pooled)
            diffs.append(pooled - new[t])
        dp = jnp.concatenate(diffs, axis=0).astype(BF16)
        k0 = pl.multiple_of((c % per_group) * lanes, lanes)
        w_rows = pw_ref[gi, pl.ds(k0, lanes), :].astype(BF16)
        cols = pl.ds(pl.multiple_of(gi * pg, pg), pg)
        y_ref[:, cols] += jnp.dot(dp, w_rows, preferred_element_type=F32)

    @pl.when(phase == 1)
    def _():
        y = y_ref[:, pl.ds(pl.multiple_of(c * lanes, lanes), lanes)] * ps_ref[...]
        for t in range(t_new):
            o_ref[token_rows(t), :] = x_ref[token_rows(t), :] + y[t * n_seq:(t + 1) * n_seq]


def _pool_sample(x, state_t, g, pw, ps, *, slab, t_new):
    n_past, n_seq, d = state_t.shape
    n_tok = n_seq * t_new
    n_chunks = d // LANES
    assert all(w & (w - 1) == 0 and w <= n_past + 1 for w in POOL_WINDOWS)
    kern = functools.partial(_pool_sample_kernel, t_new=t_new, start_pos=PAST_LEN)
    chunk = lambda rows: pl.BlockSpec((rows, LANES), lambda p, c: (0, c))
    slabs = pl.BlockSpec((n_past, n_seq, LANES), lambda p, c: (0, 0, c * (1 - p) + (n_chunks - 1) * p))
    return pl.pallas_call(
        kern,
        grid=(2, n_chunks),
        in_specs=[_resident((n_tok, d)), chunk(n_tok), slabs, chunk(1), _resident_slab(pw.shape, slab), chunk(1)],
        out_specs=[pl.BlockSpec((n_tok, LANES), lambda p, c: (0, c * p)), slabs],
        out_shape=[jax.ShapeDtypeStruct((n_tok, d), F32), jax.ShapeDtypeStruct(state_t.shape, F32)],
        scratch_shapes=[pltpu.VMEM((n_tok, LANES), F32), pltpu.VMEM((n_tok, d), F32)],
        compiler_params=_params("arbitrary", "arbitrary"),
        name="pool_sample",
    )(x, x, state_t, g, pw, ps)


def kernel(x_prompt, x_sample, cache_k, cache_v, state_conv, state_pool, norm_mix, w_in, q_norm, k_norm, sinks,
           conv_w, conv_b, conv_ln_g, conv_ln_b, w_out, pool_w, pool_scale, norm_ffn, w_gate, w_up, w_down):
    batch, seq, d = x_prompt.shape
    n_seq, t_new, _ = x_sample.shape
    n_kv = cache_k.shape[3]
    kv_w = n_kv * HEAD_DIM
    conv_ch = state_conv.shape[-1]
    attn_w = w_in.shape[-1] - 2 * kv_w - 2 * conv_ch
    n_prompt = batch * seq
    row = lambda a: a.reshape(1, -1)

    xp = x_prompt.reshape(n_prompt, d)
    xs = x_sample.reshape(n_seq * t_new, d)
    depth = norm_mix.shape[0]
    outs = {}
    for layer in range(depth):
        i = layer // 2
        if layer % 2 == 0:
            reps = MXU_DIM // HEAD_DIM
            proj = functools.partial(
                _in_proj, g=row(norm_mix[layer]), w=w_in, slab=i, qg=row(jnp.tile(q_norm[i], reps)),
                kg=row(jnp.tile(k_norm[i], reps)), attn_w=attn_w, kv_w=kv_w, conv_ch=conv_ch)
            q_p, k_p, v_p, glu_p = proj(xp)
            q_s, k_s, v_s, glu_s = proj(xs)
            attn_p = _attn_prompt(sinks[i], q_p, k_p, v_p, batch=batch, seq=seq, n_kv=n_kv)
            cache_t = lambda a: jnp.transpose(a, (0, 2, 3, 1)).reshape(n_seq, kv_w, WINDOW)
            new_t = lambda a: jnp.transpose(a.reshape(-1, WINDOW, kv_w), (0, 2, 1))
            attn_s, nk_t, nv_t = _attn_sample(sinks[i], q_s, new_t(k_s), new_t(v_s), cache_t(cache_k[i]),
                                              cache_t(cache_v[i]), t_new=t_new, n_kv=n_kv)
            uncache_t = lambda a: jnp.transpose(a.reshape(n_seq, n_kv, HEAD_DIM, WINDOW), (0, 3, 1, 2))
            ln = (row(conv_b[i]), row(conv_ln_g[i]), row(conv_ln_b[i]))
            conv_p = _conv_prompt(glu_p, conv_w[i], *ln, batch=batch, seq=seq)
            conv_s, nc_t = _conv_sample(glu_s, jnp.transpose(state_conv[i], (1, 0, 2)), conv_w[i], *ln, t_new=t_new)
            nc_s = jnp.transpose(nc_t, (1, 0, 2))
            xp = _out_proj(xp, attn_p, conv_p, w_out, slab=i)
            xs = _out_proj(xs, attn_s, conv_s, w_out, slab=i)
            last = lambda a, n: a.reshape(batch, seq, -1)[:, -n:]
            outs.setdefault("kp", []).append(last(k_p, WINDOW).reshape(batch, WINDOW, n_kv, HEAD_DIM))
            outs.setdefault("vp", []).append(last(v_p, WINDOW).reshape(batch, WINDOW, n_kv, HEAD_DIM))
            outs.setdefault("cp", []).append(last(glu_p, conv_w.shape[1] - 1))
            outs.setdefault("ks", []).append(uncache_t(nk_t))
            outs.setdefault("vs", []).append(uncache_t(nv_t))
            outs.setdefault("cs", []).append(nc_s)
        else:
            args = (row(norm_mix[layer]), pool_w, row(pool_scale[i]))
            n_past = state_pool.shape[2]
            xp, tail = _pool_prompt(xp, *args, slab=i, batch=batch, seq=seq)
            xs, np_t = _pool_sample(xs, jnp.transpose(state_pool[i], (1, 0, 2)), *args, slab=i, t_new=t_new)
            np_s = jnp.transpose(np_t, (1, 0, 2))
            outs.setdefault("pp", []).append(tail[:, -n_past:])
            outs.setdefault("ps", []).append(np_s)
        g_ffn = row(norm_ffn[layer])
        xs, wg16, wu16, wd16 = _ffn(xs, g_ffn, w_gate, w_up, w_down, layer=layer, emit_bf16=True)
        xp = _ffn(xp, g_ffn, wg16, wu16, wd16, tf=512)
    st = lambda name: jnp.stack(outs[name])
    return (xp.reshape(batch, seq, d), xs.reshape(n_seq, t_new, d),
            st("kp"), st("vp"), st("cp"), st("pp"), st("ks"), st("vs"), st("cs"), st("ps"))
```

```python
import functools

import jax
import jax.numpy as jnp
from jax import lax
from jax.experimental import pallas as pl
from jax.experimental.pallas import tpu as pltpu

F32 = jnp.float32
BF16 = jnp.bfloat16

HEAD_DIM = 64
WINDOW = 128
PAST_LEN = 8192
POOL_WINDOWS = (2, 4, 8, 16)
RMS_EPS = 1e-6
LN_EPS = 1e-5
NEG = -1e30

LANES = 128
SUBLANES = 8
MXU_DIM = 256
VMEM_LIMIT = 56 * 1024 * 1024


def _params(*semantics):
    return pltpu.CompilerParams(dimension_semantics=semantics, vmem_limit_bytes=VMEM_LIMIT)


def _rms(x, g):
    ms = jnp.mean(x * x, axis=-1, keepdims=True)
    return x * lax.rsqrt(ms + RMS_EPS) * g


def _resident(shape):
    zeros = (0,) * len(shape)
    return pl.BlockSpec(shape, lambda *_: zeros, pipeline_mode=pl.Buffered(1))


def _resident_slab(shape, slab):
    index = (slab,) + (0,) * (len(shape) - 1)
    return pl.BlockSpec((None,) + tuple(shape[1:]), lambda *_: index, pipeline_mode=pl.Buffered(1))


def _in_proj_kernel(x_ref, g_ref, w_ref, qg_ref, kg_ref, q_ref, k_ref, v_ref, glu_ref, *, attn_w, kv_w, conv_ch):
    h = _rms(x_ref[...], g_ref[...]).astype(BF16)
    row = lax.broadcasted_iota(jnp.int32, (MXU_DIM, MXU_DIM), 0) // HEAD_DIM
    col = lax.broadcasted_iota(jnp.int32, (MXU_DIM, MXU_DIM), 1) // HEAD_DIM
    pmat = jnp.where(row == col, 1.0 / HEAD_DIM, 0.0).astype(BF16)

    def proj(c0):
        return jnp.dot(h, w_ref[:, c0:c0 + MXU_DIM].astype(BF16), preferred_element_type=F32)

    def head_rms(z):
        sq = z * z
        hi = sq.astype(BF16)
        lo = (sq - hi.astype(F32)).astype(BF16)
        ms = jnp.dot(hi, pmat, preferred_element_type=F32) + jnp.dot(lo, pmat, preferred_element_type=F32)
        return z * lax.rsqrt(ms + RMS_EPS)

    scale = HEAD_DIM ** -0.5

    def put_q(c0, z):
        q_ref[:, c0:c0 + MXU_DIM] = (head_rms(z) * qg_ref[...] * scale).astype(q_ref.dtype)

    def put_k(c0, z):
        k_ref[:, c0:c0 + MXU_DIM] = head_rms(z) * kg_ref[...]

    def put_v(c0, z):
        v_ref[:, c0:c0 + MXU_DIM] = z

    def put_glu(c0, a, gate):
        glu_ref[:, c0:c0 + MXU_DIM] = a * jax.nn.sigmoid(gate)

    units = [((c0,), functools.partial(put_q, c0)) for c0 in range(0, attn_w, MXU_DIM)]
    units += [((attn_w + c0,), functools.partial(put_k, c0)) for c0 in range(0, kv_w, MXU_DIM)]
    units += [((attn_w + kv_w + c0,), functools.partial(put_v, c0)) for c0 in range(0, kv_w, MXU_DIM)]
    units += [((attn_w + 2 * kv_w + c0, attn_w + 2 * kv_w + conv_ch + c0), functools.partial(put_glu, c0))
              for c0 in range(0, conv_ch, MXU_DIM)]
    pending = None
    for cols, epilogue in units:
        zs = [proj(c0) for c0 in cols]
        if pending is not None:
            pending[0](*pending[1])
        pending = (epilogue, zs)
    pending[0](*pending[1])


def _in_proj(x, g, w, qg, kg, *, slab, attn_w, kv_w, conv_ch, tm=512):
    n, d = x.shape
    kern = functools.partial(_in_proj_kernel, attn_w=attn_w, kv_w=kv_w, conv_ch=conv_ch)
    rows = lambda width: pl.BlockSpec((tm, width), lambda i: (i, 0))
    return pl.pallas_call(
        kern,
        grid=(n // tm,),
        in_specs=[rows(d), _resident((1, d)), _resident_slab(w.shape, slab), _resident((1, MXU_DIM)), _resident((1, MXU_DIM))],
        out_specs=[rows(attn_w), rows(kv_w), rows(kv_w), rows(conv_ch)],
        out_shape=[
            jax.ShapeDtypeStruct((n, attn_w), BF16),
            jax.ShapeDtypeStruct((n, kv_w), F32),
            jax.ShapeDtypeStruct((n, kv_w), F32),
            jax.ShapeDtypeStruct((n, conv_ch), F32),
        ],
        compiler_params=_params("parallel"),
        name="in_proj",
    )(x, g, w, qg, kg)


def _pair_blocks(chunk, rolled, odd):
    lane = lax.broadcasted_iota(jnp.int32, chunk.shape, 1)
    low = lane < HEAD_DIM
    if odd:
        top = jnp.where(low, rolled, 0.0)
        bot = jnp.where(low, 0.0, chunk)
    else:
        top = jnp.where(low, chunk, 0.0)
        bot = jnp.where(low, 0.0, rolled)
    return jnp.concatenate([top, bot], axis=0).astype(BF16)


def _softmax_pv(s, ok, sink_a, sink_b, vb_ones):
    half = s.shape[1] // 2
    s = jnp.where(ok, s, NEG)
    sa, sb = s[:, :half], s[:, half:]
    ma = jnp.maximum(jnp.max(sa, axis=-1, keepdims=True), sink_a)
    mb = jnp.maximum(jnp.max(sb, axis=-1, keepdims=True), sink_b)
    p = jnp.concatenate([jnp.exp(sa - ma), jnp.exp(sb - mb)], axis=1).astype(BF16)
    r = jnp.dot(p, vb_ones, preferred_element_type=F32)
    lane = lax.broadcasted_iota(jnp.int32, (s.shape[0], LANES), 1)
    sink_term = jnp.where(lane < HEAD_DIM, jnp.exp(sink_a - ma), jnp.exp(sink_b - mb))
    return r[:, :LANES] / (r[:, LANES:] + sink_term)


def _attend(q2, kb, vb, ok, sinks_ref, h0, rows):
    s = lax.dot_general(q2, kb, (((1,), (1,)), ((), ())), preferred_element_type=F32)
    row = lax.broadcasted_iota(jnp.int32, (2 * rows, 1), 0)
    first = row < rows
    sink_a = jnp.where(first, sinks_ref[h0], sinks_ref[h0 + 2])
    sink_b = jnp.where(first, sinks_ref[h0 + 1], sinks_ref[h0 + 3])
    return _softmax_pv(s, ok, sink_a, sink_b, vb)


def _attn_prompt_kernel(sinks_ref, q_ref, kp_ref, kc_ref, vp_ref, vc_ref, o_ref, *, n_kv):
    blk = kp_ref.shape[0]
    i = pl.program_id(1)
    shape = (2 * blk, 4 * blk)
    r = lax.broadcasted_iota(jnp.int32, shape, 0) % blk
    c = lax.broadcasted_iota(jnp.int32, shape, 1) % (2 * blk)
    band = jnp.logical_and(c > r, c <= r + blk)
    band_first = jnp.logical_and(band, jnp.logical_or(c >= blk, i > 0))
    group = (q_ref.shape[1] // HEAD_DIM) // n_kv
    key_row = lax.broadcasted_iota(jnp.int32, (4 * blk, LANES), 0)
    key_lane = lax.broadcasted_iota(jnp.int32, (4 * blk, LANES), 1)
    ones_blocks = jnp.where((key_row < 2 * blk) == (key_lane < HEAD_DIM), 1.0, 0.0).astype(BF16)
    for sub in range(q_ref.shape[0] // blk):
        rows = slice(sub * blk, (sub + 1) * blk)
        if sub == 0:
            k_all = jnp.concatenate([kp_ref[...], kc_ref[rows, :]], axis=0)
            v_all = jnp.concatenate([vp_ref[...], vc_ref[rows, :]], axis=0)
        else:
            k_all = kc_ref[(sub - 1) * blk:(sub + 1) * blk, :]
            v_all = vc_ref[(sub - 1) * blk:(sub + 1) * blk, :]
        ok = band if sub else band_first
        for cch in range(n_kv // 2):
            kc = k_all[:, cch * LANES:(cch + 1) * LANES]
            vc = v_all[:, cch * LANES:(cch + 1) * LANES]
            kr = pltpu.roll(kc, HEAD_DIM, axis=1)
            vr = pltpu.roll(vc, HEAD_DIM, axis=1)
            for odd in range(2):
                j = 2 * cch + odd
                kb = _pair_blocks(kc, kr, odd)
                vb = jnp.concatenate([_pair_blocks(vc, vr, odd), ones_blocks], axis=1)
                h0 = group * j
                c0 = h0 * HEAD_DIM
                q2 = jnp.concatenate([q_ref[rows, c0:c0 + LANES], q_ref[rows, c0 + LANES:c0 + 2 * LANES]], axis=0)
                o = _attend(q2, kb, vb, ok, sinks_ref, h0, blk).astype(o_ref.dtype)
                o_ref[rows, c0:c0 + LANES] = o[:blk]
                o_ref[rows, c0 + LANES:c0 + 2 * LANES] = o[blk:]


def _attn_prompt(sinks, q, k, v, *, batch, seq, n_kv, blocks_per_step=4):
    blk = WINDOW
    tq = blocks_per_step * blk
    nb = seq // tq
    aw, kw = q.shape[1], k.shape[1]
    cur = lambda w: pl.BlockSpec((tq, w), lambda b, i: (b * nb + i, 0))
    prev = lambda w: pl.BlockSpec(
        (blk, w), lambda b, i: ((b * nb + i) * blocks_per_step - jnp.minimum(i, 1), 0))
    return pl.pallas_call(
        functools.partial(_attn_prompt_kernel, n_kv=n_kv),
        grid=(batch, nb),
        in_specs=[pl.BlockSpec(memory_space=pltpu.SMEM), cur(aw), prev(kw), cur(kw), prev(kw), cur(kw)],
        out_specs=cur(aw),
        out_shape=jax.ShapeDtypeStruct((batch * seq, aw), BF16),
        compiler_params=_params("parallel", "arbitrary"),
        name="attn_prompt",
    )(sinks, q, k, k, v, v)


def _attn_sample_kernel(sinks_ref, q_ref, kn_ref, vn_ref, ck_ref, cv_ref, o_ref, nk_ref, nv_ref, qf_ref, of_ref, *, n_kv, t_new):
    g_seqs, kw, win = ck_ref.shape
    n_heads = q_ref.shape[1] // HEAD_DIM
    group = n_heads // n_kv
    n_rows = n_heads * t_new
    qf_ref[...] = q_ref[...].astype(F32)
    t = lax.broadcasted_iota(jnp.int32, (n_rows, 2 * win), 0) % t_new
    c = lax.broadcasted_iota(jnp.int32, (n_rows, 2 * win), 1)
    ok = jnp.logical_or(jnp.logical_and(c < win, c > t),
                        jnp.logical_and(c >= 2 * win - t_new, c - (2 * win - t_new) <= t))
    sink = jnp.concatenate([jnp.full((t_new, 1), sinks_ref[h], F32) for h in range(n_heads)], axis=0)
    low = lax.broadcasted_iota(jnp.int32, (t_new, LANES), 1) < HEAD_DIM
    zero_tile = jnp.zeros((t_new, LANES), F32)
    keep_old = lax.broadcasted_iota(jnp.int32, (kw, win), 1) < win - t_new
    ones_rows = jnp.ones((LANES, 2 * win), BF16)
    kn_all = kn_ref[0]
    vn_all = vn_ref[0]

    def one_seq(g, carry):
        r0 = pl.multiple_of(g * t_new, t_new)
        k_old = ck_ref[g]
        v_old = cv_ref[g]
        shift = win - t_new * (g + 1)
        k_new = pltpu.roll(kn_all, shift, axis=1)
        v_new = pltpu.roll(vn_all, shift, axis=1)
        nk_ref[g] = jnp.where(keep_old, pltpu.roll(k_old, win - t_new, axis=1), k_new)
        nv_ref[g] = jnp.where(keep_old, pltpu.roll(v_old, win - t_new, axis=1), v_new)
        keys = jnp.concatenate([k_old, k_new], axis=1).astype(BF16)
        vals = jnp.concatenate([jnp.concatenate([v_old, v_new], axis=1).astype(BF16), ones_rows], axis=0)
        q_rows = []
        for h in range(n_heads):
            j = h // group
            src = qf_ref[pl.ds(r0, t_new), (h // 2) * LANES:(h // 2 + 1) * LANES]
            if h % 2 != j % 2:
                src = pltpu.roll(src, HEAD_DIM, axis=1)
            half = jnp.where(low, src, 0.0) if j % 2 == 0 else jnp.where(low, 0.0, src)
            tiles = [zero_tile] * (kw // LANES)
            tiles[j // 2] = half
            q_rows.append(jnp.concatenate(tiles, axis=1))
        qb = jnp.concatenate(q_rows, axis=0).astype(BF16)
        s = jnp.dot(qb, keys, preferred_element_type=F32)
        s = jnp.where(ok, s, NEG)
        m = jnp.maximum(jnp.max(s, axis=-1, keepdims=True), sink)
        p = jnp.exp(s - m).astype(BF16)
        r = lax.dot_general(p, vals, (((1,), (1,)), ((), ())), preferred_element_type=F32)
        denom = r[:, kw:] + jnp.exp(sink - m)
        o = r[:, :kw] / jnp.concatenate([denom] * (kw // LANES), axis=1)
        for pair in range(n_heads // 2):
            j = (2 * pair) // group
            lanes = slice((j // 2) * LANES, (j // 2 + 1) * LANES)
            a = o[2 * pair * t_new:(2 * pair + 1) * t_new, lanes]
            b = o[(2 * pair + 1) * t_new:(2 * pair + 2) * t_new, lanes]
            if j % 2 == 0:
                b = pltpu.roll(b, HEAD_DIM, axis=1)
            else:
                a = pltpu.roll(a, HEAD_DIM, axis=1)
            of_ref[pl.ds(r0, t_new), pair * LANES:(pair + 1) * LANES] = jnp.where(low, a, b)
        return carry

    lax.fori_loop(0, g_seqs, one_seq, 0, unroll=4)
    o_ref[...] = of_ref[...].astype(o_ref.dtype)


def _attn_sample(sinks, q, k_t, v_t, cache_k_t, cache_v_t, *, t_new, n_kv):
    n_seq, kw, win = cache_k_t.shape
    g_seqs = win // t_new
    aw = q.shape[1]
    assert k_t.shape == (n_seq // g_seqs, kw, win)
    tok = pl.BlockSpec((g_seqs * t_new, aw), lambda i: (i, 0))
    new = pl.BlockSpec((1, kw, win), lambda i: (i, 0, 0))
    cache = pl.BlockSpec((g_seqs, kw, win), lambda i: (i, 0, 0))
    return pl.pallas_call(
        functools.partial(_attn_sample_kernel, n_kv=n_kv, t_new=t_new),
        grid=(n_seq // g_seqs,),
        in_specs=[pl.BlockSpec(memory_space=pltpu.SMEM), tok, new, new, cache, cache],
        out_specs=[tok, cache, cache],
        out_shape=[
            jax.ShapeDtypeStruct((n_seq * t_new, aw), BF16),
            jax.ShapeDtypeStruct(cache_k_t.shape, F32),
            jax.ShapeDtypeStruct(cache_v_t.shape, F32),
        ],
        scratch_shapes=[pltpu.VMEM((g_seqs * t_new, aw), F32), pltpu.VMEM((g_seqs * t_new, aw), F32)],
        compiler_params=_params("parallel"),
        name="attn_sample",
    )(sinks, q, k_t, v_t, cache_k_t, cache_v_t)


def _ln_silu(y, g, b):
    yc = y - jnp.mean(y, axis=-1, keepdims=True)
    var = jnp.mean(yc * yc, axis=-1, keepdims=True)
    return jax.nn.silu(yc * lax.rsqrt(var + LN_EPS) * g + b)


def _conv_prompt_kernel(cur_ref, halo_ref, w_ref, b_ref, g_ref, beta_ref, o_ref, xc_ref, y_ref, *, width, rows_per, lanes_per):
    tc, ch = cur_ref.shape
    halo = halo_ref.shape[0]
    i = pl.program_id(1)
    xc_ref[:halo, :] = jnp.where(i == 0, 0.0, halo_ref[...])
    xc_ref[halo:, :] = cur_ref[...]
    base = halo - (width - 1)

    def col_chunk(cc, carry):
        l0 = pl.multiple_of(cc * lanes_per, lanes_per)
        cols = pl.ds(l0, lanes_per)
        for rr in range(tc // rows_per):
            acc = jnp.broadcast_to(b_ref[:, cols], (rows_per, lanes_per))
            for phase in range(SUBLANES):
                n_win = rows_per + (SUBLANES if phase else 0)
                part = None
                for j in range(width):
                    if (base + j) % SUBLANES != phase:
                        continue
                    start = rr * rows_per + ((base + j) // SUBLANES) * SUBLANES
                    term = w_ref[pl.ds(j, 1), cols] * xc_ref[pl.ds(start, n_win), cols]
                    part = term if part is None else part + term
                if part is None:
                    continue
                if phase:
                    part = pltpu.roll(part, n_win - phase, axis=0)
                acc = acc + part[:rows_per]
            y_ref[pl.ds(rr * rows_per, rows_per), cols] = acc
        return carry

    lax.fori_loop(0, ch // lanes_per, col_chunk, 0)
    o_ref[...] = _ln_silu(y_ref[...], g_ref[...], beta_ref[...]).astype(o_ref.dtype)


def _conv_prompt(glu, w, b, g, beta, *, batch, seq, tc=512, halo=32):
    ch = glu.shape[1]
    width = w.shape[0]
    nt = seq // tc
    per = tc // halo
    cur = pl.BlockSpec((tc, ch), lambda bi, i: (bi * nt + i, 0))
    prev = pl.BlockSpec((halo, ch), lambda bi, i: (jnp.maximum((bi * nt + i) * per - 1, 0), 0))
    kern = functools.partial(_conv_prompt_kernel, width=width, rows_per=128, lanes_per=LANES)
    return pl.pallas_call(
        kern,
        grid=(batch, nt),
        in_specs=[cur, prev, _resident(w.shape), _resident((1, ch)), _resident((1, ch)), _resident((1, ch))],
        out_specs=cur,
        out_shape=jax.ShapeDtypeStruct((batch * seq, ch), BF16),
        scratch_shapes=[pltpu.VMEM((halo + tc, ch), F32), pltpu.VMEM((tc, ch), F32)],
        compiler_params=_params("parallel", "arbitrary"),
        name="conv_prompt",
    )(glu, glu, w, b, g, beta)


def _conv_sample_kernel(new_ref, st_ref, w_ref, b_ref, g_ref, beta_ref, o_ref, ns_ref, y_ref, tok_ref, *, t_new):
    n_past, n_seq, wide = st_ref.shape
    lanes = tok_ref.shape[1]
    n_sub = wide // lanes
    width = w_ref.shape[0]
    c = pl.program_id(0)
    for k in range(n_sub):
        cols = slice(k * lanes, (k + 1) * lanes)
        tok_ref[...] = new_ref[:, cols]
        new = [tok_ref[pl.ds(t, n_seq, stride=t_new), :] for t in range(t_new)]
        row = lambda r: st_ref[r, :, cols] if r < n_past else new[r - n_past]
        for t in range(t_new):
            acc = jnp.broadcast_to(b_ref[:, cols], (n_seq, lanes))
            for j in range(width):
                acc = acc + w_ref[pl.ds(j, 1), cols] * row(t + j)
            y_ref[c * n_sub + k, pl.ds(t, n_seq, stride=t_new), :] = acc
        for r in range(n_past):
            ns_ref[r, :, cols] = row(r + t_new)

    @pl.when(c == pl.num_programs(0) - 1)
    def _():
        n_chunks = y_ref.shape[0]
        ch = n_chunks * lanes
        mean = jnp.sum(sum(y_ref[k] for k in range(n_chunks)), axis=-1, keepdims=True) / ch
        var = jnp.sum(sum((y_ref[k] - mean) ** 2 for k in range(n_chunks)), axis=-1, keepdims=True) / ch
        inv = lax.rsqrt(var + LN_EPS)
        for k in range(n_chunks):
            cols = slice(k * lanes, (k + 1) * lanes)
            z = (y_ref[k] - mean) * inv * g_ref[:, cols] + beta_ref[:, cols]
            o_ref[:, cols] = jax.nn.silu(z).astype(o_ref.dtype)


def _conv_sample(glu, state_t, w, b, g, beta, *, t_new, wide=2 * LANES):
    n_past, n_seq, ch = state_t.shape
    n_tok = n_seq * t_new
    assert w.shape[0] == n_past + 1
    chunk = lambda rows: pl.BlockSpec((rows, wide), lambda c: (0, c))
    slabs = pl.BlockSpec((n_past, n_seq, wide), lambda c: (0, 0, c))
    return pl.pallas_call(
        functools.partial(_conv_sample_kernel, t_new=t_new),
        grid=(ch // wide,),
        in_specs=[chunk(n_tok), slabs, chunk(w.shape[0]), chunk(1), _resident((1, ch)), _resident((1, ch))],
        out_specs=[pl.BlockSpec((n_tok, ch), lambda c: (0, 0)), slabs],
        out_shape=[jax.ShapeDtypeStruct((n_tok, ch), BF16), jax.ShapeDtypeStruct(state_t.shape, F32)],
        scratch_shapes=[pltpu.VMEM((ch // LANES, n_tok, LANES), F32), pltpu.VMEM((n_tok, LANES), F32)],
        compiler_params=_params("arbitrary"),
        name="conv_sample",
    )(glu, state_t, w, b, g, beta)


def _out_proj_kernel(x_ref, a_ref, c_ref, w_ref, o_ref):
    aw = a_ref.shape[1]
    o_ref[...] = (x_ref[...]
                  + jnp.dot(a_ref[...], w_ref[:aw, :].astype(BF16), preferred_element_type=F32)
                  + jnp.dot(c_ref[...], w_ref[aw:, :].astype(BF16), preferred_element_type=F32))


def _out_proj(x, attn, conv, w, *, slab, tm=512):
    n, d = x.shape
    rows = lambda width: pl.BlockSpec((tm, width), lambda i: (i, 0))
    return pl.pallas_call(
        _out_proj_kernel,
        grid=(n // tm,),
        in_specs=[rows(d), rows(attn.shape[1]), rows(conv.shape[1]), _resident_slab(w.shape, slab)],
        out_specs=rows(d),
        out_shape=jax.ShapeDtypeStruct((n, d), F32),
        compiler_params=_params("parallel"),
        name="out_proj",
    )(x, attn, conv, w)


def _ffn_kernel(x_ref, g_ref, wg_ref, wu_ref, wd_ref, o_ref, *rest):
    h_ref = rest[-1]

    @pl.when(pl.program_id(1) == 0)
    def _():
        x = x_ref[...]
        h_ref[...] = _rms(x, g_ref[...]).astype(BF16)
        o_ref[...] = x

    weights = [w_ref[...].astype(BF16) for w_ref in (wg_ref, wu_ref, wd_ref)]
    for w_out, w in zip(rest[:-1], weights):
        w_out[...] = w
    wg, wu, wd = weights
    h = h_ref[...]
    a = jnp.dot(h, wg, preferred_element_type=F32)
    u = jnp.dot(h, wu, preferred_element_type=F32)
    act = (jax.nn.silu(a) * u).astype(BF16)
    o_ref[...] += jnp.dot(act, wd, preferred_element_type=F32)


def _ffn(x, g, wg, wu, wd, *, layer=None, emit_bf16=False, tm=1024, tf=256):
    n, d = x.shape
    dff = wg.shape[-1]
    nt, nf = n // tm, dff // tf
    if layer is None:
        w_in_specs = [pl.BlockSpec((d, tf), lambda i, f: (0, f)), pl.BlockSpec((d, tf), lambda i, f: (0, f)),
                      pl.BlockSpec((tf, d), lambda i, f: (f, 0))]
    else:
        w_in_specs = [pl.BlockSpec((None, d, tf), lambda i, f: (layer, 0, f)),
                      pl.BlockSpec((None, d, tf), lambda i, f: (layer, 0, f)),
                      pl.BlockSpec((None, tf, d), lambda i, f: (layer, f, 0))]
    rows = pl.BlockSpec((tm, d), lambda i, f: (i, 0))
    out_specs, out_shape = [rows], [jax.ShapeDtypeStruct((n, d), F32)]
    if emit_bf16:
        assert nt == 1, "one row tile, so every bf16 weight tile is written exactly once"
        out_specs += [pl.BlockSpec((d, tf), lambda i, f: (0, f)), pl.BlockSpec((d, tf), lambda i, f: (0, f)),
                      pl.BlockSpec((tf, d), lambda i, f: (f, 0))]
        out_shape += [jax.ShapeDtypeStruct((d, dff), BF16), jax.ShapeDtypeStruct((d, dff), BF16),
                      jax.ShapeDtypeStruct((dff, d), BF16)]
    x_spec = _resident((tm, d)) if nt == 1 else rows
    outs = pl.pallas_call(
        _ffn_kernel,
        grid=(nt, nf),
        in_specs=[x_spec, _resident((1, d))] + w_in_specs,
        out_specs=out_specs,
        out_shape=out_shape,
        scratch_shapes=[pltpu.VMEM((tm, d), BF16)],
        compiler_params=_params("parallel", "arbitrary"),
        name="ffn",
    )(x, g, wg, wu, wd)
    return outs if emit_bf16 else outs[0]


def _pool_prompt_kernel(cur_ref, halo_ref, g_ref, pw_ref, ps_ref, o_ref, tail_ref, xp_ref):
    tp, d = cur_ref.shape
    halo = halo_ref.shape[0]
    i = pl.program_id(1)
    gain = g_ref[...]
    xp_ref[:halo, :] = jnp.where(i == 0, 0.0, _rms(halo_ref[...], gain))
    xp_ref[halo:, :] = _rms(cur_ref[...], gain)
    tail_ref[0] = xp_ref[tp:, :]
    pg = d // len(POOL_WINDOWS)
    pos = i * tp + lax.broadcasted_iota(jnp.int32, (tp, 1), 0)
    for gi, w in enumerate(POOL_WINDOWS):
        cols = slice(gi * pg, (gi + 1) * pg)
        ext = xp_ref[:, cols]
        h = ext[halo:]
        span = 1
        while span < w:
            ext = ext + pltpu.roll(ext, span, axis=0)
            span *= 2
        inv_count = 1.0 / jnp.minimum(pos + 1, w).astype(F32)
        dpool = (ext[halo:] * inv_count - h).astype(BF16)
        y = jnp.dot(dpool, pw_ref[gi].astype(BF16), preferred_element_type=F32)
        o_ref[:, cols] = cur_ref[:, cols] + y * ps_ref[:, cols]


def _pool_prompt(x, g, pw, ps, *, slab, batch, seq, tp=512, halo=16):
    assert all(w & (w - 1) == 0 and w <= halo for w in POOL_WINDOWS)
    d = x.shape[1]
    nt = seq // tp
    per = tp // halo
    cur = pl.BlockSpec((tp, d), lambda bi, i: (bi * nt + i, 0))
    prev = pl.BlockSpec((halo, d), lambda bi, i: (jnp.maximum((bi * nt + i) * per - 1, 0), 0))
    return pl.pallas_call(
        _pool_prompt_kernel,
        grid=(batch, nt),
        in_specs=[cur, prev, _resident((1, d)), _resident_slab(pw.shape, slab), _resident((1, d))],
        out_specs=[cur, pl.BlockSpec((1, halo, d), lambda bi, i: (bi, 0, 0))],
        out_shape=[jax.ShapeDtypeStruct((batch * seq, d), F32), jax.ShapeDtypeStruct((batch, halo, d), F32)],
        scratch_shapes=[pltpu.VMEM((halo + tp, d), F32)],
        compiler_params=_params("parallel", "arbitrary"),
        name="pool_prompt",
    )(x, x, g, pw, ps)


def _pool_sample_kernel(x_ref, st_ref, g_ref, pw_ref, ps_ref, o_ref, ns_ref, inv_ref, y_ref, tok_ref, *, t_new, start_pos):
    n_past, n_seq, pg = st_ref.shape
    lanes = tok_ref.shape[1]
    phase, gi = pl.program_id(0), pl.program_id(1)
    n_tok = x_ref.shape[0]
    token_rows = lambda t: pl.ds(t, n_seq, stride=t_new)
    group_cols = lambda k: pl.ds(pl.multiple_of(gi * pg + k * lanes, lanes), lanes)

    @pl.when(jnp.logical_and(phase == 0, gi == 0))
    def _():
        x = x_ref[...]
        inv = lax.rsqrt(jnp.mean(x * x, axis=-1, keepdims=True) + RMS_EPS)
        inv_ref[...] = jnp.broadcast_to(inv, (n_tok, lanes))

    @pl.when(phase == 0)
    def _():
        chunks = []
        for k in range(pg // lanes):
            cols = slice(k * lanes, (k + 1) * lanes)
            tok_ref[...] = x_ref[:, group_cols(k)] * inv_ref[...]
            gain = g_ref[:, group_cols(k)]
            new = [tok_ref[token_rows(t), :] * gain for t in range(t_new)]
            row = lambda r: st_ref[n_past + r, :, cols] if r < 0 else new[r]
            for r in range(n_past):
                ns_ref[r, :, cols] = row(r + t_new - n_past)
            sums = {1: {r: row(r) for r in range(-n_past, t_new)}}
            span = 1
            while span < max(POOL_WINDOWS):
                prev = sums[span]
                sums[2 * span] = {r: prev[r] + prev[r - span] for r in prev if r - span in prev}
                span *= 2
            diffs = []
            for t in range(t_new):
                pooled = None
                for i, w in enumerate(POOL_WINDOWS):
                    cand = sums[w][t] * (1.0 / min(start_pos + t + 1, w))
                    pooled = cand if pooled is None else jnp.where(gi == i, cand, pooled)
                diffs.append(pooled - new[t])
            chunks.append(jnp.concatenate(diffs, axis=0))
        dp = jnp.concatenate(chunks, axis=1).astype(BF16)
        cols = pl.ds(pl.multiple_of(gi * pg, pg), pg)
        y_ref[:, cols] = jnp.dot(dp, pw_ref[gi].astype(BF16), preferred_element_type=F32) * ps_ref[:, cols]

    @pl.when(phase == 1)
    def _():
        for k in range(pg // lanes):
            y = y_ref[:, group_cols(k)]
            tok_ref[...] = x_ref[:, group_cols(k)]
            for t in range(t_new):
                tok_ref[token_rows(t), :] = tok_ref[token_rows(t), :] + y[t * n_seq:(t + 1) * n_seq]
            o_ref[:, k * lanes:(k + 1) * lanes] = tok_ref[...]


def _pool_sample(x, state_t, g, pw, ps, *, slab, t_new):
    n_past, n_seq, d = state_t.shape
    n_tok = n_seq * t_new
    n_groups = len(POOL_WINDOWS)
    pg = d // n_groups
    assert all(w & (w - 1) == 0 and w <= n_past + 1 for w in POOL_WINDOWS)
    kern = functools.partial(_pool_sample_kernel, t_new=t_new, start_pos=PAST_LEN)
    slabs = pl.BlockSpec((n_past, n_seq, pg), lambda p, gi: (0, 0, gi * (1 - p) + (n_groups - 1) * p))
    return pl.pallas_call(
        kern,
        grid=(2, n_groups),
        in_specs=[_resident((n_tok, d)), slabs, _resident((1, d)), _resident_slab(pw.shape, slab), _resident((1, d))],
        out_specs=[pl.BlockSpec((n_tok, pg), lambda p, gi: (0, gi * p)), slabs],
        out_shape=[jax.ShapeDtypeStruct((n_tok, d), F32), jax.ShapeDtypeStruct(state_t.shape, F32)],
        scratch_shapes=[pltpu.VMEM((n_tok, LANES), F32), pltpu.VMEM((n_tok, d), F32), pltpu.VMEM((n_tok, LANES), F32)],
        compiler_params=_params("arbitrary", "arbitrary"),
        name="pool_sample",
    )(x, state_t, g, pw, ps)


def kernel(x_prompt, x_sample, cache_k, cache_v, state_conv, state_pool, norm_mix, w_in, q_norm, k_norm, sinks,
           conv_w, conv_b, conv_ln_g, conv_ln_b, w_out, pool_w, pool_scale, norm_ffn, w_gate, w_up, w_down):
    batch, seq, d = x_prompt.shape
    n_seq, t_new, _ = x_sample.shape
    n_kv = cache_k.shape[3]
    kv_w = n_kv * HEAD_DIM
    conv_ch = state_conv.shape[-1]
    attn_w = w_in.shape[-1] - 2 * kv_w - 2 * conv_ch
    n_prompt = batch * seq
    row = lambda a: a.reshape(1, -1)

    xp = x_prompt.reshape(n_prompt, d)
    xs = x_sample.reshape(n_seq * t_new, d)
    depth = norm_mix.shape[0]
    outs = {}
    for layer in range(depth):
        i = layer // 2
        if layer % 2 == 0:
            reps = MXU_DIM // HEAD_DIM
            proj = functools.partial(
                _in_proj, g=row(norm_mix[layer]), w=w_in, slab=i, qg=row(jnp.tile(q_norm[i], reps)),
                kg=row(jnp.tile(k_norm[i], reps)), attn_w=attn_w, kv_w=kv_w, conv_ch=conv_ch)
            q_p, k_p, v_p, glu_p = proj(xp)
            q_s, k_s, v_s, glu_s = proj(xs)
            attn_p = _attn_prompt(sinks[i], q_p, k_p, v_p, batch=batch, seq=seq, n_kv=n_kv)
            cache_t = lambda a: jnp.transpose(a, (0, 2, 3, 1)).reshape(n_seq, kv_w, WINDOW)
            new_t = lambda a: jnp.transpose(a.reshape(-1, WINDOW, kv_w), (0, 2, 1))
            attn_s, nk_t, nv_t = _attn_sample(sinks[i], q_s, new_t(k_s), new_t(v_s), cache_t(cache_k[i]),
                                              cache_t(cache_v[i]), t_new=t_new, n_kv=n_kv)
            uncache_t = lambda a: jnp.transpose(a.reshape(n_seq, n_kv, HEAD_DIM, WINDOW), (0, 3, 1, 2))
            ln = (row(conv_b[i]), row(conv_ln_g[i]), row(conv_ln_b[i]))
            conv_p = _conv_prompt(glu_p, conv_w[i], *ln, batch=batch, seq=seq)
            conv_s, nc_t = _conv_sample(glu_s, jnp.transpose(state_conv[i], (1, 0, 2)), conv_w[i], *ln, t_new=t_new)
            nc_s = jnp.transpose(nc_t, (1, 0, 2))
            xp = _out_proj(xp, attn_p, conv_p, w_out, slab=i)
            xs = _out_proj(xs, attn_s, conv_s, w_out, slab=i)
            last = lambda a, n: a.reshape(batch, seq, -1)[:, -n:]
            outs.setdefault("kp", []).append(last(k_p, WINDOW).reshape(batch, WINDOW, n_kv, HEAD_DIM))
            outs.setdefault("vp", []).append(last(v_p, WINDOW).reshape(batch, WINDOW, n_kv, HEAD_DIM))
            outs.setdefault("cp", []).append(last(glu_p, conv_w.shape[1] - 1))
            outs.setdefault("ks", []).append(uncache_t(nk_t))
            outs.setdefault("vs", []).append(uncache_t(nv_t))
            outs.setdefault("cs", []).append(nc_s)
        else:
            args = (row(norm_mix[layer]), pool_w, row(pool_scale[i]))
            n_past = state_pool.shape[2]
            xp, tail = _pool_prompt(xp, *args, slab=i, batch=batch, seq=seq)
            xs, np_t = _pool_sample(xs, jnp.transpose(state_pool[i], (1, 0, 2)), *args, slab=i, t_new=t_new)
            np_s = jnp.transpose(np_t, (1, 0, 2))
            outs.setdefault("pp", []).append(tail[:, -n_past:])
            outs.setdefault("ps", []).append(np_s)
        g_ffn = row(norm_ffn[layer])
        xs, wg16, wu16, wd16 = _ffn(xs, g_ffn, w_gate, w_up, w_down, layer=layer, emit_bf16=True)
        xp = _ffn(xp, g_ffn, wg16, wu16, wd16, tf=512)
    st = lambda name: jnp.stack(outs[name])
    return (xp.reshape(batch, seq, d), xs.reshape(n_seq, t_new, d),
            st("kp"), st("vp"), st("cp"), st("pp"), st("ks"), st("vs"), st("cs"), st("ps"))
```

```python
import functools

import jax
import jax.numpy as jnp
from jax import lax
from jax.experimental import pallas as pl
from jax.experimental.pallas import tpu as pltpu

F32 = jnp.float32
BF16 = jnp.bfloat16

HEAD_DIM = 64
WINDOW = 128
PAST_LEN = 8192
POOL_WINDOWS = (2, 4, 8, 16)
RMS_EPS = 1e-6
LN_EPS = 1e-5
NEG = -1e30

LANES = 128
SUBLANES = 8
MXU_DIM = 256
VMEM_LIMIT = 56 * 1024 * 1024


def _params(*semantics):
    return pltpu.CompilerParams(dimension_semantics=semantics, vmem_limit_bytes=VMEM_LIMIT)


def _rms(x, g):
    ms = jnp.mean(x * x, axis=-1, keepdims=True)
    return x * lax.rsqrt(ms + RMS_EPS) * g


def _resident(shape):
    zeros = (0,) * len(shape)
    return pl.BlockSpec(shape, lambda *_: zeros, pipeline_mode=pl.Buffered(1))


def _resident_slab(shape, slab):
    index = (slab,) + (0,) * (len(shape) - 1)
    return pl.BlockSpec((None,) + tuple(shape[1:]), lambda *_: index, pipeline_mode=pl.Buffered(1))


def _in_proj_kernel(x_ref, g_ref, w_ref, qg_ref, kg_ref, q_ref, k_ref, v_ref, glu_ref, *, attn_w, kv_w, conv_ch):
    h = _rms(x_ref[...], g_ref[...]).astype(BF16)
    row = lax.broadcasted_iota(jnp.int32, (MXU_DIM, MXU_DIM), 0) // HEAD_DIM
    col = lax.broadcasted_iota(jnp.int32, (MXU_DIM, MXU_DIM), 1) // HEAD_DIM
    pmat = jnp.where(row == col, 1.0 / HEAD_DIM, 0.0).astype(BF16)

    def proj(c0):
        return jnp.dot(h, w_ref[:, c0:c0 + MXU_DIM].astype(BF16), preferred_element_type=F32)

    def head_rms(z):
        sq = z * z
        hi = sq.astype(BF16)
        lo = (sq - hi.astype(F32)).astype(BF16)
        ms = jnp.dot(hi, pmat, preferred_element_type=F32) + jnp.dot(lo, pmat, preferred_element_type=F32)
        return z * lax.rsqrt(ms + RMS_EPS)

    scale = HEAD_DIM ** -0.5

    def put_q(c0, z):
        q_ref[:, c0:c0 + MXU_DIM] = (head_rms(z) * qg_ref[...] * scale).astype(q_ref.dtype)

    def put_k(c0, z):
        k_ref[:, c0:c0 + MXU_DIM] = head_rms(z) * kg_ref[...]

    def put_v(c0, z):
        v_ref[:, c0:c0 + MXU_DIM] = z

    def put_glu(c0, a, gate):
        glu_ref[:, c0:c0 + MXU_DIM] = a * jax.nn.sigmoid(gate)

    units = [((c0,), functools.partial(put_q, c0)) for c0 in range(0, attn_w, MXU_DIM)]
    units += [((attn_w + c0,), functools.partial(put_k, c0)) for c0 in range(0, kv_w, MXU_DIM)]
    units += [((attn_w + kv_w + c0,), functools.partial(put_v, c0)) for c0 in range(0, kv_w, MXU_DIM)]
    units += [((attn_w + 2 * kv_w + c0, attn_w + 2 * kv_w + conv_ch + c0), functools.partial(put_glu, c0))
              for c0 in range(0, conv_ch, MXU_DIM)]
    pending = None
    for cols, epilogue in units:
        zs = [proj(c0) for c0 in cols]
        if pending is not None:
            pending[0](*pending[1])
        pending = (epilogue, zs)
    pending[0](*pending[1])


def _in_proj(x, g, w, qg, kg, *, slab, attn_w, kv_w, conv_ch, tm=512):
    n, d = x.shape
    kern = functools.partial(_in_proj_kernel, attn_w=attn_w, kv_w=kv_w, conv_ch=conv_ch)
    rows = lambda width: pl.BlockSpec((tm, width), lambda i: (i, 0))
    return pl.pallas_call(
        kern,
        grid=(n // tm,),
        in_specs=[rows(d), _resident((1, d)), _resident_slab(w.shape, slab), _resident((1, MXU_DIM)), _resident((1, MXU_DIM))],
        out_specs=[rows(attn_w), rows(kv_w), rows(kv_w), rows(conv_ch)],
        out_shape=[
            jax.ShapeDtypeStruct((n, attn_w), BF16),
            jax.ShapeDtypeStruct((n, kv_w), F32),
            jax.ShapeDtypeStruct((n, kv_w), F32),
            jax.ShapeDtypeStruct((n, conv_ch), F32),
        ],
        compiler_params=_params("parallel"),
        name="in_proj",
    )(x, g, w, qg, kg)


def _pair_blocks(chunk, rolled, odd):
    lane = lax.broadcasted_iota(jnp.int32, chunk.shape, 1)
    low = lane < HEAD_DIM
    if odd:
        top = jnp.where(low, rolled, 0.0)
        bot = jnp.where(low, 0.0, chunk)
    else:
        top = jnp.where(low, chunk, 0.0)
        bot = jnp.where(low, 0.0, rolled)
    return jnp.concatenate([top, bot], axis=0).astype(BF16)


def _softmax_pv(s, ok, sink_a, sink_b, vb_ones):
    half = s.shape[1] // 2
    s = jnp.where(ok, s, NEG)
    sa, sb = s[:, :half], s[:, half:]
    ma = jnp.maximum(jnp.max(sa, axis=-1, keepdims=True), sink_a)
    mb = jnp.maximum(jnp.max(sb, axis=-1, keepdims=True), sink_b)
    p = jnp.concatenate([jnp.exp(sa - ma), jnp.exp(sb - mb)], axis=1).astype(BF16)
    r = jnp.dot(p, vb_ones, preferred_element_type=F32)
    lane = lax.broadcasted_iota(jnp.int32, (s.shape[0], LANES), 1)
    sink_term = jnp.where(lane < HEAD_DIM, jnp.exp(sink_a - ma), jnp.exp(sink_b - mb))
    return r[:, :LANES] / (r[:, LANES:] + sink_term)


def _attend(q2, kb, vb, ok, sinks_ref, h0, rows):
    s = lax.dot_general(q2, kb, (((1,), (1,)), ((), ())), preferred_element_type=F32)
    row = lax.broadcasted_iota(jnp.int32, (2 * rows, 1), 0)
    first = row < rows
    sink_a = jnp.where(first, sinks_ref[h0], sinks_ref[h0 + 2])
    sink_b = jnp.where(first, sinks_ref[h0 + 1], sinks_ref[h0 + 3])
    return _softmax_pv(s, ok, sink_a, sink_b, vb)


def _attn_prompt_kernel(sinks_ref, q_ref, kp_ref, kc_ref, vp_ref, vc_ref, o_ref, *, n_kv):
    blk = kp_ref.shape[0]
    i = pl.program_id(1)
    shape = (2 * blk, 4 * blk)
    r = lax.broadcasted_iota(jnp.int32, shape, 0) % blk
    c = lax.broadcasted_iota(jnp.int32, shape, 1) % (2 * blk)
    band = jnp.logical_and(c > r, c <= r + blk)
    band_first = jnp.logical_and(band, jnp.logical_or(c >= blk, i > 0))
    group = (q_ref.shape[1] // HEAD_DIM) // n_kv
    key_row = lax.broadcasted_iota(jnp.int32, (4 * blk, LANES), 0)
    key_lane = lax.broadcasted_iota(jnp.int32, (4 * blk, LANES), 1)
    ones_blocks = jnp.where((key_row < 2 * blk) == (key_lane < HEAD_DIM), 1.0, 0.0).astype(BF16)
    for sub in range(q_ref.shape[0] // blk):
        rows = slice(sub * blk, (sub + 1) * blk)
        if sub == 0:
            k_all = jnp.concatenate([kp_ref[...], kc_ref[rows, :]], axis=0)
            v_all = jnp.concatenate([vp_ref[...], vc_ref[rows, :]], axis=0)
        else:
            k_all = kc_ref[(sub - 1) * blk:(sub + 1) * blk, :]
            v_all = vc_ref[(sub - 1) * blk:(sub + 1) * blk, :]
        ok = band if sub else band_first
        for cch in range(n_kv // 2):
            kc = k_all[:, cch * LANES:(cch + 1) * LANES]
            vc = v_all[:, cch * LANES:(cch + 1) * LANES]
            kr = pltpu.roll(kc, HEAD_DIM, axis=1)
            vr = pltpu.roll(vc, HEAD_DIM, axis=1)
            for odd in range(2):
                j = 2 * cch + odd
                kb = _pair_blocks(kc, kr, odd)
                vb = jnp.concatenate([_pair_blocks(vc, vr, odd), ones_blocks], axis=1)
                h0 = group * j
                c0 = h0 * HEAD_DIM
                q2 = jnp.concatenate([q_ref[rows, c0:c0 + LANES], q_ref[rows, c0 + LANES:c0 + 2 * LANES]], axis=0)
                o = _attend(q2, kb, vb, ok, sinks_ref, h0, blk).astype(o_ref.dtype)
                o_ref[rows, c0:c0 + LANES] = o[:blk]
                o_ref[rows, c0 + LANES:c0 + 2 * LANES] = o[blk:]


def _attn_prompt(sinks, q, k, v, *, batch, seq, n_kv, blocks_per_step=8):
    blk = WINDOW
    tq = blocks_per_step * blk
    nb = seq // tq
    aw, kw = q.shape[1], k.shape[1]
    cur = lambda w: pl.BlockSpec((tq, w), lambda b, i: (b * nb + i, 0))
    prev = lambda w: pl.BlockSpec(
        (blk, w), lambda b, i: ((b * nb + i) * blocks_per_step - jnp.minimum(i, 1), 0))
    return pl.pallas_call(
        functools.partial(_attn_prompt_kernel, n_kv=n_kv),
        grid=(batch, nb),
        in_specs=[pl.BlockSpec(memory_space=pltpu.SMEM), cur(aw), prev(kw), cur(kw), prev(kw), cur(kw)],
        out_specs=cur(aw),
        out_shape=jax.ShapeDtypeStruct((batch * seq, aw), BF16),
        compiler_params=_params("parallel", "arbitrary"),
        name="attn_prompt",
    )(sinks, q, k, k, v, v)


def _attn_sample_kernel(sinks_ref, q_ref, kn_ref, vn_ref, ck_ref, cv_ref, o_ref, nk_ref, nv_ref, qf_ref, of_ref, *, n_kv, t_new):
    g_seqs, kw, win = ck_ref.shape
    n_heads = q_ref.shape[1] // HEAD_DIM
    group = n_heads // n_kv
    n_rows = n_heads * t_new
    qf_ref[...] = q_ref[...].astype(F32)
    t = lax.broadcasted_iota(jnp.int32, (n_rows, 2 * win), 0) % t_new
    c = lax.broadcasted_iota(jnp.int32, (n_rows, 2 * win), 1)
    ok = jnp.logical_or(jnp.logical_and(c < win, c > t),
                        jnp.logical_and(c >= 2 * win - t_new, c - (2 * win - t_new) <= t))
    sink = jnp.concatenate([jnp.full((t_new, 1), sinks_ref[h], F32) for h in range(n_heads)], axis=0)
    low = lax.broadcasted_iota(jnp.int32, (t_new, LANES), 1) < HEAD_DIM
    zero_tile = jnp.zeros((t_new, LANES), F32)
    keep_old = lax.broadcasted_iota(jnp.int32, (kw, win), 1) < win - t_new
    ones_rows = jnp.ones((LANES, 2 * win), BF16)
    kn_all = kn_ref[0]
    vn_all = vn_ref[0]

    def one_seq(g, carry):
        r0 = pl.multiple_of(g * t_new, t_new)
        k_old = ck_ref[g]
        v_old = cv_ref[g]
        shift = win - t_new * (g + 1)
        k_new = pltpu.roll(kn_all, shift, axis=1)
        v_new = pltpu.roll(vn_all, shift, axis=1)
        nk_ref[g] = jnp.where(keep_old, pltpu.roll(k_old, win - t_new, axis=1), k_new)
        nv_ref[g] = jnp.where(keep_old, pltpu.roll(v_old, win - t_new, axis=1), v_new)
        keys = jnp.concatenate([k_old, k_new], axis=1).astype(BF16)
        vals = jnp.concatenate([jnp.concatenate([v_old, v_new], axis=1).astype(BF16), ones_rows], axis=0)
        q_rows = []
        for h in range(n_heads):
            j = h // group
            src = qf_ref[pl.ds(r0, t_new), (h // 2) * LANES:(h // 2 + 1) * LANES]
            if h % 2 != j % 2:
                src = pltpu.roll(src, HEAD_DIM, axis=1)
            half = jnp.where(low, src, 0.0) if j % 2 == 0 else jnp.where(low, 0.0, src)
            tiles = [zero_tile] * (kw // LANES)
            tiles[j // 2] = half
            q_rows.append(jnp.concatenate(tiles, axis=1))
        qb = jnp.concatenate(q_rows, axis=0).astype(BF16)
        s = jnp.dot(qb, keys, preferred_element_type=F32)
        s = jnp.where(ok, s, NEG)
        m = jnp.maximum(jnp.max(s, axis=-1, keepdims=True), sink)
        p = jnp.exp(s - m).astype(BF16)
        r = lax.dot_general(p, vals, (((1,), (1,)), ((), ())), preferred_element_type=F32)
        denom = r[:, kw:] + jnp.exp(sink - m)
        o = r[:, :kw] / jnp.concatenate([denom] * (kw // LANES), axis=1)
        for pair in range(n_heads // 2):
            j = (2 * pair) // group
            lanes = slice((j // 2) * LANES, (j // 2 + 1) * LANES)
            a = o[2 * pair * t_new:(2 * pair + 1) * t_new, lanes]
            b = o[(2 * pair + 1) * t_new:(2 * pair + 2) * t_new, lanes]
            if j % 2 == 0:
                b = pltpu.roll(b, HEAD_DIM, axis=1)
            else:
                a = pltpu.roll(a, HEAD_DIM, axis=1)
            of_ref[pl.ds(r0, t_new), pair * LANES:(pair + 1) * LANES] = jnp.where(low, a, b)
        return carry

    lax.fori_loop(0, g_seqs, one_seq, 0, unroll=4)
    o_ref[...] = of_ref[...].astype(o_ref.dtype)


def _attn_sample(sinks, q, k_t, v_t, cache_k_t, cache_v_t, *, t_new, n_kv):
    n_seq, kw, win = cache_k_t.shape
    g_seqs = win // t_new
    aw = q.shape[1]
    assert k_t.shape == (n_seq // g_seqs, kw, win)
    tok = pl.BlockSpec((g_seqs * t_new, aw), lambda i: (i, 0))
    new = pl.BlockSpec((1, kw, win), lambda i: (i, 0, 0))
    cache = pl.BlockSpec((g_seqs, kw, win), lambda i: (i, 0, 0))
    return pl.pallas_call(
        functools.partial(_attn_sample_kernel, n_kv=n_kv, t_new=t_new),
        grid=(n_seq // g_seqs,),
        in_specs=[pl.BlockSpec(memory_space=pltpu.SMEM), tok, new, new, cache, cache],
        out_specs=[tok, cache, cache],
        out_shape=[
            jax.ShapeDtypeStruct((n_seq * t_new, aw), BF16),
            jax.ShapeDtypeStruct(cache_k_t.shape, F32),
            jax.ShapeDtypeStruct(cache_v_t.shape, F32),
        ],
        scratch_shapes=[pltpu.VMEM((g_seqs * t_new, aw), F32), pltpu.VMEM((g_seqs * t_new, aw), F32)],
        compiler_params=_params("parallel"),
        name="attn_sample",
    )(sinks, q, k_t, v_t, cache_k_t, cache_v_t)


def _ln_silu(y, g, b):
    yc = y - jnp.mean(y, axis=-1, keepdims=True)
    var = jnp.mean(yc * yc, axis=-1, keepdims=True)
    return jax.nn.silu(yc * lax.rsqrt(var + LN_EPS) * g + b)


def _conv_prompt_kernel(cur_ref, halo_ref, w_ref, b_ref, g_ref, beta_ref, o_ref, xc_ref, y_ref, *, width, rows_per, lanes_per):
    tc, ch = cur_ref.shape
    halo = halo_ref.shape[0]
    i = pl.program_id(1)
    xc_ref[:halo, :] = jnp.where(i == 0, 0.0, halo_ref[...])
    xc_ref[halo:, :] = cur_ref[...]
    base = halo - (width - 1)

    def col_chunk(cc, carry):
        l0 = pl.multiple_of(cc * lanes_per, lanes_per)
        cols = pl.ds(l0, lanes_per)
        for rr in range(tc // rows_per):
            acc = jnp.broadcast_to(b_ref[:, cols], (rows_per, lanes_per))
            for phase in range(SUBLANES):
                n_win = rows_per + (SUBLANES if phase else 0)
                part = None
                for j in range(width):
                    if (base + j) % SUBLANES != phase:
                        continue
                    start = rr * rows_per + ((base + j) // SUBLANES) * SUBLANES
                    term = w_ref[pl.ds(j, 1), cols] * xc_ref[pl.ds(start, n_win), cols]
                    part = term if part is None else part + term
                if part is None:
                    continue
                if phase:
                    part = pltpu.roll(part, n_win - phase, axis=0)
                acc = acc + part[:rows_per]
            y_ref[pl.ds(rr * rows_per, rows_per), cols] = acc
        return carry

    lax.fori_loop(0, ch // lanes_per, col_chunk, 0)
    o_ref[...] = _ln_silu(y_ref[...], g_ref[...], beta_ref[...]).astype(o_ref.dtype)


def _conv_prompt(glu, w, b, g, beta, *, batch, seq, tc=512, halo=32):
    ch = glu.shape[1]
    width = w.shape[0]
    nt = seq // tc
    per = tc // halo
    cur = pl.BlockSpec((tc, ch), lambda bi, i: (bi * nt + i, 0))
    prev = pl.BlockSpec((halo, ch), lambda bi, i: (jnp.maximum((bi * nt + i) * per - 1, 0), 0))
    kern = functools.partial(_conv_prompt_kernel, width=width, rows_per=128, lanes_per=LANES)
    return pl.pallas_call(
        kern,
        grid=(batch, nt),
        in_specs=[cur, prev, _resident(w.shape), _resident((1, ch)), _resident((1, ch)), _resident((1, ch))],
        out_specs=cur,
        out_shape=jax.ShapeDtypeStruct((batch * seq, ch), BF16),
        scratch_shapes=[pltpu.VMEM((halo + tc, ch), F32), pltpu.VMEM((tc, ch), F32)],
        compiler_params=_params("parallel", "arbitrary"),
        name="conv_prompt",
    )(glu, glu, w, b, g, beta)


def _conv_sample_kernel(new_ref, st_ref, w_ref, b_ref, g_ref, beta_ref, o_ref, ns_ref, y_ref, tok_ref, *, t_new):
    n_past, n_seq, wide = st_ref.shape
    lanes = tok_ref.shape[1]
    n_sub = wide // lanes
    width = w_ref.shape[0]
    c = pl.program_id(0)
    for k in range(n_sub):
        cols = slice(k * lanes, (k + 1) * lanes)
        tok_ref[...] = new_ref[:, cols]
        new = [tok_ref[pl.ds(t, n_seq, stride=t_new), :] for t in range(t_new)]
        row = lambda r: st_ref[r, :, cols] if r < n_past else new[r - n_past]
        for t in range(t_new):
            acc = jnp.broadcast_to(b_ref[:, cols], (n_seq, lanes))
            for j in range(width):
                acc = acc + w_ref[pl.ds(j, 1), cols] * row(t + j)
            y_ref[c * n_sub + k, pl.ds(t, n_seq, stride=t_new), :] = acc
        for r in range(n_past):
            ns_ref[r, :, cols] = row(r + t_new)

    @pl.when(c == pl.num_programs(0) - 1)
    def _():
        n_chunks = y_ref.shape[0]
        ch = n_chunks * lanes
        mean = jnp.sum(sum(y_ref[k] for k in range(n_chunks)), axis=-1, keepdims=True) / ch
        var = jnp.sum(sum((y_ref[k] - mean) ** 2 for k in range(n_chunks)), axis=-1, keepdims=True) / ch
        inv = lax.rsqrt(var + LN_EPS)
        for k in range(n_chunks):
            cols = slice(k * lanes, (k + 1) * lanes)
            z = (y_ref[k] - mean) * inv * g_ref[:, cols] + beta_ref[:, cols]
            o_ref[:, cols] = jax.nn.silu(z).astype(o_ref.dtype)


def _conv_sample(glu, state_t, w, b, g, beta, *, t_new, wide=2 * LANES):
    n_past, n_seq, ch = state_t.shape
    n_tok = n_seq * t_new
    assert w.shape[0] == n_past + 1
    chunk = lambda rows: pl.BlockSpec((rows, wide), lambda c: (0, c))
    slabs = pl.BlockSpec((n_past, n_seq, wide), lambda c: (0, 0, c))
    return pl.pallas_call(
        functools.partial(_conv_sample_kernel, t_new=t_new),
        grid=(ch // wide,),
        in_specs=[chunk(n_tok), slabs, chunk(w.shape[0]), chunk(1), _resident((1, ch)), _resident((1, ch))],
        out_specs=[pl.BlockSpec((n_tok, ch), lambda c: (0, 0)), slabs],
        out_shape=[jax.ShapeDtypeStruct((n_tok, ch), BF16), jax.ShapeDtypeStruct(state_t.shape, F32)],
        scratch_shapes=[pltpu.VMEM((ch // LANES, n_tok, LANES), F32), pltpu.VMEM((n_tok, LANES), F32)],
        compiler_params=_params("arbitrary"),
        name="conv_sample",
    )(glu, state_t, w, b, g, beta)


def _out_proj_kernel(x_ref, a_ref, c_ref, w_ref, o_ref):
    aw = a_ref.shape[1]
    o_ref[...] = (x_ref[...]
                  + jnp.dot(a_ref[...], w_ref[:aw, :].astype(BF16), preferred_element_type=F32)
                  + jnp.dot(c_ref[...], w_ref[aw:, :].astype(BF16), preferred_element_type=F32))


def _out_proj(x, attn, conv, w, *, slab, tm=512):
    n, d = x.shape
    rows = lambda width: pl.BlockSpec((tm, width), lambda i: (i, 0))
    return pl.pallas_call(
        _out_proj_kernel,
        grid=(n // tm,),
        in_specs=[rows(d), rows(attn.shape[1]), rows(conv.shape[1]), _resident_slab(w.shape, slab)],
        out_specs=rows(d),
        out_shape=jax.ShapeDtypeStruct((n, d), F32),
        compiler_params=_params("parallel"),
        name="out_proj",
    )(x, attn, conv, w)


def _ffn_kernel(x_ref, g_ref, wg_ref, wu_ref, wd_ref, o_ref, *rest):
    h_ref = rest[-1]

    @pl.when(pl.program_id(1) == 0)
    def _():
        x = x_ref[...]
        h_ref[...] = _rms(x, g_ref[...]).astype(BF16)
        o_ref[...] = x

    weights = [w_ref[...].astype(BF16) for w_ref in (wg_ref, wu_ref, wd_ref)]
    for w_out, w in zip(rest[:-1], weights):
        w_out[...] = w
    wg, wu, wd = weights
    h = h_ref[...]
    a = jnp.dot(h, wg, preferred_element_type=F32)
    u = jnp.dot(h, wu, preferred_element_type=F32)
    act = (jax.nn.silu(a) * u).astype(BF16)
    o_ref[...] += jnp.dot(act, wd, preferred_element_type=F32)


def _ffn(x, g, wg, wu, wd, *, layer=None, emit_bf16=False, tm=1024, tf=256):
    n, d = x.shape
    dff = wg.shape[-1]
    nt, nf = n // tm, dff // tf
    if layer is None:
        w_in_specs = [pl.BlockSpec((d, tf), lambda i, f: (0, f)), pl.BlockSpec((d, tf), lambda i, f: (0, f)),
                      pl.BlockSpec((tf, d), lambda i, f: (f, 0))]
    else:
        w_in_specs = [pl.BlockSpec((None, d, tf), lambda i, f: (layer, 0, f)),
                      pl.BlockSpec((None, d, tf), lambda i, f: (layer, 0, f)),
                      pl.BlockSpec((None, tf, d), lambda i, f: (layer, f, 0))]
    rows = pl.BlockSpec((tm, d), lambda i, f: (i, 0))
    out_specs, out_shape = [rows], [jax.ShapeDtypeStruct((n, d), F32)]
    if emit_bf16:
        assert nt == 1, "one row tile, so every bf16 weight tile is written exactly once"
        out_specs += [pl.BlockSpec((d, tf), lambda i, f: (0, f)), pl.BlockSpec((d, tf), lambda i, f: (0, f)),
                      pl.BlockSpec((tf, d), lambda i, f: (f, 0))]
        out_shape += [jax.ShapeDtypeStruct((d, dff), BF16), jax.ShapeDtypeStruct((d, dff), BF16),
                      jax.ShapeDtypeStruct((dff, d), BF16)]
    x_spec = _resident((tm, d)) if nt == 1 else rows
    outs = pl.pallas_call(
        _ffn_kernel,
        grid=(nt, nf),
        in_specs=[x_spec, _resident((1, d))] + w_in_specs,
        out_specs=out_specs,
        out_shape=out_shape,
        scratch_shapes=[pltpu.VMEM((tm, d), BF16)],
        compiler_params=_params("parallel", "arbitrary"),
        name="ffn",
    )(x, g, wg, wu, wd)
    return outs if emit_bf16 else outs[0]


def _pool_prompt_kernel(cur_ref, halo_ref, g_ref, pw_ref, ps_ref, o_ref, tail_ref, xp_ref):
    tp, d = cur_ref.shape
    halo = halo_ref.shape[0]
    i = pl.program_id(1)
    gain = g_ref[...]
    xp_ref[:halo, :] = jnp.where(i == 0, 0.0, _rms(halo_ref[...], gain))
    xp_ref[halo:, :] = _rms(cur_ref[...], gain)
    tail_ref[0] = xp_ref[tp:, :]
    pg = d // len(POOL_WINDOWS)
    pos = i * tp + lax.broadcasted_iota(jnp.int32, (tp, 1), 0)
    for gi, w in enumerate(POOL_WINDOWS):
        cols = slice(gi * pg, (gi + 1) * pg)
        ext = xp_ref[:, cols]
        h = ext[halo:]
        span = 1
        while span < w:
            ext = ext + pltpu.roll(ext, span, axis=0)
            span *= 2
        inv_count = 1.0 / jnp.minimum(pos + 1, w).astype(F32)
        dpool = (ext[halo:] * inv_count - h).astype(BF16)
        y = jnp.dot(dpool, pw_ref[gi].astype(BF16), preferred_element_type=F32)
        o_ref[:, cols] = cur_ref[:, cols] + y * ps_ref[:, cols]


def _pool_prompt(x, g, pw, ps, *, slab, batch, seq, tp=1024, halo=16):
    assert all(w & (w - 1) == 0 and w <= halo for w in POOL_WINDOWS)
    d = x.shape[1]
    nt = seq // tp
    per = tp // halo
    cur = pl.BlockSpec((tp, d), lambda bi, i: (bi * nt + i, 0))
    prev = pl.BlockSpec((halo, d), lambda bi, i: (jnp.maximum((bi * nt + i) * per - 1, 0), 0))
    return pl.pallas_call(
        _pool_prompt_kernel,
        grid=(batch, nt),
        in_specs=[cur, prev, _resident((1, d)), _resident_slab(pw.shape, slab), _resident((1, d))],
        out_specs=[cur, pl.BlockSpec((1, halo, d), lambda bi, i: (bi, 0, 0))],
        out_shape=[jax.ShapeDtypeStruct((batch * seq, d), F32), jax.ShapeDtypeStruct((batch, halo, d), F32)],
        scratch_shapes=[pltpu.VMEM((halo + tp, d), F32)],
        compiler_params=_params("parallel", "arbitrary"),
        name="pool_prompt",
    )(x, x, g, pw, ps)


def _pool_sample_kernel(x_ref, st_ref, g_ref, pw_ref, ps_ref, o_ref, ns_ref, inv_ref, y_ref, tok_ref, *, t_new, start_pos):
    n_past, n_seq, pg = st_ref.shape
    lanes = tok_ref.shape[1]
    phase, gi = pl.program_id(0), pl.program_id(1)
    n_tok = x_ref.shape[0]
    token_rows = lambda t: pl.ds(t, n_seq, stride=t_new)
    group_cols = lambda k: pl.ds(pl.multiple_of(gi * pg + k * lanes, lanes), lanes)

    @pl.when(jnp.logical_and(phase == 0, gi == 0))
    def _():
        x = x_ref[...]
        inv = lax.rsqrt(jnp.mean(x * x, axis=-1, keepdims=True) + RMS_EPS)
        inv_ref[...] = jnp.broadcast_to(inv, (n_tok, lanes))

    @pl.when(phase == 0)
    def _():
        chunks = []
        for k in range(pg // lanes):
            cols = slice(k * lanes, (k + 1) * lanes)
            tok_ref[...] = x_ref[:, group_cols(k)] * inv_ref[...]
            gain = g_ref[:, group_cols(k)]
            new = [tok_ref[token_rows(t), :] * gain for t in range(t_new)]
            row = lambda r: st_ref[n_past + r, :, cols] if r < 0 else new[r]
            for r in range(n_past):
                ns_ref[r, :, cols] = row(r + t_new - n_past)
            sums = {1: {r: row(r) for r in range(-n_past, t_new)}}
            span = 1
            while span < max(POOL_WINDOWS):
                prev = sums[span]
                sums[2 * span] = {r: prev[r] + prev[r - span] for r in prev if r - span in prev}
                span *= 2
            diffs = []
            for t in range(t_new):
                pooled = None
                for i, w in enumerate(POOL_WINDOWS):
                    cand = sums[w][t] * (1.0 / min(start_pos + t + 1, w))
                    pooled = cand if pooled is None else jnp.where(gi == i, cand, pooled)
                diffs.append(pooled - new[t])
            chunks.append(jnp.concatenate(diffs, axis=0))
        dp = jnp.concatenate(chunks, axis=1).astype(BF16)
        cols = pl.ds(pl.multiple_of(gi * pg, pg), pg)
        y_ref[:, cols] = jnp.dot(dp, pw_ref[gi].astype(BF16), preferred_element_type=F32) * ps_ref[:, cols]

    @pl.when(phase == 1)
    def _():
        for k in range(pg // lanes):
            y = y_ref[:, group_cols(k)]
            tok_ref[...] = x_ref[:, group_cols(k)]
            for t in range(t_new):
                tok_ref[token_rows(t), :] = tok_ref[token_rows(t), :] + y[t * n_seq:(t + 1) * n_seq]
            o_ref[:, k * lanes:(k + 1) * lanes] = tok_ref[...]


def _pool_sample(x, state_t, g, pw, ps, *, slab, t_new):
    n_past, n_seq, d = state_t.shape
    n_tok = n_seq * t_new
    n_groups = len(POOL_WINDOWS)
    pg = d // n_groups
    assert all(w & (w - 1) == 0 and w <= n_past + 1 for w in POOL_WINDOWS)
    kern = functools.partial(_pool_sample_kernel, t_new=t_new, start_pos=PAST_LEN)
    slabs = pl.BlockSpec((n_past, n_seq, pg), lambda p, gi: (0, 0, gi * (1 - p) + (n_groups - 1) * p))
    return pl.pallas_call(
        kern,
        grid=(2, n_groups),
        in_specs=[_resident((n_tok, d)), slabs, _resident((1, d)), _resident_slab(pw.shape, slab), _resident((1, d))],
        out_specs=[pl.BlockSpec((n_tok, pg), lambda p, gi: (0, gi * p)), slabs],
        out_shape=[jax.ShapeDtypeStruct((n_tok, d), F32), jax.ShapeDtypeStruct(state_t.shape, F32)],
        scratch_shapes=[pltpu.VMEM((n_tok, LANES), F32), pltpu.VMEM((n_tok, d), F32), pltpu.VMEM((n_tok, LANES), F32)],
        compiler_params=_params("arbitrary", "arbitrary"),
        name="pool_sample",
    )(x, state_t, g, pw, ps)


def kernel(x_prompt, x_sample, cache_k, cache_v, state_conv, state_pool, norm_mix, w_in, q_norm, k_norm, sinks,
           conv_w, conv_b, conv_ln_g, conv_ln_b, w_out, pool_w, pool_scale, norm_ffn, w_gate, w_up, w_down):
    batch, seq, d = x_prompt.shape
    n_seq, t_new, _ = x_sample.shape
    n_kv = cache_k.shape[3]
    kv_w = n_kv * HEAD_DIM
    conv_ch = state_conv.shape[-1]
    attn_w = w_in.shape[-1] - 2 * kv_w - 2 * conv_ch
    n_prompt = batch * seq
    row = lambda a: a.reshape(1, -1)

    xp = x_prompt.reshape(n_prompt, d)
    xs = x_sample.reshape(n_seq * t_new, d)
    depth = norm_mix.shape[0]
    outs = {}
    for layer in range(depth):
        i = layer // 2
        if layer % 2 == 0:
            reps = MXU_DIM // HEAD_DIM
            proj = functools.partial(
                _in_proj, g=row(norm_mix[layer]), w=w_in, slab=i, qg=row(jnp.tile(q_norm[i], reps)),
                kg=row(jnp.tile(k_norm[i], reps)), attn_w=attn_w, kv_w=kv_w, conv_ch=conv_ch)
            q_p, k_p, v_p, glu_p = proj(xp)
            q_s, k_s, v_s, glu_s = proj(xs)
            attn_p = _attn_prompt(sinks[i], q_p, k_p, v_p, batch=batch, seq=seq, n_kv=n_kv)
            cache_t = lambda a: jnp.transpose(a, (0, 2, 3, 1)).reshape(n_seq, kv_w, WINDOW)
            new_t = lambda a: jnp.transpose(a.reshape(-1, WINDOW, kv_w), (0, 2, 1))
            attn_s, nk_t, nv_t = _attn_sample(sinks[i], q_s, new_t(k_s), new_t(v_s), cache_t(cache_k[i]),
                                              cache_t(cache_v[i]), t_new=t_new, n_kv=n_kv)
            uncache_t = lambda a: jnp.transpose(a.reshape(n_seq, n_kv, HEAD_DIM, WINDOW), (0, 3, 1, 2))
            ln = (row(conv_b[i]), row(conv_ln_g[i]), row(conv_ln_b[i]))
            conv_p = _conv_prompt(glu_p, conv_w[i], *ln, batch=batch, seq=seq)
            conv_s, nc_t = _conv_sample(glu_s, jnp.transpose(state_conv[i], (1, 0, 2)), conv_w[i], *ln, t_new=t_new)
            nc_s = jnp.transpose(nc_t, (1, 0, 2))
            xp = _out_proj(xp, attn_p, conv_p, w_out, slab=i)
            xs = _out_proj(xs, attn_s, conv_s, w_out, slab=i)
            last = lambda a, n: a.reshape(batch, seq, -1)[:, -n:]
            outs.setdefault("kp", []).append(last(k_p, WINDOW).reshape(batch, WINDOW, n_kv, HEAD_DIM))
            outs.setdefault("vp", []).append(last(v_p, WINDOW).reshape(batch, WINDOW, n_kv, HEAD_DIM))
            outs.setdefault("cp", []).append(last(glu_p, conv_w.shape[1] - 1))
            outs.setdefault("ks", []).append(uncache_t(nk_t))
            outs.setdefault("vs", []).append(uncache_t(nv_t))
            outs.setdefault("cs", []).append(nc_s)
        else:
            args = (row(norm_mix[layer]), pool_w, row(pool_scale[i]))
            n_past = state_pool.shape[2]
            xp, tail = _pool_prompt(xp, *args, slab=i, batch=batch, seq=seq)
            xs, np_t = _pool_sample(xs, jnp.transpose(state_pool[i], (1, 0, 2)), *args, slab=i, t_new=t_new)
            np_s = jnp.transpose(np_t, (1, 0, 2))
            outs.setdefault("pp", []).append(tail[:, -n_past:])
            outs.setdefault("ps", []).append(np_s)
        g_ffn = row(norm_ffn[layer])
        xs, wg16, wu16, wd16 = _ffn(xs, g_ffn, w_gate, w_up, w_down, layer=layer, emit_bf16=True)
        xp = _ffn(xp, g_ffn, wg16, wu16, wd16, tf=512)
    st = lambda name: jnp.stack(outs[name])
    return (xp.reshape(batch, seq, d), xs.reshape(n_seq, t_new, d),
            st("kp"), st("vp"), st("cp"), st("pp"), st("ks"), st("vs"), st("cs"), st("ps"))
```

```python
import functools

import jax
import jax.numpy as jnp
from jax import lax
from jax.experimental import pallas as pl
from jax.experimental.pallas import tpu as pltpu

F32 = jnp.float32
BF16 = jnp.bfloat16

HEAD_DIM = 64
WINDOW = 128
PAST_LEN = 8192
POOL_WINDOWS = (2, 4, 8, 16)
RMS_EPS = 1e-6
LN_EPS = 1e-5
NEG = -1e30

LANES = 128
SUBLANES = 8
MXU_DIM = 256
VMEM_LIMIT = 56 * 1024 * 1024


def _params(*semantics):
    return pltpu.CompilerParams(dimension_semantics=semantics, vmem_limit_bytes=VMEM_LIMIT)


def _rms(x, g):
    ms = jnp.mean(x * x, axis=-1, keepdims=True)
    return x * lax.rsqrt(ms + RMS_EPS) * g


def _resident(shape):
    zeros = (0,) * len(shape)
    return pl.BlockSpec(shape, lambda *_: zeros, pipeline_mode=pl.Buffered(1))


def _resident_slab(shape, slab):
    index = (slab,) + (0,) * (len(shape) - 1)
    return pl.BlockSpec((None,) + tuple(shape[1:]), lambda *_: index, pipeline_mode=pl.Buffered(1))


def _split_rows(tm, width, tiles_a, single_buffer_b=False):
    spec_a = pl.BlockSpec((tm, width), lambda i: (jnp.minimum(i, tiles_a - 1), 0))
    mode = dict(pipeline_mode=pl.Buffered(1)) if single_buffer_b else {}
    spec_b = pl.BlockSpec((tm, width), lambda i: (jnp.maximum(i - tiles_a, 0), 0), **mode)
    return [spec_a, spec_b]


def _pick_rows(a_ref, b_ref, tiles_a):
    return jnp.where(pl.program_id(0) < tiles_a, a_ref[...], b_ref[...])


def _in_proj_kernel(xa_ref, xb_ref, g_ref, w_ref, qg_ref, kg_ref, q_ref, k_ref, v_ref, glu_ref, *, tiles_a, attn_w, kv_w, conv_ch):
    h = _rms(_pick_rows(xa_ref, xb_ref, tiles_a), g_ref[...]).astype(BF16)
    row = lax.broadcasted_iota(jnp.int32, (MXU_DIM, MXU_DIM), 0) // HEAD_DIM
    col = lax.broadcasted_iota(jnp.int32, (MXU_DIM, MXU_DIM), 1) // HEAD_DIM
    pmat = jnp.where(row == col, 1.0 / HEAD_DIM, 0.0).astype(BF16)

    def proj(c0):
        return jnp.dot(h, w_ref[:, c0:c0 + MXU_DIM].astype(BF16), preferred_element_type=F32)

    def head_rms(z):
        sq = z * z
        hi = sq.astype(BF16)
        lo = (sq - hi.astype(F32)).astype(BF16)
        ms = jnp.dot(hi, pmat, preferred_element_type=F32) + jnp.dot(lo, pmat, preferred_element_type=F32)
        return z * lax.rsqrt(ms + RMS_EPS)

    scale = HEAD_DIM ** -0.5

    def put_q(c0, z):
        q_ref[:, c0:c0 + MXU_DIM] = (head_rms(z) * qg_ref[...] * scale).astype(q_ref.dtype)

    def put_k(c0, z):
        k_ref[:, c0:c0 + MXU_DIM] = head_rms(z) * kg_ref[...]

    def put_v(c0, z):
        v_ref[:, c0:c0 + MXU_DIM] = z

    def put_glu(c0, a, gate):
        glu_ref[:, c0:c0 + MXU_DIM] = a * jax.nn.sigmoid(gate)

    units = [((c0,), functools.partial(put_q, c0)) for c0 in range(0, attn_w, MXU_DIM)]
    units += [((attn_w + c0,), functools.partial(put_k, c0)) for c0 in range(0, kv_w, MXU_DIM)]
    units += [((attn_w + kv_w + c0,), functools.partial(put_v, c0)) for c0 in range(0, kv_w, MXU_DIM)]
    units += [((attn_w + 2 * kv_w + c0, attn_w + 2 * kv_w + conv_ch + c0), functools.partial(put_glu, c0))
              for c0 in range(0, conv_ch, MXU_DIM)]
    pending = None
    for cols, epilogue in units:
        zs = [proj(c0) for c0 in cols]
        if pending is not None:
            pending[0](*pending[1])
        pending = (epilogue, zs)
    pending[0](*pending[1])


def _in_proj(xa, xb, g, w, qg, kg, *, slab, attn_w, kv_w, conv_ch, tm=512):
    d = xa.shape[1]
    n = xa.shape[0] + xb.shape[0]
    tiles_a = xa.shape[0] // tm
    kern = functools.partial(_in_proj_kernel, tiles_a=tiles_a, attn_w=attn_w, kv_w=kv_w, conv_ch=conv_ch)
    rows = lambda width: pl.BlockSpec((tm, width), lambda i: (i, 0))
    return pl.pallas_call(
        kern,
        grid=(n // tm,),
        in_specs=_split_rows(tm, d, tiles_a, single_buffer_b=True) + [
            _resident((1, d)), _resident_slab(w.shape, slab), _resident((1, MXU_DIM)), _resident((1, MXU_DIM))],
        out_specs=[rows(attn_w), rows(kv_w), rows(kv_w), rows(conv_ch)],
        out_shape=[
            jax.ShapeDtypeStruct((n, attn_w), BF16),
            jax.ShapeDtypeStruct((n, kv_w), F32),
            jax.ShapeDtypeStruct((n, kv_w), F32),
            jax.ShapeDtypeStruct((n, conv_ch), F32),
        ],
        compiler_params=_params("arbitrary"),
        name="in_proj",
    )(xa, xb, g, w, qg, kg)


def _pair_blocks(chunk, rolled, odd):
    lane = lax.broadcasted_iota(jnp.int32, chunk.shape, 1)
    low = lane < HEAD_DIM
    if odd:
        top = jnp.where(low, rolled, 0.0)
        bot = jnp.where(low, 0.0, chunk)
    else:
        top = jnp.where(low, chunk, 0.0)
        bot = jnp.where(low, 0.0, rolled)
    return jnp.concatenate([top, bot], axis=0).astype(BF16)


def _softmax_pv(s, ok, sink_a, sink_b, vb_ones):
    half = s.shape[1] // 2
    s = jnp.where(ok, s, NEG)
    sa, sb = s[:, :half], s[:, half:]
    ma = jnp.maximum(jnp.max(sa, axis=-1, keepdims=True), sink_a)
    mb = jnp.maximum(jnp.max(sb, axis=-1, keepdims=True), sink_b)
    p = jnp.concatenate([jnp.exp(sa - ma), jnp.exp(sb - mb)], axis=1).astype(BF16)
    r = jnp.dot(p, vb_ones, preferred_element_type=F32)
    lane = lax.broadcasted_iota(jnp.int32, (s.shape[0], LANES), 1)
    sink_term = jnp.where(lane < HEAD_DIM, jnp.exp(sink_a - ma), jnp.exp(sink_b - mb))
    return r[:, :LANES] / (r[:, LANES:] + sink_term)


def _attend(q2, kb, vb, ok, sinks_ref, h0, rows):
    s = lax.dot_general(q2, kb, (((1,), (1,)), ((), ())), preferred_element_type=F32)
    row = lax.broadcasted_iota(jnp.int32, (2 * rows, 1), 0)
    first = row < rows
    sink_a = jnp.where(first, sinks_ref[h0], sinks_ref[h0 + 2])
    sink_b = jnp.where(first, sinks_ref[h0 + 1], sinks_ref[h0 + 3])
    return _softmax_pv(s, ok, sink_a, sink_b, vb)


def _attn_prompt_kernel(sinks_ref, q_ref, kp_ref, kc_ref, vp_ref, vc_ref, o_ref, *, n_kv):
    blk = kp_ref.shape[0]
    i = pl.program_id(1)
    shape = (2 * blk, 4 * blk)
    r = lax.broadcasted_iota(jnp.int32, shape, 0) % blk
    c = lax.broadcasted_iota(jnp.int32, shape, 1) % (2 * blk)
    band = jnp.logical_and(c > r, c <= r + blk)
    band_first = jnp.logical_and(band, jnp.logical_or(c >= blk, i > 0))
    group = (q_ref.shape[1] // HEAD_DIM) // n_kv
    key_row = lax.broadcasted_iota(jnp.int32, (4 * blk, LANES), 0)
    key_lane = lax.broadcasted_iota(jnp.int32, (4 * blk, LANES), 1)
    ones_blocks = jnp.where((key_row < 2 * blk) == (key_lane < HEAD_DIM), 1.0, 0.0).astype(BF16)
    for sub in range(q_ref.shape[0] // blk):
        rows = slice(sub * blk, (sub + 1) * blk)
        if sub == 0:
            k_all = jnp.concatenate([kp_ref[...], kc_ref[rows, :]], axis=0)
            v_all = jnp.concatenate([vp_ref[...], vc_ref[rows, :]], axis=0)
        else:
            k_all = kc_ref[(sub - 1) * blk:(sub + 1) * blk, :]
            v_all = vc_ref[(sub - 1) * blk:(sub + 1) * blk, :]
        ok = band if sub else band_first
        for cch in range(n_kv // 2):
            kc = k_all[:, cch * LANES:(cch + 1) * LANES]
            vc = v_all[:, cch * LANES:(cch + 1) * LANES]
            kr = pltpu.roll(kc, HEAD_DIM, axis=1)
            vr = pltpu.roll(vc, HEAD_DIM, axis=1)
            for odd in range(2):
                j = 2 * cch + odd
                kb = _pair_blocks(kc, kr, odd)
                vb = jnp.concatenate([_pair_blocks(vc, vr, odd), ones_blocks], axis=1)
                h0 = group * j
                c0 = h0 * HEAD_DIM
                q2 = jnp.concatenate([q_ref[rows, c0:c0 + LANES], q_ref[rows, c0 + LANES:c0 + 2 * LANES]], axis=0)
                o = _attend(q2, kb, vb, ok, sinks_ref, h0, blk).astype(o_ref.dtype)
                o_ref[rows, c0:c0 + LANES] = o[:blk]
                o_ref[rows, c0 + LANES:c0 + 2 * LANES] = o[blk:]


def _attn_prompt(sinks, q, k, v, *, batch, seq, n_kv, blocks_per_step=8):
    blk = WINDOW
    tq = blocks_per_step * blk
    nb = seq // tq
    aw, kw = q.shape[1], k.shape[1]
    cur = lambda w: pl.BlockSpec((tq, w), lambda b, i: (b * nb + i, 0))
    prev = lambda w: pl.BlockSpec(
        (blk, w), lambda b, i: ((b * nb + i) * blocks_per_step - jnp.minimum(i, 1), 0))
    return pl.pallas_call(
        functools.partial(_attn_prompt_kernel, n_kv=n_kv),
        grid=(batch, nb),
        in_specs=[pl.BlockSpec(memory_space=pltpu.SMEM), cur(aw), prev(kw), cur(kw), prev(kw), cur(kw)],
        out_specs=cur(aw),
        out_shape=jax.ShapeDtypeStruct((batch * seq, aw), BF16),
        compiler_params=_params("parallel", "arbitrary"),
        name="attn_prompt",
    )(sinks, q, k, k, v, v)


def _attn_sample_kernel(sinks_ref, q_ref, kn_ref, vn_ref, ck_ref, cv_ref, o_ref, nk_ref, nv_ref, qf_ref, of_ref, *, n_kv, t_new):
    g_seqs, kw, win = ck_ref.shape
    n_heads = q_ref.shape[1] // HEAD_DIM
    group = n_heads // n_kv
    n_rows = n_heads * t_new
    qf_ref[...] = q_ref[...].astype(F32)
    t = lax.broadcasted_iota(jnp.int32, (n_rows, 2 * win), 0) % t_new
    c = lax.broadcasted_iota(jnp.int32, (n_rows, 2 * win), 1)
    ok = jnp.logical_or(jnp.logical_and(c < win, c > t),
                        jnp.logical_and(c >= 2 * win - t_new, c - (2 * win - t_new) <= t))
    sink = jnp.concatenate([jnp.full((t_new, 1), sinks_ref[h], F32) for h in range(n_heads)], axis=0)
    low = lax.broadcasted_iota(jnp.int32, (t_new, LANES), 1) < HEAD_DIM
    zero_tile = jnp.zeros((t_new, LANES), F32)
    keep_old = lax.broadcasted_iota(jnp.int32, (kw, win), 1) < win - t_new
    ones_rows = jnp.ones((LANES, 2 * win), BF16)
    kn_all = kn_ref[0]
    vn_all = vn_ref[0]

    def one_seq(g, carry):
        r0 = pl.multiple_of(g * t_new, t_new)
        k_old = ck_ref[g]
        v_old = cv_ref[g]
        shift = win - t_new * (g + 1)
        k_new = pltpu.roll(kn_all, shift, axis=1)
        v_new = pltpu.roll(vn_all, shift, axis=1)
        nk_ref[g] = jnp.where(keep_old, pltpu.roll(k_old, win - t_new, axis=1), k_new)
        nv_ref[g] = jnp.where(keep_old, pltpu.roll(v_old, win - t_new, axis=1), v_new)
        keys = jnp.concatenate([k_old, k_new], axis=1).astype(BF16)
        vals = jnp.concatenate([jnp.concatenate([v_old, v_new], axis=1).astype(BF16), ones_rows], axis=0)
        q_rows = []
        for h in range(n_heads):
            j = h // group
            src = qf_ref[pl.ds(r0, t_new), (h // 2) * LANES:(h // 2 + 1) * LANES]
            if h % 2 != j % 2:
                src = pltpu.roll(src, HEAD_DIM, axis=1)
            half = jnp.where(low, src, 0.0) if j % 2 == 0 else jnp.where(low, 0.0, src)
            tiles = [zero_tile] * (kw // LANES)
            tiles[j // 2] = half
            q_rows.append(jnp.concatenate(tiles, axis=1))
        qb = jnp.concatenate(q_rows, axis=0).astype(BF16)
        s = jnp.dot(qb, keys, preferred_element_type=F32)
        s = jnp.where(ok, s, NEG)
        m = jnp.maximum(jnp.max(s, axis=-1, keepdims=True), sink)
        p = jnp.exp(s - m).astype(BF16)
        r = lax.dot_general(p, vals, (((1,), (1,)), ((), ())), preferred_element_type=F32)
        denom = r[:, kw:] + jnp.exp(sink - m)
        o = r[:, :kw] / jnp.concatenate([denom] * (kw // LANES), axis=1)
        for pair in range(n_heads // 2):
            j = (2 * pair) // group
            lanes = slice((j // 2) * LANES, (j // 2 + 1) * LANES)
            a = o[2 * pair * t_new:(2 * pair + 1) * t_new, lanes]
            b = o[(2 * pair + 1) * t_new:(2 * pair + 2) * t_new, lanes]
            if j % 2 == 0:
                b = pltpu.roll(b, HEAD_DIM, axis=1)
            else:
                a = pltpu.roll(a, HEAD_DIM, axis=1)
            of_ref[pl.ds(r0, t_new), pair * LANES:(pair + 1) * LANES] = jnp.where(low, a, b)
        return carry

    lax.fori_loop(0, g_seqs, one_seq, 0, unroll=4)
    o_ref[...] = of_ref[...].astype(o_ref.dtype)


def _attn_sample(sinks, q, k_t, v_t, cache_k_t, cache_v_t, *, row0, t_new, n_kv):
    n_seq, kw, win = cache_k_t.shape
    g_seqs = win // t_new
    aw = q.shape[1]
    assert k_t.shape == (n_seq // g_seqs, kw, win)
    tile0 = row0 // (g_seqs * t_new)
    tok = pl.BlockSpec((g_seqs * t_new, aw), lambda i: (i, 0))
    q_tok = pl.BlockSpec((g_seqs * t_new, aw), lambda i: (tile0 + i, 0))
    new = pl.BlockSpec((1, kw, win), lambda i: (i, 0, 0))
    cache = pl.BlockSpec((g_seqs, kw, win), lambda i: (i, 0, 0))
    return pl.pallas_call(
        functools.partial(_attn_sample_kernel, n_kv=n_kv, t_new=t_new),
        grid=(n_seq // g_seqs,),
        in_specs=[pl.BlockSpec(memory_space=pltpu.SMEM), q_tok, new, new, cache, cache],
        out_specs=[tok, cache, cache],
        out_shape=[
            jax.ShapeDtypeStruct((n_seq * t_new, aw), BF16),
            jax.ShapeDtypeStruct(cache_k_t.shape, F32),
            jax.ShapeDtypeStruct(cache_v_t.shape, F32),
        ],
        scratch_shapes=[pltpu.VMEM((g_seqs * t_new, aw), F32), pltpu.VMEM((g_seqs * t_new, aw), F32)],
        compiler_params=_params("parallel"),
        name="attn_sample",
    )(sinks, q, k_t, v_t, cache_k_t, cache_v_t)


def _ln_silu(y, g, b):
    yc = y - jnp.mean(y, axis=-1, keepdims=True)
    var = jnp.mean(yc * yc, axis=-1, keepdims=True)
    return jax.nn.silu(yc * lax.rsqrt(var + LN_EPS) * g + b)


def _conv_prompt_kernel(cur_ref, halo_ref, w_ref, b_ref, g_ref, beta_ref, o_ref, xc_ref, y_ref, *, width, rows_per, lanes_per):
    tc, ch = cur_ref.shape
    halo = halo_ref.shape[0]
    i = pl.program_id(1)
    xc_ref[:halo, :] = jnp.where(i == 0, 0.0, halo_ref[...])
    xc_ref[halo:, :] = cur_ref[...]
    base = halo - (width - 1)

    def col_chunk(cc, carry):
        l0 = pl.multiple_of(cc * lanes_per, lanes_per)
        cols = pl.ds(l0, lanes_per)
        for rr in range(tc // rows_per):
            acc = jnp.broadcast_to(b_ref[:, cols], (rows_per, lanes_per))
            for phase in range(SUBLANES):
                n_win = rows_per + (SUBLANES if phase else 0)
                part = None
                for j in range(width):
                    if (base + j) % SUBLANES != phase:
                        continue
                    start = rr * rows_per + ((base + j) // SUBLANES) * SUBLANES
                    term = w_ref[pl.ds(j, 1), cols] * xc_ref[pl.ds(start, n_win), cols]
                    part = term if part is None else part + term
                if part is None:
                    continue
                if phase:
                    part = pltpu.roll(part, n_win - phase, axis=0)
                acc = acc + part[:rows_per]
            y_ref[pl.ds(rr * rows_per, rows_per), cols] = acc
        return carry

    lax.fori_loop(0, ch // lanes_per, col_chunk, 0)
    o_ref[...] = _ln_silu(y_ref[...], g_ref[...], beta_ref[...]).astype(o_ref.dtype)


def _conv_prompt(glu, w, b, g, beta, *, batch, seq, tc=512, halo=32):
    ch = glu.shape[1]
    width = w.shape[0]
    nt = seq // tc
    per = tc // halo
    cur = pl.BlockSpec((tc, ch), lambda bi, i: (bi * nt + i, 0))
    prev = pl.BlockSpec((halo, ch), lambda bi, i: (jnp.maximum((bi * nt + i) * per - 1, 0), 0))
    kern = functools.partial(_conv_prompt_kernel, width=width, rows_per=128, lanes_per=LANES)
    return pl.pallas_call(
        kern,
        grid=(batch, nt),
        in_specs=[cur, prev, _resident(w.shape), _resident((1, ch)), _resident((1, ch)), _resident((1, ch))],
        out_specs=cur,
        out_shape=jax.ShapeDtypeStruct((batch * seq, ch), BF16),
        scratch_shapes=[pltpu.VMEM((halo + tc, ch), F32), pltpu.VMEM((tc, ch), F32)],
        compiler_params=_params("parallel", "arbitrary"),
        name="conv_prompt",
    )(glu, glu, w, b, g, beta)


def _conv_sample_kernel(new_ref, st_ref, w_ref, b_ref, g_ref, beta_ref, o_ref, ns_ref, y_ref, tok_ref, *, t_new):
    n_past, n_seq, wide = st_ref.shape
    lanes = tok_ref.shape[1]
    n_sub = wide // lanes
    width = w_ref.shape[0]
    c = pl.program_id(0)
    for k in range(n_sub):
        cols = slice(k * lanes, (k + 1) * lanes)
        tok_ref[...] = new_ref[:, cols]
        new = [tok_ref[pl.ds(t, n_seq, stride=t_new), :] for t in range(t_new)]
        row = lambda r: st_ref[r, :, cols] if r < n_past else new[r - n_past]
        for t in range(t_new):
            acc = jnp.broadcast_to(b_ref[:, cols], (n_seq, lanes))
            for j in range(width):
                acc = acc + w_ref[pl.ds(j, 1), cols] * row(t + j)
            y_ref[c * n_sub + k, pl.ds(t, n_seq, stride=t_new), :] = acc
        for r in range(n_past):
            ns_ref[r, :, cols] = row(r + t_new)

    @pl.when(c == pl.num_programs(0) - 1)
    def _():
        n_chunks = y_ref.shape[0]
        ch = n_chunks * lanes
        mean = jnp.sum(sum(y_ref[k] for k in range(n_chunks)), axis=-1, keepdims=True) / ch
        var = jnp.sum(sum((y_ref[k] - mean) ** 2 for k in range(n_chunks)), axis=-1, keepdims=True) / ch
        inv = lax.rsqrt(var + LN_EPS)
        for k in range(n_chunks):
            cols = slice(k * lanes, (k + 1) * lanes)
            z = (y_ref[k] - mean) * inv * g_ref[:, cols] + beta_ref[:, cols]
            o_ref[:, cols] = jax.nn.silu(z).astype(o_ref.dtype)


def _conv_sample(glu, state_t, w, b, g, beta, *, row0, t_new, wide=2 * LANES):
    n_past, n_seq, ch = state_t.shape
    n_tok = n_seq * t_new
    assert w.shape[0] == n_past + 1
    chunk = lambda rows: pl.BlockSpec((rows, wide), lambda c: (0, c))
    slabs = pl.BlockSpec((n_past, n_seq, wide), lambda c: (0, 0, c))
    return pl.pallas_call(
        functools.partial(_conv_sample_kernel, t_new=t_new),
        grid=(ch // wide,),
        in_specs=[pl.BlockSpec((n_tok, wide), lambda c: (row0 // n_tok, c)), slabs, chunk(w.shape[0]), chunk(1),
                  _resident((1, ch)), _resident((1, ch))],
        out_specs=[pl.BlockSpec((n_tok, ch), lambda c: (0, 0)), slabs],
        out_shape=[jax.ShapeDtypeStruct((n_tok, ch), BF16), jax.ShapeDtypeStruct(state_t.shape, F32)],
        scratch_shapes=[pltpu.VMEM((ch // LANES, n_tok, LANES), F32), pltpu.VMEM((n_tok, LANES), F32)],
        compiler_params=_params("arbitrary"),
        name="conv_sample",
    )(glu, state_t, w, b, g, beta)


def _out_proj_kernel(xa_ref, xb_ref, aa_ref, ab_ref, ca_ref, cb_ref, w_ref, o_ref, *, tiles_a):
    aw = aa_ref.shape[1]
    attn = _pick_rows(aa_ref, ab_ref, tiles_a)
    conv = _pick_rows(ca_ref, cb_ref, tiles_a)
    o_ref[...] = (_pick_rows(xa_ref, xb_ref, tiles_a)
                  + jnp.dot(attn, w_ref[:aw, :].astype(BF16), preferred_element_type=F32)
                  + jnp.dot(conv, w_ref[aw:, :].astype(BF16), preferred_element_type=F32))


def _out_proj(x, attn, conv, w, *, slab, tm=512):
    d = x[0].shape[1]
    n = x[0].shape[0] + x[1].shape[0]
    tiles_a = x[0].shape[0] // tm
    specs = [spec for pair in (x, attn, conv) for spec in _split_rows(tm, pair[0].shape[1], tiles_a)]
    return pl.pallas_call(
        functools.partial(_out_proj_kernel, tiles_a=tiles_a),
        grid=(n // tm,),
        in_specs=specs + [_resident_slab(w.shape, slab)],
        out_specs=pl.BlockSpec((tm, d), lambda i: (i, 0)),
        out_shape=jax.ShapeDtypeStruct((n, d), F32),
        compiler_params=_params("arbitrary"),
        name="out_proj",
    )(*x, *attn, *conv, w)


def _ffn_kernel(x_ref, g_ref, wg_ref, wu_ref, wd_ref, o_ref, *rest):
    h_ref = rest[-1]

    @pl.when(pl.program_id(1) == 0)
    def _():
        x = x_ref[...]
        h_ref[...] = _rms(x, g_ref[...]).astype(BF16)
        o_ref[...] = x

    weights = [w_ref[...].astype(BF16) for w_ref in (wg_ref, wu_ref, wd_ref)]
    for w_out, w in zip(rest[:-1], weights):
        w_out[...] = w
    wg, wu, wd = weights
    h = h_ref[...]
    a = jnp.dot(h, wg, preferred_element_type=F32)
    u = jnp.dot(h, wu, preferred_element_type=F32)
    act = (jax.nn.silu(a) * u).astype(BF16)
    o_ref[...] += jnp.dot(act, wd, preferred_element_type=F32)


def _ffn(x, g, wg, wu, wd, *, layer=None, emit_bf16=False, row0=0, n_rows=None, tm=1024, tf=256):
    d = x.shape[1]
    n = x.shape[0] if n_rows is None else n_rows
    tile0 = row0 // tm
    dff = wg.shape[-1]
    nt, nf = n // tm, dff // tf
    if layer is None:
        w_in_specs = [pl.BlockSpec((d, tf), lambda i, f: (0, f)), pl.BlockSpec((d, tf), lambda i, f: (0, f)),
                      pl.BlockSpec((tf, d), lambda i, f: (f, 0))]
    else:
        w_in_specs = [pl.BlockSpec((None, d, tf), lambda i, f: (layer, 0, f)),
                      pl.BlockSpec((None, d, tf), lambda i, f: (layer, 0, f)),
                      pl.BlockSpec((None, tf, d), lambda i, f: (layer, f, 0))]
    rows = pl.BlockSpec((tm, d), lambda i, f: (i, 0))
    out_specs, out_shape = [rows], [jax.ShapeDtypeStruct((n, d), F32)]
    if emit_bf16:
        assert nt == 1, "one row tile, so every bf16 weight tile is written exactly once"
        out_specs += [pl.BlockSpec((d, tf), lambda i, f: (0, f)), pl.BlockSpec((d, tf), lambda i, f: (0, f)),
                      pl.BlockSpec((tf, d), lambda i, f: (f, 0))]
        out_shape += [jax.ShapeDtypeStruct((d, dff), BF16), jax.ShapeDtypeStruct((d, dff), BF16),
                      jax.ShapeDtypeStruct((dff, d), BF16)]
    mode = dict(pipeline_mode=pl.Buffered(1)) if nt == 1 else {}
    x_spec = pl.BlockSpec((tm, d), lambda i, f: (tile0 + i, 0), **mode)
    outs = pl.pallas_call(
        _ffn_kernel,
        grid=(nt, nf),
        in_specs=[x_spec, _resident((1, d))] + w_in_specs,
        out_specs=out_specs,
        out_shape=out_shape,
        scratch_shapes=[pltpu.VMEM((tm, d), BF16)],
        compiler_params=_params("parallel", "arbitrary"),
        name="ffn",
    )(x, g, wg, wu, wd)
    return outs if emit_bf16 else outs[0]


def _pool_prompt_kernel(cur_ref, halo_ref, g_ref, pw_ref, ps_ref, o_ref, tail_ref, xp_ref):
    tp, d = cur_ref.shape
    halo = halo_ref.shape[0]
    i = pl.program_id(1)
    gain = g_ref[...]
    xp_ref[:halo, :] = jnp.where(i == 0, 0.0, _rms(halo_ref[...], gain))
    xp_ref[halo:, :] = _rms(cur_ref[...], gain)
    tail_ref[0] = xp_ref[tp:, :]
    pg = d // len(POOL_WINDOWS)
    pos = i * tp + lax.broadcasted_iota(jnp.int32, (tp, 1), 0)
    for gi, w in enumerate(POOL_WINDOWS):
        cols = slice(gi * pg, (gi + 1) * pg)
        ext = xp_ref[:, cols]
        h = ext[halo:]
        span = 1
        while span < w:
            ext = ext + pltpu.roll(ext, span, axis=0)
            span *= 2
        inv_count = 1.0 / jnp.minimum(pos + 1, w).astype(F32)
        dpool = (ext[halo:] * inv_count - h).astype(BF16)
        y = jnp.dot(dpool, pw_ref[gi].astype(BF16), preferred_element_type=F32)
        o_ref[:, cols] = cur_ref[:, cols] + y * ps_ref[:, cols]


def _pool_prompt(x, g, pw, ps, *, slab, batch, seq, tp=1024, halo=16):
    assert all(w & (w - 1) == 0 and w <= halo for w in POOL_WINDOWS)
    d = x.shape[1]
    nt = seq // tp
    per = tp // halo
    cur = pl.BlockSpec((tp, d), lambda bi, i: (bi * nt + i, 0))
    prev = pl.BlockSpec((halo, d), lambda bi, i: (jnp.maximum((bi * nt + i) * per - 1, 0), 0))
    return pl.pallas_call(
        _pool_prompt_kernel,
        grid=(batch, nt),
        in_specs=[cur, prev, _resident((1, d)), _resident_slab(pw.shape, slab), _resident((1, d))],
        out_specs=[cur, pl.BlockSpec((1, halo, d), lambda bi, i: (bi, 0, 0))],
        out_shape=[jax.ShapeDtypeStruct((batch * seq, d), F32), jax.ShapeDtypeStruct((batch, halo, d), F32)],
        scratch_shapes=[pltpu.VMEM((halo + tp, d), F32)],
        compiler_params=_params("parallel", "arbitrary"),
        name="pool_prompt",
    )(x, x, g, pw, ps)


def _pool_sample_kernel(x_ref, st_ref, g_ref, pw_ref, ps_ref, o_ref, ns_ref, inv_ref, y_ref, tok_ref, *, t_new, start_pos):
    n_past, n_seq, pg = st_ref.shape
    lanes = tok_ref.shape[1]
    phase, gi = pl.program_id(0), pl.program_id(1)
    n_tok = x_ref.shape[0]
    token_rows = lambda t: pl.ds(t, n_seq, stride=t_new)
    group_cols = lambda k: pl.ds(pl.multiple_of(gi * pg + k * lanes, lanes), lanes)

    @pl.when(jnp.logical_and(phase == 0, gi == 0))
    def _():
        x = x_ref[...]
        inv = lax.rsqrt(jnp.mean(x * x, axis=-1, keepdims=True) + RMS_EPS)
        inv_ref[...] = jnp.broadcast_to(inv, (n_tok, lanes))

    @pl.when(phase == 0)
    def _():
        chunks = []
        for k in range(pg // lanes):
            cols = slice(k * lanes, (k + 1) * lanes)
            tok_ref[...] = x_ref[:, group_cols(k)] * inv_ref[...]
            gain = g_ref[:, group_cols(k)]
            new = [tok_ref[token_rows(t), :] * gain for t in range(t_new)]
            row = lambda r: st_ref[n_past + r, :, cols] if r < 0 else new[r]
            for r in range(n_past):
                ns_ref[r, :, cols] = row(r + t_new - n_past)
            sums = {1: {r: row(r) for r in range(-n_past, t_new)}}
            span = 1
            while span < max(POOL_WINDOWS):
                prev = sums[span]
                sums[2 * span] = {r: prev[r] + prev[r - span] for r in prev if r - span in prev}
                span *= 2
            diffs = []
            for t in range(t_new):
                pooled = None
                for i, w in enumerate(POOL_WINDOWS):
                    cand = sums[w][t] * (1.0 / min(start_pos + t + 1, w))
                    pooled = cand if pooled is None else jnp.where(gi == i, cand, pooled)
                diffs.append(pooled - new[t])
            chunks.append(jnp.concatenate(diffs, axis=0))
        dp = jnp.concatenate(chunks, axis=1).astype(BF16)
        cols = pl.ds(pl.multiple_of(gi * pg, pg), pg)
        y_ref[:, cols] = jnp.dot(dp, pw_ref[gi].astype(BF16), preferred_element_type=F32) * ps_ref[:, cols]

    @pl.when(phase == 1)
    def _():
        for k in range(pg // lanes):
            y = y_ref[:, group_cols(k)]
            tok_ref[...] = x_ref[:, group_cols(k)]
            for t in range(t_new):
                tok_ref[token_rows(t), :] = tok_ref[token_rows(t), :] + y[t * n_seq:(t + 1) * n_seq]
            o_ref[:, k * lanes:(k + 1) * lanes] = tok_ref[...]


def _pool_sample(x, state_t, g, pw, ps, *, slab, t_new):
    n_past, n_seq, d = state_t.shape
    n_tok = n_seq * t_new
    n_groups = len(POOL_WINDOWS)
    pg = d // n_groups
    assert all(w & (w - 1) == 0 and w <= n_past + 1 for w in POOL_WINDOWS)
    kern = functools.partial(_pool_sample_kernel, t_new=t_new, start_pos=PAST_LEN)
    slabs = pl.BlockSpec((n_past, n_seq, pg), lambda p, gi: (0, 0, gi * (1 - p) + (n_groups - 1) * p))
    return pl.pallas_call(
        kern,
        grid=(2, n_groups),
        in_specs=[_resident((n_tok, d)), slabs, _resident((1, d)), _resident_slab(pw.shape, slab), _resident((1, d))],
        out_specs=[pl.BlockSpec((n_tok, pg), lambda p, gi: (0, gi * p)), slabs],
        out_shape=[jax.ShapeDtypeStruct((n_tok, d), F32), jax.ShapeDtypeStruct(state_t.shape, F32)],
        scratch_shapes=[pltpu.VMEM((n_tok, LANES), F32), pltpu.VMEM((n_tok, d), F32), pltpu.VMEM((n_tok, LANES), F32)],
        compiler_params=_params("arbitrary", "arbitrary"),
        name="pool_sample",
    )(x, state_t, g, pw, ps)


def kernel(x_prompt, x_sample, cache_k, cache_v, state_conv, state_pool, norm_mix, w_in, q_norm, k_norm, sinks,
           conv_w, conv_b, conv_ln_g, conv_ln_b, w_out, pool_w, pool_scale, norm_ffn, w_gate, w_up, w_down):
    batch, seq, d = x_prompt.shape
    n_seq, t_new, _ = x_sample.shape
    n_kv = cache_k.shape[3]
    kv_w = n_kv * HEAD_DIM
    conv_ch = state_conv.shape[-1]
    attn_w = w_in.shape[-1] - 2 * kv_w - 2 * conv_ch
    n_prompt = batch * seq
    row = lambda a: a.reshape(1, -1)

    xp = x_prompt.reshape(n_prompt, d)
    xs = x_sample.reshape(n_seq * t_new, d)
    depth = norm_mix.shape[0]
    outs = {}
    for layer in range(depth):
        i = layer // 2
        if layer % 2 == 0:
            reps = MXU_DIM // HEAD_DIM
            q, k, v, glu = _in_proj(
                xp, xs, row(norm_mix[layer]), w_in, row(jnp.tile(q_norm[i], reps)), row(jnp.tile(k_norm[i], reps)),
                slab=i, attn_w=attn_w, kv_w=kv_w, conv_ch=conv_ch)
            attn_p = _attn_prompt(sinks[i], q, k, v, batch=batch, seq=seq, n_kv=n_kv)
            cache_t = lambda a: jnp.transpose(a, (0, 2, 3, 1)).reshape(n_seq, kv_w, WINDOW)
            new_t = lambda a: jnp.transpose(a[n_prompt:].reshape(-1, WINDOW, kv_w), (0, 2, 1))
            attn_s, nk_t, nv_t = _attn_sample(sinks[i], q, new_t(k), new_t(v), cache_t(cache_k[i]),
                                              cache_t(cache_v[i]), row0=n_prompt, t_new=t_new, n_kv=n_kv)
            uncache_t = lambda a: jnp.transpose(a.reshape(n_seq, n_kv, HEAD_DIM, WINDOW), (0, 3, 1, 2))
            ln = (row(conv_b[i]), row(conv_ln_g[i]), row(conv_ln_b[i]))
            conv_p = _conv_prompt(glu, conv_w[i], *ln, batch=batch, seq=seq)
            conv_s, nc_t = _conv_sample(glu, jnp.transpose(state_conv[i], (1, 0, 2)), conv_w[i], *ln,
                                        row0=n_prompt, t_new=t_new)
            nc_s = jnp.transpose(nc_t, (1, 0, 2))
            mixed = _out_proj((xp, xs), (attn_p, attn_s), (conv_p, conv_s), w_out, slab=i)
            ffn_in = ((mixed, 0, n_prompt), (mixed, n_prompt, n_seq * t_new))
            last = lambda a, n: a[:n_prompt].reshape(batch, seq, -1)[:, -n:]
            outs.setdefault("kp", []).append(last(k, WINDOW).reshape(batch, WINDOW, n_kv, HEAD_DIM))
            outs.setdefault("vp", []).append(last(v, WINDOW).reshape(batch, WINDOW, n_kv, HEAD_DIM))
            outs.setdefault("cp", []).append(last(glu, conv_w.shape[1] - 1))
            outs.setdefault("ks", []).append(uncache_t(nk_t))
            outs.setdefault("vs", []).append(uncache_t(nv_t))
            outs.setdefault("cs", []).append(nc_s)
        else:
            args = (row(norm_mix[layer]), pool_w, row(pool_scale[i]))
            n_past = state_pool.shape[2]
            xp, tail = _pool_prompt(xp, *args, slab=i, batch=batch, seq=seq)
            xs, np_t = _pool_sample(xs, jnp.transpose(state_pool[i], (1, 0, 2)), *args, slab=i, t_new=t_new)
            np_s = jnp.transpose(np_t, (1, 0, 2))
            ffn_in = ((xp, 0, n_prompt), (xs, 0, n_seq * t_new))
            outs.setdefault("pp", []).append(tail[:, -n_past:])
            outs.setdefault("ps", []).append(np_s)
        g_ffn = row(norm_ffn[layer])
        (rows_p, row0_p, n_p), (rows_s, row0_s, n_s) = ffn_in
        xs, wg16, wu16, wd16 = _ffn(rows_s, g_ffn, w_gate, w_up, w_down, layer=layer, emit_bf16=True,
                                    row0=row0_s, n_rows=n_s)
        xp = _ffn(rows_p, g_ffn, wg16, wu16, wd16, row0=row0_p, n_rows=n_p, tf=512)
    st = lambda name: jnp.stack(outs[name])
    return (xp.reshape(batch, seq, d), xs.reshape(n_seq, t_new, d),
            st("kp"), st("vp"), st("cp"), st("pp"), st("ks"), st("vs"), st("cs"), st("ps"))
```

```python
import functools

import jax
import jax.numpy as jnp
from jax import lax
from jax.experimental import pallas as pl
from jax.experimental.pallas import tpu as pltpu

F32 = jnp.float32
BF16 = jnp.bfloat16

HEAD_DIM = 64
WINDOW = 128
PAST_LEN = 8192
POOL_WINDOWS = (2, 4, 8, 16)
RMS_EPS = 1e-6
LN_EPS = 1e-5
NEG = -1e30

LANES = 128
SUBLANES = 8
MXU_DIM = 256
VMEM_LIMIT = 56 * 1024 * 1024


def _params(*semantics):
    return pltpu.CompilerParams(dimension_semantics=semantics, vmem_limit_bytes=VMEM_LIMIT)


def _rms(x, g):
    ms = jnp.mean(x * x, axis=-1, keepdims=True)
    return x * lax.rsqrt(ms + RMS_EPS) * g


def _resident(shape):
    zeros = (0,) * len(shape)
    return pl.BlockSpec(shape, lambda *_: zeros, pipeline_mode=pl.Buffered(1))


def _resident_slab(shape, slab):
    index = (slab,) + (0,) * (len(shape) - 1)
    return pl.BlockSpec((None,) + tuple(shape[1:]), lambda *_: index, pipeline_mode=pl.Buffered(1))


def _split_rows(tm, width, tiles_a, single_buffer_b=False):
    spec_a = pl.BlockSpec((tm, width), lambda i: (jnp.minimum(i, tiles_a - 1), 0))
    mode = dict(pipeline_mode=pl.Buffered(1)) if single_buffer_b else {}
    spec_b = pl.BlockSpec((tm, width), lambda i: (jnp.maximum(i - tiles_a, 0), 0), **mode)
    return [spec_a, spec_b]


def _pick_rows(a_ref, b_ref, tiles_a):
    return jnp.where(pl.program_id(0) < tiles_a, a_ref[...], b_ref[...])


def _in_proj_kernel(xa_ref, xb_ref, g_ref, w_ref, qg_ref, kg_ref, q_ref, k_ref, v_ref, glu_ref, *, tiles_a, attn_w, kv_w, conv_ch):
    h = _rms(_pick_rows(xa_ref, xb_ref, tiles_a), g_ref[...]).astype(BF16)
    row = lax.broadcasted_iota(jnp.int32, (MXU_DIM, MXU_DIM), 0) // HEAD_DIM
    col = lax.broadcasted_iota(jnp.int32, (MXU_DIM, MXU_DIM), 1) // HEAD_DIM
    pmat = jnp.where(row == col, 1.0 / HEAD_DIM, 0.0).astype(BF16)

    def proj(c0):
        return jnp.dot(h, w_ref[:, c0:c0 + MXU_DIM].astype(BF16), preferred_element_type=F32)

    def head_rms(z):
        sq = z * z
        hi = sq.astype(BF16)
        lo = (sq - hi.astype(F32)).astype(BF16)
        ms = jnp.dot(hi, pmat, preferred_element_type=F32) + jnp.dot(lo, pmat, preferred_element_type=F32)
        return z * lax.rsqrt(ms + RMS_EPS)

    scale = HEAD_DIM ** -0.5

    def put_q(c0, z):
        q_ref[:, c0:c0 + MXU_DIM] = (head_rms(z) * qg_ref[...] * scale).astype(q_ref.dtype)

    def put_k(c0, z):
        k_ref[:, c0:c0 + MXU_DIM] = head_rms(z) * kg_ref[...]

    def put_v(c0, z):
        v_ref[:, c0:c0 + MXU_DIM] = z

    def put_glu(c0, a, gate):
        glu_ref[:, c0:c0 + MXU_DIM] = a * jax.nn.sigmoid(gate)

    units = [((c0,), functools.partial(put_q, c0)) for c0 in range(0, attn_w, MXU_DIM)]
    units += [((attn_w + c0,), functools.partial(put_k, c0)) for c0 in range(0, kv_w, MXU_DIM)]
    units += [((attn_w + kv_w + c0,), functools.partial(put_v, c0)) for c0 in range(0, kv_w, MXU_DIM)]
    units += [((attn_w + 2 * kv_w + c0, attn_w + 2 * kv_w + conv_ch + c0), functools.partial(put_glu, c0))
              for c0 in range(0, conv_ch, MXU_DIM)]
    pending = None
    for cols, epilogue in units:
        zs = [proj(c0) for c0 in cols]
        if pending is not None:
            pending[0](*pending[1])
        pending = (epilogue, zs)
    pending[0](*pending[1])


def _in_proj(xa, xb, g, w, qg, kg, *, slab, attn_w, kv_w, conv_ch, tm=512):
    d = xa.shape[1]
    n = xa.shape[0] + xb.shape[0]
    tiles_a = xa.shape[0] // tm
    kern = functools.partial(_in_proj_kernel, tiles_a=tiles_a, attn_w=attn_w, kv_w=kv_w, conv_ch=conv_ch)
    rows = lambda width: pl.BlockSpec((tm, width), lambda i: (i, 0))
    return pl.pallas_call(
        kern,
        grid=(n // tm,),
        in_specs=_split_rows(tm, d, tiles_a, single_buffer_b=True) + [
            _resident((1, d)), _resident_slab(w.shape, slab), _resident((1, MXU_DIM)), _resident((1, MXU_DIM))],
        out_specs=[rows(attn_w), rows(kv_w), rows(kv_w), rows(conv_ch)],
        out_shape=[
            jax.ShapeDtypeStruct((n, attn_w), BF16),
            jax.ShapeDtypeStruct((n, kv_w), F32),
            jax.ShapeDtypeStruct((n, kv_w), F32),
            jax.ShapeDtypeStruct((n, conv_ch), F32),
        ],
        compiler_params=_params("arbitrary"),
        name="in_proj",
    )(xa, xb, g, w, qg, kg)


def _pair_blocks(chunk, rolled, odd):
    lane = lax.broadcasted_iota(jnp.int32, chunk.shape, 1)
    low = lane < HEAD_DIM
    if odd:
        top = jnp.where(low, rolled, 0.0)
        bot = jnp.where(low, 0.0, chunk)
    else:
        top = jnp.where(low, chunk, 0.0)
        bot = jnp.where(low, 0.0, rolled)
    return jnp.concatenate([top, bot], axis=0).astype(BF16)


def _softmax_pv(s, ok, sink_a, sink_b, vb_ones):
    half = s.shape[1] // 2
    s = jnp.where(ok, s, NEG)
    sa, sb = s[:, :half], s[:, half:]
    ma = jnp.maximum(jnp.max(sa, axis=-1, keepdims=True), sink_a)
    mb = jnp.maximum(jnp.max(sb, axis=-1, keepdims=True), sink_b)
    p = jnp.concatenate([jnp.exp(sa - ma), jnp.exp(sb - mb)], axis=1).astype(BF16)
    r = jnp.dot(p, vb_ones, preferred_element_type=F32)
    lane = lax.broadcasted_iota(jnp.int32, (s.shape[0], LANES), 1)
    sink_term = jnp.where(lane < HEAD_DIM, jnp.exp(sink_a - ma), jnp.exp(sink_b - mb))
    return r[:, :LANES] / (r[:, LANES:] + sink_term)


def _attend(q2, kb, vb, ok, sinks_ref, h0, rows):
    s = lax.dot_general(q2, kb, (((1,), (1,)), ((), ())), preferred_element_type=F32)
    row = lax.broadcasted_iota(jnp.int32, (2 * rows, 1), 0)
    first = row < rows
    sink_a = jnp.where(first, sinks_ref[h0], sinks_ref[h0 + 2])
    sink_b = jnp.where(first, sinks_ref[h0 + 1], sinks_ref[h0 + 3])
    return _softmax_pv(s, ok, sink_a, sink_b, vb)


def _attn_prompt_kernel(sinks_ref, q_ref, kp_ref, kc_ref, vp_ref, vc_ref, o_ref, *, n_kv):
    blk = kp_ref.shape[0]
    i = pl.program_id(1)
    shape = (2 * blk, 4 * blk)
    r = lax.broadcasted_iota(jnp.int32, shape, 0) % blk
    c = lax.broadcasted_iota(jnp.int32, shape, 1) % (2 * blk)
    band = jnp.logical_and(c > r, c <= r + blk)
    band_first = jnp.logical_and(band, jnp.logical_or(c >= blk, i > 0))
    group = (q_ref.shape[1] // HEAD_DIM) // n_kv
    key_row = lax.broadcasted_iota(jnp.int32, (4 * blk, LANES), 0)
    key_lane = lax.broadcasted_iota(jnp.int32, (4 * blk, LANES), 1)
    ones_blocks = jnp.where((key_row < 2 * blk) == (key_lane < HEAD_DIM), 1.0, 0.0).astype(BF16)
    for sub in range(q_ref.shape[0] // blk):
        rows = slice(sub * blk, (sub + 1) * blk)
        if sub == 0:
            k_all = jnp.concatenate([kp_ref[...], kc_ref[rows, :]], axis=0)
            v_all = jnp.concatenate([vp_ref[...], vc_ref[rows, :]], axis=0)
        else:
            k_all = kc_ref[(sub - 1) * blk:(sub + 1) * blk, :]
            v_all = vc_ref[(sub - 1) * blk:(sub + 1) * blk, :]
        ok = band if sub else band_first
        for cch in range(n_kv // 2):
            kc = k_all[:, cch * LANES:(cch + 1) * LANES]
            vc = v_all[:, cch * LANES:(cch + 1) * LANES]
            kr = pltpu.roll(kc, HEAD_DIM, axis=1)
            vr = pltpu.roll(vc, HEAD_DIM, axis=1)
            for odd in range(2):
                j = 2 * cch + odd
                kb = _pair_blocks(kc, kr, odd)
                vb = jnp.concatenate([_pair_blocks(vc, vr, odd), ones_blocks], axis=1)
                h0 = group * j
                c0 = h0 * HEAD_DIM
                q2 = jnp.concatenate([q_ref[rows, c0:c0 + LANES], q_ref[rows, c0 + LANES:c0 + 2 * LANES]], axis=0)
                o = _attend(q2, kb, vb, ok, sinks_ref, h0, blk).astype(o_ref.dtype)
                o_ref[rows, c0:c0 + LANES] = o[:blk]
                o_ref[rows, c0 + LANES:c0 + 2 * LANES] = o[blk:]


def _attn_prompt(sinks, q, k, v, *, batch, seq, n_kv, blocks_per_step=8):
    blk = WINDOW
    tq = blocks_per_step * blk
    nb = seq // tq
    aw, kw = q.shape[1], k.shape[1]
    cur = lambda w: pl.BlockSpec((tq, w), lambda b, i: (b * nb + i, 0))
    prev = lambda w: pl.BlockSpec(
        (blk, w), lambda b, i: ((b * nb + i) * blocks_per_step - jnp.minimum(i, 1), 0))
    return pl.pallas_call(
        functools.partial(_attn_prompt_kernel, n_kv=n_kv),
        grid=(batch, nb),
        in_specs=[pl.BlockSpec(memory_space=pltpu.SMEM), cur(aw), prev(kw), cur(kw), prev(kw), cur(kw)],
        out_specs=cur(aw),
        out_shape=jax.ShapeDtypeStruct((batch * seq, aw), BF16),
        compiler_params=_params("parallel", "arbitrary"),
        name="attn_prompt",
    )(sinks, q, k, k, v, v)


def _attn_sample_kernel(sinks_ref, q_ref, kn_ref, vn_ref, ck_ref, cv_ref, o_ref, nk_ref, nv_ref, qf_ref, of_ref, *, n_kv, t_new):
    g_seqs, kw, win = ck_ref.shape
    n_heads = q_ref.shape[1] // HEAD_DIM
    group = n_heads // n_kv
    n_rows = n_heads * t_new
    qf_ref[...] = q_ref[...].astype(F32)
    t = lax.broadcasted_iota(jnp.int32, (n_rows, 2 * win), 0) % t_new
    c = lax.broadcasted_iota(jnp.int32, (n_rows, 2 * win), 1)
    ok = jnp.logical_or(jnp.logical_and(c < win, c > t),
                        jnp.logical_and(c >= 2 * win - t_new, c - (2 * win - t_new) <= t))
    sink = jnp.concatenate([jnp.full((t_new, 1), sinks_ref[h], F32) for h in range(n_heads)], axis=0)
    low = lax.broadcasted_iota(jnp.int32, (t_new, LANES), 1) < HEAD_DIM
    zero_tile = jnp.zeros((t_new, LANES), F32)
    keep_old = lax.broadcasted_iota(jnp.int32, (kw, win), 1) < win - t_new
    ones_rows = jnp.ones((LANES, 2 * win), BF16)
    kn_all = kn_ref[0]
    vn_all = vn_ref[0]

    def one_seq(g, carry):
        r0 = pl.multiple_of(g * t_new, t_new)
        k_old = ck_ref[g]
        v_old = cv_ref[g]
        shift = win - t_new * (g + 1)
        k_new = pltpu.roll(kn_all, shift, axis=1)
        v_new = pltpu.roll(vn_all, shift, axis=1)
        nk_ref[g] = jnp.where(keep_old, pltpu.roll(k_old, win - t_new, axis=1), k_new)
        nv_ref[g] = jnp.where(keep_old, pltpu.roll(v_old, win - t_new, axis=1), v_new)
        keys = jnp.concatenate([k_old, k_new], axis=1).astype(BF16)
        vals = jnp.concatenate([jnp.concatenate([v_old, v_new], axis=1).astype(BF16), ones_rows], axis=0)
        q_rows = []
        for h in range(n_heads):
            j = h // group
            src = qf_ref[pl.ds(r0, t_new), (h // 2) * LANES:(h // 2 + 1) * LANES]
            if h % 2 != j % 2:
                src = pltpu.roll(src, HEAD_DIM, axis=1)
            half = jnp.where(low, src, 0.0) if j % 2 == 0 else jnp.where(low, 0.0, src)
            tiles = [zero_tile] * (kw // LANES)
            tiles[j // 2] = half
            q_rows.append(jnp.concatenate(tiles, axis=1))
        qb = jnp.concatenate(q_rows, axis=0).astype(BF16)
        s = jnp.dot(qb, keys, preferred_element_type=F32)
        s = jnp.where(ok, s, NEG)
        m = jnp.maximum(jnp.max(s, axis=-1, keepdims=True), sink)
        p = jnp.exp(s - m).astype(BF16)
        r = lax.dot_general(p, vals, (((1,), (1,)), ((), ())), preferred_element_type=F32)
        denom = r[:, kw:] + jnp.exp(sink - m)
        o = r[:, :kw] / jnp.concatenate([denom] * (kw // LANES), axis=1)
        for pair in range(n_heads // 2):
            j = (2 * pair) // group
            lanes = slice((j // 2) * LANES, (j // 2 + 1) * LANES)
            a = o[2 * pair * t_new:(2 * pair + 1) * t_new, lanes]
            b = o[(2 * pair + 1) * t_new:(2 * pair + 2) * t_new, lanes]
            if j % 2 == 0:
                b = pltpu.roll(b, HEAD_DIM, axis=1)
            else:
                a = pltpu.roll(a, HEAD_DIM, axis=1)
            of_ref[pl.ds(r0, t_new), pair * LANES:(pair + 1) * LANES] = jnp.where(low, a, b)
        return carry

    lax.fori_loop(0, g_seqs, one_seq, 0, unroll=4)
    o_ref[...] = of_ref[...].astype(o_ref.dtype)


def _attn_sample(sinks, q, k_t, v_t, cache_k_t, cache_v_t, *, row0, t_new, n_kv):
    n_seq, kw, win = cache_k_t.shape
    g_seqs = win // t_new
    aw = q.shape[1]
    assert k_t.shape == (n_seq // g_seqs, kw, win)
    tile0 = row0 // (g_seqs * t_new)
    tok = pl.BlockSpec((g_seqs * t_new, aw), lambda i: (i, 0))
    q_tok = pl.BlockSpec((g_seqs * t_new, aw), lambda i: (tile0 + i, 0))
    new = pl.BlockSpec((1, kw, win), lambda i: (i, 0, 0))
    cache = pl.BlockSpec((g_seqs, kw, win), lambda i: (i, 0, 0))
    return pl.pallas_call(
        functools.partial(_attn_sample_kernel, n_kv=n_kv, t_new=t_new),
        grid=(n_seq // g_seqs,),
        in_specs=[pl.BlockSpec(memory_space=pltpu.SMEM), q_tok, new, new, cache, cache],
        out_specs=[tok, cache, cache],
        out_shape=[
            jax.ShapeDtypeStruct((n_seq * t_new, aw), BF16),
            jax.ShapeDtypeStruct(cache_k_t.shape, F32),
            jax.ShapeDtypeStruct(cache_v_t.shape, F32),
        ],
        scratch_shapes=[pltpu.VMEM((g_seqs * t_new, aw), F32), pltpu.VMEM((g_seqs * t_new, aw), F32)],
        compiler_params=_params("parallel"),
        name="attn_sample",
    )(sinks, q, k_t, v_t, cache_k_t, cache_v_t)


def _ln_silu(y, g, b):
    yc = y - jnp.mean(y, axis=-1, keepdims=True)
    var = jnp.mean(yc * yc, axis=-1, keepdims=True)
    return jax.nn.silu(yc * lax.rsqrt(var + LN_EPS) * g + b)


def _conv_prompt_kernel(cur_ref, halo_ref, w_ref, b_ref, g_ref, beta_ref, o_ref, xc_ref, y_ref, *, width, rows_per, lanes_per):
    tc, ch = cur_ref.shape
    halo = halo_ref.shape[0]
    i = pl.program_id(1)
    xc_ref[:halo, :] = jnp.where(i == 0, 0.0, halo_ref[...])
    xc_ref[halo:, :] = cur_ref[...]
    base = halo - (width - 1)

    def col_chunk(cc, carry):
        l0 = pl.multiple_of(cc * lanes_per, lanes_per)
        cols = pl.ds(l0, lanes_per)
        for rr in range(tc // rows_per):
            acc = jnp.broadcast_to(b_ref[:, cols], (rows_per, lanes_per))
            for phase in range(SUBLANES):
                n_win = rows_per + (SUBLANES if phase else 0)
                part = None
                for j in range(width):
                    if (base + j) % SUBLANES != phase:
                        continue
                    start = rr * rows_per + ((base + j) // SUBLANES) * SUBLANES
                    term = w_ref[pl.ds(j, 1), cols] * xc_ref[pl.ds(start, n_win), cols]
                    part = term if part is None else part + term
                if part is None:
                    continue
                if phase:
                    part = pltpu.roll(part, n_win - phase, axis=0)
                acc = acc + part[:rows_per]
            y_ref[pl.ds(rr * rows_per, rows_per), cols] = acc
        return carry

    lax.fori_loop(0, ch // lanes_per, col_chunk, 0)
    o_ref[...] = _ln_silu(y_ref[...], g_ref[...], beta_ref[...]).astype(o_ref.dtype)


def _conv_prompt(glu, w, b, g, beta, *, batch, seq, tc=512, halo=32):
    ch = glu.shape[1]
    width = w.shape[0]
    nt = seq // tc
    per = tc // halo
    cur = pl.BlockSpec((tc, ch), lambda bi, i: (bi * nt + i, 0))
    prev = pl.BlockSpec((halo, ch), lambda bi, i: (jnp.maximum((bi * nt + i) * per - 1, 0), 0))
    kern = functools.partial(_conv_prompt_kernel, width=width, rows_per=128, lanes_per=LANES)
    return pl.pallas_call(
        kern,
        grid=(batch, nt),
        in_specs=[cur, prev, _resident(w.shape), _resident((1, ch)), _resident((1, ch)), _resident((1, ch))],
        out_specs=cur,
        out_shape=jax.ShapeDtypeStruct((batch * seq, ch), BF16),
        scratch_shapes=[pltpu.VMEM((halo + tc, ch), F32), pltpu.VMEM((tc, ch), F32)],
        compiler_params=_params("parallel", "arbitrary"),
        name="conv_prompt",
    )(glu, glu, w, b, g, beta)


def _conv_sample_kernel(new_ref, st_ref, w_ref, b_ref, g_ref, beta_ref, o_ref, ns_ref, y_ref, tok_ref, *, t_new):
    n_past, n_seq, wide = st_ref.shape
    lanes = tok_ref.shape[1]
    n_sub = wide // lanes
    width = w_ref.shape[0]
    c = pl.program_id(0)
    for k in range(n_sub):
        cols = slice(k * lanes, (k + 1) * lanes)
        tok_ref[...] = new_ref[:, cols]
        new = [tok_ref[pl.ds(t, n_seq, stride=t_new), :] for t in range(t_new)]
        row = lambda r: st_ref[r, :, cols] if r < n_past else new[r - n_past]
        for t in range(t_new):
            acc = jnp.broadcast_to(b_ref[:, cols], (n_seq, lanes))
            for j in range(width):
                acc = acc + w_ref[pl.ds(j, 1), cols] * row(t + j)
            y_ref[c * n_sub + k, pl.ds(t, n_seq, stride=t_new), :] = acc
        for r in range(n_past):
            ns_ref[r, :, cols] = row(r + t_new)

    @pl.when(c == pl.num_programs(0) - 1)
    def _():
        n_chunks = y_ref.shape[0]
        ch = n_chunks * lanes
        mean = jnp.sum(sum(y_ref[k] for k in range(n_chunks)), axis=-1, keepdims=True) / ch
        var = jnp.sum(sum((y_ref[k] - mean) ** 2 for k in range(n_chunks)), axis=-1, keepdims=True) / ch
        inv = lax.rsqrt(var + LN_EPS)
        for k in range(n_chunks):
            cols = slice(k * lanes, (k + 1) * lanes)
            z = (y_ref[k] - mean) * inv * g_ref[:, cols] + beta_ref[:, cols]
            o_ref[:, cols] = jax.nn.silu(z).astype(o_ref.dtype)


def _conv_sample(glu, state_t, w, b, g, beta, *, row0, t_new, wide=2 * LANES):
    n_past, n_seq, ch = state_t.shape
    n_tok = n_seq * t_new
    assert w.shape[0] == n_past + 1
    chunk = lambda rows: pl.BlockSpec((rows, wide), lambda c: (0, c))
    slabs = pl.BlockSpec((n_past, n_seq, wide), lambda c: (0, 0, c))
    return pl.pallas_call(
        functools.partial(_conv_sample_kernel, t_new=t_new),
        grid=(ch // wide,),
        in_specs=[pl.BlockSpec((n_tok, wide), lambda c: (row0 // n_tok, c)), slabs, chunk(w.shape[0]), chunk(1),
                  _resident((1, ch)), _resident((1, ch))],
        out_specs=[pl.BlockSpec((n_tok, ch), lambda c: (0, 0)), slabs],
        out_shape=[jax.ShapeDtypeStruct((n_tok, ch), BF16), jax.ShapeDtypeStruct(state_t.shape, F32)],
        scratch_shapes=[pltpu.VMEM((ch // LANES, n_tok, LANES), F32), pltpu.VMEM((n_tok, LANES), F32)],
        compiler_params=_params("arbitrary"),
        name="conv_sample",
    )(glu, state_t, w, b, g, beta)


def _out_proj_kernel(xa_ref, xb_ref, aa_ref, ab_ref, ca_ref, cb_ref, w_ref, o_ref, *, tiles_a):
    aw = aa_ref.shape[1]
    attn = _pick_rows(aa_ref, ab_ref, tiles_a)
    conv = _pick_rows(ca_ref, cb_ref, tiles_a)
    o_ref[...] = (_pick_rows(xa_ref, xb_ref, tiles_a)
                  + jnp.dot(attn, w_ref[:aw, :].astype(BF16), preferred_element_type=F32)
                  + jnp.dot(conv, w_ref[aw:, :].astype(BF16), preferred_element_type=F32))


def _out_proj(x, attn, conv, w, *, slab, tm=512):
    d = x[0].shape[1]
    n = x[0].shape[0] + x[1].shape[0]
    tiles_a = x[0].shape[0] // tm
    specs = [spec for pair in (x, attn, conv) for spec in _split_rows(tm, pair[0].shape[1], tiles_a)]
    return pl.pallas_call(
        functools.partial(_out_proj_kernel, tiles_a=tiles_a),
        grid=(n // tm,),
        in_specs=specs + [_resident_slab(w.shape, slab)],
        out_specs=pl.BlockSpec((tm, d), lambda i: (i, 0)),
        out_shape=jax.ShapeDtypeStruct((n, d), F32),
        compiler_params=_params("arbitrary"),
        name="out_proj",
    )(*x, *attn, *conv, w)


def _ffn_kernel(x_ref, g_ref, wg_ref, wu_ref, wd_ref, o_ref, *rest):
    h_ref = rest[-1]

    @pl.when(pl.program_id(1) == 0)
    def _():
        x = x_ref[...]
        h_ref[...] = _rms(x, g_ref[...]).astype(BF16)
        o_ref[...] = x

    weights = [w_ref[...].astype(BF16) for w_ref in (wg_ref, wu_ref, wd_ref)]
    for w_out, w in zip(rest[:-1], weights):
        w_out[...] = w
    wg, wu, wd = weights
    h = h_ref[...]
    a = jnp.dot(h, wg, preferred_element_type=F32)
    u = jnp.dot(h, wu, preferred_element_type=F32)
    act = (jax.nn.silu(a) * u).astype(BF16)
    o_ref[...] += jnp.dot(act, wd, preferred_element_type=F32)


def _ffn(x, g, wg, wu, wd, *, layer=None, emit_bf16=False, row0=0, n_rows=None, tm=1024, tf=256):
    d = x.shape[1]
    n = x.shape[0] if n_rows is None else n_rows
    tile0 = row0 // tm
    dff = wg.shape[-1]
    nt, nf = n // tm, dff // tf
    if layer is None:
        w_in_specs = [pl.BlockSpec((d, tf), lambda i, f: (0, f)), pl.BlockSpec((d, tf), lambda i, f: (0, f)),
                      pl.BlockSpec((tf, d), lambda i, f: (f, 0))]
    else:
        w_in_specs = [pl.BlockSpec((None, d, tf), lambda i, f: (layer, 0, f)),
                      pl.BlockSpec((None, d, tf), lambda i, f: (layer, 0, f)),
                      pl.BlockSpec((None, tf, d), lambda i, f: (layer, f, 0))]
    rows = pl.BlockSpec((tm, d), lambda i, f: (i, 0))
    out_specs, out_shape = [rows], [jax.ShapeDtypeStruct((n, d), F32)]
    if emit_bf16:
        assert nt == 1, "one row tile, so every bf16 weight tile is written exactly once"
        out_specs += [pl.BlockSpec((d, tf), lambda i, f: (0, f)), pl.BlockSpec((d, tf), lambda i, f: (0, f)),
                      pl.BlockSpec((tf, d), lambda i, f: (f, 0))]
        out_shape += [jax.ShapeDtypeStruct((d, dff), BF16), jax.ShapeDtypeStruct((d, dff), BF16),
                      jax.ShapeDtypeStruct((dff, d), BF16)]
    mode = dict(pipeline_mode=pl.Buffered(1)) if nt == 1 else {}
    x_spec = pl.BlockSpec((tm, d), lambda i, f: (tile0 + i, 0), **mode)
    outs = pl.pallas_call(
        _ffn_kernel,
        grid=(nt, nf),
        in_specs=[x_spec, _resident((1, d))] + w_in_specs,
        out_specs=out_specs,
        out_shape=out_shape,
        scratch_shapes=[pltpu.VMEM((tm, d), BF16)],
        compiler_params=_params("parallel", "arbitrary"),
        name="ffn",
    )(x, g, wg, wu, wd)
    return outs if emit_bf16 else outs[0]


def _pool_prompt_kernel(cur_ref, halo_ref, g_ref, pw_ref, ps_ref, o_ref, tail_ref, xp_ref):
    tp, d = cur_ref.shape
    halo = halo_ref.shape[0]
    i = pl.program_id(1)
    gain = g_ref[...]
    xp_ref[:halo, :] = jnp.where(i == 0, 0.0, _rms(halo_ref[...], gain))
    xp_ref[halo:, :] = _rms(cur_ref[...], gain)
    tail_ref[0] = xp_ref[tp:, :]
    pg = d // len(POOL_WINDOWS)
    pos = i * tp + lax.broadcasted_iota(jnp.int32, (tp, 1), 0)
    for gi, w in enumerate(POOL_WINDOWS):
        cols = slice(gi * pg, (gi + 1) * pg)
        ext = xp_ref[:, cols]
        h = ext[halo:]
        span = 1
        while span < w:
            ext = ext + pltpu.roll(ext, span, axis=0)
            span *= 2
        inv_count = 1.0 / jnp.minimum(pos + 1, w).astype(F32)
        dpool = (ext[halo:] * inv_count - h).astype(BF16)
        y = jnp.dot(dpool, pw_ref[gi].astype(BF16), preferred_element_type=F32)
        o_ref[:, cols] = cur_ref[:, cols] + y * ps_ref[:, cols]


def _pool_prompt(x, g, pw, ps, *, slab, batch, seq, tp=1024, halo=16):
    assert all(w & (w - 1) == 0 and w <= halo for w in POOL_WINDOWS)
    d = x.shape[1]
    nt = seq // tp
    per = tp // halo
    cur = pl.BlockSpec((tp, d), lambda bi, i: (bi * nt + i, 0))
    prev = pl.BlockSpec((halo, d), lambda bi, i: (jnp.maximum((bi * nt + i) * per - 1, 0), 0))
    return pl.pallas_call(
        _pool_prompt_kernel,
        grid=(batch, nt),
        in_specs=[cur, prev, _resident((1, d)), _resident_slab(pw.shape, slab), _resident((1, d))],
        out_specs=[cur, pl.BlockSpec((1, halo, d), lambda bi, i: (bi, 0, 0))],
        out_shape=[jax.ShapeDtypeStruct((batch * seq, d), F32), jax.ShapeDtypeStruct((batch, halo, d), F32)],
        scratch_shapes=[pltpu.VMEM((halo + tp, d), F32)],
        compiler_params=_params("parallel", "arbitrary"),
        name="pool_prompt",
    )(x, x, g, pw, ps)


def _pool_sample_kernel(x_ref, st_ref, g_ref, pw_ref, ps_ref, o_ref, ns_ref, inv_ref, y_ref, tok_ref, *, t_new, start_pos):
    n_past, n_seq, pg = st_ref.shape
    lanes = tok_ref.shape[1]
    phase, gi = pl.program_id(0), pl.program_id(1)
    n_tok = x_ref.shape[0]
    token_rows = lambda t: pl.ds(t, n_seq, stride=t_new)
    group_cols = lambda k: pl.ds(pl.multiple_of(gi * pg + k * lanes, lanes), lanes)

    @pl.when(jnp.logical_and(phase == 0, gi == 0))
    def _():
        x = x_ref[...]
        inv = lax.rsqrt(jnp.mean(x * x, axis=-1, keepdims=True) + RMS_EPS)
        inv_ref[...] = jnp.broadcast_to(inv, (n_tok, lanes))

    @pl.when(phase == 0)
    def _():
        chunks = []
        for k in range(pg // lanes):
            cols = slice(k * lanes, (k + 1) * lanes)
            tok_ref[...] = x_ref[:, group_cols(k)] * inv_ref[...]
            gain = g_ref[:, group_cols(k)]
            new = [tok_ref[token_rows(t), :] * gain for t in range(t_new)]
            row = lambda r: st_ref[n_past + r, :, cols] if r < 0 else new[r]
            for r in range(n_past):
                ns_ref[r, :, cols] = row(r + t_new - n_past)
            sums = {1: {r: row(r) for r in range(-n_past, t_new)}}
            span = 1
            while span < max(POOL_WINDOWS):
                prev = sums[span]
                sums[2 * span] = {r: prev[r] + prev[r - span] for r in prev if r - span in prev}
                span *= 2
            diffs = []
            for t in range(t_new):
                pooled = None
                for i, w in enumerate(POOL_WINDOWS):
                    cand = sums[w][t] * (1.0 / min(start_pos + t + 1, w))
                    pooled = cand if pooled is None else jnp.where(gi == i, cand, pooled)
                diffs.append(pooled - new[t])
            chunks.append(jnp.concatenate(diffs, axis=0))
        dp = jnp.concatenate(chunks, axis=1).astype(BF16)
        cols = pl.ds(pl.multiple_of(gi * pg, pg), pg)
        y_ref[:, cols] = jnp.dot(dp, pw_ref[gi].astype(BF16), preferred_element_type=F32) * ps_ref[:, cols]

    @pl.when(phase == 1)
    def _():
        for k in range(pg // lanes):
            y = y_ref[:, group_cols(k)]
            tok_ref[...] = x_ref[:, group_cols(k)]
            for t in range(t_new):
                tok_ref[token_rows(t), :] = tok_ref[token_rows(t), :] + y[t * n_seq:(t + 1) * n_seq]
            o_ref[:, k * lanes:(k + 1) * lanes] = tok_ref[...]


def _pool_sample(x, state_t, g, pw, ps, *, slab, t_new):
    n_past, n_seq, d = state_t.shape
    n_tok = n_seq * t_new
    n_groups = len(POOL_WINDOWS)
    pg = d // n_groups
    assert all(w & (w - 1) == 0 and w <= n_past + 1 for w in POOL_WINDOWS)
    kern = functools.partial(_pool_sample_kernel, t_new=t_new, start_pos=PAST_LEN)
    slabs = pl.BlockSpec((n_past, n_seq, pg), lambda p, gi: (0, 0, gi * (1 - p) + (n_groups - 1) * p))
    return pl.pallas_call(
        kern,
        grid=(2, n_groups),
        in_specs=[_resident((n_tok, d)), slabs, _resident((1, d)), _resident_slab(pw.shape, slab), _resident((1, d))],
        out_specs=[pl.BlockSpec((n_tok, pg), lambda p, gi: (0, gi * p)), slabs],
        out_shape=[jax.ShapeDtypeStruct((n_tok, d), F32), jax.ShapeDtypeStruct(state_t.shape, F32)],
        scratch_shapes=[pltpu.VMEM((n_tok, LANES), F32), pltpu.VMEM((n_tok, d), F32), pltpu.VMEM((n_tok, LANES), F32)],
        compiler_params=_params("arbitrary", "arbitrary"),
        name="pool_sample",
    )(x, state_t, g, pw, ps)


def kernel(x_prompt, x_sample, cache_k, cache_v, state_conv, state_pool, norm_mix, w_in, q_norm, k_norm, sinks,
           conv_w, conv_b, conv_ln_g, conv_ln_b, w_out, pool_w, pool_scale, norm_ffn, w_gate, w_up, w_down):
    batch, seq, d = x_prompt.shape
    n_seq, t_new, _ = x_sample.shape
    n_kv = cache_k.shape[3]
    kv_w = n_kv * HEAD_DIM
    conv_ch = state_conv.shape[-1]
    attn_w = w_in.shape[-1] - 2 * kv_w - 2 * conv_ch
    n_prompt = batch * seq
    row = lambda a: a.reshape(1, -1)

    xp = x_prompt.reshape(n_prompt, d)
    xs = x_sample.reshape(n_seq * t_new, d)
    depth = norm_mix.shape[0]
    outs = {}
    for layer in range(depth):
        i = layer // 2
        if layer % 2 == 0:
            reps = MXU_DIM // HEAD_DIM
            q, k, v, glu = _in_proj(
                xp, xs, row(norm_mix[layer]), w_in, row(jnp.tile(q_norm[i], reps)), row(jnp.tile(k_norm[i], reps)),
                slab=i, attn_w=attn_w, kv_w=kv_w, conv_ch=conv_ch)
            attn_p = _attn_prompt(sinks[i], q, k, v, batch=batch, seq=seq, n_kv=n_kv)
            cache_t = lambda a: jnp.transpose(a, (0, 2, 3, 1)).reshape(n_seq, kv_w, WINDOW)
            new_t = lambda a: jnp.transpose(a[n_prompt:].reshape(-1, WINDOW, kv_w), (0, 2, 1))
            attn_s, nk_t, nv_t = _attn_sample(sinks[i], q, new_t(k), new_t(v), cache_t(cache_k[i]),
                                              cache_t(cache_v[i]), row0=n_prompt, t_new=t_new, n_kv=n_kv)
            uncache_t = lambda a: jnp.transpose(a.reshape(n_seq, n_kv, HEAD_DIM, WINDOW), (0, 3, 1, 2))
            ln = (row(conv_b[i]), row(conv_ln_g[i]), row(conv_ln_b[i]))
            conv_p = _conv_prompt(glu, conv_w[i], *ln, batch=batch, seq=seq)
            conv_s, nc_t = _conv_sample(glu, jnp.transpose(state_conv[i], (1, 0, 2)), conv_w[i], *ln,
                                        row0=n_prompt, t_new=t_new)
            nc_s = jnp.transpose(nc_t, (1, 0, 2))
            mixed = _out_proj((xp, xs), (attn_p, attn_s), (conv_p, conv_s), w_out, slab=i)
            ffn_in = ((mixed, 0, n_prompt), (mixed, n_prompt, n_seq * t_new))
            last = lambda a, n: jnp.stack([a[(b + 1) * seq - n:(b + 1) * seq] for b in range(batch)])
            outs.setdefault("kp", []).append(last(k, WINDOW).reshape(batch, WINDOW, n_kv, HEAD_DIM))
            outs.setdefault("vp", []).append(last(v, WINDOW).reshape(batch, WINDOW, n_kv, HEAD_DIM))
            outs.setdefault("cp", []).append(last(glu, conv_w.shape[1] - 1))
            outs.setdefault("ks", []).append(uncache_t(nk_t))
            outs.setdefault("vs", []).append(uncache_t(nv_t))
            outs.setdefault("cs", []).append(nc_s)
        else:
            args = (row(norm_mix[layer]), pool_w, row(pool_scale[i]))
            n_past = state_pool.shape[2]
            xp, tail = _pool_prompt(xp, *args, slab=i, batch=batch, seq=seq)
            xs, np_t = _pool_sample(xs, jnp.transpose(state_pool[i], (1, 0, 2)), *args, slab=i, t_new=t_new)
            np_s = jnp.transpose(np_t, (1, 0, 2))
            ffn_in = ((xp, 0, n_prompt), (xs, 0, n_seq * t_new))
            outs.setdefault("pp", []).append(tail[:, -n_past:])
            outs.setdefault("ps", []).append(np_s)
        g_ffn = row(norm_ffn[layer])
        (rows_p, row0_p, n_p), (rows_s, row0_s, n_s) = ffn_in
        xs, wg16, wu16, wd16 = _ffn(rows_s, g_ffn, w_gate, w_up, w_down, layer=layer, emit_bf16=True,
                                    row0=row0_s, n_rows=n_s)
        xp = _ffn(rows_p, g_ffn, wg16, wu16, wd16, row0=row0_p, n_rows=n_p, tf=512)
    st = lambda name: jnp.stack(outs[name])
    return (xp.reshape(batch, seq, d), xs.reshape(n_seq, t_new, d),
            st("kp"), st("vp"), st("cp"), st("pp"), st("ks"), st("vs"), st("cs"), st("ps"))
```

```python
import functools

import jax
import jax.numpy as jnp
from jax import lax
from jax.experimental import pallas as pl
from jax.experimental.pallas import tpu as pltpu

F32 = jnp.float32
BF16 = jnp.bfloat16

HEAD_DIM = 64
WINDOW = 128
PAST_LEN = 8192
POOL_WINDOWS = (2, 4, 8, 16)
RMS_EPS = 1e-6
LN_EPS = 1e-5
NEG = -1e30

LANES = 128
SUBLANES = 8
MXU_DIM = 256
VMEM_LIMIT = 56 * 1024 * 1024


def _params(*semantics):
    return pltpu.CompilerParams(dimension_semantics=semantics, vmem_limit_bytes=VMEM_LIMIT)


def _rms(x, g):
    ms = jnp.mean(x * x, axis=-1, keepdims=True)
    return x * lax.rsqrt(ms + RMS_EPS) * g


def _resident(shape):
    zeros = (0,) * len(shape)
    return pl.BlockSpec(shape, lambda *_: zeros, pipeline_mode=pl.Buffered(1))


def _resident_slab(shape, slab):
    index = (slab,) + (0,) * (len(shape) - 1)
    return pl.BlockSpec((None,) + tuple(shape[1:]), lambda *_: index, pipeline_mode=pl.Buffered(1))


def _split_rows(tm, width, tiles_a, single_buffer_b=False):
    spec_a = pl.BlockSpec((tm, width), lambda i: (jnp.minimum(i, tiles_a - 1), 0))
    mode = dict(pipeline_mode=pl.Buffered(1)) if single_buffer_b else {}
    spec_b = pl.BlockSpec((tm, width), lambda i: (jnp.maximum(i - tiles_a, 0), 0), **mode)
    return [spec_a, spec_b]


def _pick_rows(a_ref, b_ref, tiles_a):
    return jnp.where(pl.program_id(0) < tiles_a, a_ref[...], b_ref[...])


def _in_proj_kernel(xa_ref, xb_ref, g_ref, w_ref, qg_ref, kg_ref, q_ref, k_ref, v_ref, glu_ref, *, tiles_a, attn_w, kv_w, conv_ch):
    h = _rms(_pick_rows(xa_ref, xb_ref, tiles_a), g_ref[...]).astype(BF16)
    row = lax.broadcasted_iota(jnp.int32, (MXU_DIM, MXU_DIM), 0) // HEAD_DIM
    col = lax.broadcasted_iota(jnp.int32, (MXU_DIM, MXU_DIM), 1) // HEAD_DIM
    pmat = jnp.where(row == col, 1.0 / HEAD_DIM, 0.0).astype(BF16)

    def proj(c0):
        return jnp.dot(h, w_ref[:, c0:c0 + MXU_DIM].astype(BF16), preferred_element_type=F32)

    def head_rms(z):
        sq = z * z
        hi = sq.astype(BF16)
        lo = (sq - hi.astype(F32)).astype(BF16)
        ms = jnp.dot(hi, pmat, preferred_element_type=F32) + jnp.dot(lo, pmat, preferred_element_type=F32)
        return z * lax.rsqrt(ms + RMS_EPS)

    scale = HEAD_DIM ** -0.5

    def put_q(c0, z):
        q_ref[:, c0:c0 + MXU_DIM] = (head_rms(z) * qg_ref[...] * scale).astype(q_ref.dtype)

    def put_k(c0, z):
        k_ref[:, c0:c0 + MXU_DIM] = head_rms(z) * kg_ref[...]

    def put_v(c0, z):
        v_ref[:, c0:c0 + MXU_DIM] = z

    def put_glu(c0, a, gate):
        glu_ref[:, c0:c0 + MXU_DIM] = a * jax.nn.sigmoid(gate)

    units = [((c0,), functools.partial(put_q, c0)) for c0 in range(0, attn_w, MXU_DIM)]
    units += [((attn_w + c0,), functools.partial(put_k, c0)) for c0 in range(0, kv_w, MXU_DIM)]
    units += [((attn_w + kv_w + c0,), functools.partial(put_v, c0)) for c0 in range(0, kv_w, MXU_DIM)]
    units += [((attn_w + 2 * kv_w + c0, attn_w + 2 * kv_w + conv_ch + c0), functools.partial(put_glu, c0))
              for c0 in range(0, conv_ch, MXU_DIM)]
    pending = None
    for cols, epilogue in units:
        zs = [proj(c0) for c0 in cols]
        if pending is not None:
            pending[0](*pending[1])
        pending = (epilogue, zs)
    pending[0](*pending[1])


def _in_proj(xa, xb, g, w, qg, kg, *, slab, attn_w, kv_w, conv_ch, tm=512):
    d = xa.shape[1]
    n = xa.shape[0] + xb.shape[0]
    tiles_a = xa.shape[0] // tm
    kern = functools.partial(_in_proj_kernel, tiles_a=tiles_a, attn_w=attn_w, kv_w=kv_w, conv_ch=conv_ch)
    rows = lambda width: pl.BlockSpec((tm, width), lambda i: (i, 0))
    return pl.pallas_call(
        kern,
        grid=(n // tm,),
        in_specs=_split_rows(tm, d, tiles_a, single_buffer_b=True) + [
            _resident((1, d)), _resident_slab(w.shape, slab), _resident((1, MXU_DIM)), _resident((1, MXU_DIM))],
        out_specs=[rows(attn_w), rows(kv_w), rows(kv_w), rows(conv_ch)],
        out_shape=[
            jax.ShapeDtypeStruct((n, attn_w), BF16),
            jax.ShapeDtypeStruct((n, kv_w), F32),
            jax.ShapeDtypeStruct((n, kv_w), F32),
            jax.ShapeDtypeStruct((n, conv_ch), F32),
        ],
        compiler_params=_params("arbitrary"),
        name="in_proj",
    )(xa, xb, g, w, qg, kg)


def _pair_blocks(chunk, rolled, odd):
    lane = lax.broadcasted_iota(jnp.int32, chunk.shape, 1)
    low = lane < HEAD_DIM
    if odd:
        top = jnp.where(low, rolled, 0.0)
        bot = jnp.where(low, 0.0, chunk)
    else:
        top = jnp.where(low, chunk, 0.0)
        bot = jnp.where(low, 0.0, rolled)
    return jnp.concatenate([top, bot], axis=0).astype(BF16)


def _softmax_pv(s, ok, sink_a, sink_b, vb_ones):
    half = s.shape[1] // 2
    s = jnp.where(ok, s, NEG)
    sa, sb = s[:, :half], s[:, half:]
    ma = jnp.maximum(jnp.max(sa, axis=-1, keepdims=True), sink_a)
    mb = jnp.maximum(jnp.max(sb, axis=-1, keepdims=True), sink_b)
    p = jnp.concatenate([jnp.exp(sa - ma), jnp.exp(sb - mb)], axis=1).astype(BF16)
    r = jnp.dot(p, vb_ones, preferred_element_type=F32)
    lane = lax.broadcasted_iota(jnp.int32, (s.shape[0], LANES), 1)
    sink_term = jnp.where(lane < HEAD_DIM, jnp.exp(sink_a - ma), jnp.exp(sink_b - mb))
    return r[:, :LANES] / (r[:, LANES:] + sink_term)


def _attend(q2, kb, vb, ok, sinks_ref, h0, rows):
    s = lax.dot_general(q2, kb, (((1,), (1,)), ((), ())), preferred_element_type=F32)
    row = lax.broadcasted_iota(jnp.int32, (2 * rows, 1), 0)
    first = row < rows
    sink_a = jnp.where(first, sinks_ref[h0], sinks_ref[h0 + 2])
    sink_b = jnp.where(first, sinks_ref[h0 + 1], sinks_ref[h0 + 3])
    return _softmax_pv(s, ok, sink_a, sink_b, vb)


def _attn_prompt_kernel(sinks_ref, q_ref, kp_ref, kc_ref, vp_ref, vc_ref, o_ref, *, n_kv):
    blk = kp_ref.shape[0]
    i = pl.program_id(1)
    shape = (2 * blk, 4 * blk)
    r = lax.broadcasted_iota(jnp.int32, shape, 0) % blk
    c = lax.broadcasted_iota(jnp.int32, shape, 1) % (2 * blk)
    band = jnp.logical_and(c > r, c <= r + blk)
    band_first = jnp.logical_and(band, jnp.logical_or(c >= blk, i > 0))
    group = (q_ref.shape[1] // HEAD_DIM) // n_kv
    key_row = lax.broadcasted_iota(jnp.int32, (4 * blk, LANES), 0)
    key_lane = lax.broadcasted_iota(jnp.int32, (4 * blk, LANES), 1)
    ones_blocks = jnp.where((key_row < 2 * blk) == (key_lane < HEAD_DIM), 1.0, 0.0).astype(BF16)
    for sub in range(q_ref.shape[0] // blk):
        rows = slice(sub * blk, (sub + 1) * blk)
        if sub == 0:
            k_all = jnp.concatenate([kp_ref[...], kc_ref[rows, :]], axis=0)
            v_all = jnp.concatenate([vp_ref[...], vc_ref[rows, :]], axis=0)
        else:
            k_all = kc_ref[(sub - 1) * blk:(sub + 1) * blk, :]
            v_all = vc_ref[(sub - 1) * blk:(sub + 1) * blk, :]
        ok = band if sub else band_first
        for cch in range(n_kv // 2):
            kc = k_all[:, cch * LANES:(cch + 1) * LANES]
            vc = v_all[:, cch * LANES:(cch + 1) * LANES]
            kr = pltpu.roll(kc, HEAD_DIM, axis=1)
            vr = pltpu.roll(vc, HEAD_DIM, axis=1)
            for odd in range(2):
                j = 2 * cch + odd
                kb = _pair_blocks(kc, kr, odd)
                vb = jnp.concatenate([_pair_blocks(vc, vr, odd), ones_blocks], axis=1)
                h0 = group * j
                c0 = h0 * HEAD_DIM
                q2 = jnp.concatenate([q_ref[rows, c0:c0 + LANES], q_ref[rows, c0 + LANES:c0 + 2 * LANES]], axis=0)
                o = _attend(q2, kb, vb, ok, sinks_ref, h0, blk).astype(o_ref.dtype)
                o_ref[rows, c0:c0 + LANES] = o[:blk]
                o_ref[rows, c0 + LANES:c0 + 2 * LANES] = o[blk:]


def _attn_prompt(sinks, q, k, v, *, batch, seq, n_kv, blocks_per_step=8):
    blk = WINDOW
    tq = blocks_per_step * blk
    nb = seq // tq
    aw, kw = q.shape[1], k.shape[1]
    cur = lambda w: pl.BlockSpec((tq, w), lambda b, i: (b * nb + i, 0))
    prev = lambda w: pl.BlockSpec(
        (blk, w), lambda b, i: ((b * nb + i) * blocks_per_step - jnp.minimum(i, 1), 0))
    return pl.pallas_call(
        functools.partial(_attn_prompt_kernel, n_kv=n_kv),
        grid=(batch, nb),
        in_specs=[pl.BlockSpec(memory_space=pltpu.SMEM), cur(aw), prev(kw), cur(kw), prev(kw), cur(kw)],
        out_specs=cur(aw),
        out_shape=jax.ShapeDtypeStruct((batch * seq, aw), BF16),
        compiler_params=_params("parallel", "arbitrary"),
        name="attn_prompt",
    )(sinks, q, k, k, v, v)


def _attn_sample_kernel(sinks_ref, q_ref, kn_ref, vn_ref, ck_ref, cv_ref, o_ref, nk_ref, nv_ref, qf_ref, of_ref, *, n_kv, t_new):
    g_seqs, kw, win = ck_ref.shape
    n_heads = q_ref.shape[1] // HEAD_DIM
    group = n_heads // n_kv
    n_rows = n_heads * t_new
    qf_ref[...] = q_ref[...].astype(F32)
    t = lax.broadcasted_iota(jnp.int32, (n_rows, 2 * win), 0) % t_new
    c = lax.broadcasted_iota(jnp.int32, (n_rows, 2 * win), 1)
    ok = jnp.logical_or(jnp.logical_and(c < win, c > t),
                        jnp.logical_and(c >= 2 * win - t_new, c - (2 * win - t_new) <= t))
    sink = jnp.concatenate([jnp.full((t_new, 1), sinks_ref[h], F32) for h in range(n_heads)], axis=0)
    low = lax.broadcasted_iota(jnp.int32, (t_new, LANES), 1) < HEAD_DIM
    zero_tile = jnp.zeros((t_new, LANES), F32)
    keep_old = lax.broadcasted_iota(jnp.int32, (kw, win), 1) < win - t_new
    ones_rows = jnp.ones((LANES, 2 * win), BF16)
    kn_all = kn_ref[0]
    vn_all = vn_ref[0]

    def one_seq(g, carry):
        r0 = pl.multiple_of(g * t_new, t_new)
        k_old = ck_ref[g]
        v_old = cv_ref[g]
        shift = win - t_new * (g + 1)
        k_new = pltpu.roll(kn_all, shift, axis=1)
        v_new = pltpu.roll(vn_all, shift, axis=1)
        nk_ref[g] = jnp.where(keep_old, pltpu.roll(k_old, win - t_new, axis=1), k_new)
        nv_ref[g] = jnp.where(keep_old, pltpu.roll(v_old, win - t_new, axis=1), v_new)
        keys = jnp.concatenate([k_old, k_new], axis=1).astype(BF16)
        vals = jnp.concatenate([jnp.concatenate([v_old, v_new], axis=1).astype(BF16), ones_rows], axis=0)
        q_rows = []
        for h in range(n_heads):
            j = h // group
            src = qf_ref[pl.ds(r0, t_new), (h // 2) * LANES:(h // 2 + 1) * LANES]
            if h % 2 != j % 2:
                src = pltpu.roll(src, HEAD_DIM, axis=1)
            half = jnp.where(low, src, 0.0) if j % 2 == 0 else jnp.where(low, 0.0, src)
            tiles = [zero_tile] * (kw // LANES)
            tiles[j // 2] = half
            q_rows.append(jnp.concatenate(tiles, axis=1))
        qb = jnp.concatenate(q_rows, axis=0).astype(BF16)
        s = jnp.dot(qb, keys, preferred_element_type=F32)
        s = jnp.where(ok, s, NEG)
        m = jnp.maximum(jnp.max(s, axis=-1, keepdims=True), sink)
        p = jnp.exp(s - m).astype(BF16)
        r = lax.dot_general(p, vals, (((1,), (1,)), ((), ())), preferred_element_type=F32)
        denom = r[:, kw:] + jnp.exp(sink - m)
        o = r[:, :kw] / jnp.concatenate([denom] * (kw // LANES), axis=1)
        for pair in range(n_heads // 2):
            j = (2 * pair) // group
            lanes = slice((j // 2) * LANES, (j // 2 + 1) * LANES)
            a = o[2 * pair * t_new:(2 * pair + 1) * t_new, lanes]
            b = o[(2 * pair + 1) * t_new:(2 * pair + 2) * t_new, lanes]
            if j % 2 == 0:
                b = pltpu.roll(b, HEAD_DIM, axis=1)
            else:
                a = pltpu.roll(a, HEAD_DIM, axis=1)
            of_ref[pl.ds(r0, t_new), pair * LANES:(pair + 1) * LANES] = jnp.where(low, a, b)
        return carry

    lax.fori_loop(0, g_seqs, one_seq, 0, unroll=8)
    o_ref[...] = of_ref[...].astype(o_ref.dtype)


def _attn_sample(sinks, q, k_t, v_t, cache_k_t, cache_v_t, *, row0, t_new, n_kv):
    n_seq, kw, win = cache_k_t.shape
    g_seqs = win // t_new
    aw = q.shape[1]
    assert k_t.shape == (n_seq // g_seqs, kw, win)
    tile0 = row0 // (g_seqs * t_new)
    tok = pl.BlockSpec((g_seqs * t_new, aw), lambda i: (i, 0))
    q_tok = pl.BlockSpec((g_seqs * t_new, aw), lambda i: (tile0 + i, 0))
    new = pl.BlockSpec((1, kw, win), lambda i: (i, 0, 0))
    cache = pl.BlockSpec((g_seqs, kw, win), lambda i: (i, 0, 0))
    return pl.pallas_call(
        functools.partial(_attn_sample_kernel, n_kv=n_kv, t_new=t_new),
        grid=(n_seq // g_seqs,),
        in_specs=[pl.BlockSpec(memory_space=pltpu.SMEM), q_tok, new, new, cache, cache],
        out_specs=[tok, cache, cache],
        out_shape=[
            jax.ShapeDtypeStruct((n_seq * t_new, aw), BF16),
            jax.ShapeDtypeStruct(cache_k_t.shape, F32),
            jax.ShapeDtypeStruct(cache_v_t.shape, F32),
        ],
        scratch_shapes=[pltpu.VMEM((g_seqs * t_new, aw), F32), pltpu.VMEM((g_seqs * t_new, aw), F32)],
        compiler_params=_params("parallel"),
        name="attn_sample",
    )(sinks, q, k_t, v_t, cache_k_t, cache_v_t)


def _ln_silu(y, g, b):
    yc = y - jnp.mean(y, axis=-1, keepdims=True)
    var = jnp.mean(yc * yc, axis=-1, keepdims=True)
    return jax.nn.silu(yc * lax.rsqrt(var + LN_EPS) * g + b)


def _conv_prompt_kernel(cur_ref, halo_ref, w_ref, b_ref, g_ref, beta_ref, o_ref, xc_ref, y_ref, *, width, rows_per, lanes_per):
    tc, ch = cur_ref.shape
    halo = halo_ref.shape[0]
    i = pl.program_id(1)
    xc_ref[:halo, :] = jnp.where(i == 0, 0.0, halo_ref[...])
    xc_ref[halo:, :] = cur_ref[...]
    base = halo - (width - 1)

    def col_chunk(cc, carry):
        l0 = pl.multiple_of(cc * lanes_per, lanes_per)
        cols = pl.ds(l0, lanes_per)
        for rr in range(tc // rows_per):
            acc = jnp.broadcast_to(b_ref[:, cols], (rows_per, lanes_per))
            for phase in range(SUBLANES):
                n_win = rows_per + (SUBLANES if phase else 0)
                part = None
                for j in range(width):
                    if (base + j) % SUBLANES != phase:
                        continue
                    start = rr * rows_per + ((base + j) // SUBLANES) * SUBLANES
                    term = w_ref[pl.ds(j, 1), cols] * xc_ref[pl.ds(start, n_win), cols]
                    part = term if part is None else part + term
                if part is None:
                    continue
                if phase:
                    part = pltpu.roll(part, n_win - phase, axis=0)
                acc = acc + part[:rows_per]
            y_ref[pl.ds(rr * rows_per, rows_per), cols] = acc
        return carry

    lax.fori_loop(0, ch // lanes_per, col_chunk, 0)
    o_ref[...] = _ln_silu(y_ref[...], g_ref[...], beta_ref[...]).astype(o_ref.dtype)


def _conv_prompt(glu, w, b, g, beta, *, batch, seq, tc=1024, halo=32):
    ch = glu.shape[1]
    width = w.shape[0]
    nt = seq // tc
    per = tc // halo
    cur = pl.BlockSpec((tc, ch), lambda bi, i: (bi * nt + i, 0))
    prev = pl.BlockSpec((halo, ch), lambda bi, i: (jnp.maximum((bi * nt + i) * per - 1, 0), 0))
    kern = functools.partial(_conv_prompt_kernel, width=width, rows_per=128, lanes_per=LANES)
    return pl.pallas_call(
        kern,
        grid=(batch, nt),
        in_specs=[cur, prev, _resident(w.shape), _resident((1, ch)), _resident((1, ch)), _resident((1, ch))],
        out_specs=cur,
        out_shape=jax.ShapeDtypeStruct((batch * seq, ch), BF16),
        scratch_shapes=[pltpu.VMEM((halo + tc, ch), F32), pltpu.VMEM((tc, ch), F32)],
        compiler_params=_params("parallel", "arbitrary"),
        name="conv_prompt",
    )(glu, glu, w, b, g, beta)


def _conv_sample_kernel(new_ref, st_ref, w_ref, b_ref, g_ref, beta_ref, o_ref, ns_ref, y_ref, tok_ref, *, t_new):
    n_past, n_seq, wide = st_ref.shape
    lanes = tok_ref.shape[1]
    n_sub = wide // lanes
    width = w_ref.shape[0]
    c = pl.program_id(0)
    for k in range(n_sub):
        cols = slice(k * lanes, (k + 1) * lanes)
        tok_ref[...] = new_ref[:, cols]
        new = [tok_ref[pl.ds(t, n_seq, stride=t_new), :] for t in range(t_new)]
        row = lambda r: st_ref[r, :, cols] if r < n_past else new[r - n_past]
        for t in range(t_new):
            acc = jnp.broadcast_to(b_ref[:, cols], (n_seq, lanes))
            for j in range(width):
                acc = acc + w_ref[pl.ds(j, 1), cols] * row(t + j)
            y_ref[c * n_sub + k, pl.ds(t, n_seq, stride=t_new), :] = acc
        for r in range(n_past):
            ns_ref[r, :, cols] = row(r + t_new)

    @pl.when(c == pl.num_programs(0) - 1)
    def _():
        n_chunks = y_ref.shape[0]
        ch = n_chunks * lanes
        mean = jnp.sum(sum(y_ref[k] for k in range(n_chunks)), axis=-1, keepdims=True) / ch
        var = jnp.sum(sum((y_ref[k] - mean) ** 2 for k in range(n_chunks)), axis=-1, keepdims=True) / ch
        inv = lax.rsqrt(var + LN_EPS)
        for k in range(n_chunks):
            cols = slice(k * lanes, (k + 1) * lanes)
            z = (y_ref[k] - mean) * inv * g_ref[:, cols] + beta_ref[:, cols]
            o_ref[:, cols] = jax.nn.silu(z).astype(o_ref.dtype)


def _conv_sample(glu, state_t, w, b, g, beta, *, row0, t_new, wide=2 * LANES):
    n_past, n_seq, ch = state_t.shape
    n_tok = n_seq * t_new
    assert w.shape[0] == n_past + 1
    chunk = lambda rows: pl.BlockSpec((rows, wide), lambda c: (0, c))
    slabs = pl.BlockSpec((n_past, n_seq, wide), lambda c: (0, 0, c))
    return pl.pallas_call(
        functools.partial(_conv_sample_kernel, t_new=t_new),
        grid=(ch // wide,),
        in_specs=[pl.BlockSpec((n_tok, wide), lambda c: (row0 // n_tok, c)), slabs, chunk(w.shape[0]), chunk(1),
                  _resident((1, ch)), _resident((1, ch))],
        out_specs=[pl.BlockSpec((n_tok, ch), lambda c: (0, 0)), slabs],
        out_shape=[jax.ShapeDtypeStruct((n_tok, ch), BF16), jax.ShapeDtypeStruct(state_t.shape, F32)],
        scratch_shapes=[pltpu.VMEM((ch // LANES, n_tok, LANES), F32), pltpu.VMEM((n_tok, LANES), F32)],
        compiler_params=_params("arbitrary"),
        name="conv_sample",
    )(glu, state_t, w, b, g, beta)


def _out_proj_kernel(xa_ref, xb_ref, aa_ref, ab_ref, ca_ref, cb_ref, w_ref, o_ref, *, tiles_a):
    aw = aa_ref.shape[1]
    attn = _pick_rows(aa_ref, ab_ref, tiles_a)
    conv = _pick_rows(ca_ref, cb_ref, tiles_a)
    o_ref[...] = (_pick_rows(xa_ref, xb_ref, tiles_a)
                  + jnp.dot(attn, w_ref[:aw, :].astype(BF16), preferred_element_type=F32)
                  + jnp.dot(conv, w_ref[aw:, :].astype(BF16), preferred_element_type=F32))


def _out_proj(x, attn, conv, w, *, slab, tm=512):
    d = x[0].shape[1]
    n = x[0].shape[0] + x[1].shape[0]
    tiles_a = x[0].shape[0] // tm
    specs = [spec for pair in (x, attn, conv) for spec in _split_rows(tm, pair[0].shape[1], tiles_a)]
    return pl.pallas_call(
        functools.partial(_out_proj_kernel, tiles_a=tiles_a),
        grid=(n // tm,),
        in_specs=specs + [_resident_slab(w.shape, slab)],
        out_specs=pl.BlockSpec((tm, d), lambda i: (i, 0)),
        out_shape=jax.ShapeDtypeStruct((n, d), F32),
        compiler_params=_params("arbitrary"),
        name="out_proj",
    )(*x, *attn, *conv, w)


def _ffn_kernel(x_ref, g_ref, wg_ref, wu_ref, wd_ref, o_ref, *rest):
    h_ref = rest[-1]

    @pl.when(pl.program_id(1) == 0)
    def _():
        x = x_ref[...]
        h_ref[...] = _rms(x, g_ref[...]).astype(BF16)
        o_ref[...] = x

    weights = [w_ref[...].astype(BF16) for w_ref in (wg_ref, wu_ref, wd_ref)]
    for w_out, w in zip(rest[:-1], weights):
        w_out[...] = w
    wg, wu, wd = weights
    h = h_ref[...]
    a = jnp.dot(h, wg, preferred_element_type=F32)
    u = jnp.dot(h, wu, preferred_element_type=F32)
    act = (jax.nn.silu(a) * u).astype(BF16)
    o_ref[...] += jnp.dot(act, wd, preferred_element_type=F32)


def _ffn(x, g, wg, wu, wd, *, layer=None, emit_bf16=False, row0=0, n_rows=None, tm=1024, tf=256):
    d = x.shape[1]
    n = x.shape[0] if n_rows is None else n_rows
    tile0 = row0 // tm
    dff = wg.shape[-1]
    nt, nf = n // tm, dff // tf
    if layer is None:
        w_in_specs = [pl.BlockSpec((d, tf), lambda i, f: (0, f)), pl.BlockSpec((d, tf), lambda i, f: (0, f)),
                      pl.BlockSpec((tf, d), lambda i, f: (f, 0))]
    else:
        w_in_specs = [pl.BlockSpec((None, d, tf), lambda i, f: (layer, 0, f)),
                      pl.BlockSpec((None, d, tf), lambda i, f: (layer, 0, f)),
                      pl.BlockSpec((None, tf, d), lambda i, f: (layer, f, 0))]
    rows = pl.BlockSpec((tm, d), lambda i, f: (i, 0))
    out_specs, out_shape = [rows], [jax.ShapeDtypeStruct((n, d), F32)]
    if emit_bf16:
        assert nt == 1, "one row tile, so every bf16 weight tile is written exactly once"
        out_specs += [pl.BlockSpec((d, tf), lambda i, f: (0, f)), pl.BlockSpec((d, tf), lambda i, f: (0, f)),
                      pl.BlockSpec((tf, d), lambda i, f: (f, 0))]
        out_shape += [jax.ShapeDtypeStruct((d, dff), BF16), jax.ShapeDtypeStruct((d, dff), BF16),
                      jax.ShapeDtypeStruct((dff, d), BF16)]
    mode = dict(pipeline_mode=pl.Buffered(1)) if nt == 1 else {}
    x_spec = pl.BlockSpec((tm, d), lambda i, f: (tile0 + i, 0), **mode)
    outs = pl.pallas_call(
        _ffn_kernel,
        grid=(nt, nf),
        in_specs=[x_spec, _resident((1, d))] + w_in_specs,
        out_specs=out_specs,
        out_shape=out_shape,
        scratch_shapes=[pltpu.VMEM((tm, d), BF16)],
        compiler_params=_params("parallel", "arbitrary"),
        name="ffn",
    )(x, g, wg, wu, wd)
    return outs if emit_bf16 else outs[0]


def _pool_prompt_kernel(cur_ref, halo_ref, g_ref, pw_ref, ps_ref, o_ref, tail_ref, xp_ref):
    tp, d = cur_ref.shape
    halo = halo_ref.shape[0]
    i = pl.program_id(1)
    gain = g_ref[...]
    xp_ref[:halo, :] = jnp.where(i == 0, 0.0, _rms(halo_ref[...], gain))
    xp_ref[halo:, :] = _rms(cur_ref[...], gain)
    tail_ref[0] = xp_ref[tp:, :]
    pg = d // len(POOL_WINDOWS)
    pos = i * tp + lax.broadcasted_iota(jnp.int32, (tp, 1), 0)
    for gi, w in enumerate(POOL_WINDOWS):
        cols = slice(gi * pg, (gi + 1) * pg)
        ext = xp_ref[:, cols]
        h = ext[halo:]
        span = 1
        while span < w:
            ext = ext + pltpu.roll(ext, span, axis=0)
            span *= 2
        inv_count = 1.0 / jnp.minimum(pos + 1, w).astype(F32)
        dpool = (ext[halo:] * inv_count - h).astype(BF16)
        y = jnp.dot(dpool, pw_ref[gi].astype(BF16), preferred_element_type=F32)
        o_ref[:, cols] = cur_ref[:, cols] + y * ps_ref[:, cols]


def _pool_prompt(x, g, pw, ps, *, slab, batch, seq, tp=1024, halo=16):
    assert all(w & (w - 1) == 0 and w <= halo for w in POOL_WINDOWS)
    d = x.shape[1]
    nt = seq // tp
    per = tp // halo
    cur = pl.BlockSpec((tp, d), lambda bi, i: (bi * nt + i, 0))
    prev = pl.BlockSpec((halo, d), lambda bi, i: (jnp.maximum((bi * nt + i) * per - 1, 0), 0))
    return pl.pallas_call(
        _pool_prompt_kernel,
        grid=(batch, nt),
        in_specs=[cur, prev, _resident((1, d)), _resident_slab(pw.shape, slab), _resident((1, d))],
        out_specs=[cur, pl.BlockSpec((1, halo, d), lambda bi, i: (bi, 0, 0))],
        out_shape=[jax.ShapeDtypeStruct((batch * seq, d), F32), jax.ShapeDtypeStruct((batch, halo, d), F32)],
        scratch_shapes=[pltpu.VMEM((halo + tp, d), F32)],
        compiler_params=_params("parallel", "arbitrary"),
        name="pool_prompt",
    )(x, x, g, pw, ps)


def _pool_sample_kernel(x_ref, st_ref, g_ref, pw_ref, ps_ref, o_ref, ns_ref, inv_ref, y_ref, tok_ref, *, t_new, start_pos):
    n_past, n_seq, pg = st_ref.shape
    lanes = tok_ref.shape[1]
    phase, gi = pl.program_id(0), pl.program_id(1)
    n_tok = x_ref.shape[0]
    token_rows = lambda t: pl.ds(t, n_seq, stride=t_new)
    group_cols = lambda k: pl.ds(pl.multiple_of(gi * pg + k * lanes, lanes), lanes)

    @pl.when(jnp.logical_and(phase == 0, gi == 0))
    def _():
        x = x_ref[...]
        inv = lax.rsqrt(jnp.mean(x * x, axis=-1, keepdims=True) + RMS_EPS)
        inv_ref[...] = jnp.broadcast_to(inv, (n_tok, lanes))

    @pl.when(phase == 0)
    def _():
        chunks = []
        for k in range(pg // lanes):
            cols = slice(k * lanes, (k + 1) * lanes)
            tok_ref[...] = x_ref[:, group_cols(k)] * inv_ref[...]
            gain = g_ref[:, group_cols(k)]
            new = [tok_ref[token_rows(t), :] * gain for t in range(t_new)]
            row = lambda r: st_ref[n_past + r, :, cols] if r < 0 else new[r]
            for r in range(n_past):
                ns_ref[r, :, cols] = row(r + t_new - n_past)
            sums = {1: {r: row(r) for r in range(-n_past, t_new)}}
            span = 1
            while span < max(POOL_WINDOWS):
                prev = sums[span]
                sums[2 * span] = {r: prev[r] + prev[r - span] for r in prev if r - span in prev}
                span *= 2
            diffs = []
            for t in range(t_new):
                pooled = None
                for i, w in enumerate(POOL_WINDOWS):
                    cand = sums[w][t] * (1.0 / min(start_pos + t + 1, w))
                    pooled = cand if pooled is None else jnp.where(gi == i, cand, pooled)
                diffs.append(pooled - new[t])
            chunks.append(jnp.concatenate(diffs, axis=0))
        dp = jnp.concatenate(chunks, axis=1).astype(BF16)
        cols = pl.ds(pl.multiple_of(gi * pg, pg), pg)
        y_ref[:, cols] = jnp.dot(dp, pw_ref[gi].astype(BF16), preferred_element_type=F32) * ps_ref[:, cols]

    @pl.when(phase == 1)
    def _():
        for k in range(pg // lanes):
            y = y_ref[:, group_cols(k)]
            tok_ref[...] = x_ref[:, group_cols(k)]
            for t in range(t_new):
                tok_ref[token_rows(t), :] = tok_ref[token_rows(t), :] + y[t * n_seq:(t + 1) * n_seq]
            o_ref[:, k * lanes:(k + 1) * lanes] = tok_ref[...]


def _pool_sample(x, state_t, g, pw, ps, *, slab, t_new):
    n_past, n_seq, d = state_t.shape
    n_tok = n_seq * t_new
    n_groups = len(POOL_WINDOWS)
    pg = d // n_groups
    assert all(w & (w - 1) == 0 and w <= n_past + 1 for w in POOL_WINDOWS)
    kern = functools.partial(_pool_sample_kernel, t_new=t_new, start_pos=PAST_LEN)
    slabs = pl.BlockSpec((n_past, n_seq, pg), lambda p, gi: (0, 0, gi * (1 - p) + (n_groups - 1) * p))
    return pl.pallas_call(
        kern,
        grid=(2, n_groups),
        in_specs=[_resident((n_tok, d)), slabs, _resident((1, d)), _resident_slab(pw.shape, slab), _resident((1, d))],
        out_specs=[pl.BlockSpec((n_tok, pg), lambda p, gi: (0, gi * p)), slabs],
        out_shape=[jax.ShapeDtypeStruct((n_tok, d), F32), jax.ShapeDtypeStruct(state_t.shape, F32)],
        scratch_shapes=[pltpu.VMEM((n_tok, LANES), F32), pltpu.VMEM((n_tok, d), F32), pltpu.VMEM((n_tok, LANES), F32)],
        compiler_params=_params("arbitrary", "arbitrary"),
        name="pool_sample",
    )(x, state_t, g, pw, ps)


def kernel(x_prompt, x_sample, cache_k, cache_v, state_conv, state_pool, norm_mix, w_in, q_norm, k_norm, sinks,
           conv_w, conv_b, conv_ln_g, conv_ln_b, w_out, pool_w, pool_scale, norm_ffn, w_gate, w_up, w_down):
    batch, seq, d = x_prompt.shape
    n_seq, t_new, _ = x_sample.shape
    n_kv = cache_k.shape[3]
    kv_w = n_kv * HEAD_DIM
    conv_ch = state_conv.shape[-1]
    attn_w = w_in.shape[-1] - 2 * kv_w - 2 * conv_ch
    n_prompt = batch * seq
    row = lambda a: a.reshape(1, -1)

    xp = x_prompt.reshape(n_prompt, d)
    xs = x_sample.reshape(n_seq * t_new, d)
    depth = norm_mix.shape[0]
    outs = {}
    for layer in range(depth):
        i = layer // 2
        if layer % 2 == 0:
            reps = MXU_DIM // HEAD_DIM
            q, k, v, glu = _in_proj(
                xp, xs, row(norm_mix[layer]), w_in, row(jnp.tile(q_norm[i], reps)), row(jnp.tile(k_norm[i], reps)),
                slab=i, attn_w=attn_w, kv_w=kv_w, conv_ch=conv_ch)
            attn_p = _attn_prompt(sinks[i], q, k, v, batch=batch, seq=seq, n_kv=n_kv)
            cache_t = lambda a: jnp.transpose(a, (0, 2, 3, 1)).reshape(n_seq, kv_w, WINDOW)
            new_t = lambda a: jnp.transpose(a[n_prompt:].reshape(-1, WINDOW, kv_w), (0, 2, 1))
            attn_s, nk_t, nv_t = _attn_sample(sinks[i], q, new_t(k), new_t(v), cache_t(cache_k[i]),
                                              cache_t(cache_v[i]), row0=n_prompt, t_new=t_new, n_kv=n_kv)
            uncache_t = lambda a: jnp.transpose(a.reshape(n_seq, n_kv, HEAD_DIM, WINDOW), (0, 3, 1, 2))
            ln = (row(conv_b[i]), row(conv_ln_g[i]), row(conv_ln_b[i]))
            conv_p = _conv_prompt(glu, conv_w[i], *ln, batch=batch, seq=seq)
            conv_s, nc_t = _conv_sample(glu, jnp.transpose(state_conv[i], (1, 0, 2)), conv_w[i], *ln,
                                        row0=n_prompt, t_new=t_new)
            nc_s = jnp.transpose(nc_t, (1, 0, 2))
            mixed = _out_proj((xp, xs), (attn_p, attn_s), (conv_p, conv_s), w_out, slab=i)
            ffn_in = ((mixed, 0, n_prompt), (mixed, n_prompt, n_seq * t_new))
            last = lambda a, n: jnp.stack([a[(b + 1) * seq - n:(b + 1) * seq] for b in range(batch)])
            outs.setdefault("kp", []).append(last(k, WINDOW).reshape(batch, WINDOW, n_kv, HEAD_DIM))
            outs.setdefault("vp", []).append(last(v, WINDOW).reshape(batch, WINDOW, n_kv, HEAD_DIM))
            outs.setdefault("cp", []).append(last(glu, conv_w.shape[1] - 1))
            outs.setdefault("ks", []).append(uncache_t(nk_t))
            outs.setdefault("vs", []).append(uncache_t(nv_t))
            outs.setdefault("cs", []).append(nc_s)
        else:
            args = (row(norm_mix[layer]), pool_w, row(pool_scale[i]))
            n_past = state_pool.shape[2]
            xp, tail = _pool_prompt(xp, *args, slab=i, batch=batch, seq=seq)
            xs, np_t = _pool_sample(xs, jnp.transpose(state_pool[i], (1, 0, 2)), *args, slab=i, t_new=t_new)
            np_s = jnp.transpose(np_t, (1, 0, 2))
            ffn_in = ((xp, 0, n_prompt), (xs, 0, n_seq * t_new))
            outs.setdefault("pp", []).append(tail[:, -n_past:])
            outs.setdefault("ps", []).append(np_s)
        g_ffn = row(norm_ffn[layer])
        (rows_p, row0_p, n_p), (rows_s, row0_s, n_s) = ffn_in
        xs, wg16, wu16, wd16 = _ffn(rows_s, g_ffn, w_gate, w_up, w_down, layer=layer, emit_bf16=True,
                                    row0=row0_s, n_rows=n_s)
        xp = _ffn(rows_p, g_ffn, wg16, wu16, wd16, row0=row0_p, n_rows=n_p, tf=512)
    st = lambda name: jnp.stack(outs[name])
    return (xp.reshape(batch, seq, d), xs.reshape(n_seq, t_new, d),
            st("kp"), st("vp"), st("cp"), st("pp"), st("ks"), st("vs"), st("cs"), st("ps"))
```

```python
import functools

import jax
import jax.numpy as jnp
from jax import lax
from jax.experimental import pallas as pl
from jax.experimental.pallas import tpu as pltpu

F32 = jnp.float32
BF16 = jnp.bfloat16

HEAD_DIM = 64
WINDOW = 128
PAST_LEN = 8192
POOL_WINDOWS = (2, 4, 8, 16)
RMS_EPS = 1e-6
LN_EPS = 1e-5
NEG = -1e30

LANES = 128
SUBLANES = 8
MXU_DIM = 256
VMEM_LIMIT = 56 * 1024 * 1024


def _params(*semantics):
    return pltpu.CompilerParams(dimension_semantics=semantics, vmem_limit_bytes=VMEM_LIMIT)


def _rms(x, g):
    ms = jnp.mean(x * x, axis=-1, keepdims=True)
    return x * lax.rsqrt(ms + RMS_EPS) * g


def _resident(shape):
    zeros = (0,) * len(shape)
    return pl.BlockSpec(shape, lambda *_: zeros, pipeline_mode=pl.Buffered(1))


def _resident_slab(shape, slab):
    index = (slab,) + (0,) * (len(shape) - 1)
    return pl.BlockSpec((None,) + tuple(shape[1:]), lambda *_: index, pipeline_mode=pl.Buffered(1))


def _split_rows(tm, width, tiles_a, single_buffer_b=False):
    spec_a = pl.BlockSpec((tm, width), lambda i: (jnp.minimum(i, tiles_a - 1), 0))
    mode = dict(pipeline_mode=pl.Buffered(1)) if single_buffer_b else {}
    spec_b = pl.BlockSpec((tm, width), lambda i: (jnp.maximum(i - tiles_a, 0), 0), **mode)
    return [spec_a, spec_b]


def _pick_rows(a_ref, b_ref, tiles_a):
    return jnp.where(pl.program_id(0) < tiles_a, a_ref[...], b_ref[...])


def _in_proj_kernel(xa_ref, xb_ref, g_ref, w_ref, qg_ref, kg_ref, q_ref, k_ref, v_ref, glu_ref, *, tiles_a, attn_w, kv_w, conv_ch):
    h = _rms(_pick_rows(xa_ref, xb_ref, tiles_a), g_ref[...]).astype(BF16)
    row = lax.broadcasted_iota(jnp.int32, (MXU_DIM, MXU_DIM), 0) // HEAD_DIM
    col = lax.broadcasted_iota(jnp.int32, (MXU_DIM, MXU_DIM), 1) // HEAD_DIM
    pmat = jnp.where(row == col, 1.0 / HEAD_DIM, 0.0).astype(BF16)

    def proj(c0):
        return jnp.dot(h, w_ref[:, c0:c0 + MXU_DIM].astype(BF16), preferred_element_type=F32)

    def head_rms(z):
        sq = z * z
        hi = sq.astype(BF16)
        lo = (sq - hi.astype(F32)).astype(BF16)
        ms = jnp.dot(hi, pmat, preferred_element_type=F32) + jnp.dot(lo, pmat, preferred_element_type=F32)
        return z * lax.rsqrt(ms + RMS_EPS)

    scale = HEAD_DIM ** -0.5

    def put_q(c0, z):
        q_ref[:, c0:c0 + MXU_DIM] = (head_rms(z) * qg_ref[...] * scale).astype(q_ref.dtype)

    def put_k(c0, z):
        k_ref[:, c0:c0 + MXU_DIM] = head_rms(z) * kg_ref[...]

    def put_v(c0, z):
        v_ref[:, c0:c0 + MXU_DIM] = z

    def put_glu(c0, a, gate):
        glu_ref[:, c0:c0 + MXU_DIM] = a * jax.nn.sigmoid(gate)

    units = [((c0,), functools.partial(put_q, c0)) for c0 in range(0, attn_w, MXU_DIM)]
    units += [((attn_w + c0,), functools.partial(put_k, c0)) for c0 in range(0, kv_w, MXU_DIM)]
    units += [((attn_w + kv_w + c0,), functools.partial(put_v, c0)) for c0 in range(0, kv_w, MXU_DIM)]
    units += [((attn_w + 2 * kv_w + c0, attn_w + 2 * kv_w + conv_ch + c0), functools.partial(put_glu, c0))
              for c0 in range(0, conv_ch, MXU_DIM)]
    pending = None
    for cols, epilogue in units:
        zs = [proj(c0) for c0 in cols]
        if pending is not None:
            pending[0](*pending[1])
        pending = (epilogue, zs)
    pending[0](*pending[1])


def _in_proj(xa, xb, g, w, qg, kg, *, slab, attn_w, kv_w, conv_ch, tm=512):
    d = xa.shape[1]
    n = xa.shape[0] + xb.shape[0]
    tiles_a = xa.shape[0] // tm
    kern = functools.partial(_in_proj_kernel, tiles_a=tiles_a, attn_w=attn_w, kv_w=kv_w, conv_ch=conv_ch)
    rows = lambda width: pl.BlockSpec((tm, width), lambda i: (i, 0))
    return pl.pallas_call(
        kern,
        grid=(n // tm,),
        in_specs=_split_rows(tm, d, tiles_a, single_buffer_b=True) + [
            _resident((1, d)), _resident_slab(w.shape, slab), _resident((1, MXU_DIM)), _resident((1, MXU_DIM))],
        out_specs=[rows(attn_w), rows(kv_w), rows(kv_w), rows(conv_ch)],
        out_shape=[
            jax.ShapeDtypeStruct((n, attn_w), BF16),
            jax.ShapeDtypeStruct((n, kv_w), F32),
            jax.ShapeDtypeStruct((n, kv_w), F32),
            jax.ShapeDtypeStruct((n, conv_ch), F32),
        ],
        compiler_params=_params("arbitrary"),
        name="in_proj",
    )(xa, xb, g, w, qg, kg)


def _pair_blocks(chunk, rolled, odd):
    lane = lax.broadcasted_iota(jnp.int32, chunk.shape, 1)
    low = lane < HEAD_DIM
    if odd:
        top = jnp.where(low, rolled, 0.0)
        bot = jnp.where(low, 0.0, chunk)
    else:
        top = jnp.where(low, chunk, 0.0)
        bot = jnp.where(low, 0.0, rolled)
    return jnp.concatenate([top, bot], axis=0).astype(BF16)


def _softmax_pv(s, ok, sink_a, sink_b, vb_ones):
    half = s.shape[1] // 2
    s = jnp.where(ok, s, NEG)
    sa, sb = s[:, :half], s[:, half:]
    ma = jnp.maximum(jnp.max(sa, axis=-1, keepdims=True), sink_a)
    mb = jnp.maximum(jnp.max(sb, axis=-1, keepdims=True), sink_b)
    p = jnp.concatenate([jnp.exp(sa - ma), jnp.exp(sb - mb)], axis=1).astype(BF16)
    r = jnp.dot(p, vb_ones, preferred_element_type=F32)
    lane = lax.broadcasted_iota(jnp.int32, (s.shape[0], LANES), 1)
    sink_term = jnp.where(lane < HEAD_DIM, jnp.exp(sink_a - ma), jnp.exp(sink_b - mb))
    return r[:, :LANES] / (r[:, LANES:] + sink_term)


def _attend(q2, kb, vb, ok, sinks_ref, h0, rows):
    s = lax.dot_general(q2, kb, (((1,), (1,)), ((), ())), preferred_element_type=F32)
    row = lax.broadcasted_iota(jnp.int32, (2 * rows, 1), 0)
    first = row < rows
    sink_a = jnp.where(first, sinks_ref[h0], sinks_ref[h0 + 2])
    sink_b = jnp.where(first, sinks_ref[h0 + 1], sinks_ref[h0 + 3])
    return _softmax_pv(s, ok, sink_a, sink_b, vb)


def _attn_prompt_kernel(sinks_ref, q_ref, kp_ref, kc_ref, vp_ref, vc_ref, o_ref, *, n_kv):
    blk = kp_ref.shape[0]
    i = pl.program_id(1)
    shape = (2 * blk, 4 * blk)
    r = lax.broadcasted_iota(jnp.int32, shape, 0) % blk
    c = lax.broadcasted_iota(jnp.int32, shape, 1) % (2 * blk)
    band = jnp.logical_and(c > r, c <= r + blk)
    band_first = jnp.logical_and(band, jnp.logical_or(c >= blk, i > 0))
    group = (q_ref.shape[1] // HEAD_DIM) // n_kv
    key_row = lax.broadcasted_iota(jnp.int32, (4 * blk, LANES), 0)
    key_lane = lax.broadcasted_iota(jnp.int32, (4 * blk, LANES), 1)
    ones_blocks = jnp.where((key_row < 2 * blk) == (key_lane < HEAD_DIM), 1.0, 0.0).astype(BF16)
    for sub in range(q_ref.shape[0] // blk):
        rows = slice(sub * blk, (sub + 1) * blk)
        if sub == 0:
            k_all = jnp.concatenate([kp_ref[...], kc_ref[rows, :]], axis=0)
            v_all = jnp.concatenate([vp_ref[...], vc_ref[rows, :]], axis=0)
        else:
            k_all = kc_ref[(sub - 1) * blk:(sub + 1) * blk, :]
            v_all = vc_ref[(sub - 1) * blk:(sub + 1) * blk, :]
        ok = band if sub else band_first
        for cch in range(n_kv // 2):
            kc = k_all[:, cch * LANES:(cch + 1) * LANES]
            vc = v_all[:, cch * LANES:(cch + 1) * LANES]
            kr = pltpu.roll(kc, HEAD_DIM, axis=1)
            vr = pltpu.roll(vc, HEAD_DIM, axis=1)
            for odd in range(2):
                j = 2 * cch + odd
                kb = _pair_blocks(kc, kr, odd)
                vb = jnp.concatenate([_pair_blocks(vc, vr, odd), ones_blocks], axis=1)
                h0 = group * j
                c0 = h0 * HEAD_DIM
                q2 = jnp.concatenate([q_ref[rows, c0:c0 + LANES], q_ref[rows, c0 + LANES:c0 + 2 * LANES]], axis=0)
                o = _attend(q2, kb, vb, ok, sinks_ref, h0, blk).astype(o_ref.dtype)
                o_ref[rows, c0:c0 + LANES] = o[:blk]
                o_ref[rows, c0 + LANES:c0 + 2 * LANES] = o[blk:]


def _attn_prompt(sinks, q, k, v, *, batch, seq, n_kv, blocks_per_step=8):
    blk = WINDOW
    tq = blocks_per_step * blk
    nb = seq // tq
    aw, kw = q.shape[1], k.shape[1]
    cur = lambda w: pl.BlockSpec((tq, w), lambda b, i: (b * nb + i, 0))
    prev = lambda w: pl.BlockSpec(
        (blk, w), lambda b, i: ((b * nb + i) * blocks_per_step - jnp.minimum(i, 1), 0))
    return pl.pallas_call(
        functools.partial(_attn_prompt_kernel, n_kv=n_kv),
        grid=(batch, nb),
        in_specs=[pl.BlockSpec(memory_space=pltpu.SMEM), cur(aw), prev(kw), cur(kw), prev(kw), cur(kw)],
        out_specs=cur(aw),
        out_shape=jax.ShapeDtypeStruct((batch * seq, aw), BF16),
        compiler_params=_params("parallel", "arbitrary"),
        name="attn_prompt",
    )(sinks, q, k, k, v, v)


def _attn_sample_kernel(sinks_ref, q_ref, kn_ref, vn_ref, ck_ref, cv_ref, o_ref, nk_ref, nv_ref, qf_ref, of_ref, *, n_kv, t_new):
    g_seqs, kw, win = ck_ref.shape
    n_heads = q_ref.shape[1] // HEAD_DIM
    group = n_heads // n_kv
    n_rows = n_heads * t_new
    qf_ref[...] = q_ref[...].astype(F32)
    t = lax.broadcasted_iota(jnp.int32, (n_rows, 2 * win), 0) % t_new
    c = lax.broadcasted_iota(jnp.int32, (n_rows, 2 * win), 1)
    ok = jnp.logical_or(jnp.logical_and(c < win, c > t),
                        jnp.logical_and(c >= 2 * win - t_new, c - (2 * win - t_new) <= t))
    sink = jnp.concatenate([jnp.full((t_new, 1), sinks_ref[h], F32) for h in range(n_heads)], axis=0)
    low = lax.broadcasted_iota(jnp.int32, (t_new, LANES), 1) < HEAD_DIM
    zero_tile = jnp.zeros((t_new, LANES), F32)
    keep_old = lax.broadcasted_iota(jnp.int32, (kw, win), 1) < win - t_new
    ones_rows = jnp.ones((LANES, 2 * win), BF16)
    kn_all = kn_ref[0]
    vn_all = vn_ref[0]

    def one_seq(g, carry):
        r0 = pl.multiple_of(g * t_new, t_new)
        k_old = ck_ref[g]
        v_old = cv_ref[g]
        shift = win - t_new * (g + 1)
        k_new = pltpu.roll(kn_all, shift, axis=1)
        v_new = pltpu.roll(vn_all, shift, axis=1)
        nk_ref[g] = jnp.where(keep_old, pltpu.roll(k_old, win - t_new, axis=1), k_new)
        nv_ref[g] = jnp.where(keep_old, pltpu.roll(v_old, win - t_new, axis=1), v_new)
        keys = jnp.concatenate([k_old, k_new], axis=1).astype(BF16)
        vals = jnp.concatenate([jnp.concatenate([v_old, v_new], axis=1).astype(BF16), ones_rows], axis=0)
        q_rows = []
        for h in range(n_heads):
            j = h // group
            src = qf_ref[pl.ds(r0, t_new), (h // 2) * LANES:(h // 2 + 1) * LANES]
            if h % 2 != j % 2:
                src = pltpu.roll(src, HEAD_DIM, axis=1)
            half = jnp.where(low, src, 0.0) if j % 2 == 0 else jnp.where(low, 0.0, src)
            tiles = [zero_tile] * (kw // LANES)
            tiles[j // 2] = half
            q_rows.append(jnp.concatenate(tiles, axis=1))
        qb = jnp.concatenate(q_rows, axis=0).astype(BF16)
        s = jnp.dot(qb, keys, preferred_element_type=F32)
        s = jnp.where(ok, s, NEG)
        m = jnp.maximum(jnp.max(s, axis=-1, keepdims=True), sink)
        p = jnp.exp(s - m).astype(BF16)
        r = lax.dot_general(p, vals, (((1,), (1,)), ((), ())), preferred_element_type=F32)
        denom = r[:, kw:] + jnp.exp(sink - m)
        o = r[:, :kw] / jnp.concatenate([denom] * (kw // LANES), axis=1)
        for pair in range(n_heads // 2):
            j = (2 * pair) // group
            lanes = slice((j // 2) * LANES, (j // 2 + 1) * LANES)
            a = o[2 * pair * t_new:(2 * pair + 1) * t_new, lanes]
            b = o[(2 * pair + 1) * t_new:(2 * pair + 2) * t_new, lanes]
            if j % 2 == 0:
                b = pltpu.roll(b, HEAD_DIM, axis=1)
            else:
                a = pltpu.roll(a, HEAD_DIM, axis=1)
            of_ref[pl.ds(r0, t_new), pair * LANES:(pair + 1) * LANES] = jnp.where(low, a, b)
        return carry

    lax.fori_loop(0, g_seqs, one_seq, 0, unroll=8)
    o_ref[...] = of_ref[...].astype(o_ref.dtype)


def _attn_sample(sinks, q, k_t, v_t, cache_k_t, cache_v_t, *, row0, t_new, n_kv):
    n_seq, kw, win = cache_k_t.shape
    g_seqs = win // t_new
    aw = q.shape[1]
    assert k_t.shape == (n_seq // g_seqs, kw, win)
    tile0 = row0 // (g_seqs * t_new)
    tok = pl.BlockSpec((g_seqs * t_new, aw), lambda i: (i, 0))
    q_tok = pl.BlockSpec((g_seqs * t_new, aw), lambda i: (tile0 + i, 0))
    new = pl.BlockSpec((1, kw, win), lambda i: (i, 0, 0))
    cache = pl.BlockSpec((g_seqs, kw, win), lambda i: (i, 0, 0))
    return pl.pallas_call(
        functools.partial(_attn_sample_kernel, n_kv=n_kv, t_new=t_new),
        grid=(n_seq // g_seqs,),
        in_specs=[pl.BlockSpec(memory_space=pltpu.SMEM), q_tok, new, new, cache, cache],
        out_specs=[tok, cache, cache],
        out_shape=[
            jax.ShapeDtypeStruct((n_seq * t_new, aw), BF16),
            jax.ShapeDtypeStruct(cache_k_t.shape, F32),
            jax.ShapeDtypeStruct(cache_v_t.shape, F32),
        ],
        scratch_shapes=[pltpu.VMEM((g_seqs * t_new, aw), F32), pltpu.VMEM((g_seqs * t_new, aw), F32)],
        compiler_params=_params("parallel"),
        name="attn_sample",
    )(sinks, q, k_t, v_t, cache_k_t, cache_v_t)


def _ln_silu(y, g, b):
    yc = y - jnp.mean(y, axis=-1, keepdims=True)
    var = jnp.mean(yc * yc, axis=-1, keepdims=True)
    return jax.nn.silu(yc * lax.rsqrt(var + LN_EPS) * g + b)


def _conv_prompt_kernel(cur_ref, halo_ref, w_ref, b_ref, o_ref, xc_ref, *, width, rows_per, lanes_per):
    tc, ch = cur_ref.shape
    halo = halo_ref.shape[0]
    i = pl.program_id(1)
    xc_ref[:halo, :] = jnp.where(i == 0, 0.0, halo_ref[...])
    xc_ref[halo:, :] = cur_ref[...]
    base = halo - (width - 1)

    def col_chunk(cc, carry):
        l0 = pl.multiple_of(cc * lanes_per, lanes_per)
        cols = pl.ds(l0, lanes_per)
        for rr in range(tc // rows_per):
            acc = jnp.broadcast_to(b_ref[:, cols], (rows_per, lanes_per))
            for phase in range(SUBLANES):
                n_win = rows_per + (SUBLANES if phase else 0)
                part = None
                for j in range(width):
                    if (base + j) % SUBLANES != phase:
                        continue
                    start = rr * rows_per + ((base + j) // SUBLANES) * SUBLANES
                    term = w_ref[pl.ds(j, 1), cols] * xc_ref[pl.ds(start, n_win), cols]
                    part = term if part is None else part + term
                if part is None:
                    continue
                if phase:
                    part = pltpu.roll(part, n_win - phase, axis=0)
                acc = acc + part[:rows_per]
            o_ref[pl.ds(rr * rows_per, rows_per), cols] = acc
        return carry

    lax.fori_loop(0, ch // lanes_per, col_chunk, 0)


def _conv_prompt(glu, w, b, *, batch, seq, tc=1024, halo=32):
    ch = glu.shape[1]
    width = w.shape[0]
    nt = seq // tc
    per = tc // halo
    cur = pl.BlockSpec((tc, ch), lambda bi, i: (bi * nt + i, 0))
    prev = pl.BlockSpec((halo, ch), lambda bi, i: (jnp.maximum((bi * nt + i) * per - 1, 0), 0))
    kern = functools.partial(_conv_prompt_kernel, width=width, rows_per=128, lanes_per=LANES)
    return pl.pallas_call(
        kern,
        grid=(batch, nt),
        in_specs=[cur, prev, _resident(w.shape), _resident((1, ch))],
        out_specs=cur,
        out_shape=jax.ShapeDtypeStruct((batch * seq, ch), F32),
        scratch_shapes=[pltpu.VMEM((halo + tc, ch), F32)],
        compiler_params=_params("parallel", "arbitrary"),
        name="conv_prompt",
    )(glu, glu, w, b)


def _conv_sample_kernel(new_ref, st_ref, w_ref, b_ref, o_ref, ns_ref, tok_ref, *, t_new):
    n_past, n_seq, wide = st_ref.shape
    lanes = tok_ref.shape[1]
    width = w_ref.shape[0]
    for k in range(wide // lanes):
        cols = slice(k * lanes, (k + 1) * lanes)
        tok_ref[...] = new_ref[:, cols]
        new = [tok_ref[pl.ds(t, n_seq, stride=t_new), :] for t in range(t_new)]
        row = lambda r: st_ref[r, :, cols] if r < n_past else new[r - n_past]
        for t in range(t_new):
            acc = jnp.broadcast_to(b_ref[:, cols], (n_seq, lanes))
            for j in range(width):
                acc = acc + w_ref[pl.ds(j, 1), cols] * row(t + j)
            tok_ref[pl.ds(t, n_seq, stride=t_new), :] = acc
        o_ref[:, cols] = tok_ref[...]
        for r in range(n_past):
            ns_ref[r, :, cols] = row(r + t_new)


def _conv_sample(glu, state_t, w, b, *, row0, t_new, wide=2 * LANES):
    n_past, n_seq, ch = state_t.shape
    n_tok = n_seq * t_new
    assert w.shape[0] == n_past + 1
    chunk = lambda rows: pl.BlockSpec((rows, wide), lambda c: (0, c))
    slabs = pl.BlockSpec((n_past, n_seq, wide), lambda c: (0, 0, c))
    return pl.pallas_call(
        functools.partial(_conv_sample_kernel, t_new=t_new),
        grid=(ch // wide,),
        in_specs=[pl.BlockSpec((n_tok, wide), lambda c: (row0 // n_tok, c)), slabs, chunk(w.shape[0]), chunk(1)],
        out_specs=[chunk(n_tok), slabs],
        out_shape=[jax.ShapeDtypeStruct((n_tok, ch), F32), jax.ShapeDtypeStruct(state_t.shape, F32)],
        scratch_shapes=[pltpu.VMEM((n_tok, LANES), F32)],
        compiler_params=_params("parallel"),
        name="conv_sample",
    )(glu, state_t, w, b)


def _out_proj_kernel(xa_ref, xb_ref, aa_ref, ab_ref, ca_ref, cb_ref, g_ref, beta_ref, w_ref, o_ref, *, tiles_a):
    aw = aa_ref.shape[1]
    attn = _pick_rows(aa_ref, ab_ref, tiles_a)
    conv = _ln_silu(_pick_rows(ca_ref, cb_ref, tiles_a), g_ref[...], beta_ref[...]).astype(BF16)
    o_ref[...] = (_pick_rows(xa_ref, xb_ref, tiles_a)
                  + jnp.dot(attn, w_ref[:aw, :].astype(BF16), preferred_element_type=F32)
                  + jnp.dot(conv, w_ref[aw:, :].astype(BF16), preferred_element_type=F32))


def _out_proj(x, attn, conv, ln_g, ln_b, w, *, slab, tm=512):
    d = x[0].shape[1]
    n = x[0].shape[0] + x[1].shape[0]
    tiles_a = x[0].shape[0] // tm
    specs = [spec for pair in (x, attn, conv)
             for spec in _split_rows(tm, pair[0].shape[1], tiles_a, single_buffer_b=True)]
    ch = conv[0].shape[1]
    return pl.pallas_call(
        functools.partial(_out_proj_kernel, tiles_a=tiles_a),
        grid=(n // tm,),
        in_specs=specs + [_resident((1, ch)), _resident((1, ch)), _resident_slab(w.shape, slab)],
        out_specs=pl.BlockSpec((tm, d), lambda i: (i, 0)),
        out_shape=jax.ShapeDtypeStruct((n, d), F32),
        compiler_params=_params("arbitrary"),
        name="out_proj",
    )(*x, *attn, *conv, ln_g, ln_b, w)


def _ffn_kernel(x_ref, g_ref, wg_ref, wu_ref, wd_ref, o_ref, *rest):
    h_ref = rest[-1]

    @pl.when(pl.program_id(1) == 0)
    def _():
        x = x_ref[...]
        h_ref[...] = _rms(x, g_ref[...]).astype(BF16)
        o_ref[...] = x

    weights = [w_ref[...].astype(BF16) for w_ref in (wg_ref, wu_ref, wd_ref)]
    for w_out, w in zip(rest[:-1], weights):
        w_out[...] = w
    wg, wu, wd = weights
    h = h_ref[...]
    a = jnp.dot(h, wg, preferred_element_type=F32)
    u = jnp.dot(h, wu, preferred_element_type=F32)
    act = (jax.nn.silu(a) * u).astype(BF16)
    o_ref[...] += jnp.dot(act, wd, preferred_element_type=F32)


def _ffn(x, g, wg, wu, wd, *, layer=None, emit_bf16=False, row0=0, n_rows=None, tm=1024, tf=256):
    d = x.shape[1]
    n = x.shape[0] if n_rows is None else n_rows
    tile0 = row0 // tm
    dff = wg.shape[-1]
    nt, nf = n // tm, dff // tf
    if layer is None:
        w_in_specs = [pl.BlockSpec((d, tf), lambda i, f: (0, f)), pl.BlockSpec((d, tf), lambda i, f: (0, f)),
                      pl.BlockSpec((tf, d), lambda i, f: (f, 0))]
    else:
        w_in_specs = [pl.BlockSpec((None, d, tf), lambda i, f: (layer, 0, f)),
                      pl.BlockSpec((None, d, tf), lambda i, f: (layer, 0, f)),
                      pl.BlockSpec((None, tf, d), lambda i, f: (layer, f, 0))]
    rows = pl.BlockSpec((tm, d), lambda i, f: (i, 0))
    out_specs, out_shape = [rows], [jax.ShapeDtypeStruct((n, d), F32)]
    if emit_bf16:
        assert nt == 1, "one row tile, so every bf16 weight tile is written exactly once"
        out_specs += [pl.BlockSpec((d, tf), lambda i, f: (0, f)), pl.BlockSpec((d, tf), lambda i, f: (0, f)),
                      pl.BlockSpec((tf, d), lambda i, f: (f, 0))]
        out_shape += [jax.ShapeDtypeStruct((d, dff), BF16), jax.ShapeDtypeStruct((d, dff), BF16),
                      jax.ShapeDtypeStruct((dff, d), BF16)]
    mode = dict(pipeline_mode=pl.Buffered(1)) if nt == 1 else {}
    x_spec = pl.BlockSpec((tm, d), lambda i, f: (tile0 + i, 0), **mode)
    outs = pl.pallas_call(
        _ffn_kernel,
        grid=(nt, nf),
        in_specs=[x_spec, _resident((1, d))] + w_in_specs,
        out_specs=out_specs,
        out_shape=out_shape,
        scratch_shapes=[pltpu.VMEM((tm, d), BF16)],
        compiler_params=_params("parallel", "arbitrary"),
        name="ffn",
    )(x, g, wg, wu, wd)
    return outs if emit_bf16 else outs[0]


def _pool_prompt_kernel(cur_ref, halo_ref, g_ref, pw_ref, ps_ref, o_ref, tail_ref, xp_ref):
    tp, d = cur_ref.shape
    halo = halo_ref.shape[0]
    i = pl.program_id(1)
    gain = g_ref[...]
    xp_ref[:halo, :] = jnp.where(i == 0, 0.0, _rms(halo_ref[...], gain))
    xp_ref[halo:, :] = _rms(cur_ref[...], gain)
    tail_ref[0] = xp_ref[tp:, :]
    pg = d // len(POOL_WINDOWS)
    pos = i * tp + lax.broadcasted_iota(jnp.int32, (tp, 1), 0)
    for gi, w in enumerate(POOL_WINDOWS):
        cols = slice(gi * pg, (gi + 1) * pg)
        ext = xp_ref[:, cols]
        h = ext[halo:]
        span = 1
        while span < w:
            ext = ext + pltpu.roll(ext, span, axis=0)
            span *= 2
        inv_count = 1.0 / jnp.minimum(pos + 1, w).astype(F32)
        dpool = (ext[halo:] * inv_count - h).astype(BF16)
        y = jnp.dot(dpool, pw_ref[gi].astype(BF16), preferred_element_type=F32)
        o_ref[:, cols] = cur_ref[:, cols] + y * ps_ref[:, cols]


def _pool_prompt(x, g, pw, ps, *, slab, batch, seq, tp=1024, halo=16):
    assert all(w & (w - 1) == 0 and w <= halo for w in POOL_WINDOWS)
    d = x.shape[1]
    nt = seq // tp
    per = tp // halo
    cur = pl.BlockSpec((tp, d), lambda bi, i: (bi * nt + i, 0))
    prev = pl.BlockSpec((halo, d), lambda bi, i: (jnp.maximum((bi * nt + i) * per - 1, 0), 0))
    return pl.pallas_call(
        _pool_prompt_kernel,
        grid=(batch, nt),
        in_specs=[cur, prev, _resident((1, d)), _resident_slab(pw.shape, slab), _resident((1, d))],
        out_specs=[cur, pl.BlockSpec((1, halo, d), lambda bi, i: (bi, 0, 0))],
        out_shape=[jax.ShapeDtypeStruct((batch * seq, d), F32), jax.ShapeDtypeStruct((batch, halo, d), F32)],
        scratch_shapes=[pltpu.VMEM((halo + tp, d), F32)],
        compiler_params=_params("parallel", "arbitrary"),
        name="pool_prompt",
    )(x, x, g, pw, ps)


def _pool_sample_kernel(x_ref, st_ref, g_ref, pw_ref, ps_ref, o_ref, ns_ref, inv_ref, y_ref, tok_ref, *, t_new, start_pos):
    n_past, n_seq, pg = st_ref.shape
    lanes = tok_ref.shape[1]
    phase, gi = pl.program_id(0), pl.program_id(1)
    n_tok = x_ref.shape[0]
    token_rows = lambda t: pl.ds(t, n_seq, stride=t_new)
    group_cols = lambda k: pl.ds(pl.multiple_of(gi * pg + k * lanes, lanes), lanes)

    @pl.when(jnp.logical_and(phase == 0, gi == 0))
    def _():
        x = x_ref[...]
        inv = lax.rsqrt(jnp.mean(x * x, axis=-1, keepdims=True) + RMS_EPS)
        inv_ref[...] = jnp.broadcast_to(inv, (n_tok, lanes))

    @pl.when(phase == 0)
    def _():
        chunks = []
        for k in range(pg // lanes):
            cols = slice(k * lanes, (k + 1) * lanes)
            tok_ref[...] = x_ref[:, group_cols(k)] * inv_ref[...]
            gain = g_ref[:, group_cols(k)]
            new = [tok_ref[token_rows(t), :] * gain for t in range(t_new)]
            row = lambda r: st_ref[n_past + r, :, cols] if r < 0 else new[r]
            for r in range(n_past):
                ns_ref[r, :, cols] = row(r + t_new - n_past)
            sums = {1: {r: row(r) for r in range(-n_past, t_new)}}
            span = 1
            while span < max(POOL_WINDOWS):
                prev = sums[span]
                sums[2 * span] = {r: prev[r] + prev[r - span] for r in prev if r - span in prev}
                span *= 2
            diffs = []
            for t in range(t_new):
                pooled = None
                for i, w in enumerate(POOL_WINDOWS):
                    cand = sums[w][t] * (1.0 / min(start_pos + t + 1, w))
                    pooled = cand if pooled is None else jnp.where(gi == i, cand, pooled)
                diffs.append(pooled - new[t])
            chunks.append(jnp.concatenate(diffs, axis=0))
        dp = jnp.concatenate(chunks, axis=1).astype(BF16)
        cols = pl.ds(pl.multiple_of(gi * pg, pg), pg)
        y_ref[:, cols] = jnp.dot(dp, pw_ref[gi].astype(BF16), preferred_element_type=F32) * ps_ref[:, cols]

    @pl.when(phase == 1)
    def _():
        for k in range(pg // lanes):
            y = y_ref[:, group_cols(k)]
            tok_ref[...] = x_ref[:, group_cols(k)]
            for t in range(t_new):
                tok_ref[token_rows(t), :] = tok_ref[token_rows(t), :] + y[t * n_seq:(t + 1) * n_seq]
            o_ref[:, k * lanes:(k + 1) * lanes] = tok_ref[...]


def _pool_sample(x, state_t, g, pw, ps, *, slab, t_new):
    n_past, n_seq, d = state_t.shape
    n_tok = n_seq * t_new
    n_groups = len(POOL_WINDOWS)
    pg = d // n_groups
    assert all(w & (w - 1) == 0 and w <= n_past + 1 for w in POOL_WINDOWS)
    kern = functools.partial(_pool_sample_kernel, t_new=t_new, start_pos=PAST_LEN)
    slabs = pl.BlockSpec((n_past, n_seq, pg), lambda p, gi: (0, 0, gi * (1 - p) + (n_groups - 1) * p))
    return pl.pallas_call(
        kern,
        grid=(2, n_groups),
        in_specs=[_resident((n_tok, d)), slabs, _resident((1, d)), _resident_slab(pw.shape, slab), _resident((1, d))],
        out_specs=[pl.BlockSpec((n_tok, pg), lambda p, gi: (0, gi * p)), slabs],
        out_shape=[jax.ShapeDtypeStruct((n_tok, d), F32), jax.ShapeDtypeStruct(state_t.shape, F32)],
        scratch_shapes=[pltpu.VMEM((n_tok, LANES), F32), pltpu.VMEM((n_tok, d), F32), pltpu.VMEM((n_tok, LANES), F32)],
        compiler_params=_params("arbitrary", "arbitrary"),
        name="pool_sample",
    )(x, state_t, g, pw, ps)


def kernel(x_prompt, x_sample, cache_k, cache_v, state_conv, state_pool, norm_mix, w_in, q_norm, k_norm, sinks,
           conv_w, conv_b, conv_ln_g, conv_ln_b, w_out, pool_w, pool_scale, norm_ffn, w_gate, w_up, w_down):
    batch, seq, d = x_prompt.shape
    n_seq, t_new, _ = x_sample.shape
    n_kv = cache_k.shape[3]
    kv_w = n_kv * HEAD_DIM
    conv_ch = state_conv.shape[-1]
    attn_w = w_in.shape[-1] - 2 * kv_w - 2 * conv_ch
    n_prompt = batch * seq
    row = lambda a: a.reshape(1, -1)

    xp = x_prompt.reshape(n_prompt, d)
    xs = x_sample.reshape(n_seq * t_new, d)
    depth = norm_mix.shape[0]
    outs = {}
    for layer in range(depth):
        i = layer // 2
        if layer % 2 == 0:
            reps = MXU_DIM // HEAD_DIM
            q, k, v, glu = _in_proj(
                xp, xs, row(norm_mix[layer]), w_in, row(jnp.tile(q_norm[i], reps)), row(jnp.tile(k_norm[i], reps)),
                slab=i, attn_w=attn_w, kv_w=kv_w, conv_ch=conv_ch)
            attn_p = _attn_prompt(sinks[i], q, k, v, batch=batch, seq=seq, n_kv=n_kv)
            cache_t = lambda a: jnp.transpose(a, (0, 2, 3, 1)).reshape(n_seq, kv_w, WINDOW)
            new_t = lambda a: jnp.transpose(a[n_prompt:].reshape(-1, WINDOW, kv_w), (0, 2, 1))
            attn_s, nk_t, nv_t = _attn_sample(sinks[i], q, new_t(k), new_t(v), cache_t(cache_k[i]),
                                              cache_t(cache_v[i]), row0=n_prompt, t_new=t_new, n_kv=n_kv)
            uncache_t = lambda a: jnp.transpose(a.reshape(n_seq, n_kv, HEAD_DIM, WINDOW), (0, 3, 1, 2))
            conv_p = _conv_prompt(glu, conv_w[i], row(conv_b[i]), batch=batch, seq=seq)
            conv_s, nc_t = _conv_sample(glu, jnp.transpose(state_conv[i], (1, 0, 2)), conv_w[i], row(conv_b[i]),
                                        row0=n_prompt, t_new=t_new)
            nc_s = jnp.transpose(nc_t, (1, 0, 2))
            mixed = _out_proj((xp, xs), (attn_p, attn_s), (conv_p, conv_s), row(conv_ln_g[i]), row(conv_ln_b[i]),
                              w_out, slab=i)
            ffn_in = ((mixed, 0, n_prompt), (mixed, n_prompt, n_seq * t_new))
            last = lambda a, n: jnp.stack([a[(b + 1) * seq - n:(b + 1) * seq] for b in range(batch)])
            outs.setdefault("kp", []).append(last(k, WINDOW).reshape(batch, WINDOW, n_kv, HEAD_DIM))
            outs.setdefault("vp", []).append(last(v, WINDOW).reshape(batch, WINDOW, n_kv, HEAD_DIM))
            outs.setdefault("cp", []).append(last(glu, conv_w.shape[1] - 1))
            outs.setdefault("ks", []).append(uncache_t(nk_t))
            outs.setdefault("vs", []).append(uncache_t(nv_t))
            outs.setdefault("cs", []).append(nc_s)
        else:
            args = (row(norm_mix[layer]), pool_w, row(pool_scale[i]))
            n_past = state_pool.shape[2]
            xp, tail = _pool_prompt(xp, *args, slab=i, batch=batch, seq=seq)
            xs, np_t = _pool_sample(xs, jnp.transpose(state_pool[i], (1, 0, 2)), *args, slab=i, t_new=t_new)
            np_s = jnp.transpose(np_t, (1, 0, 2))
            ffn_in = ((xp, 0, n_prompt), (xs, 0, n_seq * t_new))
            outs.setdefault("pp", []).append(tail[:, -n_past:])
            outs.setdefault("ps", []).append(np_s)
        g_ffn = row(norm_ffn[layer])
        (rows_p, row0_p, n_p), (rows_s, row0_s, n_s) = ffn_in
        xs, wg16, wu16, wd16 = _ffn(rows_s, g_ffn, w_gate, w_up, w_down, layer=layer, emit_bf16=True,
                                    row0=row0_s, n_rows=n_s)
        xp = _ffn(rows_p, g_ffn, wg16, wu16, wd16, row0=row0_p, n_rows=n_p, tf=512)
    st = lambda name: jnp.stack(outs[name])
    return (xp.reshape(batch, seq, d), xs.reshape(n_seq, t_new, d),
            st("kp"), st("vp"), st("cp"), st("pp"), st("ks"), st("vs"), st("cs"), st("ps"))
```

```python
import functools

import jax
import jax.numpy as jnp
from jax import lax
from jax.experimental import pallas as pl
from jax.experimental.pallas import tpu as pltpu

F32 = jnp.float32
BF16 = jnp.bfloat16

HEAD_DIM = 64
WINDOW = 128
PAST_LEN = 8192
POOL_WINDOWS = (2, 4, 8, 16)
RMS_EPS = 1e-6
LN_EPS = 1e-5
NEG = -1e30

LANES = 128
SUBLANES = 8
MXU_DIM = 256
VMEM_LIMIT = 56 * 1024 * 1024


def _params(*semantics):
    return pltpu.CompilerParams(dimension_semantics=semantics, vmem_limit_bytes=VMEM_LIMIT)


def _rms(x, g):
    ms = jnp.mean(x * x, axis=-1, keepdims=True)
    return x * lax.rsqrt(ms + RMS_EPS) * g


def _resident(shape):
    zeros = (0,) * len(shape)
    return pl.BlockSpec(shape, lambda *_: zeros, pipeline_mode=pl.Buffered(1))


def _resident_slab(shape, slab):
    index = (slab,) + (0,) * (len(shape) - 1)
    return pl.BlockSpec((None,) + tuple(shape[1:]), lambda *_: index, pipeline_mode=pl.Buffered(1))


def _split_rows(tm, width, tiles_a, single_buffer_b=False):
    spec_a = pl.BlockSpec((tm, width), lambda i: (jnp.minimum(i, tiles_a - 1), 0))
    mode = dict(pipeline_mode=pl.Buffered(1)) if single_buffer_b else {}
    spec_b = pl.BlockSpec((tm, width), lambda i: (jnp.maximum(i - tiles_a, 0), 0), **mode)
    return [spec_a, spec_b]


def _pick_rows(a_ref, b_ref, tiles_a):
    return jnp.where(pl.program_id(0) < tiles_a, a_ref[...], b_ref[...])


def _in_proj_kernel(xa_ref, xb_ref, g_ref, w_ref, qg_ref, kg_ref, q_ref, k_ref, v_ref, glu_ref, *, tiles_a, attn_w, kv_w, conv_ch):
    h = _rms(_pick_rows(xa_ref, xb_ref, tiles_a), g_ref[...]).astype(BF16)
    row = lax.broadcasted_iota(jnp.int32, (MXU_DIM, MXU_DIM), 0) // HEAD_DIM
    col = lax.broadcasted_iota(jnp.int32, (MXU_DIM, MXU_DIM), 1) // HEAD_DIM
    pmat = jnp.where(row == col, 1.0 / HEAD_DIM, 0.0).astype(BF16)

    def proj(c0):
        return jnp.dot(h, w_ref[:, c0:c0 + MXU_DIM].astype(BF16), preferred_element_type=F32)

    def head_rms(z):
        sq = z * z
        hi = sq.astype(BF16)
        lo = (sq - hi.astype(F32)).astype(BF16)
        ms = jnp.dot(hi, pmat, preferred_element_type=F32) + jnp.dot(lo, pmat, preferred_element_type=F32)
        return z * lax.rsqrt(ms + RMS_EPS)

    scale = HEAD_DIM ** -0.5

    def put_q(c0, z):
        q_ref[:, c0:c0 + MXU_DIM] = (head_rms(z) * qg_ref[...] * scale).astype(q_ref.dtype)

    def put_k(c0, z):
        k_ref[:, c0:c0 + MXU_DIM] = head_rms(z) * kg_ref[...]

    def put_v(c0, z):
        v_ref[:, c0:c0 + MXU_DIM] = z

    def put_glu(c0, a, gate):
        glu_ref[:, c0:c0 + MXU_DIM] = a * jax.nn.sigmoid(gate)

    units = [((c0,), functools.partial(put_q, c0)) for c0 in range(0, attn_w, MXU_DIM)]
    units += [((attn_w + c0,), functools.partial(put_k, c0)) for c0 in range(0, kv_w, MXU_DIM)]
    units += [((attn_w + kv_w + c0,), functools.partial(put_v, c0)) for c0 in range(0, kv_w, MXU_DIM)]
    units += [((attn_w + 2 * kv_w + c0, attn_w + 2 * kv_w + conv_ch + c0), functools.partial(put_glu, c0))
              for c0 in range(0, conv_ch, MXU_DIM)]
    pending = None
    for cols, epilogue in units:
        zs = [proj(c0) for c0 in cols]
        if pending is not None:
            pending[0](*pending[1])
        pending = (epilogue, zs)
    pending[0](*pending[1])


def _in_proj(xa, xb, g, w, qg, kg, *, slab, attn_w, kv_w, conv_ch, tm=512):
    d = xa.shape[1]
    n = xa.shape[0] + xb.shape[0]
    tiles_a = xa.shape[0] // tm
    kern = functools.partial(_in_proj_kernel, tiles_a=tiles_a, attn_w=attn_w, kv_w=kv_w, conv_ch=conv_ch)
    rows = lambda width: pl.BlockSpec((tm, width), lambda i: (i, 0))
    return pl.pallas_call(
        kern,
        grid=(n // tm,),
        in_specs=_split_rows(tm, d, tiles_a, single_buffer_b=True) + [
            _resident((1, d)), _resident_slab(w.shape, slab), _resident((1, MXU_DIM)), _resident((1, MXU_DIM))],
        out_specs=[rows(attn_w), rows(kv_w), rows(kv_w), rows(conv_ch)],
        out_shape=[
            jax.ShapeDtypeStruct((n, attn_w), BF16),
            jax.ShapeDtypeStruct((n, kv_w), F32),
            jax.ShapeDtypeStruct((n, kv_w), F32),
            jax.ShapeDtypeStruct((n, conv_ch), F32),
        ],
        compiler_params=_params("arbitrary"),
        name="in_proj",
    )(xa, xb, g, w, qg, kg)


def _pair_pieces(rows, n_kv):
    low = lax.broadcasted_iota(jnp.int32, (rows.shape[0], LANES), 1) < HEAD_DIM
    pieces = []
    for cch in range(n_kv // 2):
        chunk = rows[:, cch * LANES:(cch + 1) * LANES]
        rolled = pltpu.roll(chunk, HEAD_DIM, axis=1)
        for lo, hi in ((chunk, rolled), (rolled, chunk)):
            pieces.append((jnp.where(low, lo, 0.0).astype(BF16), jnp.where(low, 0.0, hi).astype(BF16)))
    return pieces


def _softmax_pv(s, ok, sink_a, sink_b, vb_ones):
    half = s.shape[1] // 2
    s = jnp.where(ok, s, NEG)
    sa, sb = s[:, :half], s[:, half:]
    ma = jnp.maximum(jnp.max(sa, axis=-1, keepdims=True), sink_a)
    mb = jnp.maximum(jnp.max(sb, axis=-1, keepdims=True), sink_b)
    p = jnp.concatenate([jnp.exp(sa - ma), jnp.exp(sb - mb)], axis=1).astype(BF16)
    r = jnp.dot(p, vb_ones, preferred_element_type=F32)
    lane = lax.broadcasted_iota(jnp.int32, (s.shape[0], LANES), 1)
    sink_term = jnp.where(lane < HEAD_DIM, jnp.exp(sink_a - ma), jnp.exp(sink_b - mb))
    return r[:, :LANES] / (r[:, LANES:] + sink_term)


def _attend(q2, kb, vb, ok, sinks_ref, h0, rows):
    s = lax.dot_general(q2, kb, (((1,), (1,)), ((), ())), preferred_element_type=F32)
    row = lax.broadcasted_iota(jnp.int32, (2 * rows, 1), 0)
    first = row < rows
    sink_a = jnp.where(first, sinks_ref[h0], sinks_ref[h0 + 2])
    sink_b = jnp.where(first, sinks_ref[h0 + 1], sinks_ref[h0 + 3])
    return _softmax_pv(s, ok, sink_a, sink_b, vb)


def _attn_prompt_kernel(sinks_ref, q_ref, kp_ref, kc_ref, vp_ref, vc_ref, o_ref, *, n_kv):
    blk = kp_ref.shape[0]
    i = pl.program_id(1)
    shape = (2 * blk, 4 * blk)
    r = lax.broadcasted_iota(jnp.int32, shape, 0) % blk
    c = lax.broadcasted_iota(jnp.int32, shape, 1) % (2 * blk)
    band = jnp.logical_and(c > r, c <= r + blk)
    band_first = jnp.logical_and(band, jnp.logical_or(c >= blk, i > 0))
    group = (q_ref.shape[1] // HEAD_DIM) // n_kv
    key_row = lax.broadcasted_iota(jnp.int32, (4 * blk, LANES), 0)
    key_lane = lax.broadcasted_iota(jnp.int32, (4 * blk, LANES), 1)
    ones_blocks = jnp.where((key_row < 2 * blk) == (key_lane < HEAD_DIM), 1.0, 0.0).astype(BF16)
    n_sub = q_ref.shape[0] // blk
    k_pieces = [_pair_pieces(kp_ref[...], n_kv)] + [_pair_pieces(kc_ref[s * blk:(s + 1) * blk, :], n_kv) for s in range(n_sub)]
    v_pieces = [_pair_pieces(vp_ref[...], n_kv)] + [_pair_pieces(vc_ref[s * blk:(s + 1) * blk, :], n_kv) for s in range(n_sub)]

    def pair_blocks(pieces, sub, j):
        (prev_top, prev_bot), (cur_top, cur_bot) = pieces[sub][j], pieces[sub + 1][j]
        return jnp.concatenate([prev_top, cur_top, prev_bot, cur_bot], axis=0)

    for sub in range(n_sub):
        rows = slice(sub * blk, (sub + 1) * blk)
        ok = band if sub else band_first
        for j in range(n_kv):
            kb = pair_blocks(k_pieces, sub, j)
            vb = jnp.concatenate([pair_blocks(v_pieces, sub, j), ones_blocks], axis=1)
            h0 = group * j
            c0 = h0 * HEAD_DIM
            q2 = jnp.concatenate([q_ref[rows, c0:c0 + LANES], q_ref[rows, c0 + LANES:c0 + 2 * LANES]], axis=0)
            o = _attend(q2, kb, vb, ok, sinks_ref, h0, blk).astype(o_ref.dtype)
            o_ref[rows, c0:c0 + LANES] = o[:blk]
            o_ref[rows, c0 + LANES:c0 + 2 * LANES] = o[blk:]


def _attn_prompt(sinks, q, k, v, *, batch, seq, n_kv, blocks_per_step=8):
    blk = WINDOW
    tq = blocks_per_step * blk
    nb = seq // tq
    aw, kw = q.shape[1], k.shape[1]
    cur = lambda w: pl.BlockSpec((tq, w), lambda b, i: (b * nb + i, 0))
    prev = lambda w: pl.BlockSpec(
        (blk, w), lambda b, i: ((b * nb + i) * blocks_per_step - jnp.minimum(i, 1), 0))
    return pl.pallas_call(
        functools.partial(_attn_prompt_kernel, n_kv=n_kv),
        grid=(batch, nb),
        in_specs=[pl.BlockSpec(memory_space=pltpu.SMEM), cur(aw), prev(kw), cur(kw), prev(kw), cur(kw)],
        out_specs=cur(aw),
        out_shape=jax.ShapeDtypeStruct((batch * seq, aw), BF16),
        compiler_params=_params("parallel", "arbitrary"),
        name="attn_prompt",
    )(sinks, q, k, k, v, v)


def _attn_sample_kernel(sinks_ref, q_ref, kn_ref, vn_ref, ck_ref, cv_ref, o_ref, nk_ref, nv_ref, qf_ref, of_ref, *, n_kv, t_new):
    g_seqs, kw, win = ck_ref.shape
    n_heads = q_ref.shape[1] // HEAD_DIM
    group = n_heads // n_kv
    n_rows = n_heads * t_new
    qf_ref[...] = q_ref[...].astype(F32)
    t = lax.broadcasted_iota(jnp.int32, (n_rows, 2 * win), 0) % t_new
    c = lax.broadcasted_iota(jnp.int32, (n_rows, 2 * win), 1)
    ok = jnp.logical_or(jnp.logical_and(c < win, c > t),
                        jnp.logical_and(c >= 2 * win - t_new, c - (2 * win - t_new) <= t))
    sink = jnp.concatenate([jnp.full((t_new, 1), sinks_ref[h], F32) for h in range(n_heads)], axis=0)
    low = lax.broadcasted_iota(jnp.int32, (t_new, LANES), 1) < HEAD_DIM
    zero_tile = jnp.zeros((t_new, LANES), F32)
    keep_old = lax.broadcasted_iota(jnp.int32, (kw, win), 1) < win - t_new
    ones_rows = jnp.ones((LANES, 2 * win), BF16)
    kn_all = kn_ref[0]
    vn_all = vn_ref[0]

    def one_seq(g, carry):
        r0 = pl.multiple_of(g * t_new, t_new)
        k_old = ck_ref[g]
        v_old = cv_ref[g]
        shift = win - t_new * (g + 1)
        k_new = pltpu.roll(kn_all, shift, axis=1)
        v_new = pltpu.roll(vn_all, shift, axis=1)
        nk_ref[g] = jnp.where(keep_old, pltpu.roll(k_old, win - t_new, axis=1), k_new)
        nv_ref[g] = jnp.where(keep_old, pltpu.roll(v_old, win - t_new, axis=1), v_new)
        keys = jnp.concatenate([k_old, k_new], axis=1).astype(BF16)
        vals = jnp.concatenate([jnp.concatenate([v_old, v_new], axis=1).astype(BF16), ones_rows], axis=0)
        q_rows = []
        for h in range(n_heads):
            j = h // group
            src = qf_ref[pl.ds(r0, t_new), (h // 2) * LANES:(h // 2 + 1) * LANES]
            if h % 2 != j % 2:
                src = pltpu.roll(src, HEAD_DIM, axis=1)
            half = jnp.where(low, src, 0.0) if j % 2 == 0 else jnp.where(low, 0.0, src)
            tiles = [zero_tile] * (kw // LANES)
            tiles[j // 2] = half
            q_rows.append(jnp.concatenate(tiles, axis=1))
        qb = jnp.concatenate(q_rows, axis=0).astype(BF16)
        s = jnp.dot(qb, keys, preferred_element_type=F32)
        s = jnp.where(ok, s, NEG)
        m = jnp.maximum(jnp.max(s, axis=-1, keepdims=True), sink)
        p = jnp.exp(s - m).astype(BF16)
        r = lax.dot_general(p, vals, (((1,), (1,)), ((), ())), preferred_element_type=F32)
        denom = r[:, kw:] + jnp.exp(sink - m)
        o = r[:, :kw] / jnp.concatenate([denom] * (kw // LANES), axis=1)
        for pair in range(n_heads // 2):
            j = (2 * pair) // group
            lanes = slice((j // 2) * LANES, (j // 2 + 1) * LANES)
            a = o[2 * pair * t_new:(2 * pair + 1) * t_new, lanes]
            b = o[(2 * pair + 1) * t_new:(2 * pair + 2) * t_new, lanes]
            if j % 2 == 0:
                b = pltpu.roll(b, HEAD_DIM, axis=1)
            else:
                a = pltpu.roll(a, HEAD_DIM, axis=1)
            of_ref[pl.ds(r0, t_new), pair * LANES:(pair + 1) * LANES] = jnp.where(low, a, b)
        return carry

    lax.fori_loop(0, g_seqs, one_seq, 0, unroll=8)
    o_ref[...] = of_ref[...].astype(o_ref.dtype)


def _attn_sample(sinks, q, k_t, v_t, cache_k_t, cache_v_t, *, row0, t_new, n_kv):
    n_seq, kw, win = cache_k_t.shape
    g_seqs = win // t_new
    aw = q.shape[1]
    assert k_t.shape == (n_seq // g_seqs, kw, win)
    tile0 = row0 // (g_seqs * t_new)
    tok = pl.BlockSpec((g_seqs * t_new, aw), lambda i: (i, 0))
    q_tok = pl.BlockSpec((g_seqs * t_new, aw), lambda i: (tile0 + i, 0))
    new = pl.BlockSpec((1, kw, win), lambda i: (i, 0, 0))
    cache = pl.BlockSpec((g_seqs, kw, win), lambda i: (i, 0, 0))
    return pl.pallas_call(
        functools.partial(_attn_sample_kernel, n_kv=n_kv, t_new=t_new),
        grid=(n_seq // g_seqs,),
        in_specs=[pl.BlockSpec(memory_space=pltpu.SMEM), q_tok, new, new, cache, cache],
        out_specs=[tok, cache, cache],
        out_shape=[
            jax.ShapeDtypeStruct((n_seq * t_new, aw), BF16),
            jax.ShapeDtypeStruct(cache_k_t.shape, F32),
            jax.ShapeDtypeStruct(cache_v_t.shape, F32),
        ],
        scratch_shapes=[pltpu.VMEM((g_seqs * t_new, aw), F32), pltpu.VMEM((g_seqs * t_new, aw), F32)],
        compiler_params=_params("parallel"),
        name="attn_sample",
    )(sinks, q, k_t, v_t, cache_k_t, cache_v_t)


def _ln_silu(y, g, b):
    yc = y - jnp.mean(y, axis=-1, keepdims=True)
    var = jnp.mean(yc * yc, axis=-1, keepdims=True)
    return jax.nn.silu(yc * lax.rsqrt(var + LN_EPS) * g + b)


def _conv_prompt_kernel(cur_ref, halo_ref, w_ref, b_ref, o_ref, xc_ref, *, width, rows_per, lanes_per):
    tc, ch = cur_ref.shape
    halo = halo_ref.shape[0]
    i = pl.program_id(1)
    xc_ref[:halo, :] = jnp.where(i == 0, 0.0, halo_ref[...])
    xc_ref[halo:, :] = cur_ref[...]
    base = halo - (width - 1)

    def col_chunk(cc, carry):
        l0 = pl.multiple_of(cc * lanes_per, lanes_per)
        cols = pl.ds(l0, lanes_per)
        for rr in range(tc // rows_per):
            acc = jnp.broadcast_to(b_ref[:, cols], (rows_per, lanes_per))
            for phase in range(SUBLANES):
                n_win = rows_per + (SUBLANES if phase else 0)
                part = None
                for j in range(width):
                    if (base + j) % SUBLANES != phase:
                        continue
                    start = rr * rows_per + ((base + j) // SUBLANES) * SUBLANES
                    term = w_ref[pl.ds(j, 1), cols] * xc_ref[pl.ds(start, n_win), cols]
                    part = term if part is None else part + term
                if part is None:
                    continue
                if phase:
                    part = pltpu.roll(part, n_win - phase, axis=0)
                acc = acc + part[:rows_per]
            o_ref[pl.ds(rr * rows_per, rows_per), cols] = acc
        return carry

    lax.fori_loop(0, ch // lanes_per, col_chunk, 0)


def _conv_prompt(glu, w, b, *, batch, seq, tc=1024, halo=32):
    ch = glu.shape[1]
    width = w.shape[0]
    nt = seq // tc
    per = tc // halo
    cur = pl.BlockSpec((tc, ch), lambda bi, i: (bi * nt + i, 0))
    prev = pl.BlockSpec((halo, ch), lambda bi, i: (jnp.maximum((bi * nt + i) * per - 1, 0), 0))
    kern = functools.partial(_conv_prompt_kernel, width=width, rows_per=128, lanes_per=LANES)
    return pl.pallas_call(
        kern,
        grid=(batch, nt),
        in_specs=[cur, prev, _resident(w.shape), _resident((1, ch))],
        out_specs=cur,
        out_shape=jax.ShapeDtypeStruct((batch * seq, ch), F32),
        scratch_shapes=[pltpu.VMEM((halo + tc, ch), F32)],
        compiler_params=_params("parallel", "arbitrary"),
        name="conv_prompt",
    )(glu, glu, w, b)


def _conv_sample_kernel(new_ref, st_ref, w_ref, b_ref, o_ref, ns_ref, tok_ref, *, t_new):
    n_past, n_seq, wide = st_ref.shape
    lanes = tok_ref.shape[1]
    width = w_ref.shape[0]
    for k in range(wide // lanes):
        cols = slice(k * lanes, (k + 1) * lanes)
        tok_ref[...] = new_ref[:, cols]
        new = [tok_ref[pl.ds(t, n_seq, stride=t_new), :] for t in range(t_new)]
        row = lambda r: st_ref[r, :, cols] if r < n_past else new[r - n_past]
        for t in range(t_new):
            acc = jnp.broadcast_to(b_ref[:, cols], (n_seq, lanes))
            for j in range(width):
                acc = acc + w_ref[pl.ds(j, 1), cols] * row(t + j)
            tok_ref[pl.ds(t, n_seq, stride=t_new), :] = acc
        o_ref[:, cols] = tok_ref[...]
        for r in range(n_past):
            ns_ref[r, :, cols] = row(r + t_new)


def _conv_sample(glu, state_t, w, b, *, row0, t_new, wide=2 * LANES):
    n_past, n_seq, ch = state_t.shape
    n_tok = n_seq * t_new
    assert w.shape[0] == n_past + 1
    chunk = lambda rows: pl.BlockSpec((rows, wide), lambda c: (0, c))
    slabs = pl.BlockSpec((n_past, n_seq, wide), lambda c: (0, 0, c))
    return pl.pallas_call(
        functools.partial(_conv_sample_kernel, t_new=t_new),
        grid=(ch // wide,),
        in_specs=[pl.BlockSpec((n_tok, wide), lambda c: (row0 // n_tok, c)), slabs, chunk(w.shape[0]), chunk(1)],
        out_specs=[chunk(n_tok), slabs],
        out_shape=[jax.ShapeDtypeStruct((n_tok, ch), F32), jax.ShapeDtypeStruct(state_t.shape, F32)],
        scratch_shapes=[pltpu.VMEM((n_tok, LANES), F32)],
        compiler_params=_params("parallel"),
        name="conv_sample",
    )(glu, state_t, w, b)


def _out_proj_kernel(xa_ref, xb_ref, aa_ref, ab_ref, ca_ref, cb_ref, g_ref, beta_ref, w_ref, o_ref, *, tiles_a):
    aw = aa_ref.shape[1]
    attn = _pick_rows(aa_ref, ab_ref, tiles_a)
    conv = _ln_silu(_pick_rows(ca_ref, cb_ref, tiles_a), g_ref[...], beta_ref[...]).astype(BF16)
    first = pl.program_id(0) < tiles_a
    half = o_ref.shape[1] // 2
    for c0 in (0, half):
        cols = slice(c0, c0 + half)
        o_ref[:, cols] = (jnp.where(first, xa_ref[:, cols], xb_ref[:, cols])
                          + jnp.dot(attn, w_ref[:aw, cols].astype(BF16), preferred_element_type=F32)
                          + jnp.dot(conv, w_ref[aw:, cols].astype(BF16), preferred_element_type=F32))


def _out_proj(x, attn, conv, ln_g, ln_b, w, *, slab, tm=512):
    d = x[0].shape[1]
    n = x[0].shape[0] + x[1].shape[0]
    tiles_a = x[0].shape[0] // tm
    specs = [spec for pair in (x, attn, conv)
             for spec in _split_rows(tm, pair[0].shape[1], tiles_a, single_buffer_b=pair is x)]
    ch = conv[0].shape[1]
    return pl.pallas_call(
        functools.partial(_out_proj_kernel, tiles_a=tiles_a),
        grid=(n // tm,),
        in_specs=specs + [_resident((1, ch)), _resident((1, ch)), _resident_slab(w.shape, slab)],
        out_specs=pl.BlockSpec((tm, d), lambda i: (i, 0)),
        out_shape=jax.ShapeDtypeStruct((n, d), F32),
        compiler_params=_params("arbitrary"),
        name="out_proj",
    )(*x, *attn, *conv, ln_g, ln_b, w)


def _ffn_kernel(x_ref, g_ref, wg_ref, wu_ref, wd_ref, o_ref, *rest):
    h_ref = rest[-1]

    @pl.when(pl.program_id(1) == 0)
    def _():
        x = x_ref[...]
        h_ref[...] = _rms(x, g_ref[...]).astype(BF16)
        o_ref[...] = x

    weights = [w_ref[...].astype(BF16) for w_ref in (wg_ref, wu_ref, wd_ref)]
    for w_out, w in zip(rest[:-1], weights):
        w_out[...] = w
    wg, wu, wd = weights
    h = h_ref[...]
    a = jnp.dot(h, wg, preferred_element_type=F32)
    u = jnp.dot(h, wu, preferred_element_type=F32)
    act = (jax.nn.silu(a) * u).astype(BF16)
    o_ref[...] += jnp.dot(act, wd, preferred_element_type=F32)


def _ffn(x, g, wg, wu, wd, *, layer=None, emit_bf16=False, row0=0, n_rows=None, tm=1024, tf=256):
    d = x.shape[1]
    n = x.shape[0] if n_rows is None else n_rows
    tile0 = row0 // tm
    dff = wg.shape[-1]
    nt, nf = n // tm, dff // tf
    if layer is None:
        w_in_specs = [pl.BlockSpec((d, tf), lambda i, f: (0, f)), pl.BlockSpec((d, tf), lambda i, f: (0, f)),
                      pl.BlockSpec((tf, d), lambda i, f: (f, 0))]
    else:
        w_in_specs = [pl.BlockSpec((None, d, tf), lambda i, f: (layer, 0, f)),
                      pl.BlockSpec((None, d, tf), lambda i, f: (layer, 0, f)),
                      pl.BlockSpec((None, tf, d), lambda i, f: (layer, f, 0))]
    rows = pl.BlockSpec((tm, d), lambda i, f: (i, 0))
    out_specs, out_shape = [rows], [jax.ShapeDtypeStruct((n, d), F32)]
    if emit_bf16:
        assert nt == 1, "one row tile, so every bf16 weight tile is written exactly once"
        out_specs += [pl.BlockSpec((d, tf), lambda i, f: (0, f)), pl.BlockSpec((d, tf), lambda i, f: (0, f)),
                      pl.BlockSpec((tf, d), lambda i, f: (f, 0))]
        out_shape += [jax.ShapeDtypeStruct((d, dff), BF16), jax.ShapeDtypeStruct((d, dff), BF16),
                      jax.ShapeDtypeStruct((dff, d), BF16)]
    mode = dict(pipeline_mode=pl.Buffered(1)) if nt == 1 else {}
    x_spec = pl.BlockSpec((tm, d), lambda i, f: (tile0 + i, 0), **mode)
    outs = pl.pallas_call(
        _ffn_kernel,
        grid=(nt, nf),
        in_specs=[x_spec, _resident((1, d))] + w_in_specs,
        out_specs=out_specs,
        out_shape=out_shape,
        scratch_shapes=[pltpu.VMEM((tm, d), BF16)],
        compiler_params=_params("parallel", "arbitrary"),
        name="ffn",
    )(x, g, wg, wu, wd)
    return outs if emit_bf16 else outs[0]


def _pool_prompt_kernel(cur_ref, halo_ref, g_ref, pw_ref, ps_ref, o_ref, tail_ref, xp_ref):
    tp, d = cur_ref.shape
    halo = halo_ref.shape[0]
    i = pl.program_id(1)
    gain = g_ref[...]
    xp_ref[:halo, :] = jnp.where(i == 0, 0.0, _rms(halo_ref[...], gain))
    xp_ref[halo:, :] = _rms(cur_ref[...], gain)
    tail_ref[0] = xp_ref[tp:, :]
    pg = d // len(POOL_WINDOWS)
    pos = i * tp + lax.broadcasted_iota(jnp.int32, (tp, 1), 0)
    for gi, w in enumerate(POOL_WINDOWS):
        cols = slice(gi * pg, (gi + 1) * pg)
        ext = xp_ref[:, cols]
        h = ext[halo:]
        span = 1
        while span < w:
            ext = ext + pltpu.roll(ext, span, axis=0)
            span *= 2
        inv_count = 1.0 / jnp.minimum(pos + 1, w).astype(F32)
        dpool = (ext[halo:] * inv_count - h).astype(BF16)
        y = jnp.dot(dpool, pw_ref[gi].astype(BF16), preferred_element_type=F32)
        o_ref[:, cols] = cur_ref[:, cols] + y * ps_ref[:, cols]


def _pool_prompt(x, g, pw, ps, *, slab, batch, seq, tp=1024, halo=16):
    assert all(w & (w - 1) == 0 and w <= halo for w in POOL_WINDOWS)
    d = x.shape[1]
    nt = seq // tp
    per = tp // halo
    cur = pl.BlockSpec((tp, d), lambda bi, i: (bi * nt + i, 0))
    prev = pl.BlockSpec((halo, d), lambda bi, i: (jnp.maximum((bi * nt + i) * per - 1, 0), 0))
    return pl.pallas_call(
        _pool_prompt_kernel,
        grid=(batch, nt),
        in_specs=[cur, prev, _resident((1, d)), _resident_slab(pw.shape, slab), _resident((1, d))],
        out_specs=[cur, pl.BlockSpec((1, halo, d), lambda bi, i: (bi, 0, 0))],
        out_shape=[jax.ShapeDtypeStruct((batch * seq, d), F32), jax.ShapeDtypeStruct((batch, halo, d), F32)],
        scratch_shapes=[pltpu.VMEM((halo + tp, d), F32)],
        compiler_params=_params("parallel", "arbitrary"),
        name="pool_prompt",
    )(x, x, g, pw, ps)


def _pool_sample_kernel(x_ref, st_ref, g_ref, pw_ref, ps_ref, o_ref, ns_ref, inv_ref, y_ref, tok_ref, *, t_new, start_pos):
    n_past, n_seq, pg = st_ref.shape
    lanes = tok_ref.shape[1]
    phase, gi = pl.program_id(0), pl.program_id(1)
    n_tok = x_ref.shape[0]
    token_rows = lambda t: pl.ds(t, n_seq, stride=t_new)
    group_cols = lambda k: pl.ds(pl.multiple_of(gi * pg + k * lanes, lanes), lanes)

    @pl.when(jnp.logical_and(phase == 0, gi == 0))
    def _():
        x = x_ref[...]
        inv = lax.rsqrt(jnp.mean(x * x, axis=-1, keepdims=True) + RMS_EPS)
        inv_ref[...] = jnp.broadcast_to(inv, (n_tok, lanes))

    @pl.when(phase == 0)
    def _():
        chunks = []
        for k in range(pg // lanes):
            cols = slice(k * lanes, (k + 1) * lanes)
            tok_ref[...] = x_ref[:, group_cols(k)] * inv_ref[...]
            gain = g_ref[:, group_cols(k)]
            new = [tok_ref[token_rows(t), :] * gain for t in range(t_new)]
            row = lambda r: st_ref[n_past + r, :, cols] if r < 0 else new[r]
            for r in range(n_past):
                ns_ref[r, :, cols] = row(r + t_new - n_past)
            sums = {1: {r: row(r) for r in range(-n_past, t_new)}}
            span = 1
            while span < max(POOL_WINDOWS):
                prev = sums[span]
                sums[2 * span] = {r: prev[r] + prev[r - span] for r in prev if r - span in prev}
                span *= 2
            diffs = []
            for t in range(t_new):
                pooled = None
                for i, w in enumerate(POOL_WINDOWS):
                    cand = sums[w][t] * (1.0 / min(start_pos + t + 1, w))
                    pooled = cand if pooled is None else jnp.where(gi == i, cand, pooled)
                diffs.append(pooled - new[t])
            chunks.append(jnp.concatenate(diffs, axis=0))
        dp = jnp.concatenate(chunks, axis=1).astype(BF16)
        cols = pl.ds(pl.multiple_of(gi * pg, pg), pg)
        y_ref[:, cols] = jnp.dot(dp, pw_ref[gi].astype(BF16), preferred_element_type=F32) * ps_ref[:, cols]

    @pl.when(phase == 1)
    def _():
        for k in range(pg // lanes):
            y = y_ref[:, group_cols(k)]
            tok_ref[...] = x_ref[:, group_cols(k)]
            for t in range(t_new):
                tok_ref[token_rows(t), :] = tok_ref[token_rows(t), :] + y[t * n_seq:(t + 1) * n_seq]
            o_ref[:, k * lanes:(k + 1) * lanes] = tok_ref[...]


def _pool_sample(x, state_t, g, pw, ps, *, slab, t_new):
    n_past, n_seq, d = state_t.shape
    n_tok = n_seq * t_new
    n_groups = len(POOL_WINDOWS)
    pg = d // n_groups
    assert all(w & (w - 1) == 0 and w <= n_past + 1 for w in POOL_WINDOWS)
    kern = functools.partial(_pool_sample_kernel, t_new=t_new, start_pos=PAST_LEN)
    slabs = pl.BlockSpec((n_past, n_seq, pg), lambda p, gi: (0, 0, gi * (1 - p) + (n_groups - 1) * p))
    return pl.pallas_call(
        kern,
        grid=(2, n_groups),
        in_specs=[_resident((n_tok, d)), slabs, _resident((1, d)), _resident_slab(pw.shape, slab), _resident((1, d))],
        out_specs=[pl.BlockSpec((n_tok, pg), lambda p, gi: (0, gi * p)), slabs],
        out_shape=[jax.ShapeDtypeStruct((n_tok, d), F32), jax.ShapeDtypeStruct(state_t.shape, F32)],
        scratch_shapes=[pltpu.VMEM((n_tok, LANES), F32), pltpu.VMEM((n_tok, d), F32), pltpu.VMEM((n_tok, LANES), F32)],
        compiler_params=_params("arbitrary", "arbitrary"),
        name="pool_sample",
    )(x, state_t, g, pw, ps)


def kernel(x_prompt, x_sample, cache_k, cache_v, state_conv, state_pool, norm_mix, w_in, q_norm, k_norm, sinks,
           conv_w, conv_b, conv_ln_g, conv_ln_b, w_out, pool_w, pool_scale, norm_ffn, w_gate, w_up, w_down):
    batch, seq, d = x_prompt.shape
    n_seq, t_new, _ = x_sample.shape
    n_kv = cache_k.shape[3]
    kv_w = n_kv * HEAD_DIM
    conv_ch = state_conv.shape[-1]
    attn_w = w_in.shape[-1] - 2 * kv_w - 2 * conv_ch
    n_prompt = batch * seq
    row = lambda a: a.reshape(1, -1)

    xp = x_prompt.reshape(n_prompt, d)
    xs = x_sample.reshape(n_seq * t_new, d)
    depth = norm_mix.shape[0]
    outs = {}
    for layer in range(depth):
        i = layer // 2
        if layer % 2 == 0:
            reps = MXU_DIM // HEAD_DIM
            q, k, v, glu = _in_proj(
                xp, xs, row(norm_mix[layer]), w_in, row(jnp.tile(q_norm[i], reps)), row(jnp.tile(k_norm[i], reps)),
                slab=i, attn_w=attn_w, kv_w=kv_w, conv_ch=conv_ch)
            attn_p = _attn_prompt(sinks[i], q, k, v, batch=batch, seq=seq, n_kv=n_kv)
            cache_t = lambda a: jnp.transpose(a, (0, 2, 3, 1)).reshape(n_seq, kv_w, WINDOW)
            new_t = lambda a: jnp.transpose(a[n_prompt:].reshape(-1, WINDOW, kv_w), (0, 2, 1))
            attn_s, nk_t, nv_t = _attn_sample(sinks[i], q, new_t(k), new_t(v), cache_t(cache_k[i]),
                                              cache_t(cache_v[i]), row0=n_prompt, t_new=t_new, n_kv=n_kv)
            uncache_t = lambda a: jnp.transpose(a.reshape(n_seq, n_kv, HEAD_DIM, WINDOW), (0, 3, 1, 2))
            conv_p = _conv_prompt(glu, conv_w[i], row(conv_b[i]), batch=batch, seq=seq)
            conv_s, nc_t = _conv_sample(glu, jnp.transpose(state_conv[i], (1, 0, 2)), conv_w[i], row(conv_b[i]),
                                        row0=n_prompt, t_new=t_new)
            nc_s = jnp.transpose(nc_t, (1, 0, 2))
            mixed = _out_proj((xp, xs), (attn_p, attn_s), (conv_p, conv_s), row(conv_ln_g[i]), row(conv_ln_b[i]),
                              w_out, slab=i)
            ffn_in = ((mixed, 0, n_prompt), (mixed, n_prompt, n_seq * t_new))
            last = lambda a, n: jnp.stack([a[(b + 1) * seq - n:(b + 1) * seq] for b in range(batch)])
            outs.setdefault("kp", []).append(last(k, WINDOW).reshape(batch, WINDOW, n_kv, HEAD_DIM))
            outs.setdefault("vp", []).append(last(v, WINDOW).reshape(batch, WINDOW, n_kv, HEAD_DIM))
            outs.setdefault("cp", []).append(last(glu, conv_w.shape[1] - 1))
            outs.setdefault("ks", []).append(uncache_t(nk_t))
            outs.setdefault("vs", []).append(uncache_t(nv_t))
            outs.setdefault("cs", []).append(nc_s)
        else:
            args = (row(norm_mix[layer]), pool_w, row(pool_scale[i]))
            n_past = state_pool.shape[2]
            xp, tail = _pool_prompt(xp, *args, slab=i, batch=batch, seq=seq)
            xs, np_t = _pool_sample(xs, jnp.transpose(state_pool[i], (1, 0, 2)), *args, slab=i, t_new=t_new)
            np_s = jnp.transpose(np_t, (1, 0, 2))
            ffn_in = ((xp, 0, n_prompt), (xs, 0, n_seq * t_new))
            outs.setdefault("pp", []).append(tail[:, -n_past:])
            outs.setdefault("ps", []).append(np_s)
        g_ffn = row(norm_ffn[layer])
        (rows_p, row0_p, n_p), (rows_s, row0_s, n_s) = ffn_in
        xs, wg16, wu16, wd16 = _ffn(rows_s, g_ffn, w_gate, w_up, w_down, layer=layer, emit_bf16=True,
                                    row0=row0_s, n_rows=n_s)
        xp = _ffn(rows_p, g_ffn, wg16, wu16, wd16, row0=row0_p, n_rows=n_p, tf=512)
    st = lambda name: jnp.stack(outs[name])
    return (xp.reshape(batch, seq, d), xs.reshape(n_seq, t_new, d),
            st("kp"), st("vp"), st("cp"), st("pp"), st("ks"), st("vs"), st("cs"), st("ps"))
```

```python
import functools

import jax
import jax.numpy as jnp
from jax import lax
from jax.experimental import pallas as pl
from jax.experimental.pallas import tpu as pltpu

F32 = jnp.float32
BF16 = jnp.bfloat16

HEAD_DIM = 64
WINDOW = 128
PAST_LEN = 8192
POOL_WINDOWS = (2, 4, 8, 16)
RMS_EPS = 1e-6
LN_EPS = 1e-5
NEG = -1e30

LANES = 128
SUBLANES = 8
MXU_DIM = 256
VMEM_LIMIT = 56 * 1024 * 1024


def _params(*semantics):
    return pltpu.CompilerParams(dimension_semantics=semantics, vmem_limit_bytes=VMEM_LIMIT)


def _rms(x, g):
    ms = jnp.mean(x * x, axis=-1, keepdims=True)
    return x * lax.rsqrt(ms + RMS_EPS) * g


def _resident(shape):
    zeros = (0,) * len(shape)
    return pl.BlockSpec(shape, lambda *_: zeros, pipeline_mode=pl.Buffered(1))


def _resident_slab(shape, slab):
    index = (slab,) + (0,) * (len(shape) - 1)
    return pl.BlockSpec((None,) + tuple(shape[1:]), lambda *_: index, pipeline_mode=pl.Buffered(1))


def _split_rows(tm, width, tiles_a, single_buffer_b=False):
    spec_a = pl.BlockSpec((tm, width), lambda i: (jnp.minimum(i, tiles_a - 1), 0))
    mode = dict(pipeline_mode=pl.Buffered(1)) if single_buffer_b else {}
    spec_b = pl.BlockSpec((tm, width), lambda i: (jnp.maximum(i - tiles_a, 0), 0), **mode)
    return [spec_a, spec_b]


def _pick_rows(a_ref, b_ref, tiles_a):
    return jnp.where(pl.program_id(0) < tiles_a, a_ref[...], b_ref[...])


def _in_proj_kernel(xa_ref, xb_ref, g_ref, w_ref, qg_ref, kg_ref, q_ref, k_ref, v_ref, glu_ref, *, tiles_a, attn_w, kv_w, conv_ch):
    h = _rms(_pick_rows(xa_ref, xb_ref, tiles_a), g_ref[...]).astype(BF16)
    row = lax.broadcasted_iota(jnp.int32, (MXU_DIM, MXU_DIM), 0) // HEAD_DIM
    col = lax.broadcasted_iota(jnp.int32, (MXU_DIM, MXU_DIM), 1) // HEAD_DIM
    pmat = jnp.where(row == col, 1.0 / HEAD_DIM, 0.0).astype(BF16)

    def proj(c0):
        return jnp.dot(h, w_ref[:, c0:c0 + MXU_DIM].astype(BF16), preferred_element_type=F32)

    def head_rms(z):
        sq = z * z
        hi = sq.astype(BF16)
        lo = (sq - hi.astype(F32)).astype(BF16)
        ms = jnp.dot(hi, pmat, preferred_element_type=F32) + jnp.dot(lo, pmat, preferred_element_type=F32)
        return z * lax.rsqrt(ms + RMS_EPS)

    scale = HEAD_DIM ** -0.5

    def put_q(c0, z):
        q_ref[:, c0:c0 + MXU_DIM] = (head_rms(z) * qg_ref[...] * scale).astype(q_ref.dtype)

    def put_k(c0, z):
        k_ref[:, c0:c0 + MXU_DIM] = head_rms(z) * kg_ref[...]

    def put_v(c0, z):
        v_ref[:, c0:c0 + MXU_DIM] = z

    def put_glu(c0, a, gate):
        glu_ref[:, c0:c0 + MXU_DIM] = a * jax.nn.sigmoid(gate)

    units = [((c0,), functools.partial(put_q, c0)) for c0 in range(0, attn_w, MXU_DIM)]
    units += [((attn_w + c0,), functools.partial(put_k, c0)) for c0 in range(0, kv_w, MXU_DIM)]
    units += [((attn_w + kv_w + c0,), functools.partial(put_v, c0)) for c0 in range(0, kv_w, MXU_DIM)]
    units += [((attn_w + 2 * kv_w + c0, attn_w + 2 * kv_w + conv_ch + c0), functools.partial(put_glu, c0))
              for c0 in range(0, conv_ch, MXU_DIM)]
    pending = None
    for cols, epilogue in units:
        zs = [proj(c0) for c0 in cols]
        if pending is not None:
            pending[0](*pending[1])
        pending = (epilogue, zs)
    pending[0](*pending[1])


def _in_proj(xa, xb, g, w, qg, kg, *, slab, attn_w, kv_w, conv_ch, tm=512):
    d = xa.shape[1]
    n = xa.shape[0] + xb.shape[0]
    tiles_a = xa.shape[0] // tm
    kern = functools.partial(_in_proj_kernel, tiles_a=tiles_a, attn_w=attn_w, kv_w=kv_w, conv_ch=conv_ch)
    rows = lambda width: pl.BlockSpec((tm, width), lambda i: (i, 0))
    return pl.pallas_call(
        kern,
        grid=(n // tm,),
        in_specs=_split_rows(tm, d, tiles_a, single_buffer_b=True) + [
            _resident((1, d)), _resident_slab(w.shape, slab), _resident((1, MXU_DIM)), _resident((1, MXU_DIM))],
        out_specs=[rows(attn_w), rows(kv_w), rows(kv_w), rows(conv_ch)],
        out_shape=[
            jax.ShapeDtypeStruct((n, attn_w), BF16),
            jax.ShapeDtypeStruct((n, kv_w), F32),
            jax.ShapeDtypeStruct((n, kv_w), F32),
            jax.ShapeDtypeStruct((n, conv_ch), F32),
        ],
        compiler_params=_params("arbitrary"),
        name="in_proj",
    )(xa, xb, g, w, qg, kg)


def _pair_pieces(rows, n_kv):
    low = lax.broadcasted_iota(jnp.int32, (rows.shape[0], LANES), 1) < HEAD_DIM
    pieces = []
    for cch in range(n_kv // 2):
        chunk = rows[:, cch * LANES:(cch + 1) * LANES]
        rolled = pltpu.roll(chunk, HEAD_DIM, axis=1)
        for lo, hi in ((chunk, rolled), (rolled, chunk)):
            pieces.append((jnp.where(low, lo, 0.0).astype(BF16), jnp.where(low, 0.0, hi).astype(BF16)))
    return pieces


def _softmax_pv(s, ok, sink_a, sink_b, vb_ones):
    half = s.shape[1] // 2
    s = jnp.where(ok, s, NEG)
    sa, sb = s[:, :half], s[:, half:]
    ma = jnp.maximum(jnp.max(sa, axis=-1, keepdims=True), sink_a)
    mb = jnp.maximum(jnp.max(sb, axis=-1, keepdims=True), sink_b)
    p = jnp.concatenate([jnp.exp(sa - ma), jnp.exp(sb - mb)], axis=1).astype(BF16)
    r = jnp.dot(p, vb_ones, preferred_element_type=F32)
    lane = lax.broadcasted_iota(jnp.int32, (s.shape[0], LANES), 1)
    sink_term = jnp.where(lane < HEAD_DIM, jnp.exp(sink_a - ma), jnp.exp(sink_b - mb))
    return r[:, :LANES] / (r[:, LANES:] + sink_term)


def _attend(q2, kb, vb, ok, sinks_ref, h0, rows):
    s = lax.dot_general(q2, kb, (((1,), (1,)), ((), ())), preferred_element_type=F32)
    row = lax.broadcasted_iota(jnp.int32, (2 * rows, 1), 0)
    first = row < rows
    sink_a = jnp.where(first, sinks_ref[h0], sinks_ref[h0 + 2])
    sink_b = jnp.where(first, sinks_ref[h0 + 1], sinks_ref[h0 + 3])
    return _softmax_pv(s, ok, sink_a, sink_b, vb)


def _attn_prompt_kernel(sinks_ref, q_ref, kp_ref, kc_ref, vp_ref, vc_ref, o_ref, *, n_kv):
    blk = kp_ref.shape[0]
    i = pl.program_id(1)
    shape = (2 * blk, 4 * blk)
    r = lax.broadcasted_iota(jnp.int32, shape, 0) % blk
    c = lax.broadcasted_iota(jnp.int32, shape, 1) % (2 * blk)
    band = jnp.logical_and(c > r, c <= r + blk)
    band_first = jnp.logical_and(band, jnp.logical_or(c >= blk, i > 0))
    group = (q_ref.shape[1] // HEAD_DIM) // n_kv
    key_row = lax.broadcasted_iota(jnp.int32, (4 * blk, LANES), 0)
    key_lane = lax.broadcasted_iota(jnp.int32, (4 * blk, LANES), 1)
    ones_blocks = jnp.where((key_row < 2 * blk) == (key_lane < HEAD_DIM), 1.0, 0.0).astype(BF16)
    n_sub = q_ref.shape[0] // blk
    k_pieces = [_pair_pieces(kp_ref[...], n_kv)] + [_pair_pieces(kc_ref[s * blk:(s + 1) * blk, :], n_kv) for s in range(n_sub)]
    v_pieces = [_pair_pieces(vp_ref[...], n_kv)] + [_pair_pieces(vc_ref[s * blk:(s + 1) * blk, :], n_kv) for s in range(n_sub)]

    def pair_blocks(pieces, sub, j):
        (prev_top, prev_bot), (cur_top, cur_bot) = pieces[sub][j], pieces[sub + 1][j]
        return jnp.concatenate([prev_top, cur_top, prev_bot, cur_bot], axis=0)

    for sub in range(n_sub):
        rows = slice(sub * blk, (sub + 1) * blk)
        ok = band if sub else band_first
        for j in range(n_kv):
            kb = pair_blocks(k_pieces, sub, j)
            vb = jnp.concatenate([pair_blocks(v_pieces, sub, j), ones_blocks], axis=1)
            h0 = group * j
            c0 = h0 * HEAD_DIM
            q2 = jnp.concatenate([q_ref[rows, c0:c0 + LANES], q_ref[rows, c0 + LANES:c0 + 2 * LANES]], axis=0)
            o = _attend(q2, kb, vb, ok, sinks_ref, h0, blk).astype(o_ref.dtype)
            o_ref[rows, c0:c0 + LANES] = o[:blk]
            o_ref[rows, c0 + LANES:c0 + 2 * LANES] = o[blk:]


def _attn_prompt(sinks, q, k, v, *, batch, seq, n_kv, blocks_per_step=16):
    blk = WINDOW
    tq = blocks_per_step * blk
    nb = seq // tq
    aw, kw = q.shape[1], k.shape[1]
    cur = lambda w: pl.BlockSpec((tq, w), lambda b, i: (b * nb + i, 0))
    prev = lambda w: pl.BlockSpec(
        (blk, w), lambda b, i: ((b * nb + i) * blocks_per_step - jnp.minimum(i, 1), 0))
    return pl.pallas_call(
        functools.partial(_attn_prompt_kernel, n_kv=n_kv),
        grid=(batch, nb),
        in_specs=[pl.BlockSpec(memory_space=pltpu.SMEM), cur(aw), prev(kw), cur(kw), prev(kw), cur(kw)],
        out_specs=cur(aw),
        out_shape=jax.ShapeDtypeStruct((batch * seq, aw), BF16),
        compiler_params=_params("parallel", "arbitrary"),
        name="attn_prompt",
    )(sinks, q, k, k, v, v)


def _attn_sample_kernel(sinks_ref, q_ref, kn_ref, vn_ref, ck_ref, cv_ref, o_ref, nk_ref, nv_ref, qf_ref, of_ref, *, n_kv, t_new):
    g_seqs, kw, win = ck_ref.shape
    n_heads = q_ref.shape[1] // HEAD_DIM
    group = n_heads // n_kv
    n_rows = n_heads * t_new
    qf_ref[...] = q_ref[...].astype(F32)
    t = lax.broadcasted_iota(jnp.int32, (n_rows, 2 * win), 0) % t_new
    c = lax.broadcasted_iota(jnp.int32, (n_rows, 2 * win), 1)
    ok = jnp.logical_or(jnp.logical_and(c < win, c > t),
                        jnp.logical_and(c >= 2 * win - t_new, c - (2 * win - t_new) <= t))
    sink = jnp.concatenate([jnp.full((t_new, 1), sinks_ref[h], F32) for h in range(n_heads)], axis=0)
    low = lax.broadcasted_iota(jnp.int32, (t_new, LANES), 1) < HEAD_DIM
    zero_tile = jnp.zeros((t_new, LANES), F32)
    keep_old = lax.broadcasted_iota(jnp.int32, (kw, win), 1) < win - t_new
    ones_rows = jnp.ones((LANES, 2 * win), BF16)
    kn_all = kn_ref[0]
    vn_all = vn_ref[0]

    def one_seq(g, carry):
        r0 = pl.multiple_of(g * t_new, t_new)
        k_old = ck_ref[g]
        v_old = cv_ref[g]
        shift = win - t_new * (g + 1)
        k_new = pltpu.roll(kn_all, shift, axis=1)
        v_new = pltpu.roll(vn_all, shift, axis=1)
        nk_ref[g] = jnp.where(keep_old, pltpu.roll(k_old, win - t_new, axis=1), k_new)
        nv_ref[g] = jnp.where(keep_old, pltpu.roll(v_old, win - t_new, axis=1), v_new)
        keys = jnp.concatenate([k_old, k_new], axis=1).astype(BF16)
        vals = jnp.concatenate([jnp.concatenate([v_old, v_new], axis=1).astype(BF16), ones_rows], axis=0)
        q_rows = []
        for h in range(n_heads):
            j = h // group
            src = qf_ref[pl.ds(r0, t_new), (h // 2) * LANES:(h // 2 + 1) * LANES]
            if h % 2 != j % 2:
                src = pltpu.roll(src, HEAD_DIM, axis=1)
            half = jnp.where(low, src, 0.0) if j % 2 == 0 else jnp.where(low, 0.0, src)
            tiles = [zero_tile] * (kw // LANES)
            tiles[j // 2] = half
            q_rows.append(jnp.concatenate(tiles, axis=1))
        qb = jnp.concatenate(q_rows, axis=0).astype(BF16)
        s = jnp.dot(qb, keys, preferred_element_type=F32)
        s = jnp.where(ok, s, NEG)
        m = jnp.maximum(jnp.max(s, axis=-1, keepdims=True), sink)
        p = jnp.exp(s - m).astype(BF16)
        r = lax.dot_general(p, vals, (((1,), (1,)), ((), ())), preferred_element_type=F32)
        denom = r[:, kw:] + jnp.exp(sink - m)
        o = r[:, :kw] / jnp.concatenate([denom] * (kw // LANES), axis=1)
        for pair in range(n_heads // 2):
            j = (2 * pair) // group
            lanes = slice((j // 2) * LANES, (j // 2 + 1) * LANES)
            a = o[2 * pair * t_new:(2 * pair + 1) * t_new, lanes]
            b = o[(2 * pair + 1) * t_new:(2 * pair + 2) * t_new, lanes]
            if j % 2 == 0:
                b = pltpu.roll(b, HEAD_DIM, axis=1)
            else:
                a = pltpu.roll(a, HEAD_DIM, axis=1)
            of_ref[pl.ds(r0, t_new), pair * LANES:(pair + 1) * LANES] = jnp.where(low, a, b)
        return carry

    lax.fori_loop(0, g_seqs, one_seq, 0, unroll=8)
    o_ref[...] = of_ref[...].astype(o_ref.dtype)


def _attn_sample(sinks, q, k_t, v_t, cache_k_t, cache_v_t, *, row0, t_new, n_kv):
    n_seq, kw, win = cache_k_t.shape
    g_seqs = win // t_new
    aw = q.shape[1]
    assert k_t.shape == (n_seq // g_seqs, kw, win)
    tile0 = row0 // (g_seqs * t_new)
    tok = pl.BlockSpec((g_seqs * t_new, aw), lambda i: (i, 0))
    q_tok = pl.BlockSpec((g_seqs * t_new, aw), lambda i: (tile0 + i, 0))
    new = pl.BlockSpec((1, kw, win), lambda i: (i, 0, 0))
    cache = pl.BlockSpec((g_seqs, kw, win), lambda i: (i, 0, 0))
    return pl.pallas_call(
        functools.partial(_attn_sample_kernel, n_kv=n_kv, t_new=t_new),
        grid=(n_seq // g_seqs,),
        in_specs=[pl.BlockSpec(memory_space=pltpu.SMEM), q_tok, new, new, cache, cache],
        out_specs=[tok, cache, cache],
        out_shape=[
            jax.ShapeDtypeStruct((n_seq * t_new, aw), BF16),
            jax.ShapeDtypeStruct(cache_k_t.shape, F32),
            jax.ShapeDtypeStruct(cache_v_t.shape, F32),
        ],
        scratch_shapes=[pltpu.VMEM((g_seqs * t_new, aw), F32), pltpu.VMEM((g_seqs * t_new, aw), F32)],
        compiler_params=_params("parallel"),
        name="attn_sample",
    )(sinks, q, k_t, v_t, cache_k_t, cache_v_t)


def _ln_silu(y, g, b):
    yc = y - jnp.mean(y, axis=-1, keepdims=True)
    var = jnp.mean(yc * yc, axis=-1, keepdims=True)
    return jax.nn.silu(yc * lax.rsqrt(var + LN_EPS) * g + b)


def _conv_prompt_kernel(cur_ref, halo_ref, w_ref, b_ref, o_ref, xc_ref, *, width, rows_per, lanes_per):
    tc, ch = cur_ref.shape
    halo = halo_ref.shape[0]
    i = pl.program_id(1)
    xc_ref[:halo, :] = jnp.where(i == 0, 0.0, halo_ref[...])
    xc_ref[halo:, :] = cur_ref[...]
    base = halo - (width - 1)

    def col_chunk(cc, carry):
        l0 = pl.multiple_of(cc * lanes_per, lanes_per)
        cols = pl.ds(l0, lanes_per)
        for rr in range(tc // rows_per):
            acc = jnp.broadcast_to(b_ref[:, cols], (rows_per, lanes_per))
            for phase in range(SUBLANES):
                n_win = rows_per + (SUBLANES if phase else 0)
                part = None
                for j in range(width):
                    if (base + j) % SUBLANES != phase:
                        continue
                    start = rr * rows_per + ((base + j) // SUBLANES) * SUBLANES
                    term = w_ref[pl.ds(j, 1), cols] * xc_ref[pl.ds(start, n_win), cols]
                    part = term if part is None else part + term
                if part is None:
                    continue
                if phase:
                    part = pltpu.roll(part, n_win - phase, axis=0)
                acc = acc + part[:rows_per]
            o_ref[pl.ds(rr * rows_per, rows_per), cols] = acc
        return carry

    lax.fori_loop(0, ch // lanes_per, col_chunk, 0)


def _conv_prompt(glu, w, b, *, batch, seq, tc=1024, halo=32):
    ch = glu.shape[1]
    width = w.shape[0]
    nt = seq // tc
    per = tc // halo
    cur = pl.BlockSpec((tc, ch), lambda bi, i: (bi * nt + i, 0))
    prev = pl.BlockSpec((halo, ch), lambda bi, i: (jnp.maximum((bi * nt + i) * per - 1, 0), 0))
    kern = functools.partial(_conv_prompt_kernel, width=width, rows_per=128, lanes_per=LANES)
    return pl.pallas_call(
        kern,
        grid=(batch, nt),
        in_specs=[cur, prev, _resident(w.shape), _resident((1, ch))],
        out_specs=cur,
        out_shape=jax.ShapeDtypeStruct((batch * seq, ch), F32),
        scratch_shapes=[pltpu.VMEM((halo + tc, ch), F32)],
        compiler_params=_params("parallel", "arbitrary"),
        name="conv_prompt",
    )(glu, glu, w, b)


def _conv_sample_kernel(new_ref, st_ref, w_ref, b_ref, o_ref, ns_ref, tok_ref, *, t_new):
    n_past, n_seq, wide = st_ref.shape
    lanes = tok_ref.shape[1]
    width = w_ref.shape[0]
    for k in range(wide // lanes):
        cols = slice(k * lanes, (k + 1) * lanes)
        tok_ref[...] = new_ref[:, cols]
        new = [tok_ref[pl.ds(t, n_seq, stride=t_new), :] for t in range(t_new)]
        row = lambda r: st_ref[r, :, cols] if r < n_past else new[r - n_past]
        for t in range(t_new):
            acc = jnp.broadcast_to(b_ref[:, cols], (n_seq, lanes))
            for j in range(width):
                acc = acc + w_ref[pl.ds(j, 1), cols] * row(t + j)
            tok_ref[pl.ds(t, n_seq, stride=t_new), :] = acc
        o_ref[:, cols] = tok_ref[...]
        for r in range(n_past):
            ns_ref[r, :, cols] = row(r + t_new)


def _conv_sample(glu, state_t, w, b, *, row0, t_new, wide=2 * LANES):
    n_past, n_seq, ch = state_t.shape
    n_tok = n_seq * t_new
    assert w.shape[0] == n_past + 1
    chunk = lambda rows: pl.BlockSpec((rows, wide), lambda c: (0, c))
    slabs = pl.BlockSpec((n_past, n_seq, wide), lambda c: (0, 0, c))
    return pl.pallas_call(
        functools.partial(_conv_sample_kernel, t_new=t_new),
        grid=(ch // wide,),
        in_specs=[pl.BlockSpec((n_tok, wide), lambda c: (row0 // n_tok, c)), slabs, chunk(w.shape[0]), chunk(1)],
        out_specs=[chunk(n_tok), slabs],
        out_shape=[jax.ShapeDtypeStruct((n_tok, ch), F32), jax.ShapeDtypeStruct(state_t.shape, F32)],
        scratch_shapes=[pltpu.VMEM((n_tok, LANES), F32)],
        compiler_params=_params("parallel"),
        name="conv_sample",
    )(glu, state_t, w, b)


def _out_proj_kernel(xa_ref, xb_ref, aa_ref, ab_ref, ca_ref, cb_ref, g_ref, beta_ref, w_ref, o_ref, *, tiles_a):
    aw = aa_ref.shape[1]
    attn = _pick_rows(aa_ref, ab_ref, tiles_a)
    conv = _ln_silu(_pick_rows(ca_ref, cb_ref, tiles_a), g_ref[...], beta_ref[...]).astype(BF16)
    o_ref[...] = (_pick_rows(xa_ref, xb_ref, tiles_a)
                  + jnp.dot(attn, w_ref[:aw, :].astype(BF16), preferred_element_type=F32)
                  + jnp.dot(conv, w_ref[aw:, :].astype(BF16), preferred_element_type=F32))


def _out_proj(x, attn, conv, ln_g, ln_b, w, *, slab, tm=512):
    d = x[0].shape[1]
    n = x[0].shape[0] + x[1].shape[0]
    tiles_a = x[0].shape[0] // tm
    specs = [spec for pair in (x, attn, conv)
             for spec in _split_rows(tm, pair[0].shape[1], tiles_a, single_buffer_b=True)]
    ch = conv[0].shape[1]
    return pl.pallas_call(
        functools.partial(_out_proj_kernel, tiles_a=tiles_a),
        grid=(n // tm,),
        in_specs=specs + [_resident((1, ch)), _resident((1, ch)), _resident_slab(w.shape, slab)],
        out_specs=pl.BlockSpec((tm, d), lambda i: (i, 0)),
        out_shape=jax.ShapeDtypeStruct((n, d), F32),
        compiler_params=_params("arbitrary"),
        name="out_proj",
    )(*x, *attn, *conv, ln_g, ln_b, w)


def _ffn_kernel(x_ref, g_ref, wg_ref, wu_ref, wd_ref, o_ref, *rest):
    h_ref = rest[-1]

    @pl.when(pl.program_id(1) == 0)
    def _():
        x = x_ref[...]
        h_ref[...] = _rms(x, g_ref[...]).astype(BF16)
        o_ref[...] = x

    weights = [w_ref[...].astype(BF16) for w_ref in (wg_ref, wu_ref, wd_ref)]
    for w_out, w in zip(rest[:-1], weights):
        w_out[...] = w
    wg, wu, wd = weights
    h = h_ref[...]
    a = jnp.dot(h, wg, preferred_element_type=F32)
    u = jnp.dot(h, wu, preferred_element_type=F32)
    act = (jax.nn.silu(a) * u).astype(BF16)
    o_ref[...] += jnp.dot(act, wd, preferred_element_type=F32)


def _ffn(x, g, wg, wu, wd, *, layer=None, emit_bf16=False, row0=0, n_rows=None, tm=1024, tf=256):
    d = x.shape[1]
    n = x.shape[0] if n_rows is None else n_rows
    tile0 = row0 // tm
    dff = wg.shape[-1]
    nt, nf = n // tm, dff // tf
    if layer is None:
        w_in_specs = [pl.BlockSpec((d, tf), lambda i, f: (0, f)), pl.BlockSpec((d, tf), lambda i, f: (0, f)),
                      pl.BlockSpec((tf, d), lambda i, f: (f, 0))]
    else:
        w_in_specs = [pl.BlockSpec((None, d, tf), lambda i, f: (layer, 0, f)),
                      pl.BlockSpec((None, d, tf), lambda i, f: (layer, 0, f)),
                      pl.BlockSpec((None, tf, d), lambda i, f: (layer, f, 0))]
    rows = pl.BlockSpec((tm, d), lambda i, f: (i, 0))
    out_specs, out_shape = [rows], [jax.ShapeDtypeStruct((n, d), F32)]
    if emit_bf16:
        assert nt == 1, "one row tile, so every bf16 weight tile is written exactly once"
        out_specs += [pl.BlockSpec((d, tf), lambda i, f: (0, f)), pl.BlockSpec((d, tf), lambda i, f: (0, f)),
                      pl.BlockSpec((tf, d), lambda i, f: (f, 0))]
        out_shape += [jax.ShapeDtypeStruct((d, dff), BF16), jax.ShapeDtypeStruct((d, dff), BF16),
                      jax.ShapeDtypeStruct((dff, d), BF16)]
    mode = dict(pipeline_mode=pl.Buffered(1)) if nt == 1 else {}
    x_spec = pl.BlockSpec((tm, d), lambda i, f: (tile0 + i, 0), **mode)
    outs = pl.pallas_call(
        _ffn_kernel,
        grid=(nt, nf),
        in_specs=[x_spec, _resident((1, d))] + w_in_specs,
        out_specs=out_specs,
        out_shape=out_shape,
        scratch_shapes=[pltpu.VMEM((tm, d), BF16)],
        compiler_params=_params("parallel", "arbitrary"),
        name="ffn",
    )(x, g, wg, wu, wd)
    return outs if emit_bf16 else outs[0]


def _pool_prompt_kernel(cur_ref, halo_ref, g_ref, pw_ref, ps_ref, o_ref, tail_ref, xp_ref):
    tp, d = cur_ref.shape
    halo = halo_ref.shape[0]
    i = pl.program_id(1)
    gain = g_ref[...]
    xp_ref[:halo, :] = jnp.where(i == 0, 0.0, _rms(halo_ref[...], gain))
    xp_ref[halo:, :] = _rms(cur_ref[...], gain)
    tail_ref[0] = xp_ref[tp:, :]
    pg = d // len(POOL_WINDOWS)
    pos = i * tp + lax.broadcasted_iota(jnp.int32, (tp, 1), 0)
    for gi, w in enumerate(POOL_WINDOWS):
        cols = slice(gi * pg, (gi + 1) * pg)
        ext = xp_ref[:, cols]
        h = ext[halo:]
        span = 1
        while span < w:
            ext = ext + pltpu.roll(ext, span, axis=0)
            span *= 2
        inv_count = 1.0 / jnp.minimum(pos + 1, w).astype(F32)
        dpool = (ext[halo:] * inv_count - h).astype(BF16)
        y = jnp.dot(dpool, pw_ref[gi].astype(BF16), preferred_element_type=F32)
        o_ref[:, cols] = cur_ref[:, cols] + y * ps_ref[:, cols]


def _pool_prompt(x, g, pw, ps, *, slab, batch, seq, tp=1024, halo=16):
    assert all(w & (w - 1) == 0 and w <= halo for w in POOL_WINDOWS)
    d = x.shape[1]
    nt = seq // tp
    per = tp // halo
    cur = pl.BlockSpec((tp, d), lambda bi, i: (bi * nt + i, 0))
    prev = pl.BlockSpec((halo, d), lambda bi, i: (jnp.maximum((bi * nt + i) * per - 1, 0), 0))
    return pl.pallas_call(
        _pool_prompt_kernel,
        grid=(batch, nt),
        in_specs=[cur, prev, _resident((1, d)), _resident_slab(pw.shape, slab), _resident((1, d))],
        out_specs=[cur, pl.BlockSpec((1, halo, d), lambda bi, i: (bi, 0, 0))],
        out_shape=[jax.ShapeDtypeStruct((batch * seq, d), F32), jax.ShapeDtypeStruct((batch, halo, d), F32)],
        scratch_shapes=[pltpu.VMEM((halo + tp, d), F32)],
        compiler_params=_params("parallel", "arbitrary"),
        name="pool_prompt",
    )(x, x, g, pw, ps)


def _pool_sample_kernel(x_ref, st_ref, g_ref, pw_ref, ps_ref, o_ref, ns_ref, inv_ref, y_ref, tok_ref, *, t_new, start_pos):
    n_past, n_seq, pg = st_ref.shape
    lanes = tok_ref.shape[1]
    phase, gi = pl.program_id(0), pl.program_id(1)
    n_tok = x_ref.shape[0]
    token_rows = lambda t: pl.ds(t, n_seq, stride=t_new)
    group_cols = lambda k: pl.ds(pl.multiple_of(gi * pg + k * lanes, lanes), lanes)

    @pl.when(jnp.logical_and(phase == 0, gi == 0))
    def _():
        x = x_ref[...]
        inv = lax.rsqrt(jnp.mean(x * x, axis=-1, keepdims=True) + RMS_EPS)
        inv_ref[...] = jnp.broadcast_to(inv, (n_tok, lanes))

    @pl.when(phase == 0)
    def _():
        chunks = []
        for k in range(pg // lanes):
            cols = slice(k * lanes, (k + 1) * lanes)
            tok_ref[...] = x_ref[:, group_cols(k)] * inv_ref[...]
            gain = g_ref[:, group_cols(k)]
            new = [tok_ref[token_rows(t), :] * gain for t in range(t_new)]
            row = lambda r: st_ref[n_past + r, :, cols] if r < 0 else new[r]
            for r in range(n_past):
                ns_ref[r, :, cols] = row(r + t_new - n_past)
            sums = {1: {r: row(r) for r in range(-n_past, t_new)}}
            span = 1
            while span < max(POOL_WINDOWS):
                prev = sums[span]
                sums[2 * span] = {r: prev[r] + prev[r - span] for r in prev if r - span in prev}
                span *= 2
            diffs = []
            for t in range(t_new):
                pooled = None
                for i, w in enumerate(POOL_WINDOWS):
                    cand = sums[w][t] * (1.0 / min(start_pos + t + 1, w))
                    pooled = cand if pooled is None else jnp.where(gi == i, cand, pooled)
                diffs.append(pooled - new[t])
            chunks.append(jnp.concatenate(diffs, axis=0))
        dp = jnp.concatenate(chunks, axis=1).astype(BF16)
        cols = pl.ds(pl.multiple_of(gi * pg, pg), pg)
        y_ref[:, cols] = jnp.dot(dp, pw_ref[gi].astype(BF16), preferred_element_type=F32) * ps_ref[:, cols]

    @pl.when(phase == 1)
    def _():
        for k in range(pg // lanes):
            y = y_ref[:, group_cols(k)]
            tok_ref[...] = x_ref[:, group_cols(k)]
            for t in range(t_new):
                tok_ref[token_rows(t), :] = tok_ref[token_rows(t), :] + y[t * n_seq:(t + 1) * n_seq]
            o_ref[:, k * lanes:(k + 1) * lanes] = tok_ref[...]


def _pool_sample(x, state_t, g, pw, ps, *, slab, t_new):
    n_past, n_seq, d = state_t.shape
    n_tok = n_seq * t_new
    n_groups = len(POOL_WINDOWS)
    pg = d // n_groups
    assert all(w & (w - 1) == 0 and w <= n_past + 1 for w in POOL_WINDOWS)
    kern = functools.partial(_pool_sample_kernel, t_new=t_new, start_pos=PAST_LEN)
    slabs = pl.BlockSpec((n_past, n_seq, pg), lambda p, gi: (0, 0, gi * (1 - p) + (n_groups - 1) * p))
    return pl.pallas_call(
        kern,
        grid=(2, n_groups),
        in_specs=[_resident((n_tok, d)), slabs, _resident((1, d)), _resident_slab(pw.shape, slab), _resident((1, d))],
        out_specs=[pl.BlockSpec((n_tok, pg), lambda p, gi: (0, gi * p)), slabs],
        out_shape=[jax.ShapeDtypeStruct((n_tok, d), F32), jax.ShapeDtypeStruct(state_t.shape, F32)],
        scratch_shapes=[pltpu.VMEM((n_tok, LANES), F32), pltpu.VMEM((n_tok, d), F32), pltpu.VMEM((n_tok, LANES), F32)],
        compiler_params=_params("arbitrary", "arbitrary"),
        name="pool_sample",
    )(x, state_t, g, pw, ps)


def kernel(x_prompt, x_sample, cache_k, cache_v, state_conv, state_pool, norm_mix, w_in, q_norm, k_norm, sinks,
           conv_w, conv_b, conv_ln_g, conv_ln_b, w_out, pool_w, pool_scale, norm_ffn, w_gate, w_up, w_down):
    batch, seq, d = x_prompt.shape
    n_seq, t_new, _ = x_sample.shape
    n_kv = cache_k.shape[3]
    kv_w = n_kv * HEAD_DIM
    conv_ch = state_conv.shape[-1]
    attn_w = w_in.shape[-1] - 2 * kv_w - 2 * conv_ch
    n_prompt = batch * seq
    row = lambda a: a.reshape(1, -1)

    xp = x_prompt.reshape(n_prompt, d)
    xs = x_sample.reshape(n_seq * t_new, d)
    depth = norm_mix.shape[0]
    outs = {}
    for layer in range(depth):
        i = layer // 2
        if layer % 2 == 0:
            reps = MXU_DIM // HEAD_DIM
            q, k, v, glu = _in_proj(
                xp, xs, row(norm_mix[layer]), w_in, row(jnp.tile(q_norm[i], reps)), row(jnp.tile(k_norm[i], reps)),
                slab=i, attn_w=attn_w, kv_w=kv_w, conv_ch=conv_ch)
            attn_p = _attn_prompt(sinks[i], q, k, v, batch=batch, seq=seq, n_kv=n_kv)
            cache_t = lambda a: jnp.transpose(a, (0, 2, 3, 1)).reshape(n_seq, kv_w, WINDOW)
            new_t = lambda a: jnp.transpose(a[n_prompt:].reshape(-1, WINDOW, kv_w), (0, 2, 1))
            attn_s, nk_t, nv_t = _attn_sample(sinks[i], q, new_t(k), new_t(v), cache_t(cache_k[i]),
                                              cache_t(cache_v[i]), row0=n_prompt, t_new=t_new, n_kv=n_kv)
            uncache_t = lambda a: jnp.transpose(a.reshape(n_seq, n_kv, HEAD_DIM, WINDOW), (0, 3, 1, 2))
            conv_p = _conv_prompt(glu, conv_w[i], row(conv_b[i]), batch=batch, seq=seq)
            conv_s, nc_t = _conv_sample(glu, jnp.transpose(state_conv[i], (1, 0, 2)), conv_w[i], row(conv_b[i]),
                                        row0=n_prompt, t_new=t_new)
            nc_s = jnp.transpose(nc_t, (1, 0, 2))
            mixed = _out_proj((xp, xs), (attn_p, attn_s), (conv_p, conv_s), row(conv_ln_g[i]), row(conv_ln_b[i]),
                              w_out, slab=i)
            ffn_in = ((mixed, 0, n_prompt), (mixed, n_prompt, n_seq * t_new))
            last = lambda a, n: jnp.stack([a[(b + 1) * seq - n:(b + 1) * seq] for b in range(batch)])
            outs.setdefault("kp", []).append(last(k, WINDOW).reshape(batch, WINDOW, n_kv, HEAD_DIM))
            outs.setdefault("vp", []).append(last(v, WINDOW).reshape(batch, WINDOW, n_kv, HEAD_DIM))
            outs.setdefault("cp", []).append(last(glu, conv_w.shape[1] - 1))
            outs.setdefault("ks", []).append(uncache_t(nk_t))
            outs.setdefault("vs", []).append(uncache_t(nv_t))
            outs.setdefault("cs", []).append(nc_s)
        else:
            args = (row(norm_mix[layer]), pool_w, row(pool_scale[i]))
            n_past = state_pool.shape[2]
            xp, tail = _pool_prompt(xp, *args, slab=i, batch=batch, seq=seq)
            xs, np_t = _pool_sample(xs, jnp.transpose(state_pool[i], (1, 0, 2)), *args, slab=i, t_new=t_new)
            np_s = jnp.transpose(np_t, (1, 0, 2))
            ffn_in = ((xp, 0, n_prompt), (xs, 0, n_seq * t_new))
            outs.setdefault("pp", []).append(tail[:, -n_past:])
            outs.setdefault("ps", []).append(np_s)
        g_ffn = row(norm_ffn[layer])
        (rows_p, row0_p, n_p), (rows_s, row0_s, n_s) = ffn_in
        xs, wg16, wu16, wd16 = _ffn(rows_s, g_ffn, w_gate, w_up, w_down, layer=layer, emit_bf16=True,
                                    row0=row0_s, n_rows=n_s)
        xp = _ffn(rows_p, g_ffn, wg16, wu16, wd16, row0=row0_p, n_rows=n_p, tf=512)
    st = lambda name: jnp.stack(outs[name])
    return (xp.reshape(batch, seq, d), xs.reshape(n_seq, t_new, d),
            st("kp"), st("vp"), st("cp"), st("pp"), st("ks"), st("vs"), st("cs"), st("ps"))
```
